```python
import jax, jax.numpy as jnp
from jax import lax
import numpy as np

D_MODEL = 1024
BATCH = 1
SEQ = 16384
DEPTH = 2
DEC_BATCH = 8
DEC_SEQ = 16
PAST_LEN = 1024

CHUNK = 64
N_MIXERS = 2
N_MLSTM_LAYERS = (DEPTH + 1) // 2
N_DSA_LAYERS = DEPTH // 2

ML_HEADS = 4
ML_DV = D_MODEL // ML_HEADS
ML_DK = ML_DV // 2
ML_QK = ML_HEADS * ML_DK
ML_V = ML_HEADS * ML_DV
ML_IN = 2 * ML_QK + 2 * ML_V + 2 * ML_HEADS
ML_SPLITS = [ML_QK, 2 * ML_QK, 2 * ML_QK + ML_V, 2 * ML_QK + 2 * ML_V, 2 * ML_QK + 2 * ML_V + ML_HEADS]

DSA_HEADS = 8
DSA_KV_HEADS = 2
DSA_HD = D_MODEL // DSA_HEADS
DSA_GROUP = DSA_HEADS // DSA_KV_HEADS
IDX_HEADS = 8
IDX_DIM = 64
DSA_TOPK = 256
Q_BLOCK = 128
DSA_Q = DSA_HEADS * DSA_HD
DSA_KV = DSA_KV_HEADS * DSA_HD
DSA_IN = DSA_Q + 2 * DSA_KV + IDX_HEADS * IDX_DIM + IDX_DIM + IDX_HEADS
DSA_SPLITS = [DSA_Q, DSA_Q + DSA_KV, DSA_Q + 2 * DSA_KV, DSA_Q + 2 * DSA_KV + IDX_HEADS * IDX_DIM,
              DSA_Q + 2 * DSA_KV + IDX_HEADS * IDX_DIM + IDX_DIM]
ROPE_THETA = 500000.0
ROPE_FRACTION = 4

N_EXPERTS = 32
TOP_K = 4
D_FF = D_MODEL
SWIGLU_LIMIT = 7.0
SWIGLU_ALPHA = 1.702
MOE_BLOCK = 64

DEEPNORM_ALPHA = (2 * DEPTH) ** 0.25
DEEPNORM_BETA = (8 * DEPTH) ** -0.25
LN_EPS = 1e-5

kernel_name = 'mlstm_dsa_moe_streaming_step'


def layer_norm(x, g, b):
    xf = x.astype(jnp.float32)
    mu = xf.mean(-1, keepdims=True)
    var = jnp.square(xf - mu).mean(-1, keepdims=True)
    return ((xf - mu) * lax.rsqrt(var + LN_EPS) * g + b).astype(x.dtype)


def partial_rope(x, pos):
    rd = x.shape[-1] // ROPE_FRACTION
    half = rd // 2
    inv = jnp.power(ROPE_THETA, -jnp.arange(half, dtype=jnp.float32) / half)
    ang = pos.astype(jnp.float32)[:, None] * inv[None, :]
    cos = jnp.cos(ang)[None, :, None, :]
    sin = jnp.sin(ang)[None, :, None, :]
    xr = x[..., :rd].astype(jnp.float32)
    x1, x2 = xr[..., :half], xr[..., half:]
    rot = jnp.concatenate([x1 * cos - x2 * sin, x2 * cos + x1 * sin], axis=-1)
    return jnp.concatenate([rot.astype(x.dtype), x[..., rd:]], axis=-1)


def mlstm_chunkwise(q, k, v, logi, logf, C0, n0, m0, chunk):
    B, T, H, _ = q.shape
    nc = T // chunk

    def to_chunks(a):
        a = a.reshape((B, nc, chunk, H) + a.shape[3:])
        return jnp.moveaxis(a, (1, 3), (0, 2))

    causal = jnp.tril(jnp.ones((chunk, chunk), dtype=bool))

    def step(carry, xs):
        C, n, m = carry
        qc, kc, vc, ic, fc = xs
        b = jnp.cumsum(fc, axis=-1)
        dmat = b[..., :, None] - b[..., None, :] + ic[..., None, :]
        dmat = jnp.where(causal, dmat, -jnp.inf)
        inter = b + m[..., None]
        mt = jnp.maximum(inter, dmat.max(-1))
        s = jnp.exp(dmat - mt[..., None]) * jnp.einsum('bhtd,bhsd->bhts', qc, kc)
        wp = jnp.exp(inter - mt)
        num = jnp.einsum('bhts,bhsv->bhtv', s, vc) + wp[..., None] * jnp.einsum('bhvd,bhtd->bhtv', C, qc)
        den = s.sum(-1) + wp * jnp.einsum('bhd,bhtd->bht', n, qc)
        h = num / jnp.maximum(jnp.abs(den), jnp.exp(-mt))[..., None]
        m_new = mt[..., -1]
        wk = jnp.exp(dmat[..., -1, :] - m_new[..., None])
        wprev = jnp.exp(inter[..., -1] - m_new)
        C_new = wprev[..., None, None] * C + jnp.einsum('bhs,bhsv,bhsd->bhvd', wk, vc, kc)
        n_new = wprev[..., None] * n + jnp.einsum('bhs,bhsd->bhd', wk, kc)
        return (C_new, n_new, m_new), h

    xs = (to_chunks(q), to_chunks(k), to_chunks(v), to_chunks(logi), to_chunks(logf))
    (C, n, m), h = lax.scan(step, (C0, n0, m0), xs)
    h = jnp.moveaxis(h, (0, 2), (1, 3)).reshape(B, T, H, -1)
    return h, (C, n, m)


def mlstm_mixer(u, w_in, b_gates, norm_g, w_out, C0, n0, m0, chunk):
    B, T, _ = u.shape
    q, k, v, o, ig, fg = jnp.split(u @ w_in, ML_SPLITS, axis=-1)
    q = q.reshape(B, T, ML_HEADS, ML_DK).astype(jnp.float32) * (ML_DK ** -0.5)
    k = k.reshape(B, T, ML_HEADS, ML_DK).astype(jnp.float32)
    v = v.reshape(B, T, ML_HEADS, ML_DV).astype(jnp.float32)
    gates = jnp.concatenate([ig, fg], axis=-1).astype(jnp.float32) + b_gates.astype(jnp.float32)
    logi = gates[..., :ML_HEADS]
    logf = jax.nn.log_sigmoid(gates[..., ML_HEADS:])
    h, st = mlstm_chunkwise(q, k, v, logi, logf, C0, n0, m0, chunk)
    mu = h.mean(-1, keepdims=True)
    var = jnp.square(h - mu).mean(-1, keepdims=True)
    hn = ((h - mu) * lax.rsqrt(var + LN_EPS)).reshape(B, T, ML_V) * norm_g
    y = (jax.nn.sigmoid(o.astype(jnp.float32)) * hn).astype(u.dtype) @ w_out
    return y, st


def dsa_mixer(u, pos, w_in, w_out, cache_k, cache_v, cache_ki):
    B, T, _ = u.shape
    q, k, v, qi, ki, wi = jnp.split(u @ w_in, DSA_SPLITS, axis=-1)
    q = partial_rope(q.reshape(B, T, DSA_HEADS, DSA_HD), pos)
    k = partial_rope(k.reshape(B, T, DSA_KV_HEADS, DSA_HD), pos)
    v = v.reshape(B, T, DSA_KV_HEADS, DSA_HD)
    qi = partial_rope(qi.reshape(B, T, IDX_HEADS, IDX_DIM), pos)
    ki = partial_rope(ki[:, :, None, :], pos)[:, :, 0, :]
    wi = wi.astype(jnp.float32) * (IDX_HEADS ** -0.5)
    if cache_k is None:
        k_all, v_all, ki_all, kpos = k, v, ki, pos
    else:
        k_all = jnp.concatenate([cache_k, k], axis=1)
        v_all = jnp.concatenate([cache_v, v], axis=1)
        ki_all = jnp.concatenate([cache_ki, ki], axis=1)
        kpos = jnp.concatenate([jnp.arange(cache_k.shape[1], dtype=jnp.int32), pos])
    n_sel = min(DSA_TOPK, k_all.shape[1] // 4)
    qlim = (pos // CHUNK + 1) * CHUNK
    ki_f = ki_all.astype(jnp.float32)

    def attend(args):
        qb, qib, wib, limb = args
        n = qb.shape[1]
        sc = jnp.einsum('bqhd,bsd->bqhs', qib.astype(jnp.float32), ki_f) * (IDX_DIM ** -0.5)
        score = jnp.einsum('bqhs,bqh->bqs', jax.nn.relu(sc), wib)
        adm = kpos[None, :] < limb[:, None]
        score = jnp.where(adm[None], score, -jnp.inf)
        vals, idx = lax.top_k(score, n_sel)
        valid = jnp.isfinite(vals)
        kg = jax.vmap(lambda a, i: a[i])(k_all, idx)
        vg = jax.vmap(lambda a, i: a[i])(v_all, idx)
        qg = qb.reshape(B, n, DSA_KV_HEADS, DSA_GROUP, DSA_HD)
        logits = jnp.einsum('bqhgd,bqshd->bqhgs', qg, kg).astype(jnp.float32) * (DSA_HD ** -0.5)
        logits = jnp.where(valid[:, :, None, None, :], logits, -jnp.inf)
        p = jax.nn.softmax(logits, axis=-1).astype(vg.dtype)
        o = jnp.einsum('bqhgs,bqshd->bqhgd', p, vg)
        return o.reshape(B, n, DSA_Q)

    if T % Q_BLOCK == 0:
        nb = T // Q_BLOCK
        blk = lambda a: jnp.moveaxis(a.reshape((B, nb, Q_BLOCK) + a.shape[2:]), 1, 0)
        o = lax.map(attend, (blk(q), blk(qi), blk(wi), qlim.reshape(nb, Q_BLOCK)))
        o = jnp.moveaxis(o, 0, 1).reshape(B, T, DSA_Q)
    else:
        o = attend((q, qi, wi, qlim))
    return o @ w_out, (k, v, ki)


def moe_ffn(u, w_r, b_r, w1, b1, w2, b2):
    B, T, D = u.shape
    xt = u.reshape(B * T, D)
    N = xt.shape[0]
    logits = (xt @ w_r + b_r).astype(jnp.float32)
    top_v, top_e = lax.top_k(logits, TOP_K)
    gates = jax.nn.softmax(top_v, axis=-1)
    A = N * TOP_K
    e_flat = top_e.reshape(-1)
    tok = jnp.repeat(jnp.arange(N, dtype=jnp.int32), TOP_K)
    g_flat = gates.reshape(-1)
    order = jnp.argsort(e_flat)
    e_s, tok_s, g_s = e_flat[order], tok[order], g_flat[order]
    counts = jnp.bincount(e_flat, length=N_EXPERTS)
    padded = (counts + MOE_BLOCK - 1) // MOE_BLOCK * MOE_BLOCK
    pend = jnp.cumsum(padded)
    pstart = pend - padded
    ustart = jnp.cumsum(counts) - counts
    dest = pstart[e_s] + jnp.arange(A, dtype=jnp.int32) - ustart[e_s]
    P = (A + N_EXPERTS * (MOE_BLOCK - 1) + MOE_BLOCK - 1) // MOE_BLOCK * MOE_BLOCK
    nb = P // MOE_BLOCK
    x_pad = jnp.zeros((P, D), xt.dtype).at[dest].set(xt[tok_s])
    g_pad = jnp.zeros((P,), jnp.float32).at[dest].set(g_s)
    t_pad = jnp.zeros((P,), jnp.int32).at[dest].set(tok_s)
    blk_e = jnp.minimum(jnp.searchsorted(pend, jnp.arange(nb) * MOE_BLOCK, side='right'), N_EXPERTS - 1)

    def expert_block(args):
        xb, e = args
        h = xb @ w1[e] + b1[e]
        hg, hl = h[:, :D_FF], h[:, D_FF:]
        hg = jnp.minimum(hg, SWIGLU_LIMIT)
        hl = jnp.clip(hl, -SWIGLU_LIMIT, SWIGLU_LIMIT)
        a = hg * jax.nn.sigmoid(SWIGLU_ALPHA * hg) * (hl + 1)
        return a @ w2[e] + b2[e]

    y_pad = lax.map(expert_block, (x_pad.reshape(nb, MOE_BLOCK, D), blk_e)).reshape(P, D)
    y = jnp.zeros((N, D), jnp.float32).at[t_pad].add(y_pad.astype(jnp.float32) * g_pad[:, None])
    return y.astype(u.dtype).reshape(B, T, D)


def run_trunk(x, c, pos, ml_state, ds_cache, prm):
    B, T, _ = x.shape
    chunk = CHUNK if T % CHUNK == 0 else T
    f32 = jnp.float32
    cond = jax.nn.silu(c.astype(f32)).astype(x.dtype)
    ml_new, ds_new = [], []
    for i in range(DEPTH):
        j = i // N_MIXERS
        mod = (cond @ prm['w_ada'][i] + prm['b_ada'][i])[:, None, :]
        sh1, sc1, g1, sh2, sc2, g2 = jnp.split(mod, 6, axis=-1)
        u = x * (1 + sc1) + sh1
        if i % N_MIXERS == 0:
            if ml_state is None:
                C0 = jnp.zeros((B, ML_HEADS, ML_DV, ML_DK), f32)
                n0 = jnp.zeros((B, ML_HEADS, ML_DK), f32)
                m0 = jnp.zeros((B, ML_HEADS), f32)
            else:
                C0 = ml_state[0][j].astype(f32)
                n0 = ml_state[1][j].astype(f32)
                m0 = ml_state[2][j].astype(f32)
            y, st = mlstm_mixer(u, prm['ml_w_in'][j], prm['ml_b_gates'][j], prm['ml_norm_g'][j],
                                prm['ml_w_out'][j], C0, n0, m0, chunk)
            ml_new.append(st)
        else:
            if ds_cache is None:
                ck, cv, cki = None, None, None
            else:
                ck, cv, cki = ds_cache[0][j], ds_cache[1][j], ds_cache[2][j]
            y, rows = dsa_mixer(u, pos, prm['ds_w_in'][j], prm['ds_w_out'][j], ck, cv, cki)
            ds_new.append(rows)
        x = layer_norm(DEEPNORM_ALPHA * x + (1 + g1) * y, prm['ln_g'][i, 0], prm['ln_b'][i, 0])
        u = x * (1 + sc2) + sh2
        f = moe_ffn(u, prm['moe_w_router'][i], prm['moe_b_router'][i], prm['moe_w1'][i],
                    prm['moe_b1'][i], prm['moe_w2'][i], prm['moe_b2'][i])
        x = layer_norm(DEEPNORM_ALPHA * x + (1 + g2) * f, prm['ln_g'][i, 1], prm['ln_b'][i, 1])
    ml_out = [jnp.stack([s[t] for s in ml_new]).astype(x.dtype) for t in range(3)]
    ds_out = [jnp.stack([r[t] for r in ds_new]) for t in range(3)]
    return x, ml_out, ds_out


def setup_inputs(seed: int = 0) -> dict:
    key = jax.random.key(seed)
    ks = jax.random.split(key, 32)
    f32 = jnp.float32

    def nrm(k, shape, s):
        return jax.random.normal(k, shape, f32) * s

    beta = DEEPNORM_BETA
    x_prompt = nrm(ks[0], (BATCH, SEQ, D_MODEL), 1.0)
    x_sample = nrm(ks[1], (DEC_BATCH, DEC_SEQ, D_MODEL), 1.0)
    state_mlstm_C = nrm(ks[2], (N_MLSTM_LAYERS, DEC_BATCH, ML_HEADS, ML_DV, ML_DK), 0.1)
    state_mlstm_n = nrm(ks[3], (N_MLSTM_LAYERS, DEC_BATCH, ML_HEADS, ML_DK), 0.1)
    state_mlstm_m = nrm(ks[4], (N_MLSTM_LAYERS, DEC_BATCH, ML_HEADS), 1.0)
    cache_dsa_k = nrm(ks[5], (N_DSA_LAYERS, DEC_BATCH, PAST_LEN, DSA_KV_HEADS, DSA_HD), 1.0)
    cache_dsa_v = nrm(ks[6], (N_DSA_LAYERS, DEC_BATCH, PAST_LEN, DSA_KV_HEADS, DSA_HD), 1.0)
    cache_dsa_kidx = nrm(ks[7], (N_DSA_LAYERS, DEC_BATCH, PAST_LEN, IDX_DIM), 1.0)
    c_prompt = nrm(ks[8], (BATCH, D_MODEL), 1.0)
    c_sample = nrm(ks[9], (DEC_BATCH, D_MODEL), 1.0)
    w_ada = nrm(ks[10], (DEPTH, D_MODEL, 6 * D_MODEL), 0.5 * D_MODEL ** -0.5)
    b_ada = nrm(ks[11], (DEPTH, 6 * D_MODEL), 0.02)
    ln_g = 1.0 + nrm(ks[12], (DEPTH, 2, D_MODEL), 0.02)
    ln_b = nrm(ks[13], (DEPTH, 2, D_MODEL), 0.02)
    ml_cols = jnp.concatenate([jnp.ones((2 * ML_QK,), f32), jnp.full((ML_V,), beta, f32),
                               jnp.ones((ML_V + 2 * ML_HEADS,), f32)])
    ml_w_in = nrm(ks[14], (N_MLSTM_LAYERS, D_MODEL, ML_IN), D_MODEL ** -0.5) * ml_cols
    ml_b_gates = jnp.concatenate([nrm(ks[15], (N_MLSTM_LAYERS, ML_HEADS), 0.5),
                                  3.0 + nrm(ks[16], (N_MLSTM_LAYERS, ML_HEADS), 0.5)], axis=-1)
    ml_norm_g = 1.0 + nrm(ks[17], (N_MLSTM_LAYERS, ML_V), 0.02)
    ml_w_out = nrm(ks[18], (N_MLSTM_LAYERS, ML_V, D_MODEL), beta * ML_V ** -0.5)
    ds_cols = jnp.concatenate([jnp.ones((DSA_Q + DSA_KV,), f32), jnp.full((DSA_KV,), beta, f32),
                               jnp.ones((DSA_IN - DSA_Q - 2 * DSA_KV,), f32)])
    ds_w_in = nrm(ks[19], (N_DSA_LAYERS, D_MODEL, DSA_IN), D_MODEL ** -0.5) * ds_cols
    ds_w_out = nrm(ks[20], (N_DSA_LAYERS, DSA_Q, D_MODEL), beta * DSA_Q ** -0.5)
    moe_w_router = nrm(ks[21], (DEPTH, D_MODEL, N_EXPERTS), D_MODEL ** -0.5)
    moe_b_router = nrm(ks[22], (DEPTH, N_EXPERTS), 0.01)
    moe_w1 = nrm(ks[23], (DEPTH, N_EXPERTS, D_MODEL, 2 * D_FF), beta * D_MODEL ** -0.5)
    moe_b1 = nrm(ks[24], (DEPTH, N_EXPERTS, 2 * D_FF), 0.01)
    moe_w2 = nrm(ks[25], (DEPTH, N_EXPERTS, D_FF, D_MODEL), beta * D_FF ** -0.5)
    moe_b2 = nrm(ks[26], (DEPTH, N_EXPERTS, D_MODEL), 0.01)
    return {'x_prompt': x_prompt, 'x_sample': x_sample,
            'state_mlstm_C': state_mlstm_C, 'state_mlstm_n': state_mlstm_n, 'state_mlstm_m': state_mlstm_m,
            'cache_dsa_k': cache_dsa_k, 'cache_dsa_v': cache_dsa_v, 'cache_dsa_kidx': cache_dsa_kidx,
            'c_prompt': c_prompt, 'c_sample': c_sample,
            'w_ada': w_ada, 'b_ada': b_ada, 'ln_g': ln_g, 'ln_b': ln_b,
            'ml_w_in': ml_w_in, 'ml_b_gates': ml_b_gates, 'ml_norm_g': ml_norm_g, 'ml_w_out': ml_w_out,
            'ds_w_in': ds_w_in, 'ds_w_out': ds_w_out,
            'moe_w_router': moe_w_router, 'moe_b_router': moe_b_router,
            'moe_w1': moe_w1, 'moe_b1': moe_b1, 'moe_w2': moe_w2, 'moe_b2': moe_b2}


def reference(x_prompt, x_sample, state_mlstm_C, state_mlstm_n, state_mlstm_m,
              cache_dsa_k, cache_dsa_v, cache_dsa_kidx, c_prompt, c_sample,
              w_ada, b_ada, ln_g, ln_b, ml_w_in, ml_b_gates, ml_norm_g, ml_w_out,
              ds_w_in, ds_w_out, moe_w_router, moe_b_router, moe_w1, moe_b1, moe_w2, moe_b2):
    prm = {'w_ada': w_ada, 'b_ada': b_ada, 'ln_g': ln_g, 'ln_b': ln_b,
           'ml_w_in': ml_w_in, 'ml_b_gates': ml_b_gates, 'ml_norm_g': ml_norm_g, 'ml_w_out': ml_w_out,
           'ds_w_in': ds_w_in, 'ds_w_out': ds_w_out,
           'moe_w_router': moe_w_router, 'moe_b_router': moe_b_router,
           'moe_w1': moe_w1, 'moe_b1': moe_b1, 'moe_w2': moe_w2, 'moe_b2': moe_b2}
    pos_p = jnp.arange(x_prompt.shape[1], dtype=jnp.int32)
    y_prompt, ml_p, ds_p = run_trunk(x_prompt, c_prompt, pos_p, None, None, prm)
    past = cache_dsa_k.shape[2]
    pos_s = past + jnp.arange(x_sample.shape[1], dtype=jnp.int32)
    y_sample, ml_s, ds_s = run_trunk(x_sample, c_sample, pos_s,
                                     (state_mlstm_C, state_mlstm_n, state_mlstm_m),
                                     (cache_dsa_k, cache_dsa_v, cache_dsa_kidx), prm)
    p_C, p_n, p_m = ml_p
    p_k, p_v, p_kidx = ds_p
    s_C, s_n, s_m = ml_s
    s_k, s_v, s_kidx = ds_s
    return (y_prompt, y_sample, p_C, p_n, p_m, p_k, p_v, p_kidx, s_C, s_n, s_m, s_k, s_v, s_kidx)
```

```python
import functools

import numpy as np
import jax
import jax.numpy as jnp
from jax import lax
from jax.experimental import pallas as pl
from jax.experimental.pallas import tpu as pltpu

F32 = jnp.float32
BF16 = jnp.bfloat16
I32 = jnp.int32

D_MODEL = 1024
DEPTH = 2
CHUNK = 64
ML_HEADS = 4
ML_DV = D_MODEL // ML_HEADS
ML_DK = ML_DV // 2
ML_QK = ML_HEADS * ML_DK
ML_V = ML_HEADS * ML_DV
ML_IN = 2 * ML_QK + 2 * ML_V + 2 * ML_HEADS
ML_GATE_OFF = 2 * ML_QK + 2 * ML_V
DSA_HEADS = 8
DSA_KV_HEADS = 2
DSA_HD = D_MODEL // DSA_HEADS
DSA_GROUP = DSA_HEADS // DSA_KV_HEADS
IDX_HEADS = 8
IDX_DIM = 64
DSA_TOPK = 256
DSA_Q = DSA_HEADS * DSA_HD
DSA_KV = DSA_KV_HEADS * DSA_HD
DSA_IN = DSA_Q + 2 * DSA_KV + IDX_HEADS * IDX_DIM + IDX_DIM + IDX_HEADS
ROPE_THETA = 500000.0
ROPE_FRACTION = 4
N_EXPERTS = 32
TOP_K = 4
D_FF = D_MODEL
SWIGLU_LIMIT = 7.0
SWIGLU_ALPHA = 1.702
DEEPNORM_ALPHA = (2 * DEPTH) ** 0.25
LN_EPS = 1e-5

LANES = 128
VMEM_LIMIT = 56 * 1024 * 1024

ML_IN_PAD = -(-ML_IN // LANES) * LANES
DSA_IN_PAD = -(-DSA_IN // LANES) * LANES
NEG_BIG = -1e30
F32_MAX = float(np.finfo(np.float32).max)
HIGHEST = lax.Precision.HIGHEST
NT_DIMS = (((1,), (1,)), ((), ()))
TN_DIMS = (((0,), (0,)), ((), ()))


def _params(*sem):
    return pltpu.CompilerParams(dimension_semantics=sem, vmem_limit_bytes=VMEM_LIMIT)


def _row_tile(t, pref):
    return pref if t % pref == 0 else t


def _ada_kernel(c_ref, w_ref, b_ref, o_ref):
    c = c_ref[...]
    cond = (c * jax.nn.sigmoid(c)).astype(BF16)
    o_ref[0] = jnp.dot(cond, w_ref[0].astype(BF16), preferred_element_type=F32) + b_ref[0]


def _ada(c_rows, w_ada, b_ada):
    rows = c_rows.shape[0]
    n = w_ada.shape[-1]
    tn = 1536
    return pl.pallas_call(
        _ada_kernel,
        grid=(DEPTH, n // tn),
        in_specs=[
            pl.BlockSpec((rows, D_MODEL), lambda i, j: (0, 0)),
            pl.BlockSpec((1, D_MODEL, tn), lambda i, j: (i, 0, j)),
            pl.BlockSpec((1, 1, tn), lambda i, j: (i, 0, j)),
        ],
        out_specs=pl.BlockSpec((1, rows, tn), lambda i, j: (i, 0, j)),
        out_shape=jax.ShapeDtypeStruct((DEPTH, rows, n), F32),
        compiler_params=_params("arbitrary", "arbitrary"),
        name="ada_mod",
    )(c_rows, w_ada, b_ada.reshape(DEPTH, 1, n))


def _layer_norm(z, g, b):
    mu = jnp.mean(z, axis=-1, keepdims=True)
    zc = z - mu
    var = jnp.mean(zc * zc, axis=-1, keepdims=True)
    return zc * lax.rsqrt(var + LN_EPS) * g + b


def _inproj_kernel(*refs, has_res):
    if has_res:
        x_ref, f_ref, g_ref, lng_ref, lnb_ref, sc_ref, sh_ref, w_ref, xo_ref, p_ref = refs
        z = DEEPNORM_ALPHA * x_ref[0] + (1.0 + g_ref[0]) * f_ref[0]
        x = _layer_norm(z, lng_ref[...], lnb_ref[...])
        xo_ref[0] = x
    else:
        x_ref, sc_ref, sh_ref, w_ref, p_ref = refs
        x = x_ref[0]
    u = x * (1.0 + sc_ref[0]) + sh_ref[0]
    p_ref[0] = jnp.dot(u.astype(BF16), w_ref[...], preferred_element_type=F32)


def _inproj(x, sc, sh, w_bf, res=None):
    b, t, d = x.shape
    n = w_bf.shape[1]
    tm = _row_tile(t, 512)
    row = pl.BlockSpec((1, tm, d), lambda i, j: (i, j, 0))
    vec = pl.BlockSpec((1, 1, d), lambda i, j: (i, 0, 0))
    par = pl.BlockSpec((1, d), lambda i, j: (0, 0))
    wspec = pl.BlockSpec((d, n), lambda i, j: (0, 0))
    pspec = pl.BlockSpec((1, tm, n), lambda i, j: (i, j, 0))
    pshape = jax.ShapeDtypeStruct((b, t, n), F32)
    if res is None:
        return None, pl.pallas_call(
            functools.partial(_inproj_kernel, has_res=False),
            grid=(b, t // tm),
            in_specs=[row, vec, vec, wspec],
            out_specs=pspec,
            out_shape=pshape,
            compiler_params=_params("arbitrary", "arbitrary"),
            name="inproj",
        )(x, sc, sh, w_bf)
    f, g, lng, lnb = res
    return pl.pallas_call(
        functools.partial(_inproj_kernel, has_res=True),
        grid=(b, t // tm),
        in_specs=[row, row, vec, par, par, vec, vec, wspec],
        out_specs=[row, pspec],
        out_shape=[jax.ShapeDtypeStruct((b, t, d), F32), pshape],
        compiler_params=_params("arbitrary", "arbitrary"),
        name="ln_inproj",
    )(x, f, g, lng, lnb, sc, sh, w_bf)


def _final_ln_kernel(x_ref, f_ref, g_ref, lng_ref, lnb_ref, o_ref):
    z = DEEPNORM_ALPHA * x_ref[0] + (1.0 + g_ref[0]) * f_ref[0]
    o_ref[0] = _layer_norm(z, lng_ref[...], lnb_ref[...])


def _final_ln(x, f, g, lng, lnb):
    b, t, d = x.shape
    tm = _row_tile(t, 1024)
    row = pl.BlockSpec((1, tm, d), lambda i, j: (i, j, 0))
    vec = pl.BlockSpec((1, 1, d), lambda i, j: (i, 0, 0))
    par = pl.BlockSpec((1, d), lambda i, j: (0, 0))
    return pl.pallas_call(
        _final_ln_kernel,
        grid=(b, t // tm),
        in_specs=[row, row, vec, par, par],
        out_specs=row,
        out_shape=jax.ShapeDtypeStruct((b, t, d), F32),
        compiler_params=_params("arbitrary", "arbitrary"),
        name="final_ln",
    )(x, f, g, lng, lnb)


def _log_sigmoid(x):
    return jnp.minimum(x, 0.0) - jnp.log1p(jnp.exp(-jnp.abs(x)))


def _mlstm_kernel(p_ref, gt_ref, bcol_ref, brow_ref, ng_ref, c0_ref, n0_ref, m0_ref,
                  hg_ref, c_ref, n_ref, m_ref, *, chunk):
    L = chunk

    @pl.when(pl.program_id(1) == 0)
    def _():
        c_ref[...] = c0_ref[...]
        n_ref[...] = n0_ref[...]
        m_ref[...] = m0_ref[...]

    gcol = p_ref[0, :, ML_GATE_OFF:ML_GATE_OFF + 2 * ML_HEADS] + bcol_ref[...]
    grow = gt_ref[0] + brow_ref[...]
    lf_col = _log_sigmoid(gcol)
    lf_row = _log_sigmoid(grow)
    ti = lax.broadcasted_iota(I32, (L, L), 0)
    si = lax.broadcasted_iota(I32, (L, L), 1)
    causal = si <= ti
    tril = jnp.where(causal, 1.0, 0.0).astype(F32)
    triu = jnp.where(ti <= si, 1.0, 0.0).astype(F32)
    b_col = jnp.dot(tril, lf_col, precision=HIGHEST, preferred_element_type=F32)
    b_row = jnp.dot(lf_row, triu, precision=HIGHEST, preferred_element_type=F32)

    for h in range(ML_HEADS):
        q = p_ref[0, :, h * ML_DK:(h + 1) * ML_DK]
        k = p_ref[0, :, ML_QK + h * ML_DK:ML_QK + (h + 1) * ML_DK]
        v = p_ref[0, :, 2 * ML_QK + h * ML_DV:2 * ML_QK + (h + 1) * ML_DV]
        o = p_ref[0, :, 2 * ML_QK + ML_V + h * ML_DV:2 * ML_QK + ML_V + (h + 1) * ML_DV]
        qs = (q * (ML_DK ** -0.5)).astype(BF16)
        kb = k.astype(BF16)
        vb = v.astype(BF16)
        bc = b_col[:, ML_HEADS + h:ML_HEADS + h + 1]
        ic = gcol[:, h:h + 1]
        br = b_row[ML_HEADS + h:ML_HEADS + h + 1, :]
        ir = grow[h:h + 1, :]
        c_old = c_ref[0, h]
        n_old = n_ref[0, h:h + 1, :]
        m_old = m_ref[0, h:h + 1, 0:1]

        dm = jnp.where(causal, bc - br + ir, -jnp.inf)
        inter = bc + m_old
        mt = jnp.maximum(inter, jnp.max(dm, axis=-1, keepdims=True))
        qk = lax.dot_general(qs, kb, NT_DIMS, preferred_element_type=F32)
        s = jnp.exp(dm - mt) * qk
        wp = jnp.exp(inter - mt)
        qc = lax.dot_general(qs, c_old.astype(BF16), NT_DIMS, preferred_element_type=F32)
        num = jnp.dot(s.astype(BF16), vb, preferred_element_type=F32) + wp * qc
        qn = jnp.sum(qs.astype(F32) * n_old, axis=-1, keepdims=True)
        den = jnp.sum(s, axis=-1, keepdims=True) + wp * qn
        hh = num / jnp.maximum(jnp.abs(den), jnp.exp(-mt))

        m_new = mt[L - 1:L, :]
        wk = jnp.exp(bc[L - 1:L, :] - bc + ic - m_new)
        wprev = jnp.exp(inter[L - 1:L, :] - m_new)
        vw = (v * wk).astype(BF16)
        c_ref[0, h] = wprev * c_old + lax.dot_general(vw, kb, TN_DIMS, preferred_element_type=F32)
        n_ref[0, h:h + 1, :] = wprev * n_old + jnp.sum(wk * kb.astype(F32), axis=0, keepdims=True)
        m_ref[0, h:h + 1, :] = jnp.broadcast_to(m_new, (1, ML_DK))

        mu = jnp.mean(hh, axis=-1, keepdims=True)
        hc = hh - mu
        var = jnp.mean(hc * hc, axis=-1, keepdims=True)
        hn = hc * lax.rsqrt(var + LN_EPS) * ng_ref[:, h * ML_DV:(h + 1) * ML_DV]
        hg_ref[0, :, h * ML_DV:(h + 1) * ML_DV] = (jax.nn.sigmoid(o) * hn).astype(BF16)


def _mlstm(p, b_gates, norm_g, c0, n0, m0, chunk):
    b, t, n = p.shape
    nc = t // chunk
    gt = jnp.swapaxes(p[:, :, ML_GATE_OFF:ML_GATE_OFF + 2 * ML_HEADS], 1, 2)
    if nc > 1:
        gt_spec = pl.BlockSpec((1, 2 * ML_HEADS, chunk), lambda i, j: (i, 0, j))
    else:
        gt_spec = pl.BlockSpec((1, 2 * ML_HEADS, t), lambda i, j: (i, 0, 0))
    if nc > 1 and chunk % LANES != 0:
        gt = gt.reshape(b, 2 * ML_HEADS, nc, chunk).transpose(0, 2, 1, 3).reshape(b * nc, 2 * ML_HEADS, chunk)
        gt_spec = pl.BlockSpec((1, 2 * ML_HEADS, chunk), lambda i, j: (i * nc + j, 0, 0))
    m0b = jnp.broadcast_to(m0[..., None], (b, ML_HEADS, ML_DK))
    cspec = pl.BlockSpec((1, ML_HEADS, ML_DV, ML_DK), lambda i, j: (i, 0, 0, 0))
    nspec = pl.BlockSpec((1, ML_HEADS, ML_DK), lambda i, j: (i, 0, 0))
    hg, c, nn, m = pl.pallas_call(
        functools.partial(_mlstm_kernel, chunk=chunk),
        grid=(b, nc),
        in_specs=[
            pl.BlockSpec((1, chunk, n), lambda i, j: (i, j, 0)),
            gt_spec,
            pl.BlockSpec((1, 2 * ML_HEADS), lambda i, j: (0, 0)),
            pl.BlockSpec((2 * ML_HEADS, 1), lambda i, j: (0, 0)),
            pl.BlockSpec((1, ML_V), lambda i, j: (0, 0)),
            cspec, nspec, nspec,
        ],
        out_specs=[pl.BlockSpec((1, chunk, ML_V), lambda i, j: (i, j, 0)), cspec, nspec, nspec],
        out_shape=[
            jax.ShapeDtypeStruct((b, t, ML_V), BF16),
            jax.ShapeDtypeStruct((b, ML_HEADS, ML_DV, ML_DK), F32),
            jax.ShapeDtypeStruct((b, ML_HEADS, ML_DK), F32),
            jax.ShapeDtypeStruct((b, ML_HEADS, ML_DK), F32),
        ],
        compiler_params=_params("arbitrary", "arbitrary"),
        name="mlstm_scan",
    )(p, gt, b_gates.reshape(1, -1), b_gates.reshape(-1, 1), norm_g.reshape(1, -1), c0, n0, m0b)
    return hg, c, nn, m[..., 0]


def _outproj_kernel(a_ref, w_ref, x_ref, g_ref, sc_ref, sh_ref, lng_ref, lnb_ref, wr_ref, br_ref,
                    x1_ref, u2_ref, te_ref, tg_ref):
    y = jnp.dot(a_ref[0], w_ref[...], preferred_element_type=F32)
    z = DEEPNORM_ALPHA * x_ref[0] + (1.0 + g_ref[0]) * y
    x1 = _layer_norm(z, lng_ref[...], lnb_ref[...])
    x1_ref[0] = x1
    u2 = x1 * (1.0 + sc_ref[0]) + sh_ref[0]
    u2_ref[0] = u2.astype(BF16)
    logits = jnp.dot(u2, wr_ref[...], precision=HIGHEST, preferred_element_type=F32) + br_ref[...]
    lane = lax.broadcasted_iota(I32, logits.shape, 1).astype(F32)
    vals, idxs = [], []
    cur = logits
    for _ in range(TOP_K):
        mx = jnp.max(cur, axis=-1, keepdims=True)
        idx = jnp.min(jnp.where(cur == mx, lane, float(LANES)), axis=-1, keepdims=True)
        vals.append(mx)
        idxs.append(idx)
        cur = jnp.where(lane == idx, -jnp.inf, cur)
    es = [jnp.exp(v - vals[0]) for v in vals]
    tot = es[0] + es[1] + es[2] + es[3]
    for k in range(TOP_K):
        te_ref[0, :, k:k + 1] = idxs[k].astype(I32)
        tg_ref[0, :, k:k + 1] = es[k] / tot


def _outproj(a, w_bf, x, g, sc, sh, lng, lnb, wr_pad, br_pad):
    b, t, d = x.shape
    tm = _row_tile(t, 512)
    row = pl.BlockSpec((1, tm, d), lambda i, j: (i, j, 0))
    vec = pl.BlockSpec((1, 1, d), lambda i, j: (i, 0, 0))
    par = pl.BlockSpec((1, d), lambda i, j: (0, 0))
    top = pl.BlockSpec((1, tm, TOP_K), lambda i, j: (i, j, 0))
    return pl.pallas_call(
        _outproj_kernel,
        grid=(b, t // tm),
        in_specs=[row, pl.BlockSpec((d, d), lambda i, j: (0, 0)), row, vec, vec, vec, par, par,
                  pl.BlockSpec((d, LANES), lambda i, j: (0, 0)), pl.BlockSpec((1, LANES), lambda i, j: (0, 0))],
        out_specs=[row, row, top, top],
        out_shape=[jax.ShapeDtypeStruct((b, t, d), F32), jax.ShapeDtypeStruct((b, t, d), BF16),
                   jax.ShapeDtypeStruct((b, t, TOP_K), I32), jax.ShapeDtypeStruct((b, t, TOP_K), F32)],
        compiler_params=_params("arbitrary", "arbitrary"),
        name="outproj_ln_router",
    )(a, w_bf, x, g, sc, sh, lng, lnb, wr_pad, br_pad)


def _moe_kernel(x_ref, te_ref, tg_ref, w1_ref, b1_ref, w2_ref, b2_ref, o_ref, pos_ref, tri_ref, *, tm, rb):
    e = pl.program_id(1)

    @pl.when(jnp.logical_and(pl.program_id(0) == 0, e == 0))
    def _():
        si = lax.broadcasted_iota(I32, (tm, tm), 0)
        ti = lax.broadcasted_iota(I32, (tm, tm), 1)
        tri_ref[...] = jnp.where(si <= ti, 1.0, 0.0).astype(BF16)

    @pl.when(e == 0)
    def _():
        o_ref[...] = jnp.zeros_like(o_ref)
        eio = lax.broadcasted_iota(I32, (N_EXPERTS, tm), 0)
        sel = jnp.zeros((N_EXPERTS, tm), F32)
        for k in range(TOP_K):
            sel = sel + jnp.where(te_ref[k:k + 1, :] == eio, 1.0, 0.0)
        rank = jnp.dot(sel.astype(BF16), tri_ref[...], preferred_element_type=F32)
        pos_ref[...] = rank * sel

    pm = pos_ref[pl.ds(e, 1), :]
    gate = jnp.zeros((1, tm), F32)
    for k in range(TOP_K):
        gate = gate + jnp.where(te_ref[k:k + 1, :] == e, tg_ref[k:k + 1, :], 0.0)
    cnt = jnp.max(pm).astype(I32)
    nblk = (cnt + (rb - 1)) // rb

    def body(blk, carry):
        r = (blk * rb + 1 + lax.broadcasted_iota(I32, (rb, 1), 0)).astype(F32)
        hit = pm == r
        onehot = jnp.where(hit, 1.0, 0.0).astype(BF16)
        xg = jnp.dot(onehot, x_ref[...], preferred_element_type=F32).astype(BF16)
        h = jnp.dot(xg, w1_ref[0], preferred_element_type=F32) + b1_ref[0]
        hg = jnp.minimum(h[:, :D_FF], SWIGLU_LIMIT)
        hl = jnp.clip(h[:, D_FF:], -SWIGLU_LIMIT, SWIGLU_LIMIT)
        a = hg * jax.nn.sigmoid(SWIGLU_ALPHA * hg) * (hl + 1.0)
        y = jnp.dot(a.astype(BF16), w2_ref[0], preferred_element_type=F32) + b2_ref[0]
        grow = jnp.sum(jnp.where(hit, gate, 0.0), axis=-1, keepdims=True)
        yg = (y * grow).astype(BF16)
        o_ref[...] += lax.dot_general(onehot, yg, TN_DIMS, preferred_element_type=F32)
        return carry

    lax.fori_loop(0, nblk, body, 0)


def _moe(u2, te, tg, w1_bf, b1, w2_bf, b2):
    n, d = u2.shape
    tm = _row_tile(n, 1024)
    rb = min(128, tm)
    te_t = te.T
    tg_t = tg.T
    return pl.pallas_call(
        functools.partial(_moe_kernel, tm=tm, rb=rb),
        grid=(n // tm, N_EXPERTS),
        in_specs=[
            pl.BlockSpec((tm, d), lambda i, e: (i, 0)),
            pl.BlockSpec((TOP_K, tm), lambda i, e: (0, i)),
            pl.BlockSpec((TOP_K, tm), lambda i, e: (0, i)),
            pl.BlockSpec((1, d, 2 * D_FF), lambda i, e: (e, 0, 0)),
            pl.BlockSpec((1, 1, 2 * D_FF), lambda i, e: (e, 0, 0)),
            pl.BlockSpec((1, D_FF, d), lambda i, e: (e, 0, 0)),
            pl.BlockSpec((1, 1, d), lambda i, e: (e, 0, 0)),
        ],
        out_specs=pl.BlockSpec((tm, d), lambda i, e: (i, 0)),
        out_shape=jax.ShapeDtypeStruct((n, d), F32),
        scratch_shapes=[pltpu.VMEM((N_EXPERTS, tm), F32), pltpu.VMEM((tm, tm), BF16)],
        compiler_params=_params("arbitrary", "arbitrary"),
        name="moe_experts",
    )(u2, te_t, tg_t, w1_bf, b1.reshape(N_EXPERTS, 1, -1), w2_bf, b2.reshape(N_EXPERTS, 1, -1))


def _rope_tables(pos, head_dim):
    rd = head_dim // ROPE_FRACTION
    half = rd // 2
    inv = jnp.power(ROPE_THETA, -jnp.arange(half, dtype=F32) / half)
    ang = jnp.asarray(pos).astype(F32)[:, None] * inv[None, :]
    cos, sin = jnp.cos(ang), jnp.sin(ang)
    t = pos.shape[0]
    ones = jnp.ones((t, head_dim - rd), F32)
    zeros = jnp.zeros((t, head_dim - rd), F32)
    zh = jnp.zeros((t, half), F32)
    c = jnp.concatenate([cos, cos, ones], axis=-1)
    sa = jnp.concatenate([zh, sin, zeros], axis=-1)
    sb = jnp.concatenate([-sin, zh, zeros], axis=-1)
    rep = LANES // head_dim
    return tuple(jnp.tile(a, (1, rep)) for a in (c, sa, sb))


def _rope(z, c, sa, sb, half):
    w = z.shape[-1]
    rep = w // LANES
    if rep > 1:
        c, sa, sb = (jnp.concatenate([a] * rep, axis=-1) for a in (c, sa, sb))
    return z * c + pltpu.roll(z, half, 1) * sa + pltpu.roll(z, w - half, 1) * sb


def _dsa_prep_kernel(p_ref, c1_ref, sa1_ref, sb1_ref, c2_ref, sa2_ref, sb2_ref,
                     q_ref, kf_ref, vf_ref, kif_ref, kb_ref, vb_ref, kib_ref, qi_ref, w_ref):
    c1, sa1, sb1 = c1_ref[...], sa1_ref[...], sb1_ref[...]
    c2, sa2, sb2 = c2_ref[...], sa2_ref[...], sb2_ref[...]
    h1 = DSA_HD // ROPE_FRACTION // 2
    h2 = IDX_DIM // ROPE_FRACTION // 2
    o_k = DSA_Q
    o_v = DSA_Q + DSA_KV
    o_qi = DSA_Q + 2 * DSA_KV
    o_ki = o_qi + IDX_HEADS * IDX_DIM
    q = _rope(p_ref[0, :, :DSA_Q], c1, sa1, sb1, h1)
    q_ref[0] = q.astype(BF16)
    k = _rope(p_ref[0, :, o_k:o_v], c1, sa1, sb1, h1)
    kf_ref[0] = k
    kb_ref[0] = k.astype(BF16)
    v = p_ref[0, :, o_v:o_qi]
    vf_ref[0] = v
    vb_ref[0] = v.astype(BF16)
    qi = _rope(p_ref[0, :, o_qi:o_ki], c2, sa2, sb2, h2)
    qi_ref[0] = qi.astype(BF16)
    tail = p_ref[0, :, o_ki:o_ki + LANES]
    ki = _rope(tail, c2, sa2, sb2, h2)[:, :IDX_DIM]
    kif_ref[0] = ki
    kib_ref[0] = ki.astype(BF16)
    wi = tail[:, IDX_DIM:IDX_DIM + IDX_HEADS]
    w_ref[0] = (wi * (IDX_HEADS ** -0.5)) * (IDX_DIM ** -0.5)


def _dsa_prep(p, pos):
    b, t, n = p.shape
    tm = _row_tile(t, 512)
    tabs = _rope_tables(pos, DSA_HD) + _rope_tables(pos, IDX_DIM)
    tab = pl.BlockSpec((tm, LANES), lambda i, j: (j, 0))

    def row(w):
        return pl.BlockSpec((1, tm, w), lambda i, j: (i, j, 0))

    widths = [(DSA_Q, BF16), (DSA_KV, F32), (DSA_KV, F32), (IDX_DIM, F32), (DSA_KV, BF16), (DSA_KV, BF16),
              (IDX_DIM, BF16), (IDX_HEADS * IDX_DIM, BF16), (IDX_HEADS, F32)]
    return pl.pallas_call(
        _dsa_prep_kernel,
        grid=(b, t // tm),
        in_specs=[row(n)] + [tab] * 6,
        out_specs=[row(w) for w, _ in widths],
        out_shape=[jax.ShapeDtypeStruct((b, t, w), dt) for w, dt in widths],
        compiler_params=_params("arbitrary", "arbitrary"),
        name="dsa_rope_split",
    )(p, *tabs)


def _f2key(f):
    bits = lax.bitcast_convert_type(f, I32)
    return bits ^ ((bits >> 31) & 0x7FFFFFFF)


def _key2f(k):
    return lax.bitcast_convert_type(k ^ ((k >> 31) & 0x7FFFFFFF), F32)


def _dsa_kernel(nkt_ref, q_ref, qi_ref, w_ref, k_ref, v_ref, ki_ref, kpos_ref, qlim_ref, o_ref,
                s_ref, qg_ref, m_ref, l_ref, acc_ref, *, tq, tk, nsel, value_iters):
    nkt = nkt_ref[pl.program_id(1)]
    qlim = qlim_ref[...]
    w = w_ref[0]
    qi = qi_ref[0]
    qis = [qi[:, h * IDX_DIM:(h + 1) * IDX_DIM] for h in range(IDX_HEADS)]
    nlg = tk // LANES

    def p1(kt, carry):
        nadm, rmax, rmin = carry
        off = pl.multiple_of(kt * tk, tk)
        kit = ki_ref[0, pl.ds(off, tk), :]
        acc = jnp.zeros((tq, tk), F32)
        for h in range(IDX_HEADS):
            d = lax.dot_general(qis[h], kit, NT_DIMS, preferred_element_type=F32)
            acc = acc + jnp.maximum(d, 0.0) * w[:, h:h + 1]
        adm = kpos_ref[kt] < qlim
        s = jnp.where(adm, acc, -jnp.inf)
        s_ref[kt] = s
        nadm = nadm + jnp.sum(jnp.where(adm, 1.0, 0.0), axis=-1, keepdims=True)
        rmax = jnp.maximum(rmax, jnp.max(s, axis=-1, keepdims=True))
        rmin = jnp.minimum(rmin, jnp.min(jnp.where(adm, acc, jnp.inf), axis=-1, keepdims=True))
        return nadm, rmax, rmin

    init = (jnp.zeros((tq, 1), F32), jnp.full((tq, 1), -jnp.inf, F32), jnp.full((tq, 1), jnp.inf, F32))
    nadm, rmax, rmin = lax.fori_loop(0, nkt, p1, init)

    def count_ge(t):
        def body(kt, c):
            hit = jnp.where(s_ref[kt] >= t, 1.0, 0.0)
            for g in range(nlg):
                c = c + hit[:, g * LANES:(g + 1) * LANES]
            return c
        c = lax.fori_loop(0, nkt, body, jnp.zeros((tq, LANES), F32))
        return jnp.sum(c, axis=-1, keepdims=True)

    few = nadm <= float(nsel)
    lo0 = _f2key(jnp.where(few, 0.0, rmin))
    hi0 = _f2key(jnp.where(few, 0.0, rmax)) + 1
    act0 = jnp.where(few, 0.0, 1.0)
    t0 = jnp.full((tq, 1), -jnp.inf, F32)
    need0 = jnp.full((tq, 1), float(2 ** 30), F32)

    def cond(st):
        return st[0] > 0.0

    def step(st):
        _, it, lo, hi, chi, act, thr, need = st
        conv = jnp.logical_and(act > 0.0, lo + 1 >= hi)
        thr = jnp.where(conv, _key2f(lo), thr)
        need = jnp.where(conv, float(nsel) - chi, need)
        act = jnp.where(conv, 0.0, act)
        mid_i = (lo >> 1) + (hi >> 1) + (lo & hi & 1)
        mid_v = _f2key(0.5 * _key2f(lo) + 0.5 * _key2f(hi))
        use_v = jnp.logical_and(it < value_iters, jnp.logical_and(mid_v > lo, mid_v < hi))
        mid = jnp.where(use_v, mid_v, mid_i)
        tm = _key2f(mid)
        c = count_ge(jnp.where(act > 0.0, tm, jnp.inf))
        hit = jnp.logical_and(act > 0.0, c == float(nsel))
        thr = jnp.where(hit, tm, thr)
        up = jnp.logical_and(act > 0.0, c > float(nsel))
        dn = jnp.logical_and(act > 0.0, c < float(nsel))
        lo = jnp.where(up, mid, lo)
        hi = jnp.where(dn, mid, hi)
        chi = jnp.where(dn, c, chi)
        act = jnp.where(hit, 0.0, act)
        return jnp.max(act), it + 1, lo, hi, chi, act, thr, need

    st = (jnp.max(act0), jnp.int32(0), lo0, hi0, jnp.zeros((tq, 1), F32), act0, t0, need0)
    _, _, _, _, _, _, thr, need = lax.while_loop(cond, step, st)
    has_ties = jnp.max(jnp.where(need < float(2 ** 29), 1.0, 0.0)) > 0.0
    thr_eff = jnp.maximum(thr, -F32_MAX)

    for g in range(DSA_KV_HEADS):
        for hh in range(DSA_GROUP):
            hd = g * DSA_GROUP + hh
            qg_ref[g, hh * tq:(hh + 1) * tq, :] = q_ref[0, :, hd * DSA_HD:(hd + 1) * DSA_HD]
    m_ref[...] = jnp.full(m_ref.shape, NEG_BIG, F32)
    l_ref[...] = jnp.zeros(l_ref.shape, F32)
    acc_ref[...] = jnp.zeros(acc_ref.shape, F32)

    def attend(kt, bias):
        off = pl.multiple_of(kt * tk, tk)
        bias = jnp.concatenate([bias] * DSA_GROUP, axis=0)
        for g in range(DSA_KV_HEADS):
            kt_g = k_ref[0, pl.ds(off, tk), g * DSA_HD:(g + 1) * DSA_HD]
            vt_g = v_ref[0, pl.ds(off, tk), g * DSA_HD:(g + 1) * DSA_HD]
            lg = lax.dot_general(qg_ref[g], kt_g, NT_DIMS, preferred_element_type=F32) * (DSA_HD ** -0.5) + bias
            m_old = m_ref[g]
            m_new = jnp.maximum(m_old, jnp.max(lg, axis=-1, keepdims=True))
            alpha = jnp.exp(m_old - m_new)
            p = jnp.exp(lg - m_new)
            l_ref[g] = alpha * l_ref[g] + jnp.sum(p, axis=-1, keepdims=True)
            acc_ref[g] = alpha * acc_ref[g] + jnp.dot(p.astype(BF16), vt_g, preferred_element_type=F32)
            m_ref[g] = m_new

    @pl.when(jnp.logical_not(has_ties))
    def _():
        def body(kt, carry):
            attend(kt, jnp.where(s_ref[kt] >= thr_eff, 0.0, NEG_BIG))
            return carry
        lax.fori_loop(0, nkt, body, 0)

    @pl.when(has_ties)
    def _():
        si = lax.broadcasted_iota(I32, (tk, tk), 0)
        ti = lax.broadcasted_iota(I32, (tk, tk), 1)
        triu = jnp.where(si <= ti, 1.0, 0.0).astype(BF16)

        def body(kt, seen):
            s = s_ref[kt]
            eq = jnp.where(s == thr, 1.0, 0.0)
            rank = seen + jnp.dot(eq.astype(BF16), triu, preferred_element_type=F32)
            tie_bias = jnp.where(rank <= need, 0.0, NEG_BIG)
            attend(kt, jnp.where(eq > 0.0, tie_bias, jnp.where(s >= thr_eff, 0.0, NEG_BIG)))
            return seen + jnp.sum(eq, axis=-1, keepdims=True)
        lax.fori_loop(0, nkt, body, jnp.zeros((tq, 1), F32))

    for g in range(DSA_KV_HEADS):
        og = acc_ref[g] / l_ref[g]
        for hh in range(DSA_GROUP):
            hd = g * DSA_GROUP + hh
            o_ref[0, :, hd * DSA_HD:(hd + 1) * DSA_HD] = og[hh * tq:(hh + 1) * tq, :].astype(BF16)


def _dsa_attend(q, qi, w, k, v, ki, kpos, qlim, tq, tk):
    b, t, _ = q.shape
    lp = k.shape[1]
    nq = t // tq
    nsel = min(DSA_TOPK, int(np.sum(kpos < np.iinfo(np.int32).max)) // 4)
    qmax = qlim.reshape(nq, tq).max(axis=1)
    nkt = np.minimum(-(-np.searchsorted(kpos, qmax, side="left") // tk), lp // tk).astype(np.int32)
    nkt = np.maximum(nkt, 1)
    single = pl.Buffered(1)

    def full(wd):
        return pl.BlockSpec((1, lp, wd), lambda i, j, n: (i, 0, 0), pipeline_mode=single)

    def row(wd):
        return pl.BlockSpec((1, tq, wd), lambda i, j, n: (i, j, 0))

    grid_spec = pltpu.PrefetchScalarGridSpec(
        num_scalar_prefetch=1,
        grid=(b, nq),
        in_specs=[row(DSA_Q), row(IDX_HEADS * IDX_DIM), row(IDX_HEADS), full(DSA_KV), full(DSA_KV), full(IDX_DIM),
                  pl.BlockSpec((lp // tk, 1, tk), lambda i, j, n: (0, 0, 0)),
                  pl.BlockSpec((tq, 1), lambda i, j, n: (j, 0))],
        out_specs=row(DSA_Q),
        scratch_shapes=[
            pltpu.VMEM((lp // tk, tq, tk), F32),
            pltpu.VMEM((DSA_KV_HEADS, DSA_GROUP * tq, DSA_HD), BF16),
            pltpu.VMEM((DSA_KV_HEADS, DSA_GROUP * tq, 1), F32),
            pltpu.VMEM((DSA_KV_HEADS, DSA_GROUP * tq, 1), F32),
            pltpu.VMEM((DSA_KV_HEADS, DSA_GROUP * tq, DSA_HD), F32),
        ],
    )
    return pl.pallas_call(
        functools.partial(_dsa_kernel, tq=tq, tk=tk, nsel=nsel, value_iters=16),
        grid_spec=grid_spec,
        out_shape=jax.ShapeDtypeStruct((b, t, DSA_Q), BF16),
        compiler_params=_params("arbitrary", "arbitrary"),
        name="dsa_select_attend",
    )(jnp.asarray(nkt), q, qi, w, k, v, ki, jnp.asarray(kpos, I32).reshape(lp // tk, 1, tk),
      jnp.asarray(qlim, I32).reshape(t, 1))


def _trunk(x, mods, pos, ml_state, ds_cache, prm):
    b, t, d = x.shape
    chunk = CHUNK if t % CHUNK == 0 else t

    def mod(i, k):
        return mods[i][:, None, k * d:(k + 1) * d]

    _, p = _inproj(x, mod(0, 1), mod(0, 0), prm["ml_w_in"])
    if ml_state is None:
        c0 = jnp.zeros((b, ML_HEADS, ML_DV, ML_DK), F32)
        n0 = jnp.zeros((b, ML_HEADS, ML_DK), F32)
        m0 = jnp.zeros((b, ML_HEADS), F32)
    else:
        c0, n0, m0 = (s[0] for s in ml_state)
    hg, c_new, n_new, m_new = _mlstm(p, prm["ml_b_gates"], prm["ml_norm_g"], c0, n0, m0, chunk)
    x1, u2, te, tg = _outproj(hg, prm["ml_w_out"], x, mod(0, 2), mod(0, 4), mod(0, 3),
                              prm["ln_g"][0, 0:1], prm["ln_b"][0, 0:1], prm["w_router"][0], prm["b_router"][0])
    f = _moe(u2.reshape(b * t, d), te.reshape(b * t, TOP_K), tg.reshape(b * t, TOP_K),
             prm["moe_w1"][0], prm["moe_b1"][0], prm["moe_w2"][0], prm["moe_b2"][0]).reshape(b, t, d)

    x2, p = _inproj(x1, mod(1, 1), mod(1, 0), prm["ds_w_in"],
                    res=(f, mod(0, 5), prm["ln_g"][0, 1:2], prm["ln_b"][0, 1:2]))
    q, kf, vf, kif, kb, vb, kib, qi, w = _dsa_prep(p, pos)
    pos_np = np.asarray(pos)
    qlim = (pos_np // CHUNK + 1) * CHUNK
    if ds_cache is None:
        kpos = pos_np
        tq, tk = 128, 512
        if t % tk:
            tq, tk = t, t
        k_all, v_all, ki_all = kb, vb, kib
    else:
        ck, cv, cki = ds_cache
        past = ck.shape[1]
        total = past + t
        lp = -(-total // (3 * LANES)) * (3 * LANES)
        tq, tk = t, 3 * LANES
        kpos = np.concatenate([np.arange(past), pos_np, np.full((lp - total,), np.iinfo(np.int32).max)])

        def cat(cache, new):
            full = jnp.concatenate([cache.reshape(b, past, -1).astype(BF16), new], axis=1)
            return jnp.pad(full, ((0, 0), (0, lp - total), (0, 0)))

        k_all, v_all, ki_all = cat(ck, kb), cat(cv, vb), cat(cki, kib)
    o = _dsa_attend(q, qi, w, k_all, v_all, ki_all, kpos.astype(np.int64), qlim, tq, tk)
    x3, u2, te, tg = _outproj(o, prm["ds_w_out"], x2, mod(1, 2), mod(1, 4), mod(1, 3),
                              prm["ln_g"][1, 0:1], prm["ln_b"][1, 0:1], prm["w_router"][1], prm["b_router"][1])
    f = _moe(u2.reshape(b * t, d), te.reshape(b * t, TOP_K), tg.reshape(b * t, TOP_K),
             prm["moe_w1"][1], prm["moe_b1"][1], prm["moe_w2"][1], prm["moe_b2"][1]).reshape(b, t, d)
    y = _final_ln(x3, f, mod(1, 5), prm["ln_g"][1, 1:2], prm["ln_b"][1, 1:2])

    ml_out = (c_new[None], n_new[None], m_new[None])
    ds_out = (kf.reshape(1, b, t, DSA_KV_HEADS, DSA_HD), vf.reshape(1, b, t, DSA_KV_HEADS, DSA_HD), kif[None])
    return y, ml_out, ds_out


def _prepare(w_ada, b_ada, ln_g, ln_b, ml_w_in, ml_b_gates, ml_norm_g, ml_w_out, ds_w_in, ds_w_out,
             moe_w_router, moe_b_router, moe_w1, moe_b1, moe_w2, moe_b2):
    return {
        "ln_g": ln_g, "ln_b": ln_b,
        "ml_w_in": jnp.pad(ml_w_in[0], ((0, 0), (0, ML_IN_PAD - ML_IN))).astype(BF16),
        "ml_b_gates": ml_b_gates[0], "ml_norm_g": ml_norm_g[0],
        "ml_w_out": ml_w_out[0].astype(BF16),
        "ds_w_in": jnp.pad(ds_w_in[0], ((0, 0), (0, DSA_IN_PAD - DSA_IN))).astype(BF16),
        "ds_w_out": ds_w_out[0].astype(BF16),
        "w_router": jnp.pad(moe_w_router, ((0, 0), (0, 0), (0, LANES - N_EXPERTS))),
        "b_router": jnp.pad(moe_b_router, ((0, 0), (0, LANES - N_EXPERTS)), constant_values=NEG_BIG)[:, None, :],
        "moe_w1": moe_w1.astype(BF16), "moe_b1": moe_b1,
        "moe_w2": moe_w2.astype(BF16), "moe_b2": moe_b2,
    }


def kernel(x_prompt, x_sample, state_mlstm_C, state_mlstm_n, state_mlstm_m, cache_dsa_k, cache_dsa_v,
           cache_dsa_kidx, c_prompt, c_sample, w_ada, b_ada, ln_g, ln_b, ml_w_in, ml_b_gates, ml_norm_g,
           ml_w_out, ds_w_in, ds_w_out, moe_w_router, moe_b_router, moe_w1, moe_b1, moe_w2, moe_b2):
    prm = _prepare(w_ada, b_ada, ln_g, ln_b, ml_w_in, ml_b_gates, ml_norm_g, ml_w_out, ds_w_in, ds_w_out,
                   moe_w_router, moe_b_router, moe_w1, moe_b1, moe_w2, moe_b2)
    bp, bs = c_prompt.shape[0], c_sample.shape[0]
    rows = -(-(bp + bs) // 8) * 8
    c_rows = jnp.pad(jnp.concatenate([c_prompt, c_sample], axis=0), ((0, rows - bp - bs), (0, 0)))
    mods = _ada(c_rows, w_ada, b_ada)
    mods_p = [mods[i, :bp] for i in range(DEPTH)]
    mods_s = [mods[i, bp:bp + bs] for i in range(DEPTH)]

    pos_p = np.arange(x_prompt.shape[1], dtype=np.int32)
    y_p, ml_p, ds_p = _trunk(x_prompt, mods_p, pos_p, None, None, prm)
    past = cache_dsa_k.shape[2]
    pos_s = past + np.arange(x_sample.shape[1], dtype=np.int32)
    y_s, ml_s, ds_s = _trunk(x_sample, mods_s, pos_s, (state_mlstm_C, state_mlstm_n, state_mlstm_m),
                             (cache_dsa_k[0], cache_dsa_v[0], cache_dsa_kidx[0]), prm)
    return (y_p, y_s, ml_p[0], ml_p[1], ml_p[2], ds_p[0], ds_p[1], ds_p[2],
            ml_s[0], ml_s[1], ml_s[2], ds_s[0], ds_s[1], ds_s[2])
```

```python
import functools

import numpy as np
import jax
import jax.numpy as jnp
from jax import lax
from jax.experimental import pallas as pl
from jax.experimental.pallas import tpu as pltpu

F32 = jnp.float32
BF16 = jnp.bfloat16
I32 = jnp.int32

D_MODEL = 1024
DEPTH = 2
CHUNK = 64
ML_HEADS = 4
ML_DV = D_MODEL // ML_HEADS
ML_DK = ML_DV // 2
ML_QK = ML_HEADS * ML_DK
ML_V = ML_HEADS * ML_DV
ML_IN = 2 * ML_QK + 2 * ML_V + 2 * ML_HEADS
ML_GATE_OFF = 2 * ML_QK + 2 * ML_V
DSA_HEADS = 8
DSA_KV_HEADS = 2
DSA_HD = D_MODEL // DSA_HEADS
DSA_GROUP = DSA_HEADS // DSA_KV_HEADS
IDX_HEADS = 8
IDX_DIM = 64
DSA_TOPK = 256
DSA_Q = DSA_HEADS * DSA_HD
DSA_KV = DSA_KV_HEADS * DSA_HD
DSA_IN = DSA_Q + 2 * DSA_KV + IDX_HEADS * IDX_DIM + IDX_DIM + IDX_HEADS
ROPE_THETA = 500000.0
ROPE_FRACTION = 4
N_EXPERTS = 32
TOP_K = 4
D_FF = D_MODEL
SWIGLU_LIMIT = 7.0
SWIGLU_ALPHA = 1.702
DEEPNORM_ALPHA = (2 * DEPTH) ** 0.25
LN_EPS = 1e-5

LANES = 128
VMEM_LIMIT = 56 * 1024 * 1024

ML_IN_PAD = -(-ML_IN // LANES) * LANES
DSA_IN_PAD = -(-DSA_IN // LANES) * LANES
NEG_BIG = -1e30
F32_MAX = float(np.finfo(np.float32).max)
HIGHEST = lax.Precision.HIGHEST
NT_DIMS = (((1,), (1,)), ((), ()))
TN_DIMS = (((0,), (0,)), ((), ()))


def _params(*sem):
    return pltpu.CompilerParams(dimension_semantics=sem, vmem_limit_bytes=VMEM_LIMIT)


def _row_tile(t, pref):
    return pref if t % pref == 0 else t


def _ada_kernel(c_ref, w_ref, b_ref, o_ref):
    c = c_ref[...]
    cond = (c * jax.nn.sigmoid(c)).astype(BF16)
    o_ref[0] = jnp.dot(cond, w_ref[0].astype(BF16), preferred_element_type=F32) + b_ref[0]


def _ada(c_rows, w_ada, b_ada):
    rows = c_rows.shape[0]
    n = w_ada.shape[-1]
    tn = 1536
    return pl.pallas_call(
        _ada_kernel,
        grid=(DEPTH, n // tn),
        in_specs=[
            pl.BlockSpec((rows, D_MODEL), lambda i, j: (0, 0)),
            pl.BlockSpec((1, D_MODEL, tn), lambda i, j: (i, 0, j)),
            pl.BlockSpec((1, 1, tn), lambda i, j: (i, 0, j)),
        ],
        out_specs=pl.BlockSpec((1, rows, tn), lambda i, j: (i, 0, j)),
        out_shape=jax.ShapeDtypeStruct((DEPTH, rows, n), F32),
        compiler_params=_params("arbitrary", "arbitrary"),
        name="ada_mod",
    )(c_rows, w_ada, b_ada.reshape(DEPTH, 1, n))


def _layer_norm(z, g, b):
    mu = jnp.mean(z, axis=-1, keepdims=True)
    zc = z - mu
    var = jnp.mean(zc * zc, axis=-1, keepdims=True)
    return zc * lax.rsqrt(var + LN_EPS) * g + b


def _inproj_kernel(*refs, has_res):
    if has_res:
        x_ref, f_ref, g_ref, lng_ref, lnb_ref, sc_ref, sh_ref, w_ref, xo_ref, p_ref = refs
        z = DEEPNORM_ALPHA * x_ref[0] + (1.0 + g_ref[0]) * f_ref[0]
        x = _layer_norm(z, lng_ref[...], lnb_ref[...])
        xo_ref[0] = x
    else:
        x_ref, sc_ref, sh_ref, w_ref, p_ref = refs
        x = x_ref[0]
    u = x * (1.0 + sc_ref[0]) + sh_ref[0]
    p_ref[0] = jnp.dot(u.astype(BF16), w_ref[...], preferred_element_type=F32)


def _inproj(x, sc, sh, w_bf, res=None):
    b, t, d = x.shape
    n = w_bf.shape[1]
    tm = _row_tile(t, 512)
    row = pl.BlockSpec((1, tm, d), lambda i, j: (i, j, 0))
    vec = pl.BlockSpec((1, 1, d), lambda i, j: (i, 0, 0))
    par = pl.BlockSpec((1, d), lambda i, j: (0, 0))
    wspec = pl.BlockSpec((d, n), lambda i, j: (0, 0))
    pspec = pl.BlockSpec((1, tm, n), lambda i, j: (i, j, 0))
    pshape = jax.ShapeDtypeStruct((b, t, n), F32)
    if res is None:
        return None, pl.pallas_call(
            functools.partial(_inproj_kernel, has_res=False),
            grid=(b, t // tm),
            in_specs=[row, vec, vec, wspec],
            out_specs=pspec,
            out_shape=pshape,
            compiler_params=_params("arbitrary", "arbitrary"),
            name="inproj",
        )(x, sc, sh, w_bf)
    f, g, lng, lnb = res
    return pl.pallas_call(
        functools.partial(_inproj_kernel, has_res=True),
        grid=(b, t // tm),
        in_specs=[row, row, vec, par, par, vec, vec, wspec],
        out_specs=[row, pspec],
        out_shape=[jax.ShapeDtypeStruct((b, t, d), F32), pshape],
        compiler_params=_params("arbitrary", "arbitrary"),
        name="ln_inproj",
    )(x, f, g, lng, lnb, sc, sh, w_bf)


def _final_ln_kernel(x_ref, f_ref, g_ref, lng_ref, lnb_ref, o_ref):
    z = DEEPNORM_ALPHA * x_ref[0] + (1.0 + g_ref[0]) * f_ref[0]
    o_ref[0] = _layer_norm(z, lng_ref[...], lnb_ref[...])


def _final_ln(x, f, g, lng, lnb):
    b, t, d = x.shape
    tm = _row_tile(t, 1024)
    row = pl.BlockSpec((1, tm, d), lambda i, j: (i, j, 0))
    vec = pl.BlockSpec((1, 1, d), lambda i, j: (i, 0, 0))
    par = pl.BlockSpec((1, d), lambda i, j: (0, 0))
    return pl.pallas_call(
        _final_ln_kernel,
        grid=(b, t // tm),
        in_specs=[row, row, vec, par, par],
        out_specs=row,
        out_shape=jax.ShapeDtypeStruct((b, t, d), F32),
        compiler_params=_params("arbitrary", "arbitrary"),
        name="final_ln",
    )(x, f, g, lng, lnb)


def _log_sigmoid(x):
    return jnp.minimum(x, 0.0) - jnp.log1p(jnp.exp(-jnp.abs(x)))


def _mlstm_kernel(p_ref, gt_ref, bcol_ref, brow_ref, ng_ref, c0_ref, n0_ref, m0_ref,
                  hg_ref, c_ref, n_ref, m_ref, *, chunk):
    L = chunk

    @pl.when(pl.program_id(1) == 0)
    def _():
        c_ref[...] = c0_ref[...]
        n_ref[...] = n0_ref[...]
        m_ref[...] = m0_ref[...]

    gcol = p_ref[0, :, ML_GATE_OFF:ML_GATE_OFF + 2 * ML_HEADS] + bcol_ref[...]
    grow = gt_ref[0] + brow_ref[...]
    lf_col = _log_sigmoid(gcol)
    lf_row = _log_sigmoid(grow)
    ti = lax.broadcasted_iota(I32, (L, L), 0)
    si = lax.broadcasted_iota(I32, (L, L), 1)
    causal = si <= ti
    tril = jnp.where(causal, 1.0, 0.0).astype(F32)
    triu = jnp.where(ti <= si, 1.0, 0.0).astype(F32)
    b_col = jnp.dot(tril, lf_col, precision=HIGHEST, preferred_element_type=F32)
    b_row = jnp.dot(lf_row, triu, precision=HIGHEST, preferred_element_type=F32)

    for h in range(ML_HEADS):
        q = p_ref[0, :, h * ML_DK:(h + 1) * ML_DK]
        k = p_ref[0, :, ML_QK + h * ML_DK:ML_QK + (h + 1) * ML_DK]
        v = p_ref[0, :, 2 * ML_QK + h * ML_DV:2 * ML_QK + (h + 1) * ML_DV]
        o = p_ref[0, :, 2 * ML_QK + ML_V + h * ML_DV:2 * ML_QK + ML_V + (h + 1) * ML_DV]
        qs = (q * (ML_DK ** -0.5)).astype(BF16)
        kb = k.astype(BF16)
        vb = v.astype(BF16)
        bc = b_col[:, ML_HEADS + h:ML_HEADS + h + 1]
        ic = gcol[:, h:h + 1]
        br = b_row[ML_HEADS + h:ML_HEADS + h + 1, :]
        ir = grow[h:h + 1, :]
        c_old = c_ref[0, h]
        n_old = n_ref[0, h:h + 1, :]
        m_old = m_ref[0, h:h + 1, 0:1]

        dm = jnp.where(causal, bc - br + ir, -jnp.inf)
        inter = bc + m_old
        mt = jnp.maximum(inter, jnp.max(dm, axis=-1, keepdims=True))
        qk = lax.dot_general(qs, kb, NT_DIMS, preferred_element_type=F32)
        s = jnp.exp(dm - mt) * qk
        wp = jnp.exp(inter - mt)
        qc = lax.dot_general(qs, c_old.astype(BF16), NT_DIMS, preferred_element_type=F32)
        num = jnp.dot(s.astype(BF16), vb, preferred_element_type=F32) + wp * qc
        qn = jnp.sum(qs.astype(F32) * n_old, axis=-1, keepdims=True)
        den = jnp.sum(s, axis=-1, keepdims=True) + wp * qn
        hh = num / jnp.maximum(jnp.abs(den), jnp.exp(-mt))

        m_new = mt[L - 1:L, :]
        wk = jnp.exp(bc[L - 1:L, :] - bc + ic - m_new)
        wprev = jnp.exp(inter[L - 1:L, :] - m_new)
        vw = (v * wk).astype(BF16)
        c_ref[0, h] = wprev * c_old + lax.dot_general(vw, kb, TN_DIMS, preferred_element_type=F32)
        n_ref[0, h:h + 1, :] = wprev * n_old + jnp.sum(wk * kb.astype(F32), axis=0, keepdims=True)
        m_ref[0, h:h + 1, :] = jnp.broadcast_to(m_new, (1, ML_DK))

        mu = jnp.mean(hh, axis=-1, keepdims=True)
        hc = hh - mu
        var = jnp.mean(hc * hc, axis=-1, keepdims=True)
        hn = hc * lax.rsqrt(var + LN_EPS) * ng_ref[:, h * ML_DV:(h + 1) * ML_DV]
        hg_ref[0, :, h * ML_DV:(h + 1) * ML_DV] = (jax.nn.sigmoid(o) * hn).astype(BF16)


def _mlstm(p, b_gates, norm_g, c0, n0, m0, chunk):
    b, t, n = p.shape
    nc = t // chunk
    gt = jnp.swapaxes(p[:, :, ML_GATE_OFF:ML_GATE_OFF + 2 * ML_HEADS], 1, 2)
    if nc > 1:
        gt_spec = pl.BlockSpec((1, 2 * ML_HEADS, chunk), lambda i, j: (i, 0, j))
    else:
        gt_spec = pl.BlockSpec((1, 2 * ML_HEADS, t), lambda i, j: (i, 0, 0))
    if nc > 1 and chunk % LANES != 0:
        gt = gt.reshape(b, 2 * ML_HEADS, nc, chunk).transpose(0, 2, 1, 3).reshape(b * nc, 2 * ML_HEADS, chunk)
        gt_spec = pl.BlockSpec((1, 2 * ML_HEADS, chunk), lambda i, j: (i * nc + j, 0, 0))
    m0b = jnp.broadcast_to(m0[..., None], (b, ML_HEADS, ML_DK))
    cspec = pl.BlockSpec((1, ML_HEADS, ML_DV, ML_DK), lambda i, j: (i, 0, 0, 0))
    nspec = pl.BlockSpec((1, ML_HEADS, ML_DK), lambda i, j: (i, 0, 0))
    hg, c, nn, m = pl.pallas_call(
        functools.partial(_mlstm_kernel, chunk=chunk),
        grid=(b, nc),
        in_specs=[
            pl.BlockSpec((1, chunk, n), lambda i, j: (i, j, 0)),
            gt_spec,
            pl.BlockSpec((1, 2 * ML_HEADS), lambda i, j: (0, 0)),
            pl.BlockSpec((2 * ML_HEADS, 1), lambda i, j: (0, 0)),
            pl.BlockSpec((1, ML_V), lambda i, j: (0, 0)),
            cspec, nspec, nspec,
        ],
        out_specs=[pl.BlockSpec((1, chunk, ML_V), lambda i, j: (i, j, 0)), cspec, nspec, nspec],
        out_shape=[
            jax.ShapeDtypeStruct((b, t, ML_V), BF16),
            jax.ShapeDtypeStruct((b, ML_HEADS, ML_DV, ML_DK), F32),
            jax.ShapeDtypeStruct((b, ML_HEADS, ML_DK), F32),
            jax.ShapeDtypeStruct((b, ML_HEADS, ML_DK), F32),
        ],
        compiler_params=_params("arbitrary", "arbitrary"),
        name="mlstm_scan",
    )(p, gt, b_gates.reshape(1, -1), b_gates.reshape(-1, 1), norm_g.reshape(1, -1), c0, n0, m0b)
    return hg, c, nn, m[..., 0]


def _outproj_kernel(a_ref, w_ref, x_ref, g_ref, sc_ref, sh_ref, lng_ref, lnb_ref, wr_ref, br_ref,
                    x1_ref, u2_ref, te_ref, tg_ref):
    y = jnp.dot(a_ref[0], w_ref[...], preferred_element_type=F32)
    z = DEEPNORM_ALPHA * x_ref[0] + (1.0 + g_ref[0]) * y
    x1 = _layer_norm(z, lng_ref[...], lnb_ref[...])
    x1_ref[0] = x1
    u2 = x1 * (1.0 + sc_ref[0]) + sh_ref[0]
    u2_ref[0] = u2.astype(BF16)
    logits = jnp.dot(u2, wr_ref[...], precision=HIGHEST, preferred_element_type=F32) + br_ref[...]
    lane = lax.broadcasted_iota(I32, logits.shape, 1).astype(F32)
    vals, idxs = [], []
    cur = logits
    for _ in range(TOP_K):
        mx = jnp.max(cur, axis=-1, keepdims=True)
        idx = jnp.min(jnp.where(cur == mx, lane, float(LANES)), axis=-1, keepdims=True)
        vals.append(mx)
        idxs.append(idx)
        cur = jnp.where(lane == idx, -jnp.inf, cur)
    es = [jnp.exp(v - vals[0]) for v in vals]
    tot = es[0] + es[1] + es[2] + es[3]
    for k in range(TOP_K):
        te_ref[0, :, k:k + 1] = idxs[k].astype(I32)
        tg_ref[0, :, k:k + 1] = es[k] / tot


def _outproj(a, w_bf, x, g, sc, sh, lng, lnb, wr_pad, br_pad):
    b, t, d = x.shape
    tm = _row_tile(t, 512)
    row = pl.BlockSpec((1, tm, d), lambda i, j: (i, j, 0))
    vec = pl.BlockSpec((1, 1, d), lambda i, j: (i, 0, 0))
    par = pl.BlockSpec((1, d), lambda i, j: (0, 0))
    top = pl.BlockSpec((1, tm, TOP_K), lambda i, j: (i, j, 0))
    return pl.pallas_call(
        _outproj_kernel,
        grid=(b, t // tm),
        in_specs=[row, pl.BlockSpec((d, d), lambda i, j: (0, 0)), row, vec, vec, vec, par, par,
                  pl.BlockSpec((d, LANES), lambda i, j: (0, 0)), pl.BlockSpec((1, LANES), lambda i, j: (0, 0))],
        out_specs=[row, row, top, top],
        out_shape=[jax.ShapeDtypeStruct((b, t, d), F32), jax.ShapeDtypeStruct((b, t, d), BF16),
                   jax.ShapeDtypeStruct((b, t, TOP_K), I32), jax.ShapeDtypeStruct((b, t, TOP_K), F32)],
        compiler_params=_params("arbitrary", "arbitrary"),
        name="outproj_ln_router",
    )(a, w_bf, x, g, sc, sh, lng, lnb, wr_pad, br_pad)


def _moe_kernel(x_ref, te_ref, tg_ref, w1_ref, b1_ref, w2_ref, b2_ref, o_ref, pos_ref, tri_ref, *, tm, rb):
    e = pl.program_id(1)

    @pl.when(jnp.logical_and(pl.program_id(0) == 0, e == 0))
    def _():
        si = lax.broadcasted_iota(I32, (tm, tm), 0)
        ti = lax.broadcasted_iota(I32, (tm, tm), 1)
        tri_ref[...] = jnp.where(si <= ti, 1.0, 0.0).astype(BF16)

    @pl.when(e == 0)
    def _():
        o_ref[...] = jnp.zeros_like(o_ref)
        eio = lax.broadcasted_iota(I32, (N_EXPERTS, tm), 0)
        sel = jnp.zeros((N_EXPERTS, tm), F32)
        for k in range(TOP_K):
            sel = sel + jnp.where(te_ref[k:k + 1, :] == eio, 1.0, 0.0)
        rank = jnp.dot(sel.astype(BF16), tri_ref[...], preferred_element_type=F32)
        pos_ref[...] = rank * sel

    pm = pos_ref[pl.ds(e, 1), :]
    gate = jnp.zeros((1, tm), F32)
    for k in range(TOP_K):
        gate = gate + jnp.where(te_ref[k:k + 1, :] == e, tg_ref[k:k + 1, :], 0.0)
    cnt = jnp.max(pm).astype(I32)
    nblk = (cnt + (rb - 1)) // rb

    def body(blk, carry):
        r = (blk * rb + 1 + lax.broadcasted_iota(I32, (rb, 1), 0)).astype(F32)
        hit = pm == r
        onehot = jnp.where(hit, 1.0, 0.0).astype(BF16)
        xg = jnp.dot(onehot, x_ref[...], preferred_element_type=F32).astype(BF16)
        h = jnp.dot(xg, w1_ref[0], preferred_element_type=F32) + b1_ref[0]
        hg = jnp.minimum(h[:, :D_FF], SWIGLU_LIMIT)
        hl = jnp.clip(h[:, D_FF:], -SWIGLU_LIMIT, SWIGLU_LIMIT)
        a = hg * jax.nn.sigmoid(SWIGLU_ALPHA * hg) * (hl + 1.0)
        y = jnp.dot(a.astype(BF16), w2_ref[0], preferred_element_type=F32) + b2_ref[0]
        grow = jnp.sum(jnp.where(hit, gate, 0.0), axis=-1, keepdims=True)
        yg = (y * grow).astype(BF16)
        o_ref[...] += lax.dot_general(onehot, yg, TN_DIMS, preferred_element_type=F32)
        return carry

    lax.fori_loop(0, nblk, body, 0)


def _moe(u2, te, tg, w1_bf, b1, w2_bf, b2):
    n, d = u2.shape
    tm = _row_tile(n, 1024)
    rb = min(tm, tm * TOP_K // N_EXPERTS + 32)
    te_t = te.T
    tg_t = tg.T
    return pl.pallas_call(
        functools.partial(_moe_kernel, tm=tm, rb=rb),
        grid=(n // tm, N_EXPERTS),
        in_specs=[
            pl.BlockSpec((tm, d), lambda i, e: (i, 0)),
            pl.BlockSpec((TOP_K, tm), lambda i, e: (0, i)),
            pl.BlockSpec((TOP_K, tm), lambda i, e: (0, i)),
            pl.BlockSpec((1, d, 2 * D_FF), lambda i, e: (e, 0, 0)),
            pl.BlockSpec((1, 1, 2 * D_FF), lambda i, e: (e, 0, 0)),
            pl.BlockSpec((1, D_FF, d), lambda i, e: (e, 0, 0)),
            pl.BlockSpec((1, 1, d), lambda i, e: (e, 0, 0)),
        ],
        out_specs=pl.BlockSpec((tm, d), lambda i, e: (i, 0)),
        out_shape=jax.ShapeDtypeStruct((n, d), F32),
        scratch_shapes=[pltpu.VMEM((N_EXPERTS, tm), F32), pltpu.VMEM((tm, tm), BF16)],
        compiler_params=_params("arbitrary", "arbitrary"),
        name="moe_experts",
    )(u2, te_t, tg_t, w1_bf, b1.reshape(N_EXPERTS, 1, -1), w2_bf, b2.reshape(N_EXPERTS, 1, -1))


def _rope_tables(pos, head_dim):
    rd = head_dim // ROPE_FRACTION
    half = rd // 2
    inv = jnp.power(ROPE_THETA, -jnp.arange(half, dtype=F32) / half)
    ang = jnp.asarray(pos).astype(F32)[:, None] * inv[None, :]
    cos, sin = jnp.cos(ang), jnp.sin(ang)
    t = pos.shape[0]
    ones = jnp.ones((t, head_dim - rd), F32)
    zeros = jnp.zeros((t, head_dim - rd), F32)
    zh = jnp.zeros((t, half), F32)
    c = jnp.concatenate([cos, cos, ones], axis=-1)
    sa = jnp.concatenate([zh, sin, zeros], axis=-1)
    sb = jnp.concatenate([-sin, zh, zeros], axis=-1)
    rep = LANES // head_dim
    return tuple(jnp.tile(a, (1, rep)) for a in (c, sa, sb))


def _rope(z, c, sa, sb, half):
    w = z.shape[-1]
    rep = w // LANES
    if rep > 1:
        c, sa, sb = (jnp.concatenate([a] * rep, axis=-1) for a in (c, sa, sb))
    return z * c + pltpu.roll(z, half, 1) * sa + pltpu.roll(z, w - half, 1) * sb


def _dsa_prep_kernel(p_ref, c1_ref, sa1_ref, sb1_ref, c2_ref, sa2_ref, sb2_ref,
                     q_ref, kf_ref, vf_ref, kif_ref, kb_ref, vb_ref, kib_ref, qi_ref, w_ref):
    c1, sa1, sb1 = c1_ref[...], sa1_ref[...], sb1_ref[...]
    c2, sa2, sb2 = c2_ref[...], sa2_ref[...], sb2_ref[...]
    h1 = DSA_HD // ROPE_FRACTION // 2
    h2 = IDX_DIM // ROPE_FRACTION // 2
    o_k = DSA_Q
    o_v = DSA_Q + DSA_KV
    o_qi = DSA_Q + 2 * DSA_KV
    o_ki = o_qi + IDX_HEADS * IDX_DIM
    q = _rope(p_ref[0, :, :DSA_Q], c1, sa1, sb1, h1)
    q_ref[0] = q.astype(BF16)
    k = _rope(p_ref[0, :, o_k:o_v], c1, sa1, sb1, h1)
    kf_ref[0] = k
    kb_ref[0] = k.astype(BF16)
    v = p_ref[0, :, o_v:o_qi]
    vf_ref[0] = v
    vb_ref[0] = v.astype(BF16)
    qi = _rope(p_ref[0, :, o_qi:o_ki], c2, sa2, sb2, h2)
    qi_ref[0] = qi.astype(BF16)
    tail = p_ref[0, :, o_ki:o_ki + LANES]
    ki = _rope(tail, c2, sa2, sb2, h2)[:, :IDX_DIM]
    kif_ref[0] = ki
    kib_ref[0] = ki.astype(BF16)
    wi = tail[:, IDX_DIM:IDX_DIM + IDX_HEADS]
    w_ref[0] = (wi * (IDX_HEADS ** -0.5)) * (IDX_DIM ** -0.5)


def _dsa_prep(p, pos):
    b, t, n = p.shape
    tm = _row_tile(t, 512)
    tabs = _rope_tables(pos, DSA_HD) + _rope_tables(pos, IDX_DIM)
    tab = pl.BlockSpec((tm, LANES), lambda i, j: (j, 0))

    def row(w):
        return pl.BlockSpec((1, tm, w), lambda i, j: (i, j, 0))

    widths = [(DSA_Q, BF16), (DSA_KV, F32), (DSA_KV, F32), (IDX_DIM, F32), (DSA_KV, BF16), (DSA_KV, BF16),
              (IDX_DIM, BF16), (IDX_HEADS * IDX_DIM, BF16), (IDX_HEADS, F32)]
    return pl.pallas_call(
        _dsa_prep_kernel,
        grid=(b, t // tm),
        in_specs=[row(n)] + [tab] * 6,
        out_specs=[row(w) for w, _ in widths],
        out_shape=[jax.ShapeDtypeStruct((b, t, w), dt) for w, dt in widths],
        compiler_params=_params("arbitrary", "arbitrary"),
        name="dsa_rope_split",
    )(p, *tabs)


def _f2key(f):
    bits = lax.bitcast_convert_type(f, I32)
    return bits ^ ((bits >> 31) & 0x7FFFFFFF)


def _key2f(k):
    return lax.bitcast_convert_type(k ^ ((k >> 31) & 0x7FFFFFFF), F32)


def _dsa_kernel(nkt_ref, q_ref, qi_ref, w_ref, k_ref, v_ref, ki_ref, kpos_ref, qlim_ref, o_ref,
                s_ref, wb_ref, qg_ref, m_ref, l_ref, acc_ref, *, tq, tk, nsel, value_iters):
    nkt = nkt_ref[pl.program_id(1)]
    qlim = qlim_ref[...]
    w = w_ref[0]
    qi = qi_ref[0]
    qis = [qi[:, h * IDX_DIM:(h + 1) * IDX_DIM] for h in range(IDX_HEADS)]
    nlg = tk // LANES
    for h in range(IDX_HEADS):
        wb_ref[h] = jnp.broadcast_to(w[:, h:h + 1], (tq, LANES))

    def p1(kt, carry):
        nadm, rmax, rmin = carry
        off = pl.multiple_of(kt * tk, tk)
        kit = ki_ref[0, pl.ds(off, tk), :]
        acc = jnp.zeros((tq, tk), F32)
        for h in range(IDX_HEADS):
            d = lax.dot_general(qis[h], kit, NT_DIMS, preferred_element_type=F32)
            acc = acc + jnp.maximum(d, 0.0) * jnp.concatenate([wb_ref[h]] * nlg, axis=1)
        adm = kpos_ref[kt] < qlim
        s = jnp.where(adm, acc, -jnp.inf)
        s_ref[kt] = s
        nadm = nadm + jnp.sum(jnp.where(adm, 1.0, 0.0), axis=-1, keepdims=True)
        rmax = jnp.maximum(rmax, jnp.max(s, axis=-1, keepdims=True))
        rmin = jnp.minimum(rmin, jnp.min(jnp.where(adm, acc, jnp.inf), axis=-1, keepdims=True))
        return nadm, rmax, rmin

    init = (jnp.zeros((tq, 1), F32), jnp.full((tq, 1), -jnp.inf, F32), jnp.full((tq, 1), jnp.inf, F32))
    nadm, rmax, rmin = lax.fori_loop(0, nkt, p1, init)

    rg = min(tq, 32)

    def count_ge(t):
        tb = jnp.broadcast_to(t, (tq, LANES))

        def body(kt, c):
            out = []
            for r in range(tq // rg):
                tr = jnp.concatenate([tb[r * rg:(r + 1) * rg]] * nlg, axis=1)
                hit = jnp.where(s_ref[kt, r * rg:(r + 1) * rg, :] >= tr, 1.0, 0.0)
                cr = c[r * rg:(r + 1) * rg]
                for g in range(nlg):
                    cr = cr + hit[:, g * LANES:(g + 1) * LANES]
                out.append(cr)
            return jnp.concatenate(out, axis=0)
        c = lax.fori_loop(0, nkt, body, jnp.zeros((tq, LANES), F32))
        return jnp.sum(c, axis=-1, keepdims=True)

    few = nadm <= float(nsel)
    lo0 = _f2key(jnp.where(few, 0.0, rmin))
    hi0 = _f2key(jnp.where(few, 0.0, rmax)) + 1
    act0 = jnp.where(few, 0.0, 1.0)
    t0 = jnp.full((tq, 1), -jnp.inf, F32)
    need0 = jnp.full((tq, 1), float(2 ** 30), F32)

    def cond(st):
        return st[0] > 0.0

    def step(st):
        _, it, lo, hi, clo, chi, act, thr, need = st
        conv = jnp.logical_and(act > 0.0, lo + 1 >= hi)
        thr = jnp.where(conv, _key2f(lo), thr)
        need = jnp.where(conv, float(nsel) - chi, need)
        act = jnp.where(conv, 0.0, act)
        flo, fhi = _key2f(lo), _key2f(hi)
        mid_i = (lo >> 1) + (hi >> 1) + (lo & hi & 1)
        frac = jnp.where(it % 2 == 0, 0.5, (clo - (float(nsel) - 0.5)) / (clo - chi))
        mid_v = _f2key(flo + frac * (fhi - flo))
        use_v = jnp.logical_and(it < value_iters, jnp.logical_and(mid_v > lo, mid_v < hi))
        mid = jnp.where(use_v, mid_v, mid_i)
        tm = _key2f(mid)
        c = count_ge(jnp.where(act > 0.0, tm, jnp.inf))
        hit = jnp.logical_and(act > 0.0, c == float(nsel))
        thr = jnp.where(hit, tm, thr)
        up = jnp.logical_and(act > 0.0, c > float(nsel))
        dn = jnp.logical_and(act > 0.0, c < float(nsel))
        lo = jnp.where(up, mid, lo)
        clo = jnp.where(up, c, clo)
        hi = jnp.where(dn, mid, hi)
        chi = jnp.where(dn, c, chi)
        act = jnp.where(hit, 0.0, act)
        return jnp.max(act), it + 1, lo, hi, clo, chi, act, thr, need

    st = (jnp.max(act0), jnp.int32(0), lo0, hi0, nadm, jnp.zeros((tq, 1), F32), act0, t0, need0)
    _, _, _, _, _, _, _, thr, need = lax.while_loop(cond, step, st)
    has_ties = jnp.max(jnp.where(need < float(2 ** 29), 1.0, 0.0)) > 0.0
    thr_eff = jnp.maximum(thr, -F32_MAX)

    for g in range(DSA_KV_HEADS):
        for hh in range(DSA_GROUP):
            hd = g * DSA_GROUP + hh
            qg_ref[g, hh * tq:(hh + 1) * tq, :] = q_ref[0, :, hd * DSA_HD:(hd + 1) * DSA_HD]
    m_ref[...] = jnp.full(m_ref.shape, NEG_BIG, F32)
    l_ref[...] = jnp.zeros(l_ref.shape, F32)
    acc_ref[...] = jnp.zeros(acc_ref.shape, F32)

    log2_scale = (DSA_HD ** -0.5) * float(np.log2(np.e))
    ones_cols = jnp.ones((tk, DSA_HD), BF16)

    def attend(kt, bias):
        off = pl.multiple_of(kt * tk, tk)
        bias = jnp.concatenate([bias] * DSA_GROUP, axis=0)
        for g in range(DSA_KV_HEADS):
            kt_g = k_ref[0, pl.ds(off, tk), g * DSA_HD:(g + 1) * DSA_HD]
            vt_g = v_ref[0, pl.ds(off, tk), g * DSA_HD:(g + 1) * DSA_HD]
            lg = lax.dot_general(qg_ref[g], kt_g, NT_DIMS, preferred_element_type=F32) * log2_scale + bias
            m_old = m_ref[g]
            m_new = jnp.maximum(m_old, jnp.max(lg, axis=-1, keepdims=True))
            alpha = jnp.exp2(m_old - m_new)
            p = jnp.exp2(lg - jnp.concatenate([m_new] * nlg, axis=1))
            pv = jnp.dot(p.astype(BF16), jnp.concatenate([vt_g, ones_cols], axis=1),
                         preferred_element_type=F32)
            acc_ref[g] = alpha * acc_ref[g] + pv[:, :DSA_HD]
            l_ref[g] = alpha * l_ref[g] + pv[:, DSA_HD:]
            m_ref[g] = m_new

    @pl.when(jnp.logical_not(has_ties))
    def _():
        def body(kt, carry):
            attend(kt, jnp.where(s_ref[kt] >= thr_eff, 0.0, NEG_BIG))
            return carry
        lax.fori_loop(0, nkt, body, 0)

    @pl.when(has_ties)
    def _():
        si = lax.broadcasted_iota(I32, (tk, tk), 0)
        ti = lax.broadcasted_iota(I32, (tk, tk), 1)
        triu = jnp.where(si <= ti, 1.0, 0.0).astype(BF16)

        def body(kt, seen):
            s = s_ref[kt]
            eq = jnp.where(s == thr, 1.0, 0.0)
            rank = seen + jnp.dot(eq.astype(BF16), triu, preferred_element_type=F32)
            tie_bias = jnp.where(rank <= need, 0.0, NEG_BIG)
            attend(kt, jnp.where(eq > 0.0, tie_bias, jnp.where(s >= thr_eff, 0.0, NEG_BIG)))
            return seen + jnp.sum(eq, axis=-1, keepdims=True)
        lax.fori_loop(0, nkt, body, jnp.zeros((tq, 1), F32))

    for g in range(DSA_KV_HEADS):
        og = acc_ref[g] / l_ref[g]
        for hh in range(DSA_GROUP):
            hd = g * DSA_GROUP + hh
            o_ref[0, :, hd * DSA_HD:(hd + 1) * DSA_HD] = og[hh * tq:(hh + 1) * tq, :].astype(BF16)


def _dsa_attend(q, qi, w, k, v, ki, kpos, qlim, tq, tk):
    b, t, _ = q.shape
    lp = k.shape[1]
    nq = t // tq
    nsel = min(DSA_TOPK, int(np.sum(kpos < np.iinfo(np.int32).max)) // 4)
    qmax = qlim.reshape(nq, tq).max(axis=1)
    nkt = np.minimum(-(-np.searchsorted(kpos, qmax, side="left") // tk), lp // tk).astype(np.int32)
    nkt = np.maximum(nkt, 1)
    single = pl.Buffered(1)

    def full(wd):
        return pl.BlockSpec((1, lp, wd), lambda i, j, n: (i, 0, 0), pipeline_mode=single)

    def row(wd):
        return pl.BlockSpec((1, tq, wd), lambda i, j, n: (i, j, 0))

    grid_spec = pltpu.PrefetchScalarGridSpec(
        num_scalar_prefetch=1,
        grid=(b, nq),
        in_specs=[row(DSA_Q), row(IDX_HEADS * IDX_DIM), row(IDX_HEADS), full(DSA_KV), full(DSA_KV), full(IDX_DIM),
                  pl.BlockSpec((lp // tk, 1, tk), lambda i, j, n: (0, 0, 0)),
                  pl.BlockSpec((tq, 1), lambda i, j, n: (j, 0))],
        out_specs=row(DSA_Q),
        scratch_shapes=[
            pltpu.VMEM((lp // tk, tq, tk), F32),
            pltpu.VMEM((IDX_HEADS, tq, LANES), F32),
            pltpu.VMEM((DSA_KV_HEADS, DSA_GROUP * tq, DSA_HD), BF16),
            pltpu.VMEM((DSA_KV_HEADS, DSA_GROUP * tq, LANES), F32),
            pltpu.VMEM((DSA_KV_HEADS, DSA_GROUP * tq, LANES), F32),
            pltpu.VMEM((DSA_KV_HEADS, DSA_GROUP * tq, DSA_HD), F32),
        ],
    )
    return pl.pallas_call(
        functools.partial(_dsa_kernel, tq=tq, tk=tk, nsel=nsel, value_iters=16),
        grid_spec=grid_spec,
        out_shape=jax.ShapeDtypeStruct((b, t, DSA_Q), BF16),
        compiler_params=_params("arbitrary", "arbitrary"),
        name="dsa_select_attend",
    )(jnp.asarray(nkt), q, qi, w, k, v, ki, jnp.asarray(kpos, I32).reshape(lp // tk, 1, tk),
      jnp.asarray(qlim, I32).reshape(t, 1))


def _trunk(x, mods, pos, ml_state, ds_cache, prm):
    b, t, d = x.shape
    chunk = CHUNK if t % CHUNK == 0 else t

    def mod(i, k):
        return mods[i][:, None, k * d:(k + 1) * d]

    _, p = _inproj(x, mod(0, 1), mod(0, 0), prm["ml_w_in"])
    if ml_state is None:
        c0 = jnp.zeros((b, ML_HEADS, ML_DV, ML_DK), F32)
        n0 = jnp.zeros((b, ML_HEADS, ML_DK), F32)
        m0 = jnp.zeros((b, ML_HEADS), F32)
    else:
        c0, n0, m0 = (s[0] for s in ml_state)
    hg, c_new, n_new, m_new = _mlstm(p, prm["ml_b_gates"], prm["ml_norm_g"], c0, n0, m0, chunk)
    x1, u2, te, tg = _outproj(hg, prm["ml_w_out"], x, mod(0, 2), mod(0, 4), mod(0, 3),
                              prm["ln_g"][0, 0:1], prm["ln_b"][0, 0:1], prm["w_router"][0], prm["b_router"][0])
    f = _moe(u2.reshape(b * t, d), te.reshape(b * t, TOP_K), tg.reshape(b * t, TOP_K),
             prm["moe_w1"][0], prm["moe_b1"][0], prm["moe_w2"][0], prm["moe_b2"][0]).reshape(b, t, d)

    x2, p = _inproj(x1, mod(1, 1), mod(1, 0), prm["ds_w_in"],
                    res=(f, mod(0, 5), prm["ln_g"][0, 1:2], prm["ln_b"][0, 1:2]))
    q, kf, vf, kif, kb, vb, kib, qi, w = _dsa_prep(p, pos)
    pos_np = np.asarray(pos)
    qlim = (pos_np // CHUNK + 1) * CHUNK
    if ds_cache is None:
        kpos = pos_np
        tq, tk = 128, 512
        if t % tk:
            tq, tk = t, t
        k_all, v_all, ki_all = kb, vb, kib
    else:
        ck, cv, cki = ds_cache
        past = ck.shape[1]
        total = past + t
        lp = -(-total // (3 * LANES)) * (3 * LANES)
        tq, tk = t, 3 * LANES
        kpos = np.concatenate([np.arange(past), pos_np, np.full((lp - total,), np.iinfo(np.int32).max)])

        def cat(cache, new):
            full = jnp.concatenate([cache.reshape(b, past, -1).astype(BF16), new], axis=1)
            return jnp.pad(full, ((0, 0), (0, lp - total), (0, 0)))

        k_all, v_all, ki_all = cat(ck, kb), cat(cv, vb), cat(cki, kib)
    o = _dsa_attend(q, qi, w, k_all, v_all, ki_all, kpos.astype(np.int64), qlim, tq, tk)
    x3, u2, te, tg = _outproj(o, prm["ds_w_out"], x2, mod(1, 2), mod(1, 4), mod(1, 3),
                              prm["ln_g"][1, 0:1], prm["ln_b"][1, 0:1], prm["w_router"][1], prm["b_router"][1])
    f = _moe(u2.reshape(b * t, d), te.reshape(b * t, TOP_K), tg.reshape(b * t, TOP_K),
             prm["moe_w1"][1], prm["moe_b1"][1], prm["moe_w2"][1], prm["moe_b2"][1]).reshape(b, t, d)
    y = _final_ln(x3, f, mod(1, 5), prm["ln_g"][1, 1:2], prm["ln_b"][1, 1:2])

    ml_out = (c_new[None], n_new[None], m_new[None])
    ds_out = (kf.reshape(1, b, t, DSA_KV_HEADS, DSA_HD), vf.reshape(1, b, t, DSA_KV_HEADS, DSA_HD), kif[None])
    return y, ml_out, ds_out


def _prepare(w_ada, b_ada, ln_g, ln_b, ml_w_in, ml_b_gates, ml_norm_g, ml_w_out, ds_w_in, ds_w_out,
             moe_w_router, moe_b_router, moe_w1, moe_b1, moe_w2, moe_b2):
    return {
        "ln_g": ln_g, "ln_b": ln_b,
        "ml_w_in": jnp.pad(ml_w_in[0], ((0, 0), (0, ML_IN_PAD - ML_IN))).astype(BF16),
        "ml_b_gates": ml_b_gates[0], "ml_norm_g": ml_norm_g[0],
        "ml_w_out": ml_w_out[0].astype(BF16),
        "ds_w_in": jnp.pad(ds_w_in[0], ((0, 0), (0, DSA_IN_PAD - DSA_IN))).astype(BF16),
        "ds_w_out": ds_w_out[0].astype(BF16),
        "w_router": jnp.pad(moe_w_router, ((0, 0), (0, 0), (0, LANES - N_EXPERTS))),
        "b_router": jnp.pad(moe_b_router, ((0, 0), (0, LANES - N_EXPERTS)), constant_values=NEG_BIG)[:, None, :],
        "moe_w1": moe_w1.astype(BF16), "moe_b1": moe_b1,
        "moe_w2": moe_w2.astype(BF16), "moe_b2": moe_b2,
    }


def kernel(x_prompt, x_sample, state_mlstm_C, state_mlstm_n, state_mlstm_m, cache_dsa_k, cache_dsa_v,
           cache_dsa_kidx, c_prompt, c_sample, w_ada, b_ada, ln_g, ln_b, ml_w_in, ml_b_gates, ml_norm_g,
           ml_w_out, ds_w_in, ds_w_out, moe_w_router, moe_b_router, moe_w1, moe_b1, moe_w2, moe_b2):
    prm = _prepare(w_ada, b_ada, ln_g, ln_b, ml_w_in, ml_b_gates, ml_norm_g, ml_w_out, ds_w_in, ds_w_out,
                   moe_w_router, moe_b_router, moe_w1, moe_b1, moe_w2, moe_b2)
    bp, bs = c_prompt.shape[0], c_sample.shape[0]
    rows = -(-(bp + bs) // 8) * 8
    c_rows = jnp.pad(jnp.concatenate([c_prompt, c_sample], axis=0), ((0, rows - bp - bs), (0, 0)))
    mods = _ada(c_rows, w_ada, b_ada)
    mods_p = [mods[i, :bp] for i in range(DEPTH)]
    mods_s = [mods[i, bp:bp + bs] for i in range(DEPTH)]

    pos_p = np.arange(x_prompt.shape[1], dtype=np.int32)
    y_p, ml_p, ds_p = _trunk(x_prompt, mods_p, pos_p, None, None, prm)
    past = cache_dsa_k.shape[2]
    pos_s = past + np.arange(x_sample.shape[1], dtype=np.int32)
    y_s, ml_s, ds_s = _trunk(x_sample, mods_s, pos_s, (state_mlstm_C, state_mlstm_n, state_mlstm_m),
                             (cache_dsa_k[0], cache_dsa_v[0], cache_dsa_kidx[0]), prm)
    return (y_p, y_s, ml_p[0], ml_p[1], ml_p[2], ds_p[0], ds_p[1], ds_p[2],
            ml_s[0], ml_s[1], ml_s[2], ds_s[0], ds_s[1], ds_s[2])
```

```python
import functools

import numpy as np
import jax
import jax.numpy as jnp
from jax import lax
from jax.experimental import pallas as pl
from jax.experimental.pallas import tpu as pltpu

F32 = jnp.float32
BF16 = jnp.bfloat16
I32 = jnp.int32

D_MODEL = 1024
DEPTH = 2
CHUNK = 64
ML_HEADS = 4
ML_DV = D_MODEL // ML_HEADS
ML_DK = ML_DV // 2
ML_QK = ML_HEADS * ML_DK
ML_V = ML_HEADS * ML_DV
ML_IN = 2 * ML_QK + 2 * ML_V + 2 * ML_HEADS
ML_GATE_OFF = 2 * ML_QK + 2 * ML_V
DSA_HEADS = 8
DSA_KV_HEADS = 2
DSA_HD = D_MODEL // DSA_HEADS
DSA_GROUP = DSA_HEADS // DSA_KV_HEADS
IDX_HEADS = 8
IDX_DIM = 64
DSA_TOPK = 256
DSA_Q = DSA_HEADS * DSA_HD
DSA_KV = DSA_KV_HEADS * DSA_HD
DSA_IN = DSA_Q + 2 * DSA_KV + IDX_HEADS * IDX_DIM + IDX_DIM + IDX_HEADS
ROPE_THETA = 500000.0
ROPE_FRACTION = 4
N_EXPERTS = 32
TOP_K = 4
D_FF = D_MODEL
SWIGLU_LIMIT = 7.0
SWIGLU_ALPHA = 1.702
DEEPNORM_ALPHA = (2 * DEPTH) ** 0.25
LN_EPS = 1e-5

LANES = 128
VMEM_LIMIT = 56 * 1024 * 1024

ML_IN_PAD = -(-ML_IN // LANES) * LANES
DSA_IN_PAD = -(-DSA_IN // LANES) * LANES
NEG_BIG = -1e30
F32_MAX = float(np.finfo(np.float32).max)
HIGHEST = lax.Precision.HIGHEST
NT_DIMS = (((1,), (1,)), ((), ()))
TN_DIMS = (((0,), (0,)), ((), ()))


def _params(*sem):
    return pltpu.CompilerParams(dimension_semantics=sem, vmem_limit_bytes=VMEM_LIMIT)


def _row_tile(t, pref):
    return pref if t % pref == 0 else t


def _ada_kernel(c_ref, w_ref, b_ref, o_ref):
    c = c_ref[...]
    cond = (c * jax.nn.sigmoid(c)).astype(BF16)
    o_ref[0] = jnp.dot(cond, w_ref[0].astype(BF16), preferred_element_type=F32) + b_ref[0]


def _ada(c_rows, w_ada, b_ada):
    rows = c_rows.shape[0]
    n = w_ada.shape[-1]
    tn = 1536
    return pl.pallas_call(
        _ada_kernel,
        grid=(DEPTH, n // tn),
        in_specs=[
            pl.BlockSpec((rows, D_MODEL), lambda i, j: (0, 0)),
            pl.BlockSpec((1, D_MODEL, tn), lambda i, j: (i, 0, j)),
            pl.BlockSpec((1, 1, tn), lambda i, j: (i, 0, j)),
        ],
        out_specs=pl.BlockSpec((1, rows, tn), lambda i, j: (i, 0, j)),
        out_shape=jax.ShapeDtypeStruct((DEPTH, rows, n), F32),
        compiler_params=_params("arbitrary", "arbitrary"),
        name="ada_mod",
    )(c_rows, w_ada, b_ada.reshape(DEPTH, 1, n))


def _layer_norm(z, g, b):
    mu = jnp.mean(z, axis=-1, keepdims=True)
    zc = z - mu
    var = jnp.mean(zc * zc, axis=-1, keepdims=True)
    return zc * lax.rsqrt(var + LN_EPS) * g + b


def _inproj_kernel(*refs, has_res):
    if has_res:
        x_ref, f_ref, g_ref, lng_ref, lnb_ref, sc_ref, sh_ref, w_ref, xo_ref, p_ref = refs
        z = DEEPNORM_ALPHA * x_ref[0] + (1.0 + g_ref[0]) * f_ref[0]
        x = _layer_norm(z, lng_ref[...], lnb_ref[...])
        xo_ref[0] = x
    else:
        x_ref, sc_ref, sh_ref, w_ref, p_ref = refs
        x = x_ref[0]
    u = x * (1.0 + sc_ref[0]) + sh_ref[0]
    p_ref[0] = jnp.dot(u.astype(BF16), w_ref[...], preferred_element_type=F32)


def _inproj(x, sc, sh, w_bf, res=None):
    b, t, d = x.shape
    n = w_bf.shape[1]
    tm = _row_tile(t, 512)
    row = pl.BlockSpec((1, tm, d), lambda i, j: (i, j, 0))
    vec = pl.BlockSpec((1, 1, d), lambda i, j: (i, 0, 0))
    par = pl.BlockSpec((1, d), lambda i, j: (0, 0))
    wspec = pl.BlockSpec((d, n), lambda i, j: (0, 0))
    pspec = pl.BlockSpec((1, tm, n), lambda i, j: (i, j, 0))
    pshape = jax.ShapeDtypeStruct((b, t, n), F32)
    if res is None:
        return None, pl.pallas_call(
            functools.partial(_inproj_kernel, has_res=False),
            grid=(b, t // tm),
            in_specs=[row, vec, vec, wspec],
            out_specs=pspec,
            out_shape=pshape,
            compiler_params=_params("arbitrary", "arbitrary"),
            name="inproj",
        )(x, sc, sh, w_bf)
    f, g, lng, lnb = res
    return pl.pallas_call(
        functools.partial(_inproj_kernel, has_res=True),
        grid=(b, t // tm),
        in_specs=[row, row, vec, par, par, vec, vec, wspec],
        out_specs=[row, pspec],
        out_shape=[jax.ShapeDtypeStruct((b, t, d), F32), pshape],
        compiler_params=_params("arbitrary", "arbitrary"),
        name="ln_inproj",
    )(x, f, g, lng, lnb, sc, sh, w_bf)


def _final_ln_kernel(x_ref, f_ref, g_ref, lng_ref, lnb_ref, o_ref):
    z = DEEPNORM_ALPHA * x_ref[0] + (1.0 + g_ref[0]) * f_ref[0]
    o_ref[0] = _layer_norm(z, lng_ref[...], lnb_ref[...])


def _final_ln(x, f, g, lng, lnb):
    b, t, d = x.shape
    tm = _row_tile(t, 1024)
    row = pl.BlockSpec((1, tm, d), lambda i, j: (i, j, 0))
    vec = pl.BlockSpec((1, 1, d), lambda i, j: (i, 0, 0))
    par = pl.BlockSpec((1, d), lambda i, j: (0, 0))
    return pl.pallas_call(
        _final_ln_kernel,
        grid=(b, t // tm),
        in_specs=[row, row, vec, par, par],
        out_specs=row,
        out_shape=jax.ShapeDtypeStruct((b, t, d), F32),
        compiler_params=_params("arbitrary", "arbitrary"),
        name="final_ln",
    )(x, f, g, lng, lnb)


def _log_sigmoid(x):
    return jnp.minimum(x, 0.0) - jnp.log1p(jnp.exp(-jnp.abs(x)))


def _mlstm_kernel(p_ref, gt_ref, bcol_ref, brow_ref, ng_ref, c0_ref, n0_ref, m0_ref,
                  hg_ref, c_ref, n_ref, m_ref, *, chunk):
    L = chunk

    @pl.when(pl.program_id(1) == 0)
    def _():
        c_ref[...] = c0_ref[...]
        n_ref[...] = n0_ref[...]
        m_ref[...] = m0_ref[...]

    gcol = p_ref[0, :, ML_GATE_OFF:ML_GATE_OFF + 2 * ML_HEADS] + bcol_ref[...]
    grow = gt_ref[0] + brow_ref[...]
    lf_col = _log_sigmoid(gcol)
    lf_row = _log_sigmoid(grow)
    ti = lax.broadcasted_iota(I32, (L, L), 0)
    si = lax.broadcasted_iota(I32, (L, L), 1)
    causal = si <= ti
    tril = jnp.where(causal, 1.0, 0.0).astype(F32)
    triu = jnp.where(ti <= si, 1.0, 0.0).astype(F32)
    b_col = jnp.dot(tril, lf_col, precision=HIGHEST, preferred_element_type=F32)
    b_row = jnp.dot(lf_row, triu, precision=HIGHEST, preferred_element_type=F32)

    for h in range(ML_HEADS):
        q = p_ref[0, :, h * ML_DK:(h + 1) * ML_DK]
        k = p_ref[0, :, ML_QK + h * ML_DK:ML_QK + (h + 1) * ML_DK]
        v = p_ref[0, :, 2 * ML_QK + h * ML_DV:2 * ML_QK + (h + 1) * ML_DV]
        o = p_ref[0, :, 2 * ML_QK + ML_V + h * ML_DV:2 * ML_QK + ML_V + (h + 1) * ML_DV]
        qs = (q * (ML_DK ** -0.5)).astype(BF16)
        kb = k.astype(BF16)
        vb = v.astype(BF16)
        bc = b_col[:, ML_HEADS + h:ML_HEADS + h + 1]
        ic = gcol[:, h:h + 1]
        br = b_row[ML_HEADS + h:ML_HEADS + h + 1, :]
        ir = grow[h:h + 1, :]
        c_old = c_ref[0, h]
        n_old = n_ref[0, h:h + 1, :]
        m_old = m_ref[0, h:h + 1, 0:1]

        dm = jnp.where(causal, bc - br + ir, -jnp.inf)
        inter = bc + m_old
        mt = jnp.maximum(inter, jnp.max(dm, axis=-1, keepdims=True))
        qk = lax.dot_general(qs, kb, NT_DIMS, preferred_element_type=F32)
        s = jnp.exp(dm - mt) * qk
        wp = jnp.exp(inter - mt)
        qc = lax.dot_general(qs, c_old.astype(BF16), NT_DIMS, preferred_element_type=F32)
        num = jnp.dot(s.astype(BF16), vb, preferred_element_type=F32) + wp * qc
        qn = jnp.sum(qs.astype(F32) * n_old, axis=-1, keepdims=True)
        den = jnp.sum(s, axis=-1, keepdims=True) + wp * qn
        hh = num / jnp.maximum(jnp.abs(den), jnp.exp(-mt))

        m_new = mt[L - 1:L, :]
        wk = jnp.exp(bc[L - 1:L, :] - bc + ic - m_new)
        wprev = jnp.exp(inter[L - 1:L, :] - m_new)
        vw = (v * wk).astype(BF16)
        c_ref[0, h] = wprev * c_old + lax.dot_general(vw, kb, TN_DIMS, preferred_element_type=F32)
        n_ref[0, h:h + 1, :] = wprev * n_old + jnp.sum(wk * kb.astype(F32), axis=0, keepdims=True)
        m_ref[0, h:h + 1, :] = jnp.broadcast_to(m_new, (1, ML_DK))

        mu = jnp.mean(hh, axis=-1, keepdims=True)
        hc = hh - mu
        var = jnp.mean(hc * hc, axis=-1, keepdims=True)
        hn = hc * lax.rsqrt(var + LN_EPS) * ng_ref[:, h * ML_DV:(h + 1) * ML_DV]
        hg_ref[0, :, h * ML_DV:(h + 1) * ML_DV] = (jax.nn.sigmoid(o) * hn).astype(BF16)


def _mlstm(p, b_gates, norm_g, c0, n0, m0, chunk):
    b, t, n = p.shape
    nc = t // chunk
    gt = jnp.swapaxes(p[:, :, ML_GATE_OFF:ML_GATE_OFF + 2 * ML_HEADS], 1, 2)
    if nc > 1:
        gt_spec = pl.BlockSpec((1, 2 * ML_HEADS, chunk), lambda i, j: (i, 0, j))
    else:
        gt_spec = pl.BlockSpec((1, 2 * ML_HEADS, t), lambda i, j: (i, 0, 0))
    if nc > 1 and chunk % LANES != 0:
        gt = gt.reshape(b, 2 * ML_HEADS, nc, chunk).transpose(0, 2, 1, 3).reshape(b * nc, 2 * ML_HEADS, chunk)
        gt_spec = pl.BlockSpec((1, 2 * ML_HEADS, chunk), lambda i, j: (i * nc + j, 0, 0))
    m0b = jnp.broadcast_to(m0[..., None], (b, ML_HEADS, ML_DK))
    cspec = pl.BlockSpec((1, ML_HEADS, ML_DV, ML_DK), lambda i, j: (i, 0, 0, 0))
    nspec = pl.BlockSpec((1, ML_HEADS, ML_DK), lambda i, j: (i, 0, 0))
    hg, c, nn, m = pl.pallas_call(
        functools.partial(_mlstm_kernel, chunk=chunk),
        grid=(b, nc),
        in_specs=[
            pl.BlockSpec((1, chunk, n), lambda i, j: (i, j, 0)),
            gt_spec,
            pl.BlockSpec((1, 2 * ML_HEADS), lambda i, j: (0, 0)),
            pl.BlockSpec((2 * ML_HEADS, 1), lambda i, j: (0, 0)),
            pl.BlockSpec((1, ML_V), lambda i, j: (0, 0)),
            cspec, nspec, nspec,
        ],
        out_specs=[pl.BlockSpec((1, chunk, ML_V), lambda i, j: (i, j, 0)), cspec, nspec, nspec],
        out_shape=[
            jax.ShapeDtypeStruct((b, t, ML_V), BF16),
            jax.ShapeDtypeStruct((b, ML_HEADS, ML_DV, ML_DK), F32),
            jax.ShapeDtypeStruct((b, ML_HEADS, ML_DK), F32),
            jax.ShapeDtypeStruct((b, ML_HEADS, ML_DK), F32),
        ],
        compiler_params=_params("arbitrary", "arbitrary"),
        name="mlstm_scan",
    )(p, gt, b_gates.reshape(1, -1), b_gates.reshape(-1, 1), norm_g.reshape(1, -1), c0, n0, m0b)
    return hg, c, nn, m[..., 0]


def _outproj_kernel(a_ref, w_ref, x_ref, g_ref, sc_ref, sh_ref, lng_ref, lnb_ref, wr_ref, br_ref,
                    x1_ref, u2_ref, te_ref, tg_ref):
    y = jnp.dot(a_ref[0], w_ref[...], preferred_element_type=F32)
    z = DEEPNORM_ALPHA * x_ref[0] + (1.0 + g_ref[0]) * y
    x1 = _layer_norm(z, lng_ref[...], lnb_ref[...])
    x1_ref[0] = x1
    u2 = x1 * (1.0 + sc_ref[0]) + sh_ref[0]
    u2_ref[0] = u2.astype(BF16)
    logits = jnp.dot(u2, wr_ref[...], precision=HIGHEST, preferred_element_type=F32) + br_ref[...]
    lane = lax.broadcasted_iota(I32, logits.shape, 1).astype(F32)
    vals, idxs = [], []
    cur = logits
    for _ in range(TOP_K):
        mx = jnp.max(cur, axis=-1, keepdims=True)
        idx = jnp.min(jnp.where(cur == mx, lane, float(LANES)), axis=-1, keepdims=True)
        vals.append(mx)
        idxs.append(idx)
        cur = jnp.where(lane == idx, -jnp.inf, cur)
    es = [jnp.exp(v - vals[0]) for v in vals]
    tot = es[0] + es[1] + es[2] + es[3]
    for k in range(TOP_K):
        te_ref[0, :, k:k + 1] = idxs[k].astype(I32)
        tg_ref[0, :, k:k + 1] = es[k] / tot


def _outproj(a, w_bf, x, g, sc, sh, lng, lnb, wr_pad, br_pad):
    b, t, d = x.shape
    tm = _row_tile(t, 512)
    row = pl.BlockSpec((1, tm, d), lambda i, j: (i, j, 0))
    vec = pl.BlockSpec((1, 1, d), lambda i, j: (i, 0, 0))
    par = pl.BlockSpec((1, d), lambda i, j: (0, 0))
    top = pl.BlockSpec((1, tm, TOP_K), lambda i, j: (i, j, 0))
    return pl.pallas_call(
        _outproj_kernel,
        grid=(b, t // tm),
        in_specs=[row, pl.BlockSpec((d, d), lambda i, j: (0, 0)), row, vec, vec, vec, par, par,
                  pl.BlockSpec((d, LANES), lambda i, j: (0, 0)), pl.BlockSpec((1, LANES), lambda i, j: (0, 0))],
        out_specs=[row, row, top, top],
        out_shape=[jax.ShapeDtypeStruct((b, t, d), F32), jax.ShapeDtypeStruct((b, t, d), BF16),
                   jax.ShapeDtypeStruct((b, t, TOP_K), I32), jax.ShapeDtypeStruct((b, t, TOP_K), F32)],
        compiler_params=_params("arbitrary", "arbitrary"),
        name="outproj_ln_router",
    )(a, w_bf, x, g, sc, sh, lng, lnb, wr_pad, br_pad)


def _moe_kernel(x_ref, te_ref, tg_ref, w1_ref, b1_ref, w2_ref, b2_ref, o_ref, pos_ref, tri_ref, *, tm, rb):
    e = pl.program_id(1)

    @pl.when(jnp.logical_and(pl.program_id(0) == 0, e == 0))
    def _():
        si = lax.broadcasted_iota(I32, (tm, tm), 0)
        ti = lax.broadcasted_iota(I32, (tm, tm), 1)
        tri_ref[...] = jnp.where(si <= ti, 1.0, 0.0).astype(BF16)

    @pl.when(e == 0)
    def _():
        o_ref[...] = jnp.zeros_like(o_ref)
        eio = lax.broadcasted_iota(I32, (N_EXPERTS, tm), 0)
        sel = jnp.zeros((N_EXPERTS, tm), F32)
        for k in range(TOP_K):
            sel = sel + jnp.where(te_ref[k:k + 1, :] == eio, 1.0, 0.0)
        rank = jnp.dot(sel.astype(BF16), tri_ref[...], preferred_element_type=F32)
        pos_ref[...] = rank * sel

    pm = pos_ref[pl.ds(e, 1), :]
    gate = jnp.zeros((1, tm), F32)
    for k in range(TOP_K):
        gate = gate + jnp.where(te_ref[k:k + 1, :] == e, tg_ref[k:k + 1, :], 0.0)
    cnt = jnp.max(pm).astype(I32)
    nblk = (cnt + (rb - 1)) // rb

    def body(blk, carry):
        r = (blk * rb + 1 + lax.broadcasted_iota(I32, (rb, 1), 0)).astype(F32)
        hit = pm == r
        onehot = jnp.where(hit, 1.0, 0.0).astype(BF16)
        xg = jnp.dot(onehot, x_ref[...], preferred_element_type=F32).astype(BF16)
        h = jnp.dot(xg, w1_ref[0], preferred_element_type=F32) + b1_ref[0]
        hg = jnp.minimum(h[:, :D_FF], SWIGLU_LIMIT)
        hl = jnp.clip(h[:, D_FF:], -SWIGLU_LIMIT, SWIGLU_LIMIT)
        a = hg * jax.nn.sigmoid(SWIGLU_ALPHA * hg) * (hl + 1.0)
        y = jnp.dot(a.astype(BF16), w2_ref[0], preferred_element_type=F32) + b2_ref[0]
        grow = jnp.sum(jnp.where(hit, gate, 0.0), axis=-1, keepdims=True)
        yg = (y * grow).astype(BF16)
        o_ref[...] += lax.dot_general(onehot, yg, TN_DIMS, preferred_element_type=F32)
        return carry

    lax.fori_loop(0, nblk, body, 0)


def _moe(u2, te, tg, w1_bf, b1, w2_bf, b2, layer):
    n, d = u2.shape
    tm = _row_tile(n, 1024)
    rb = min(tm, tm * TOP_K // N_EXPERTS + 32)
    te_t = te.T
    tg_t = tg.T
    return pl.pallas_call(
        functools.partial(_moe_kernel, tm=tm, rb=rb),
        grid=(n // tm, N_EXPERTS),
        in_specs=[
            pl.BlockSpec((tm, d), lambda i, e: (i, 0)),
            pl.BlockSpec((TOP_K, tm), lambda i, e: (0, i)),
            pl.BlockSpec((TOP_K, tm), lambda i, e: (0, i)),
            pl.BlockSpec((None, 1, d, 2 * D_FF), lambda i, e: (layer, e, 0, 0)),
            pl.BlockSpec((None, 1, 1, 2 * D_FF), lambda i, e: (layer, e, 0, 0)),
            pl.BlockSpec((None, 1, D_FF, d), lambda i, e: (layer, e, 0, 0)),
            pl.BlockSpec((None, 1, 1, d), lambda i, e: (layer, e, 0, 0)),
        ],
        out_specs=pl.BlockSpec((tm, d), lambda i, e: (i, 0)),
        out_shape=jax.ShapeDtypeStruct((n, d), F32),
        scratch_shapes=[pltpu.VMEM((N_EXPERTS, tm), F32), pltpu.VMEM((tm, tm), BF16)],
        compiler_params=_params("arbitrary", "arbitrary"),
        name="moe_experts",
    )(u2, te_t, tg_t, w1_bf, b1.reshape(DEPTH, N_EXPERTS, 1, -1), w2_bf, b2.reshape(DEPTH, N_EXPERTS, 1, -1))


def _rope_tables(pos, head_dim):
    rd = head_dim // ROPE_FRACTION
    half = rd // 2
    inv = jnp.power(ROPE_THETA, -jnp.arange(half, dtype=F32) / half)
    ang = jnp.asarray(pos).astype(F32)[:, None] * inv[None, :]
    cos, sin = jnp.cos(ang), jnp.sin(ang)
    t = pos.shape[0]
    ones = jnp.ones((t, head_dim - rd), F32)
    zeros = jnp.zeros((t, head_dim - rd), F32)
    zh = jnp.zeros((t, half), F32)
    c = jnp.concatenate([cos, cos, ones], axis=-1)
    sa = jnp.concatenate([zh, sin, zeros], axis=-1)
    sb = jnp.concatenate([-sin, zh, zeros], axis=-1)
    rep = LANES // head_dim
    return tuple(jnp.tile(a, (1, rep)) for a in (c, sa, sb))


def _rope(z, c, sa, sb, half):
    w = z.shape[-1]
    rep = w // LANES
    if rep > 1:
        c, sa, sb = (jnp.concatenate([a] * rep, axis=-1) for a in (c, sa, sb))
    return z * c + pltpu.roll(z, half, 1) * sa + pltpu.roll(z, w - half, 1) * sb


def _dsa_prep_kernel(p_ref, c1_ref, sa1_ref, sb1_ref, c2_ref, sa2_ref, sb2_ref,
                     q_ref, kf_ref, vf_ref, kif_ref, kb_ref, vb_ref, kib_ref, qi_ref, w_ref):
    c1, sa1, sb1 = c1_ref[...], sa1_ref[...], sb1_ref[...]
    c2, sa2, sb2 = c2_ref[...], sa2_ref[...], sb2_ref[...]
    h1 = DSA_HD // ROPE_FRACTION // 2
    h2 = IDX_DIM // ROPE_FRACTION // 2
    o_k = DSA_Q
    o_v = DSA_Q + DSA_KV
    o_qi = DSA_Q + 2 * DSA_KV
    o_ki = o_qi + IDX_HEADS * IDX_DIM
    q = _rope(p_ref[0, :, :DSA_Q], c1, sa1, sb1, h1)
    q_ref[0] = q.astype(BF16)
    k = _rope(p_ref[0, :, o_k:o_v], c1, sa1, sb1, h1)
    kf_ref[0] = k
    kb_ref[0] = k.astype(BF16)
    v = p_ref[0, :, o_v:o_qi]
    vf_ref[0] = v
    vb_ref[0] = v.astype(BF16)
    qi = _rope(p_ref[0, :, o_qi:o_ki], c2, sa2, sb2, h2)
    qi_ref[0] = qi.astype(BF16)
    tail = p_ref[0, :, o_ki:o_ki + LANES]
    ki = _rope(tail, c2, sa2, sb2, h2)[:, :IDX_DIM]
    kif_ref[0] = ki
    kib_ref[0] = ki.astype(BF16)
    wi = tail[:, IDX_DIM:IDX_DIM + IDX_HEADS]
    w_ref[0] = (wi * (IDX_HEADS ** -0.5)) * (IDX_DIM ** -0.5)


def _dsa_prep(p, pos):
    b, t, n = p.shape
    tm = _row_tile(t, 512)
    tabs = _rope_tables(pos, DSA_HD) + _rope_tables(pos, IDX_DIM)
    tab = pl.BlockSpec((tm, LANES), lambda i, j: (j, 0))

    def row(w):
        return pl.BlockSpec((1, tm, w), lambda i, j: (i, j, 0))

    widths = [(DSA_Q, BF16), (DSA_KV, F32), (DSA_KV, F32), (IDX_DIM, F32), (DSA_KV, BF16), (DSA_KV, BF16),
              (IDX_DIM, BF16), (IDX_HEADS * IDX_DIM, BF16), (IDX_HEADS, F32)]
    return pl.pallas_call(
        _dsa_prep_kernel,
        grid=(b, t // tm),
        in_specs=[row(n)] + [tab] * 6,
        out_specs=[row(w) for w, _ in widths],
        out_shape=[jax.ShapeDtypeStruct((b, t, w), dt) for w, dt in widths],
        compiler_params=_params("arbitrary", "arbitrary"),
        name="dsa_rope_split",
    )(p, *tabs)


def _f2key(f):
    bits = lax.bitcast_convert_type(f, I32)
    return bits ^ ((bits >> 31) & 0x7FFFFFFF)


def _key2f(k):
    return lax.bitcast_convert_type(k ^ ((k >> 31) & 0x7FFFFFFF), F32)


def _dsa_kernel(nkt_ref, q_ref, qi_ref, w_ref, k_ref, v_ref, ki_ref, kpos_ref, qlim_ref, o_ref,
                s_ref, wb_ref, qg_ref, m_ref, l_ref, acc_ref, flag_ref, *, tq, tk, nsel, value_iters):
    nkt = nkt_ref[pl.program_id(1)]
    qlim = qlim_ref[...]
    w = w_ref[0]
    qi = qi_ref[0]
    qis = [qi[:, h * IDX_DIM:(h + 1) * IDX_DIM] for h in range(IDX_HEADS)]
    nlg = tk // LANES
    for h in range(IDX_HEADS):
        wb_ref[h] = jnp.broadcast_to(w[:, h:h + 1], (tq, LANES))

    def p1(kt, carry):
        nadm, rmax, rmin, cgt0, cge0 = carry
        off = pl.multiple_of(kt * tk, tk)
        kit = ki_ref[0, pl.ds(off, tk), :]
        acc = jnp.zeros((tq, tk), F32)
        for h in range(IDX_HEADS):
            d = lax.dot_general(qis[h], kit, NT_DIMS, preferred_element_type=F32)
            acc = acc + jnp.maximum(d, 0.0) * jnp.concatenate([wb_ref[h]] * nlg, axis=1)
        adm = kpos_ref[kt] < qlim
        s = jnp.where(adm, acc, -jnp.inf)
        s_ref[kt] = s
        nadm = nadm + jnp.sum(jnp.where(adm, 1.0, 0.0), axis=-1, keepdims=True)
        rmax = jnp.maximum(rmax, jnp.max(s, axis=-1, keepdims=True))
        rmin = jnp.minimum(rmin, jnp.min(jnp.where(adm, acc, jnp.inf), axis=-1, keepdims=True))
        cgt0 = cgt0 + jnp.sum(jnp.where(s > 0.0, 1.0, 0.0), axis=-1, keepdims=True)
        cge0 = cge0 + jnp.sum(jnp.where(s >= 0.0, 1.0, 0.0), axis=-1, keepdims=True)
        return nadm, rmax, rmin, cgt0, cge0

    zcol = jnp.zeros((tq, 1), F32)
    init = (zcol, jnp.full((tq, 1), -jnp.inf, F32), jnp.full((tq, 1), jnp.inf, F32), zcol, zcol)
    nadm, rmax, rmin, cgt0, cge0 = lax.fori_loop(0, nkt, p1, init)

    rg = min(tq, 32)
    fsel = float(nsel)
    unbounded = float(2 ** 30)

    def count_ge(t):
        tb = jnp.broadcast_to(t, (tq, LANES))

        def body(kt, c):
            out = []
            for r in range(tq // rg):
                tr = jnp.concatenate([tb[r * rg:(r + 1) * rg]] * nlg, axis=1)
                hit = jnp.where(s_ref[kt, r * rg:(r + 1) * rg, :] >= tr, 1.0, 0.0)
                cr = c[r * rg:(r + 1) * rg]
                for g in range(nlg):
                    cr = cr + hit[:, g * LANES:(g + 1) * LANES]
                out.append(cr)
            return jnp.concatenate(out, axis=0)
        c = lax.fori_loop(0, nkt, body, jnp.zeros((tq, LANES), F32))
        return jnp.sum(c, axis=-1, keepdims=True)

    few = nadm <= fsel
    many = jnp.logical_not(few)
    pos = jnp.logical_and(many, cgt0 > fsel)
    neg = jnp.logical_and(many, cge0 < fsel)
    zero = jnp.logical_and(many, jnp.logical_and(cgt0 <= fsel, cge0 >= fsel))
    lo0 = jnp.where(pos, 0, _f2key(jnp.where(neg, rmin, 0.0)))
    hi0 = jnp.where(pos, _f2key(jnp.where(pos, rmax, 0.0)) + 1, 0)
    clo0 = jnp.where(pos, cge0, nadm)
    chi0 = jnp.where(pos, 0.0, cge0)
    act0 = jnp.where(jnp.logical_or(pos, neg), 1.0, 0.0)
    t0 = jnp.where(few, -jnp.inf, 0.0)
    need0 = jnp.where(zero, fsel - cgt0, unbounded)

    def cond(st):
        return st[0] > 0.0

    def step(st):
        _, it, lo, hi, clo, chi, act, thr, need = st
        conv = jnp.logical_and(act > 0.0, lo + 1 >= hi)
        thr = jnp.where(conv, _key2f(lo), thr)
        need = jnp.where(conv, fsel - chi, need)
        act = jnp.where(conv, 0.0, act)
        flo, fhi = _key2f(lo), _key2f(hi)
        mid_i = (lo >> 1) + (hi >> 1) + (lo & hi & 1)
        lclo = jnp.log(clo)
        interp = (lclo - float(np.log(nsel - 0.5))) / (lclo - jnp.log(jnp.maximum(chi, 0.5)))
        frac = jnp.where(it % 3 == 2, 0.5, interp)
        mid_v = _f2key(flo + frac * (fhi - flo))
        use_v = jnp.logical_and(it < value_iters, jnp.logical_and(mid_v > lo, mid_v < hi))
        mid = jnp.where(use_v, mid_v, mid_i)
        tm = _key2f(mid)
        c = count_ge(jnp.where(act > 0.0, tm, jnp.inf))
        hit = jnp.logical_and(act > 0.0, c == fsel)
        thr = jnp.where(hit, tm, thr)
        up = jnp.logical_and(act > 0.0, c > fsel)
        dn = jnp.logical_and(act > 0.0, c < fsel)
        lo = jnp.where(up, mid, lo)
        clo = jnp.where(up, c, clo)
        hi = jnp.where(dn, mid, hi)
        chi = jnp.where(dn, c, chi)
        act = jnp.where(hit, 0.0, act)
        return jnp.max(act), it + 1, lo, hi, clo, chi, act, thr, need

    st = (jnp.max(act0), jnp.int32(0), lo0, hi0, clo0, chi0, act0, t0, need0)
    _, _, _, _, _, _, _, thr, need = lax.while_loop(cond, step, st)
    tied = need < unbounded * 0.5
    has_ties = jnp.max(jnp.where(tied, 1.0, 0.0)) > 0.0
    thr = jnp.maximum(thr, -F32_MAX)

    for g in range(DSA_KV_HEADS):
        for hh in range(DSA_GROUP):
            hd = g * DSA_GROUP + hh
            qg_ref[g, hh * tq:(hh + 1) * tq, :] = q_ref[0, :, hd * DSA_HD:(hd + 1) * DSA_HD]
    m_ref[...] = jnp.full(m_ref.shape, NEG_BIG, F32)
    l_ref[...] = jnp.zeros(l_ref.shape, F32)
    acc_ref[...] = jnp.zeros(acc_ref.shape, F32)

    log2_scale = (DSA_HD ** -0.5) * float(np.log2(np.e))
    ones_cols = jnp.ones((tk, DSA_HD), BF16)

    def attend(kt, bias):
        off = pl.multiple_of(kt * tk, tk)
        bias = jnp.concatenate([bias] * DSA_GROUP, axis=0)
        for g in range(DSA_KV_HEADS):
            kt_g = k_ref[0, pl.ds(off, tk), g * DSA_HD:(g + 1) * DSA_HD]
            vt_g = v_ref[0, pl.ds(off, tk), g * DSA_HD:(g + 1) * DSA_HD]
            lg = lax.dot_general(qg_ref[g], kt_g, NT_DIMS, preferred_element_type=F32) * log2_scale + bias
            m_old = m_ref[g]
            m_new = jnp.maximum(m_old, jnp.max(lg, axis=-1, keepdims=True))
            alpha = jnp.exp2(m_old - m_new)
            p = jnp.exp2(lg - jnp.concatenate([m_new] * nlg, axis=1))
            pv = jnp.dot(p.astype(BF16), jnp.concatenate([vt_g, ones_cols], axis=1),
                         preferred_element_type=F32)
            acc_ref[g] = alpha * acc_ref[g] + pv[:, :DSA_HD]
            l_ref[g] = alpha * l_ref[g] + pv[:, DSA_HD:]
            m_ref[g] = m_new

    @pl.when(jnp.logical_not(has_ties))
    def _():
        def body(kt, carry):
            attend(kt, jnp.where(s_ref[kt] >= thr, 0.0, NEG_BIG))
            return carry
        lax.fori_loop(0, nkt, body, 0)

    @pl.when(has_ties)
    def _():
        def scan(kt, carry):
            seen, cut, nin = carry
            eqc = jnp.sum(jnp.where(s_ref[kt] == thr, 1.0, 0.0), axis=-1, keepdims=True)
            here = jnp.logical_and(tied, jnp.logical_and(seen < need, seen + eqc >= need))
            cut = jnp.where(here, kt, cut)
            nin = jnp.where(here, need - seen, nin)
            flag_ref[kt] = jnp.max(jnp.where(here, 1.0, 0.0)).astype(I32)
            return seen + eqc, cut, nin

        cut0 = jnp.where(tied, -1, nkt).astype(I32)
        _, cut, nin = lax.fori_loop(0, nkt, scan, (zcol, cut0, zcol))

        def body(kt, carry):
            s = s_ref[kt]
            before = jnp.where(kt < cut, 0.0, NEG_BIG)

            def plain():
                return jnp.where(s == thr, before, NEG_BIG)

            def ranked():
                si = lax.broadcasted_iota(I32, (tk, tk), 0)
                ti = lax.broadcasted_iota(I32, (tk, tk), 1)
                triu = jnp.where(si <= ti, 1.0, 0.0).astype(BF16)
                eq = jnp.where(s == thr, 1.0, 0.0).astype(BF16)
                rank = jnp.dot(eq, triu, preferred_element_type=F32)
                inside = jnp.where(rank <= nin, 0.0, NEG_BIG)
                return jnp.where(s == thr, jnp.where(kt == cut, inside, before), NEG_BIG)

            tie_bias = lax.cond(flag_ref[kt] > 0, ranked, plain)
            attend(kt, jnp.where(s > thr, 0.0, tie_bias))
            return carry
        lax.fori_loop(0, nkt, body, 0)

    for g in range(DSA_KV_HEADS):
        og = acc_ref[g] / l_ref[g]
        for hh in range(DSA_GROUP):
            hd = g * DSA_GROUP + hh
            o_ref[0, :, hd * DSA_HD:(hd + 1) * DSA_HD] = og[hh * tq:(hh + 1) * tq, :].astype(BF16)


def _dsa_attend(q, qi, w, k, v, ki, kpos, qlim, tq, tk):
    b, t, _ = q.shape
    lp = k.shape[1]
    nq = t // tq
    nsel = min(DSA_TOPK, int(np.sum(kpos < np.iinfo(np.int32).max)) // 4)
    qmax = qlim.reshape(nq, tq).max(axis=1)
    nkt = np.minimum(-(-np.searchsorted(kpos, qmax, side="left") // tk), lp // tk).astype(np.int32)
    nkt = np.maximum(nkt, 1)
    single = pl.Buffered(1)

    def full(wd):
        return pl.BlockSpec((1, lp, wd), lambda i, j, n: (i, 0, 0), pipeline_mode=single)

    def row(wd):
        return pl.BlockSpec((1, tq, wd), lambda i, j, n: (i, j, 0))

    grid_spec = pltpu.PrefetchScalarGridSpec(
        num_scalar_prefetch=1,
        grid=(b, nq),
        in_specs=[row(DSA_Q), row(IDX_HEADS * IDX_DIM), row(IDX_HEADS), full(DSA_KV), full(DSA_KV), full(IDX_DIM),
                  pl.BlockSpec((lp // tk, 1, tk), lambda i, j, n: (0, 0, 0)),
                  pl.BlockSpec((tq, 1), lambda i, j, n: (j, 0))],
        out_specs=row(DSA_Q),
        scratch_shapes=[
            pltpu.VMEM((lp // tk, tq, tk), F32),
            pltpu.VMEM((IDX_HEADS, tq, LANES), F32),
            pltpu.VMEM((DSA_KV_HEADS, DSA_GROUP * tq, DSA_HD), BF16),
            pltpu.VMEM((DSA_KV_HEADS, DSA_GROUP * tq, LANES), F32),
            pltpu.VMEM((DSA_KV_HEADS, DSA_GROUP * tq, LANES), F32),
            pltpu.VMEM((DSA_KV_HEADS, DSA_GROUP * tq, DSA_HD), F32),
            pltpu.SMEM((lp // tk,), I32),
        ],
    )
    return pl.pallas_call(
        functools.partial(_dsa_kernel, tq=tq, tk=tk, nsel=nsel, value_iters=16),
        grid_spec=grid_spec,
        out_shape=jax.ShapeDtypeStruct((b, t, DSA_Q), BF16),
        compiler_params=_params("arbitrary", "arbitrary"),
        name="dsa_select_attend",
    )(jnp.asarray(nkt), q, qi, w, k, v, ki, jnp.asarray(kpos, I32).reshape(lp // tk, 1, tk),
      jnp.asarray(qlim, I32).reshape(t, 1))


def _trunk(x, mods, pos, ml_state, ds_cache, prm):
    b, t, d = x.shape
    chunk = CHUNK if t % CHUNK == 0 else t

    def mod(i, k):
        return mods[i][:, None, k * d:(k + 1) * d]

    _, p = _inproj(x, mod(0, 1), mod(0, 0), prm["ml_w_in"])
    if ml_state is None:
        c0 = jnp.zeros((b, ML_HEADS, ML_DV, ML_DK), F32)
        n0 = jnp.zeros((b, ML_HEADS, ML_DK), F32)
        m0 = jnp.zeros((b, ML_HEADS), F32)
    else:
        c0, n0, m0 = (s[0] for s in ml_state)
    hg, c_new, n_new, m_new = _mlstm(p, prm["ml_b_gates"], prm["ml_norm_g"], c0, n0, m0, chunk)
    x1, u2, te, tg = _outproj(hg, prm["ml_w_out"], x, mod(0, 2), mod(0, 4), mod(0, 3),
                              prm["ln_g"][0, 0:1], prm["ln_b"][0, 0:1], prm["w_router"][0], prm["b_router"][0])
    f = _moe(u2.reshape(b * t, d), te.reshape(b * t, TOP_K), tg.reshape(b * t, TOP_K),
             prm["moe_w1"], prm["moe_b1"], prm["moe_w2"], prm["moe_b2"], 0).reshape(b, t, d)

    x2, p = _inproj(x1, mod(1, 1), mod(1, 0), prm["ds_w_in"],
                    res=(f, mod(0, 5), prm["ln_g"][0, 1:2], prm["ln_b"][0, 1:2]))
    q, kf, vf, kif, kb, vb, kib, qi, w = _dsa_prep(p, pos)
    pos_np = np.asarray(pos)
    qlim = (pos_np // CHUNK + 1) * CHUNK
    if ds_cache is None:
        kpos = pos_np
        tq, tk = 128, 512
        if t % tk:
            tq, tk = t, t
        k_all, v_all, ki_all = kb, vb, kib
    else:
        ck, cv, cki = ds_cache
        past = ck.shape[1]
        total = past + t
        lp = -(-total // (3 * LANES)) * (3 * LANES)
        tq, tk = t, 3 * LANES
        kpos = np.concatenate([np.arange(past), pos_np, np.full((lp - total,), np.iinfo(np.int32).max)])

        def cat(cache, new):
            full = jnp.concatenate([cache.reshape(b, past, -1).astype(BF16), new], axis=1)
            return jnp.pad(full, ((0, 0), (0, lp - total), (0, 0)))

        k_all, v_all, ki_all = cat(ck, kb), cat(cv, vb), cat(cki, kib)
    o = _dsa_attend(q, qi, w, k_all, v_all, ki_all, kpos.astype(np.int64), qlim, tq, tk)
    x3, u2, te, tg = _outproj(o, prm["ds_w_out"], x2, mod(1, 2), mod(1, 4), mod(1, 3),
                              prm["ln_g"][1, 0:1], prm["ln_b"][1, 0:1], prm["w_router"][1], prm["b_router"][1])
    f = _moe(u2.reshape(b * t, d), te.reshape(b * t, TOP_K), tg.reshape(b * t, TOP_K),
             prm["moe_w1"], prm["moe_b1"], prm["moe_w2"], prm["moe_b2"], 1).reshape(b, t, d)
    y = _final_ln(x3, f, mod(1, 5), prm["ln_g"][1, 1:2], prm["ln_b"][1, 1:2])

    ml_out = (c_new[None], n_new[None], m_new[None])
    ds_out = (kf.reshape(1, b, t, DSA_KV_HEADS, DSA_HD), vf.reshape(1, b, t, DSA_KV_HEADS, DSA_HD), kif[None])
    return y, ml_out, ds_out


def _prepare(w_ada, b_ada, ln_g, ln_b, ml_w_in, ml_b_gates, ml_norm_g, ml_w_out, ds_w_in, ds_w_out,
             moe_w_router, moe_b_router, moe_w1, moe_b1, moe_w2, moe_b2):
    return {
        "ln_g": ln_g, "ln_b": ln_b,
        "ml_w_in": jnp.pad(ml_w_in[0], ((0, 0), (0, ML_IN_PAD - ML_IN))).astype(BF16),
        "ml_b_gates": ml_b_gates[0], "ml_norm_g": ml_norm_g[0],
        "ml_w_out": ml_w_out[0].astype(BF16),
        "ds_w_in": jnp.pad(ds_w_in[0], ((0, 0), (0, DSA_IN_PAD - DSA_IN))).astype(BF16),
        "ds_w_out": ds_w_out[0].astype(BF16),
        "w_router": jnp.pad(moe_w_router, ((0, 0), (0, 0), (0, LANES - N_EXPERTS))),
        "b_router": jnp.pad(moe_b_router, ((0, 0), (0, LANES - N_EXPERTS)), constant_values=NEG_BIG)[:, None, :],
        "moe_w1": moe_w1.astype(BF16), "moe_b1": moe_b1,
        "moe_w2": moe_w2.astype(BF16), "moe_b2": moe_b2,
    }


def kernel(x_prompt, x_sample, state_mlstm_C, state_mlstm_n, state_mlstm_m, cache_dsa_k, cache_dsa_v,
           cache_dsa_kidx, c_prompt, c_sample, w_ada, b_ada, ln_g, ln_b, ml_w_in, ml_b_gates, ml_norm_g,
           ml_w_out, ds_w_in, ds_w_out, moe_w_router, moe_b_router, moe_w1, moe_b1, moe_w2, moe_b2):
    prm = _prepare(w_ada, b_ada, ln_g, ln_b, ml_w_in, ml_b_gates, ml_norm_g, ml_w_out, ds_w_in, ds_w_out,
                   moe_w_router, moe_b_router, moe_w1, moe_b1, moe_w2, moe_b2)
    bp, bs = c_prompt.shape[0], c_sample.shape[0]
    rows = -(-(bp + bs) // 8) * 8
    c_rows = jnp.pad(jnp.concatenate([c_prompt, c_sample], axis=0), ((0, rows - bp - bs), (0, 0)))
    mods = _ada(c_rows, w_ada, b_ada)
    mods_p = [mods[i, :bp] for i in range(DEPTH)]
    mods_s = [mods[i, bp:bp + bs] for i in range(DEPTH)]

    pos_p = np.arange(x_prompt.shape[1], dtype=np.int32)
    y_p, ml_p, ds_p = _trunk(x_prompt, mods_p, pos_p, None, None, prm)
    past = cache_dsa_k.shape[2]
    pos_s = past + np.arange(x_sample.shape[1], dtype=np.int32)
    y_s, ml_s, ds_s = _trunk(x_sample, mods_s, pos_s, (state_mlstm_C, state_mlstm_n, state_mlstm_m),
                             (cache_dsa_k[0], cache_dsa_v[0], cache_dsa_kidx[0]), prm)
    return (y_p, y_s, ml_p[0], ml_p[1], ml_p[2], ds_p[0], ds_p[1], ds_p[2],
            ml_s[0], ml_s[1], ml_s[2], ds_s[0], ds_s[1], ds_s[2])
```

```python
import functools

import numpy as np
import jax
import jax.numpy as jnp
from jax import lax
from jax.experimental import pallas as pl
from jax.experimental.pallas import tpu as pltpu

F32 = jnp.float32
BF16 = jnp.bfloat16
I32 = jnp.int32

D_MODEL = 1024
DEPTH = 2
CHUNK = 64
ML_HEADS = 4
ML_DV = D_MODEL // ML_HEADS
ML_DK = ML_DV // 2
ML_QK = ML_HEADS * ML_DK
ML_V = ML_HEADS * ML_DV
ML_IN = 2 * ML_QK + 2 * ML_V + 2 * ML_HEADS
ML_GATE_OFF = 2 * ML_QK + 2 * ML_V
ML_EVAL_CHUNK = 256
DSA_HEADS = 8
DSA_KV_HEADS = 2
DSA_HD = D_MODEL // DSA_HEADS
DSA_GROUP = DSA_HEADS // DSA_KV_HEADS
IDX_HEADS = 8
IDX_DIM = 64
DSA_TOPK = 256
DSA_Q = DSA_HEADS * DSA_HD
DSA_KV = DSA_KV_HEADS * DSA_HD
DSA_IN = DSA_Q + 2 * DSA_KV + IDX_HEADS * IDX_DIM + IDX_DIM + IDX_HEADS
ROPE_THETA = 500000.0
ROPE_FRACTION = 4
N_EXPERTS = 32
TOP_K = 4
D_FF = D_MODEL
SWIGLU_LIMIT = 7.0
SWIGLU_ALPHA = 1.702
DEEPNORM_ALPHA = (2 * DEPTH) ** 0.25
LN_EPS = 1e-5

LANES = 128
VMEM_LIMIT = 56 * 1024 * 1024

ML_IN_PAD = -(-ML_IN // LANES) * LANES
DSA_IN_PAD = -(-DSA_IN // LANES) * LANES
NEG_BIG = -1e30
F32_MAX = float(np.finfo(np.float32).max)
HIGHEST = lax.Precision.HIGHEST
NT_DIMS = (((1,), (1,)), ((), ()))
TN_DIMS = (((0,), (0,)), ((), ()))


def _params(*sem):
    return pltpu.CompilerParams(dimension_semantics=sem, vmem_limit_bytes=VMEM_LIMIT)


def _row_tile(t, pref):
    return pref if t % pref == 0 else t


def _ada_kernel(c_ref, w_ref, b_ref, o_ref):
    c = c_ref[...]
    cond = (c * jax.nn.sigmoid(c)).astype(BF16)
    o_ref[0] = jnp.dot(cond, w_ref[0].astype(BF16), preferred_element_type=F32) + b_ref[0]


def _ada(c_rows, w_ada, b_ada):
    rows = c_rows.shape[0]
    n = w_ada.shape[-1]
    tn = 1536
    return pl.pallas_call(
        _ada_kernel,
        grid=(DEPTH, n // tn),
        in_specs=[
            pl.BlockSpec((rows, D_MODEL), lambda i, j: (0, 0)),
            pl.BlockSpec((1, D_MODEL, tn), lambda i, j: (i, 0, j)),
            pl.BlockSpec((1, 1, tn), lambda i, j: (i, 0, j)),
        ],
        out_specs=pl.BlockSpec((1, rows, tn), lambda i, j: (i, 0, j)),
        out_shape=jax.ShapeDtypeStruct((DEPTH, rows, n), F32),
        compiler_params=_params("arbitrary", "arbitrary"),
        name="ada_mod",
    )(c_rows, w_ada, b_ada.reshape(DEPTH, 1, n))


def _layer_norm(z, g, b):
    mu = jnp.mean(z, axis=-1, keepdims=True)
    zc = z - mu
    var = jnp.mean(zc * zc, axis=-1, keepdims=True)
    return zc * lax.rsqrt(var + LN_EPS) * g + b


def _inproj_kernel(*refs, has_res):
    if has_res:
        x_ref, f_ref, g_ref, lng_ref, lnb_ref, sc_ref, sh_ref, w_ref, xo_ref, p_ref = refs
        z = DEEPNORM_ALPHA * x_ref[0] + (1.0 + g_ref[0]) * f_ref[0]
        x = _layer_norm(z, lng_ref[...], lnb_ref[...])
        xo_ref[0] = x
    else:
        x_ref, sc_ref, sh_ref, w_ref, p_ref = refs
        x = x_ref[0]
    u = x * (1.0 + sc_ref[0]) + sh_ref[0]
    p_ref[0] = jnp.dot(u.astype(BF16), w_ref[...], preferred_element_type=F32)


def _inproj(x, sc, sh, w_bf, res=None):
    b, t, d = x.shape
    n = w_bf.shape[1]
    tm = _row_tile(t, 512)
    row = pl.BlockSpec((1, tm, d), lambda i, j: (i, j, 0))
    vec = pl.BlockSpec((1, 1, d), lambda i, j: (i, 0, 0))
    par = pl.BlockSpec((1, d), lambda i, j: (0, 0))
    wspec = pl.BlockSpec((d, n), lambda i, j: (0, 0))
    pspec = pl.BlockSpec((1, tm, n), lambda i, j: (i, j, 0))
    pshape = jax.ShapeDtypeStruct((b, t, n), F32)
    if res is None:
        return None, pl.pallas_call(
            functools.partial(_inproj_kernel, has_res=False),
            grid=(b, t // tm),
            in_specs=[row, vec, vec, wspec],
            out_specs=pspec,
            out_shape=pshape,
            compiler_params=_params("arbitrary", "arbitrary"),
            name="inproj",
        )(x, sc, sh, w_bf)
    f, g, lng, lnb = res
    return pl.pallas_call(
        functools.partial(_inproj_kernel, has_res=True),
        grid=(b, t // tm),
        in_specs=[row, row, vec, par, par, vec, vec, wspec],
        out_specs=[row, pspec],
        out_shape=[jax.ShapeDtypeStruct((b, t, d), F32), pshape],
        compiler_params=_params("arbitrary", "arbitrary"),
        name="ln_inproj",
    )(x, f, g, lng, lnb, sc, sh, w_bf)


def _final_ln_kernel(x_ref, f_ref, g_ref, lng_ref, lnb_ref, o_ref):
    z = DEEPNORM_ALPHA * x_ref[0] + (1.0 + g_ref[0]) * f_ref[0]
    o_ref[0] = _layer_norm(z, lng_ref[...], lnb_ref[...])


def _final_ln(x, f, g, lng, lnb):
    b, t, d = x.shape
    tm = _row_tile(t, 1024)
    row = pl.BlockSpec((1, tm, d), lambda i, j: (i, j, 0))
    vec = pl.BlockSpec((1, 1, d), lambda i, j: (i, 0, 0))
    par = pl.BlockSpec((1, d), lambda i, j: (0, 0))
    return pl.pallas_call(
        _final_ln_kernel,
        grid=(b, t // tm),
        in_specs=[row, row, vec, par, par],
        out_specs=row,
        out_shape=jax.ShapeDtypeStruct((b, t, d), F32),
        compiler_params=_params("arbitrary", "arbitrary"),
        name="final_ln",
    )(x, f, g, lng, lnb)


def _log_sigmoid(x):
    return jnp.minimum(x, 0.0) - jnp.log1p(jnp.exp(-jnp.abs(x)))


def _mlstm_kernel(p_ref, gt_ref, bcol_ref, brow_ref, ng_ref, c0_ref, n0_ref, m0_ref,
                  hg_ref, c_ref, n_ref, m_ref, *, chunk):
    L = chunk

    @pl.when(pl.program_id(1) == 0)
    def _():
        c_ref[...] = c0_ref[...]
        n_ref[...] = n0_ref[...]
        m_ref[...] = m0_ref[...]

    gcol = p_ref[0, :, ML_GATE_OFF:ML_GATE_OFF + 2 * ML_HEADS] + bcol_ref[...]
    grow = gt_ref[0] + brow_ref[...]
    lf_col = _log_sigmoid(gcol)
    lf_row = _log_sigmoid(grow)
    ti = lax.broadcasted_iota(I32, (L, L), 0)
    si = lax.broadcasted_iota(I32, (L, L), 1)
    causal = si <= ti
    tril = jnp.where(causal, 1.0, 0.0).astype(F32)
    triu = jnp.where(ti <= si, 1.0, 0.0).astype(F32)
    b_col = jnp.dot(tril, lf_col, precision=HIGHEST, preferred_element_type=F32)
    b_row = jnp.dot(lf_row, triu, precision=HIGHEST, preferred_element_type=F32)

    for h in range(ML_HEADS):
        q = p_ref[0, :, h * ML_DK:(h + 1) * ML_DK]
        k = p_ref[0, :, ML_QK + h * ML_DK:ML_QK + (h + 1) * ML_DK]
        v = p_ref[0, :, 2 * ML_QK + h * ML_DV:2 * ML_QK + (h + 1) * ML_DV]
        o = p_ref[0, :, 2 * ML_QK + ML_V + h * ML_DV:2 * ML_QK + ML_V + (h + 1) * ML_DV]
        qs = (q * (ML_DK ** -0.5)).astype(BF16)
        kb = k.astype(BF16)
        vb = v.astype(BF16)
        bc = b_col[:, ML_HEADS + h:ML_HEADS + h + 1]
        ic = gcol[:, h:h + 1]
        br = b_row[ML_HEADS + h:ML_HEADS + h + 1, :]
        ir = grow[h:h + 1, :]
        c_old = c_ref[0, h]
        n_old = n_ref[0, h:h + 1, :]
        m_old = m_ref[0, h:h + 1, 0:1]

        dm = jnp.where(causal, bc - br + ir, -jnp.inf)
        inter = bc + m_old
        mt = jnp.maximum(inter, jnp.max(dm, axis=-1, keepdims=True))
        qk = lax.dot_general(qs, kb, NT_DIMS, preferred_element_type=F32)
        s = jnp.exp(dm - mt) * qk
        wp = jnp.exp(inter - mt)
        qc = lax.dot_general(qs, c_old.astype(BF16), NT_DIMS, preferred_element_type=F32)
        num = jnp.dot(s.astype(BF16), vb, preferred_element_type=F32) + wp * qc
        qn = jnp.sum(qs.astype(F32) * n_old, axis=-1, keepdims=True)
        den = jnp.sum(s, axis=-1, keepdims=True) + wp * qn
        hh = num / jnp.maximum(jnp.abs(den), jnp.exp(-mt))

        m_new = mt[L - 1:L, :]
        wk = jnp.exp(bc[L - 1:L, :] - bc + ic - m_new)
        wprev = jnp.exp(inter[L - 1:L, :] - m_new)
        vw = (v * wk).astype(BF16)
        c_ref[0, h] = wprev * c_old + lax.dot_general(vw, kb, TN_DIMS, preferred_element_type=F32)
        n_ref[0, h:h + 1, :] = wprev * n_old + jnp.sum(wk * kb.astype(F32), axis=0, keepdims=True)
        m_ref[0, h:h + 1, :] = jnp.broadcast_to(m_new, (1, ML_DK))

        mu = jnp.mean(hh, axis=-1, keepdims=True)
        hc = hh - mu
        var = jnp.mean(hc * hc, axis=-1, keepdims=True)
        hn = hc * lax.rsqrt(var + LN_EPS) * ng_ref[:, h * ML_DV:(h + 1) * ML_DV]
        hg_ref[0, :, h * ML_DV:(h + 1) * ML_DV] = (jax.nn.sigmoid(o) * hn).astype(BF16)


def _mlstm(p, b_gates, norm_g, c0, n0, m0, chunk):
    b, t, n = p.shape
    nc = t // chunk
    gt = jnp.swapaxes(p[:, :, ML_GATE_OFF:ML_GATE_OFF + 2 * ML_HEADS], 1, 2)
    if nc > 1:
        gt_spec = pl.BlockSpec((1, 2 * ML_HEADS, chunk), lambda i, j: (i, 0, j))
    else:
        gt_spec = pl.BlockSpec((1, 2 * ML_HEADS, t), lambda i, j: (i, 0, 0))
    if nc > 1 and chunk % LANES != 0:
        gt = gt.reshape(b, 2 * ML_HEADS, nc, chunk).transpose(0, 2, 1, 3).reshape(b * nc, 2 * ML_HEADS, chunk)
        gt_spec = pl.BlockSpec((1, 2 * ML_HEADS, chunk), lambda i, j: (i * nc + j, 0, 0))
    m0b = jnp.broadcast_to(m0[..., None], (b, ML_HEADS, ML_DK))
    cspec = pl.BlockSpec((1, ML_HEADS, ML_DV, ML_DK), lambda i, j: (i, 0, 0, 0))
    nspec = pl.BlockSpec((1, ML_HEADS, ML_DK), lambda i, j: (i, 0, 0))
    hg, c, nn, m = pl.pallas_call(
        functools.partial(_mlstm_kernel, chunk=chunk),
        grid=(b, nc),
        in_specs=[
            pl.BlockSpec((1, chunk, n), lambda i, j: (i, j, 0)),
            gt_spec,
            pl.BlockSpec((1, 2 * ML_HEADS), lambda i, j: (0, 0)),
            pl.BlockSpec((2 * ML_HEADS, 1), lambda i, j: (0, 0)),
            pl.BlockSpec((1, ML_V), lambda i, j: (0, 0)),
            cspec, nspec, nspec,
        ],
        out_specs=[pl.BlockSpec((1, chunk, ML_V), lambda i, j: (i, j, 0)), cspec, nspec, nspec],
        out_shape=[
            jax.ShapeDtypeStruct((b, t, ML_V), BF16),
            jax.ShapeDtypeStruct((b, ML_HEADS, ML_DV, ML_DK), F32),
            jax.ShapeDtypeStruct((b, ML_HEADS, ML_DK), F32),
            jax.ShapeDtypeStruct((b, ML_HEADS, ML_DK), F32),
        ],
        compiler_params=_params("arbitrary", "arbitrary"),
        name="mlstm_scan",
    )(p, gt, b_gates.reshape(1, -1), b_gates.reshape(-1, 1), norm_g.reshape(1, -1), c0, n0, m0b)
    return hg, c, nn, m[..., 0]


def _outproj_kernel(a_ref, w_ref, x_ref, g_ref, sc_ref, sh_ref, lng_ref, lnb_ref, wrh_ref, wrl_ref, br_ref,
                    x1_ref, u2_ref, te_ref, tg_ref):
    y = jnp.dot(a_ref[0], w_ref[...], preferred_element_type=F32)
    z = DEEPNORM_ALPHA * x_ref[0] + (1.0 + g_ref[0]) * y
    x1 = _layer_norm(z, lng_ref[...], lnb_ref[...])
    x1_ref[0] = x1
    u2 = x1 * (1.0 + sc_ref[0]) + sh_ref[0]
    u_hi = u2.astype(BF16)
    u2_ref[0] = u_hi
    u_lo = (u2 - u_hi.astype(F32)).astype(BF16)
    logits = (jnp.dot(u_hi, wrh_ref[...], preferred_element_type=F32)
              + jnp.dot(u_lo, wrh_ref[...], preferred_element_type=F32)
              + jnp.dot(u_hi, wrl_ref[...], preferred_element_type=F32)) + br_ref[...]
    lane = lax.broadcasted_iota(I32, logits.shape, 1).astype(F32)
    vals, idxs = [], []
    cur = logits
    for _ in range(TOP_K):
        mx = jnp.max(cur, axis=-1, keepdims=True)
        idx = jnp.min(jnp.where(cur == mx, lane, float(LANES)), axis=-1, keepdims=True)
        vals.append(mx)
        idxs.append(idx)
        cur = jnp.where(lane == idx, -jnp.inf, cur)
    es = [jnp.exp(v - vals[0]) for v in vals]
    tot = es[0] + es[1] + es[2] + es[3]
    for k in range(TOP_K):
        te_ref[0, :, k:k + 1] = idxs[k].astype(I32)
        tg_ref[0, :, k:k + 1] = es[k] / tot


def _outproj(a, w_bf, x, g, sc, sh, lng, lnb, wr_pad, br_pad):
    b, t, d = x.shape
    tm = _row_tile(t, 256)
    wr_hi = wr_pad.astype(BF16)
    wr_lo = (wr_pad - wr_hi.astype(F32)).astype(BF16)
    row = pl.BlockSpec((1, tm, d), lambda i, j: (i, j, 0))
    vec = pl.BlockSpec((1, 1, d), lambda i, j: (i, 0, 0))
    par = pl.BlockSpec((1, d), lambda i, j: (0, 0))
    top = pl.BlockSpec((1, tm, TOP_K), lambda i, j: (i, j, 0))
    return pl.pallas_call(
        _outproj_kernel,
        grid=(b, t // tm),
        in_specs=[row, pl.BlockSpec((d, d), lambda i, j: (0, 0)), row, vec, vec, vec, par, par,
                  pl.BlockSpec((d, LANES), lambda i, j: (0, 0)), pl.BlockSpec((d, LANES), lambda i, j: (0, 0)),
                  pl.BlockSpec((1, LANES), lambda i, j: (0, 0))],
        out_specs=[row, row, top, top],
        out_shape=[jax.ShapeDtypeStruct((b, t, d), F32), jax.ShapeDtypeStruct((b, t, d), BF16),
                   jax.ShapeDtypeStruct((b, t, TOP_K), I32), jax.ShapeDtypeStruct((b, t, TOP_K), F32)],
        compiler_params=_params("arbitrary", "arbitrary"),
        name="outproj_ln_router",
    )(a, w_bf, x, g, sc, sh, lng, lnb, wr_hi, wr_lo, br_pad)


def _moe_kernel(x_ref, te_ref, tg_ref, w1_ref, b1_ref, w2_ref, b2_ref, o_ref, pos_ref, tri_ref, *, tm, rb):
    e = pl.program_id(1)

    @pl.when(jnp.logical_and(pl.program_id(0) == 0, e == 0))
    def _():
        si = lax.broadcasted_iota(I32, (tm, tm), 0)
        ti = lax.broadcasted_iota(I32, (tm, tm), 1)
        tri_ref[...] = jnp.where(si <= ti, 1.0, 0.0).astype(BF16)

    @pl.when(e == 0)
    def _():
        o_ref[...] = jnp.zeros_like(o_ref)
        eio = lax.broadcasted_iota(I32, (N_EXPERTS, tm), 0)
        sel = jnp.zeros((N_EXPERTS, tm), F32)
        for k in range(TOP_K):
            sel = sel + jnp.where(te_ref[k:k + 1, :] == eio, 1.0, 0.0)
        rank = jnp.dot(sel.astype(BF16), tri_ref[...], preferred_element_type=F32)
        pos_ref[...] = rank * sel

    pm = pos_ref[pl.ds(e, 1), :]
    gate = jnp.zeros((1, tm), F32)
    for k in range(TOP_K):
        gate = gate + jnp.where(te_ref[k:k + 1, :] == e, tg_ref[k:k + 1, :], 0.0)
    cnt = jnp.max(pm).astype(I32)
    nblk = (cnt + (rb - 1)) // rb

    def body(blk, carry):
        r = (blk * rb + 1 + lax.broadcasted_iota(I32, (rb, 1), 0)).astype(F32)
        hit = pm == r
        onehot = jnp.where(hit, 1.0, 0.0).astype(BF16)
        xg = jnp.dot(onehot, x_ref[...], preferred_element_type=F32).astype(BF16)
        h = jnp.dot(xg, w1_ref[0], preferred_element_type=F32) + b1_ref[0]
        hg = jnp.minimum(h[:, :D_FF], SWIGLU_LIMIT)
        hl = jnp.clip(h[:, D_FF:], -SWIGLU_LIMIT, SWIGLU_LIMIT)
        a = hg * jax.nn.sigmoid(SWIGLU_ALPHA * hg) * (hl + 1.0)
        y = jnp.dot(a.astype(BF16), w2_ref[0], preferred_element_type=F32) + b2_ref[0]
        grow = jnp.sum(jnp.where(hit, gate, 0.0), axis=-1, keepdims=True)
        yg = (y * grow).astype(BF16)
        o_ref[...] += lax.dot_general(onehot, yg, TN_DIMS, preferred_element_type=F32)
        return carry

    lax.fori_loop(0, nblk, body, 0)


def _moe(u2, te, tg, w1_bf, b1, w2_bf, b2, layer):
    n, d = u2.shape
    tm = _row_tile(n, 1024)
    rb = min(tm, tm * TOP_K // N_EXPERTS + 32)
    te_t = te.T
    tg_t = tg.T
    return pl.pallas_call(
        functools.partial(_moe_kernel, tm=tm, rb=rb),
        grid=(n // tm, N_EXPERTS),
        in_specs=[
            pl.BlockSpec((tm, d), lambda i, e: (i, 0)),
            pl.BlockSpec((TOP_K, tm), lambda i, e: (0, i)),
            pl.BlockSpec((TOP_K, tm), lambda i, e: (0, i)),
            pl.BlockSpec((None, 1, d, 2 * D_FF), lambda i, e: (layer, e, 0, 0)),
            pl.BlockSpec((None, 1, 1, 2 * D_FF), lambda i, e: (layer, e, 0, 0)),
            pl.BlockSpec((None, 1, D_FF, d), lambda i, e: (layer, e, 0, 0)),
            pl.BlockSpec((None, 1, 1, d), lambda i, e: (layer, e, 0, 0)),
        ],
        out_specs=pl.BlockSpec((tm, d), lambda i, e: (i, 0)),
        out_shape=jax.ShapeDtypeStruct((n, d), F32),
        scratch_shapes=[pltpu.VMEM((N_EXPERTS, tm), F32), pltpu.VMEM((tm, tm), BF16)],
        compiler_params=_params("arbitrary", "arbitrary"),
        name="moe_experts",
    )(u2, te_t, tg_t, w1_bf, b1.reshape(DEPTH, N_EXPERTS, 1, -1), w2_bf, b2.reshape(DEPTH, N_EXPERTS, 1, -1))


def _rope_tables(pos, head_dim):
    rd = head_dim // ROPE_FRACTION
    half = rd // 2
    inv = jnp.power(ROPE_THETA, -jnp.arange(half, dtype=F32) / half)
    ang = jnp.asarray(pos).astype(F32)[:, None] * inv[None, :]
    cos, sin = jnp.cos(ang), jnp.sin(ang)
    t = pos.shape[0]
    ones = jnp.ones((t, head_dim - rd), F32)
    zeros = jnp.zeros((t, head_dim - rd), F32)
    zh = jnp.zeros((t, half), F32)
    c = jnp.concatenate([cos, cos, ones], axis=-1)
    sa = jnp.concatenate([zh, sin, zeros], axis=-1)
    sb = jnp.concatenate([-sin, zh, zeros], axis=-1)
    rep = LANES // head_dim
    return tuple(jnp.tile(a, (1, rep)) for a in (c, sa, sb))


def _rope(z, c, sa, sb, half):
    w = z.shape[-1]
    rep = w // LANES
    if rep > 1:
        c, sa, sb = (jnp.concatenate([a] * rep, axis=-1) for a in (c, sa, sb))
    return z * c + pltpu.roll(z, half, 1) * sa + pltpu.roll(z, w - half, 1) * sb


def _dsa_prep_kernel(p_ref, c1_ref, sa1_ref, sb1_ref, c2_ref, sa2_ref, sb2_ref,
                     q_ref, kf_ref, vf_ref, kif_ref, kb_ref, vb_ref, kib_ref, qi_ref, w_ref):
    c1, sa1, sb1 = c1_ref[...], sa1_ref[...], sb1_ref[...]
    c2, sa2, sb2 = c2_ref[...], sa2_ref[...], sb2_ref[...]
    h1 = DSA_HD // ROPE_FRACTION // 2
    h2 = IDX_DIM // ROPE_FRACTION // 2
    o_k = DSA_Q
    o_v = DSA_Q + DSA_KV
    o_qi = DSA_Q + 2 * DSA_KV
    o_ki = o_qi + IDX_HEADS * IDX_DIM
    q = _rope(p_ref[0, :, :DSA_Q], c1, sa1, sb1, h1)
    q_ref[0] = q.astype(BF16)
    k = _rope(p_ref[0, :, o_k:o_v], c1, sa1, sb1, h1)
    kf_ref[0] = k
    kb_ref[0] = k.astype(BF16)
    v = p_ref[0, :, o_v:o_qi]
    vf_ref[0] = v
    vb_ref[0] = v.astype(BF16)
    qi = _rope(p_ref[0, :, o_qi:o_ki], c2, sa2, sb2, h2)
    qi_ref[0] = qi.astype(BF16)
    tail = p_ref[0, :, o_ki:o_ki + LANES]
    ki = _rope(tail, c2, sa2, sb2, h2)[:, :IDX_DIM]
    kif_ref[0] = ki
    kib_ref[0] = ki.astype(BF16)
    wi = tail[:, IDX_DIM:IDX_DIM + IDX_HEADS]
    w_ref[0] = (wi * (IDX_HEADS ** -0.5)) * (IDX_DIM ** -0.5)


def _dsa_prep(p, pos):
    b, t, n = p.shape
    tm = _row_tile(t, 512)
    tabs = _rope_tables(pos, DSA_HD) + _rope_tables(pos, IDX_DIM)
    tab = pl.BlockSpec((tm, LANES), lambda i, j: (j, 0))

    def row(w):
        return pl.BlockSpec((1, tm, w), lambda i, j: (i, j, 0))

    widths = [(DSA_Q, BF16), (DSA_KV, F32), (DSA_KV, F32), (IDX_DIM, F32), (DSA_KV, BF16), (DSA_KV, BF16),
              (IDX_DIM, BF16), (IDX_HEADS * IDX_DIM, BF16), (IDX_HEADS, F32)]
    return pl.pallas_call(
        _dsa_prep_kernel,
        grid=(b, t // tm),
        in_specs=[row(n)] + [tab] * 6,
        out_specs=[row(w) for w, _ in widths],
        out_shape=[jax.ShapeDtypeStruct((b, t, w), dt) for w, dt in widths],
        compiler_params=_params("arbitrary", "arbitrary"),
        name="dsa_rope_split",
    )(p, *tabs)


def _f2key(f):
    bits = lax.bitcast_convert_type(f, I32)
    return bits ^ ((bits >> 31) & 0x7FFFFFFF)


def _key2f(k):
    return lax.bitcast_convert_type(k ^ ((k >> 31) & 0x7FFFFFFF), F32)


def _dsa_kernel(nkt_ref, q_ref, qi_ref, w_ref, k_ref, v_ref, ki_ref, kpos_ref, qlim_ref, o_ref,
                s_ref, wb_ref, qg_ref, m_ref, l_ref, acc_ref, flag_ref, *, tq, tk, nsel, value_iters):
    nkt = nkt_ref[pl.program_id(1)]
    qlim = qlim_ref[...]
    w = w_ref[0]
    qi = qi_ref[0]
    qis = [qi[:, h * IDX_DIM:(h + 1) * IDX_DIM] for h in range(IDX_HEADS)]
    nlg = tk // LANES
    for h in range(IDX_HEADS):
        wb_ref[h] = jnp.broadcast_to(w[:, h:h + 1], (tq, LANES))

    def p1(kt, carry):
        nadm, rmax, rmin, cgt0, cge0 = carry
        off = pl.multiple_of(kt * tk, tk)
        kit = ki_ref[0, pl.ds(off, tk), :]
        acc = jnp.zeros((tq, tk), F32)
        for h in range(IDX_HEADS):
            d = lax.dot_general(qis[h], kit, NT_DIMS, preferred_element_type=F32)
            acc = acc + jnp.maximum(d, 0.0) * jnp.concatenate([wb_ref[h]] * nlg, axis=1)
        adm = kpos_ref[kt] < qlim
        s = jnp.where(adm, acc, -jnp.inf)
        s_ref[kt] = s
        nadm = nadm + jnp.sum(jnp.where(adm, 1.0, 0.0), axis=-1, keepdims=True)
        rmax = jnp.maximum(rmax, jnp.max(s, axis=-1, keepdims=True))
        rmin = jnp.minimum(rmin, jnp.min(jnp.where(adm, acc, jnp.inf), axis=-1, keepdims=True))
        cgt0 = cgt0 + jnp.sum(jnp.where(s > 0.0, 1.0, 0.0), axis=-1, keepdims=True)
        cge0 = cge0 + jnp.sum(jnp.where(s >= 0.0, 1.0, 0.0), axis=-1, keepdims=True)
        return nadm, rmax, rmin, cgt0, cge0

    zcol = jnp.zeros((tq, 1), F32)
    init = (zcol, jnp.full((tq, 1), -jnp.inf, F32), jnp.full((tq, 1), jnp.inf, F32), zcol, zcol)
    nadm, rmax, rmin, cgt0, cge0 = lax.fori_loop(0, nkt, p1, init)

    rg = min(tq, 32)
    fsel = float(nsel)
    unbounded = float(2 ** 30)

    def count_ge(t):
        tb = jnp.broadcast_to(t, (tq, LANES))

        def body(kt, c):
            out = []
            for r in range(tq // rg):
                tr = jnp.concatenate([tb[r * rg:(r + 1) * rg]] * nlg, axis=1)
                hit = jnp.where(s_ref[kt, r * rg:(r + 1) * rg, :] >= tr, 1.0, 0.0)
                cr = c[r * rg:(r + 1) * rg]
                for g in range(nlg):
                    cr = cr + hit[:, g * LANES:(g + 1) * LANES]
                out.append(cr)
            return jnp.concatenate(out, axis=0)
        c = lax.fori_loop(0, nkt, body, jnp.zeros((tq, LANES), F32))
        return jnp.sum(c, axis=-1, keepdims=True)

    few = nadm <= fsel
    many = jnp.logical_not(few)
    pos = jnp.logical_and(many, cgt0 > fsel)
    neg = jnp.logical_and(many, cge0 < fsel)
    zero = jnp.logical_and(many, jnp.logical_and(cgt0 <= fsel, cge0 >= fsel))
    lo0 = jnp.where(pos, 0, _f2key(jnp.where(neg, rmin, 0.0)))
    hi0 = jnp.where(pos, _f2key(jnp.where(pos, rmax, 0.0)) + 1, 0)
    clo0 = jnp.where(pos, cge0, nadm)
    chi0 = jnp.where(pos, 0.0, cge0)
    act0 = jnp.where(jnp.logical_or(pos, neg), 1.0, 0.0)
    t0 = jnp.where(few, -jnp.inf, 0.0)
    need0 = jnp.where(zero, fsel - cgt0, unbounded)

    def cond(st):
        return st[0] > 0.0

    def step(st):
        _, it, lo, hi, clo, chi, act, thr, need = st
        conv = jnp.logical_and(act > 0.0, lo + 1 >= hi)
        thr = jnp.where(conv, _key2f(lo), thr)
        need = jnp.where(conv, fsel - chi, need)
        act = jnp.where(conv, 0.0, act)
        flo, fhi = _key2f(lo), _key2f(hi)
        mid_i = (lo >> 1) + (hi >> 1) + (lo & hi & 1)
        lclo = jnp.log(clo)
        interp = (lclo - float(np.log(nsel - 0.5))) / (lclo - jnp.log(jnp.maximum(chi, 0.5)))
        frac = jnp.where(it % 3 == 2, 0.5, interp)
        mid_v = _f2key(flo + frac * (fhi - flo))
        use_v = jnp.logical_and(it < value_iters, jnp.logical_and(mid_v > lo, mid_v < hi))
        mid = jnp.where(use_v, mid_v, mid_i)
        tm = _key2f(mid)
        c = count_ge(jnp.where(act > 0.0, tm, jnp.inf))
        hit = jnp.logical_and(act > 0.0, c == fsel)
        thr = jnp.where(hit, tm, thr)
        up = jnp.logical_and(act > 0.0, c > fsel)
        dn = jnp.logical_and(act > 0.0, c < fsel)
        lo = jnp.where(up, mid, lo)
        clo = jnp.where(up, c, clo)
        hi = jnp.where(dn, mid, hi)
        chi = jnp.where(dn, c, chi)
        act = jnp.where(hit, 0.0, act)
        return jnp.max(act), it + 1, lo, hi, clo, chi, act, thr, need

    st = (jnp.max(act0), jnp.int32(0), lo0, hi0, clo0, chi0, act0, t0, need0)
    _, _, _, _, _, _, _, thr, need = lax.while_loop(cond, step, st)
    tied = need < unbounded * 0.5
    has_ties = jnp.max(jnp.where(tied, 1.0, 0.0)) > 0.0
    thr = jnp.maximum(thr, -F32_MAX)

    for g in range(DSA_KV_HEADS):
        for hh in range(DSA_GROUP):
            hd = g * DSA_GROUP + hh
            qg_ref[g, hh * tq:(hh + 1) * tq, :] = q_ref[0, :, hd * DSA_HD:(hd + 1) * DSA_HD]
    m_ref[...] = jnp.full(m_ref.shape, NEG_BIG, F32)
    l_ref[...] = jnp.zeros(l_ref.shape, F32)
    acc_ref[...] = jnp.zeros(acc_ref.shape, F32)

    log2_scale = (DSA_HD ** -0.5) * float(np.log2(np.e))
    ones_cols = jnp.ones((tk, DSA_HD), BF16)

    def attend(kt, bias):
        off = pl.multiple_of(kt * tk, tk)
        bias = jnp.concatenate([bias] * DSA_GROUP, axis=0)
        qk = [lax.dot_general(qg_ref[g], k_ref[0, pl.ds(off, tk), g * DSA_HD:(g + 1) * DSA_HD], NT_DIMS,
                              preferred_element_type=F32) for g in range(DSA_KV_HEADS)]
        for g in range(DSA_KV_HEADS):
            vt_g = v_ref[0, pl.ds(off, tk), g * DSA_HD:(g + 1) * DSA_HD]
            lg = qk[g] * log2_scale + bias
            m_old = m_ref[g]
            m_new = jnp.maximum(m_old, jnp.max(lg, axis=-1, keepdims=True))
            alpha = jnp.exp2(m_old - m_new)
            p = jnp.exp2(lg - jnp.concatenate([m_new] * nlg, axis=1))
            pv = jnp.dot(p.astype(BF16), jnp.concatenate([vt_g, ones_cols], axis=1),
                         preferred_element_type=F32)
            acc_ref[g] = alpha * acc_ref[g] + pv[:, :DSA_HD]
            l_ref[g] = alpha * l_ref[g] + pv[:, DSA_HD:]
            m_ref[g] = m_new

    @pl.when(jnp.logical_not(has_ties))
    def _():
        def body(kt, carry):
            attend(kt, jnp.where(s_ref[kt] >= thr, 0.0, NEG_BIG))
            return carry
        lax.fori_loop(0, nkt, body, 0)

    @pl.when(has_ties)
    def _():
        def scan(kt, carry):
            seen, cut, nin = carry
            eqc = jnp.sum(jnp.where(s_ref[kt] == thr, 1.0, 0.0), axis=-1, keepdims=True)
            here = jnp.logical_and(tied, jnp.logical_and(seen < need, seen + eqc >= need))
            cut = jnp.where(here, kt, cut)
            nin = jnp.where(here, need - seen, nin)
            flag_ref[kt] = jnp.max(jnp.where(here, 1.0, 0.0)).astype(I32)
            return seen + eqc, cut, nin

        cut0 = jnp.where(tied, -1, nkt).astype(I32)
        _, cut, nin = lax.fori_loop(0, nkt, scan, (zcol, cut0, zcol))

        def body(kt, carry):
            s = s_ref[kt]
            before = jnp.where(kt < cut, 0.0, NEG_BIG)

            def plain():
                return jnp.where(s == thr, before, NEG_BIG)

            def ranked():
                si = lax.broadcasted_iota(I32, (tk, tk), 0)
                ti = lax.broadcasted_iota(I32, (tk, tk), 1)
                triu = jnp.where(si <= ti, 1.0, 0.0).astype(BF16)
                eq = jnp.where(s == thr, 1.0, 0.0).astype(BF16)
                rank = jnp.dot(eq, triu, preferred_element_type=F32)
                inside = jnp.where(rank <= nin, 0.0, NEG_BIG)
                return jnp.where(s == thr, jnp.where(kt == cut, inside, before), NEG_BIG)

            tie_bias = lax.cond(flag_ref[kt] > 0, ranked, plain)
            attend(kt, jnp.where(s > thr, 0.0, tie_bias))
            return carry
        lax.fori_loop(0, nkt, body, 0)

    for g in range(DSA_KV_HEADS):
        og = acc_ref[g] / l_ref[g]
        for hh in range(DSA_GROUP):
            hd = g * DSA_GROUP + hh
            o_ref[0, :, hd * DSA_HD:(hd + 1) * DSA_HD] = og[hh * tq:(hh + 1) * tq, :].astype(BF16)


def _dsa_attend(q, qi, w, k, v, ki, kpos, qlim, tq, tk):
    b, t, _ = q.shape
    lp = k.shape[1]
    nq = t // tq
    nsel = min(DSA_TOPK, int(np.sum(kpos < np.iinfo(np.int32).max)) // 4)
    qmax = qlim.reshape(nq, tq).max(axis=1)
    nkt = np.minimum(-(-np.searchsorted(kpos, qmax, side="left") // tk), lp // tk).astype(np.int32)
    nkt = np.maximum(nkt, 1)
    single = pl.Buffered(1)

    def full(wd):
        return pl.BlockSpec((1, lp, wd), lambda i, j, n: (i, 0, 0), pipeline_mode=single)

    def row(wd):
        return pl.BlockSpec((1, tq, wd), lambda i, j, n: (i, j, 0))

    grid_spec = pltpu.PrefetchScalarGridSpec(
        num_scalar_prefetch=1,
        grid=(b, nq),
        in_specs=[row(DSA_Q), row(IDX_HEADS * IDX_DIM), row(IDX_HEADS), full(DSA_KV), full(DSA_KV), full(IDX_DIM),
                  pl.BlockSpec((lp // tk, 1, tk), lambda i, j, n: (0, 0, 0)),
                  pl.BlockSpec((tq, 1), lambda i, j, n: (j, 0))],
        out_specs=row(DSA_Q),
        scratch_shapes=[
            pltpu.VMEM((lp // tk, tq, tk), F32),
            pltpu.VMEM((IDX_HEADS, tq, LANES), F32),
            pltpu.VMEM((DSA_KV_HEADS, DSA_GROUP * tq, DSA_HD), BF16),
            pltpu.VMEM((DSA_KV_HEADS, DSA_GROUP * tq, LANES), F32),
            pltpu.VMEM((DSA_KV_HEADS, DSA_GROUP * tq, LANES), F32),
            pltpu.VMEM((DSA_KV_HEADS, DSA_GROUP * tq, DSA_HD), F32),
            pltpu.SMEM((lp // tk,), I32),
        ],
    )
    return pl.pallas_call(
        functools.partial(_dsa_kernel, tq=tq, tk=tk, nsel=nsel, value_iters=16),
        grid_spec=grid_spec,
        out_shape=jax.ShapeDtypeStruct((b, t, DSA_Q), BF16),
        compiler_params=_params("arbitrary", "arbitrary"),
        name="dsa_select_attend",
    )(jnp.asarray(nkt), q, qi, w, k, v, ki, jnp.asarray(kpos, I32).reshape(lp // tk, 1, tk),
      jnp.asarray(qlim, I32).reshape(t, 1))


def _trunk(x, mods, pos, ml_state, ds_cache, prm):
    b, t, d = x.shape
    chunk = next((c for c in (ML_EVAL_CHUNK, CHUNK) if t % c == 0), t)

    def mod(i, k):
        return mods[i][:, None, k * d:(k + 1) * d]

    _, p = _inproj(x, mod(0, 1), mod(0, 0), prm["ml_w_in"])
    if ml_state is None:
        c0 = jnp.zeros((b, ML_HEADS, ML_DV, ML_DK), F32)
        n0 = jnp.zeros((b, ML_HEADS, ML_DK), F32)
        m0 = jnp.zeros((b, ML_HEADS), F32)
    else:
        c0, n0, m0 = (s[0] for s in ml_state)
    hg, c_new, n_new, m_new = _mlstm(p, prm["ml_b_gates"], prm["ml_norm_g"], c0, n0, m0, chunk)
    x1, u2, te, tg = _outproj(hg, prm["ml_w_out"], x, mod(0, 2), mod(0, 4), mod(0, 3),
                              prm["ln_g"][0, 0:1], prm["ln_b"][0, 0:1], prm["w_router"][0], prm["b_router"][0])
    f = _moe(u2.reshape(b * t, d), te.reshape(b * t, TOP_K), tg.reshape(b * t, TOP_K),
             prm["moe_w1"], prm["moe_b1"], prm["moe_w2"], prm["moe_b2"], 0).reshape(b, t, d)

    x2, p = _inproj(x1, mod(1, 1), mod(1, 0), prm["ds_w_in"],
                    res=(f, mod(0, 5), prm["ln_g"][0, 1:2], prm["ln_b"][0, 1:2]))
    q, kf, vf, kif, kb, vb, kib, qi, w = _dsa_prep(p, pos)
    pos_np = np.asarray(pos)
    qlim = (pos_np // CHUNK + 1) * CHUNK
    if ds_cache is None:
        kpos = pos_np
        tq, tk = 128, 1024
        if t % tk:
            tq, tk = t, t
        k_all, v_all, ki_all = kb, vb, kib
    else:
        ck, cv, cki = ds_cache
        past = ck.shape[1]
        total = past + t
        lp = -(-total // (3 * LANES)) * (3 * LANES)
        tq, tk = t, 3 * LANES
        kpos = np.concatenate([np.arange(past), pos_np, np.full((lp - total,), np.iinfo(np.int32).max)])

        def cat(cache, new):
            full = jnp.concatenate([cache.reshape(b, past, -1).astype(BF16), new], axis=1)
            return jnp.pad(full, ((0, 0), (0, lp - total), (0, 0)))

        k_all, v_all, ki_all = cat(ck, kb), cat(cv, vb), cat(cki, kib)
    o = _dsa_attend(q, qi, w, k_all, v_all, ki_all, kpos.astype(np.int64), qlim, tq, tk)
    x3, u2, te, tg = _outproj(o, prm["ds_w_out"], x2, mod(1, 2), mod(1, 4), mod(1, 3),
                              prm["ln_g"][1, 0:1], prm["ln_b"][1, 0:1], prm["w_router"][1], prm["b_router"][1])
    f = _moe(u2.reshape(b * t, d), te.reshape(b * t, TOP_K), tg.reshape(b * t, TOP_K),
             prm["moe_w1"], prm["moe_b1"], prm["moe_w2"], prm["moe_b2"], 1).reshape(b, t, d)
    y = _final_ln(x3, f, mod(1, 5), prm["ln_g"][1, 1:2], prm["ln_b"][1, 1:2])

    ml_out = (c_new[None], n_new[None], m_new[None])
    ds_out = (kf.reshape(1, b, t, DSA_KV_HEADS, DSA_HD), vf.reshape(1, b, t, DSA_KV_HEADS, DSA_HD), kif[None])
    return y, ml_out, ds_out


def _prepare(w_ada, b_ada, ln_g, ln_b, ml_w_in, ml_b_gates, ml_norm_g, ml_w_out, ds_w_in, ds_w_out,
             moe_w_router, moe_b_router, moe_w1, moe_b1, moe_w2, moe_b2):
    return {
        "ln_g": ln_g, "ln_b": ln_b,
        "ml_w_in": jnp.pad(ml_w_in[0], ((0, 0), (0, ML_IN_PAD - ML_IN))).astype(BF16),
        "ml_b_gates": ml_b_gates[0], "ml_norm_g": ml_norm_g[0],
        "ml_w_out": ml_w_out[0].astype(BF16),
        "ds_w_in": jnp.pad(ds_w_in[0], ((0, 0), (0, DSA_IN_PAD - DSA_IN))).astype(BF16),
        "ds_w_out": ds_w_out[0].astype(BF16),
        "w_router": jnp.pad(moe_w_router, ((0, 0), (0, 0), (0, LANES - N_EXPERTS))),
        "b_router": jnp.pad(moe_b_router, ((0, 0), (0, LANES - N_EXPERTS)), constant_values=NEG_BIG)[:, None, :],
        "moe_w1": moe_w1.astype(BF16), "moe_b1": moe_b1,
        "moe_w2": moe_w2.astype(BF16), "moe_b2": moe_b2,
    }


def kernel(x_prompt, x_sample, state_mlstm_C, state_mlstm_n, state_mlstm_m, cache_dsa_k, cache_dsa_v,
           cache_dsa_kidx, c_prompt, c_sample, w_ada, b_ada, ln_g, ln_b, ml_w_in, ml_b_gates, ml_norm_g,
           ml_w_out, ds_w_in, ds_w_out, moe_w_router, moe_b_router, moe_w1, moe_b1, moe_w2, moe_b2):
    prm = _prepare(w_ada, b_ada, ln_g, ln_b, ml_w_in, ml_b_gates, ml_norm_g, ml_w_out, ds_w_in, ds_w_out,
                   moe_w_router, moe_b_router, moe_w1, moe_b1, moe_w2, moe_b2)
    bp, bs = c_prompt.shape[0], c_sample.shape[0]
    rows = -(-(bp + bs) // 8) * 8
    c_rows = jnp.pad(jnp.concatenate([c_prompt, c_sample], axis=0), ((0, rows - bp - bs), (0, 0)))
    mods = _ada(c_rows, w_ada, b_ada)
    mods_p = [mods[i, :bp] for i in range(DEPTH)]
    mods_s = [mods[i, bp:bp + bs] for i in range(DEPTH)]

    pos_p = np.arange(x_prompt.shape[1], dtype=np.int32)
    y_p, ml_p, ds_p = _trunk(x_prompt, mods_p, pos_p, None, None, prm)
    past = cache_dsa_k.shape[2]
    pos_s = past + np.arange(x_sample.shape[1], dtype=np.int32)
    y_s, ml_s, ds_s = _trunk(x_sample, mods_s, pos_s, (state_mlstm_C, state_mlstm_n, state_mlstm_m),
                             (cache_dsa_k[0], cache_dsa_v[0], cache_dsa_kidx[0]), prm)
    return (y_p, y_s, ml_p[0], ml_p[1], ml_p[2], ds_p[0], ds_p[1], ds_p[2],
            ml_s[0], ml_s[1], ml_s[2], ds_s[0], ds_s[1], ds_s[2])
```

```python
import functools

import numpy as np
import jax
import jax.numpy as jnp
from jax import lax
from jax.experimental import pallas as pl
from jax.experimental.pallas import tpu as pltpu

F32 = jnp.float32
BF16 = jnp.bfloat16
I32 = jnp.int32

D_MODEL = 1024
DEPTH = 2
CHUNK = 64
ML_HEADS = 4
ML_DV = D_MODEL // ML_HEADS
ML_DK = ML_DV // 2
ML_QK = ML_HEADS * ML_DK
ML_V = ML_HEADS * ML_DV
ML_IN = 2 * ML_QK + 2 * ML_V + 2 * ML_HEADS
ML_GATE_OFF = 2 * ML_QK + 2 * ML_V
ML_EVAL_CHUNK = 256
DSA_HEADS = 8
DSA_KV_HEADS = 2
DSA_HD = D_MODEL // DSA_HEADS
DSA_GROUP = DSA_HEADS // DSA_KV_HEADS
IDX_HEADS = 8
IDX_DIM = 64
DSA_TOPK = 256
DSA_Q = DSA_HEADS * DSA_HD
DSA_KV = DSA_KV_HEADS * DSA_HD
DSA_IN = DSA_Q + 2 * DSA_KV + IDX_HEADS * IDX_DIM + IDX_DIM + IDX_HEADS
ROPE_THETA = 500000.0
ROPE_FRACTION = 4
N_EXPERTS = 32
TOP_K = 4
D_FF = D_MODEL
SWIGLU_LIMIT = 7.0
SWIGLU_ALPHA = 1.702
DEEPNORM_ALPHA = (2 * DEPTH) ** 0.25
LN_EPS = 1e-5

LANES = 128
VMEM_LIMIT = 56 * 1024 * 1024

ML_IN_PAD = -(-ML_IN // LANES) * LANES
DSA_IN_PAD = -(-DSA_IN // LANES) * LANES
NEG_BIG = -1e30
F32_MAX = float(np.finfo(np.float32).max)
HIGHEST = lax.Precision.HIGHEST
NT_DIMS = (((1,), (1,)), ((), ()))
TN_DIMS = (((0,), (0,)), ((), ()))


def _params(*sem):
    return pltpu.CompilerParams(dimension_semantics=sem, vmem_limit_bytes=VMEM_LIMIT)


def _row_tile(t, pref):
    return pref if t % pref == 0 else t


def _ada_kernel(c_ref, w_ref, b_ref, o_ref):
    c = c_ref[...]
    cond = (c * jax.nn.sigmoid(c)).astype(BF16)
    o_ref[0] = jnp.dot(cond, w_ref[0].astype(BF16), preferred_element_type=F32) + b_ref[0]


def _ada(c_rows, w_ada, b_ada):
    rows = c_rows.shape[0]
    n = w_ada.shape[-1]
    tn = 1536
    return pl.pallas_call(
        _ada_kernel,
        grid=(DEPTH, n // tn),
        in_specs=[
            pl.BlockSpec((rows, D_MODEL), lambda i, j: (0, 0)),
            pl.BlockSpec((1, D_MODEL, tn), lambda i, j: (i, 0, j)),
            pl.BlockSpec((1, 1, tn), lambda i, j: (i, 0, j)),
        ],
        out_specs=pl.BlockSpec((1, rows, tn), lambda i, j: (i, 0, j)),
        out_shape=jax.ShapeDtypeStruct((DEPTH, rows, n), F32),
        compiler_params=_params("arbitrary", "arbitrary"),
        name="ada_mod",
    )(c_rows, w_ada, b_ada.reshape(DEPTH, 1, n))


def _layer_norm(z, g, b):
    mu = jnp.mean(z, axis=-1, keepdims=True)
    zc = z - mu
    var = jnp.mean(zc * zc, axis=-1, keepdims=True)
    return zc * lax.rsqrt(var + LN_EPS) * g + b


def _inproj_kernel(*refs, has_res):
    if has_res:
        x_ref, f_ref, g_ref, lng_ref, lnb_ref, sc_ref, sh_ref, w_ref, xo_ref, p_ref = refs
        z = DEEPNORM_ALPHA * x_ref[0] + (1.0 + g_ref[0]) * f_ref[0]
        x = _layer_norm(z, lng_ref[...], lnb_ref[...])
        xo_ref[0] = x
    else:
        x_ref, sc_ref, sh_ref, w_ref, p_ref = refs
        x = x_ref[0]
    u = x * (1.0 + sc_ref[0]) + sh_ref[0]
    p_ref[0] = jnp.dot(u.astype(BF16), w_ref[...], preferred_element_type=F32)


def _inproj(x, sc, sh, w_bf, res=None):
    b, t, d = x.shape
    n = w_bf.shape[1]
    tm = _row_tile(t, 512)
    row = pl.BlockSpec((1, tm, d), lambda i, j: (i, j, 0))
    vec = pl.BlockSpec((1, 1, d), lambda i, j: (i, 0, 0))
    par = pl.BlockSpec((1, d), lambda i, j: (0, 0))
    wspec = pl.BlockSpec((d, n), lambda i, j: (0, 0))
    pspec = pl.BlockSpec((1, tm, n), lambda i, j: (i, j, 0))
    pshape = jax.ShapeDtypeStruct((b, t, n), F32)
    if res is None:
        return None, pl.pallas_call(
            functools.partial(_inproj_kernel, has_res=False),
            grid=(b, t // tm),
            in_specs=[row, vec, vec, wspec],
            out_specs=pspec,
            out_shape=pshape,
            compiler_params=_params("arbitrary", "arbitrary"),
            name="inproj",
        )(x, sc, sh, w_bf)
    f, g, lng, lnb = res
    return pl.pallas_call(
        functools.partial(_inproj_kernel, has_res=True),
        grid=(b, t // tm),
        in_specs=[row, row, vec, par, par, vec, vec, wspec],
        out_specs=[row, pspec],
        out_shape=[jax.ShapeDtypeStruct((b, t, d), F32), pshape],
        compiler_params=_params("arbitrary", "arbitrary"),
        name="ln_inproj",
    )(x, f, g, lng, lnb, sc, sh, w_bf)


def _final_ln_kernel(x_ref, f_ref, g_ref, lng_ref, lnb_ref, o_ref):
    z = DEEPNORM_ALPHA * x_ref[0] + (1.0 + g_ref[0]) * f_ref[0]
    o_ref[0] = _layer_norm(z, lng_ref[...], lnb_ref[...])


def _final_ln(x, f, g, lng, lnb):
    b, t, d = x.shape
    tm = _row_tile(t, 1024)
    row = pl.BlockSpec((1, tm, d), lambda i, j: (i, j, 0))
    vec = pl.BlockSpec((1, 1, d), lambda i, j: (i, 0, 0))
    par = pl.BlockSpec((1, d), lambda i, j: (0, 0))
    return pl.pallas_call(
        _final_ln_kernel,
        grid=(b, t // tm),
        in_specs=[row, row, vec, par, par],
        out_specs=row,
        out_shape=jax.ShapeDtypeStruct((b, t, d), F32),
        compiler_params=_params("arbitrary", "arbitrary"),
        name="final_ln",
    )(x, f, g, lng, lnb)


def _log_sigmoid(x):
    return jnp.minimum(x, 0.0) - jnp.log1p(jnp.exp(-jnp.abs(x)))


def _mlstm_kernel(p_ref, gt_ref, bcol_ref, brow_ref, ng_ref, c0_ref, n0_ref, m0_ref,
                  hg_ref, c_ref, n_ref, m_ref, *, chunk):
    L = chunk

    @pl.when(pl.program_id(1) == 0)
    def _():
        c_ref[...] = c0_ref[...]
        n_ref[...] = n0_ref[...]
        m_ref[...] = m0_ref[...]

    gcol = p_ref[0, :, ML_GATE_OFF:ML_GATE_OFF + 2 * ML_HEADS] + bcol_ref[...]
    grow = gt_ref[0] + brow_ref[...]
    lf_col = _log_sigmoid(gcol)
    lf_row = _log_sigmoid(grow)
    ti = lax.broadcasted_iota(I32, (L, L), 0)
    si = lax.broadcasted_iota(I32, (L, L), 1)
    causal = si <= ti
    tril = jnp.where(causal, 1.0, 0.0).astype(F32)
    triu = jnp.where(ti <= si, 1.0, 0.0).astype(F32)
    b_col = jnp.dot(tril, lf_col, precision=HIGHEST, preferred_element_type=F32)
    b_row = jnp.dot(lf_row, triu, precision=HIGHEST, preferred_element_type=F32)

    for h in range(ML_HEADS):
        q = p_ref[0, :, h * ML_DK:(h + 1) * ML_DK]
        k = p_ref[0, :, ML_QK + h * ML_DK:ML_QK + (h + 1) * ML_DK]
        v = p_ref[0, :, 2 * ML_QK + h * ML_DV:2 * ML_QK + (h + 1) * ML_DV]
        o = p_ref[0, :, 2 * ML_QK + ML_V + h * ML_DV:2 * ML_QK + ML_V + (h + 1) * ML_DV]
        qs = (q * (ML_DK ** -0.5)).astype(BF16)
        kb = k.astype(BF16)
        vb = v.astype(BF16)
        bc = b_col[:, ML_HEADS + h:ML_HEADS + h + 1]
        ic = gcol[:, h:h + 1]
        br = b_row[ML_HEADS + h:ML_HEADS + h + 1, :]
        ir = grow[h:h + 1, :]
        c_old = c_ref[0, h]
        n_old = n_ref[0, h:h + 1, :]
        m_old = m_ref[0, h:h + 1, 0:1]

        dm = jnp.where(causal, bc - br + ir, -jnp.inf)
        inter = bc + m_old
        mt = jnp.maximum(inter, jnp.max(dm, axis=-1, keepdims=True))
        qk = lax.dot_general(qs, kb, NT_DIMS, preferred_element_type=F32)
        s = jnp.exp(dm - mt) * qk
        wp = jnp.exp(inter - mt)
        qc = lax.dot_general(qs, c_old.astype(BF16), NT_DIMS, preferred_element_type=F32)
        num = jnp.dot(s.astype(BF16), vb, preferred_element_type=F32) + wp * qc
        qn = jnp.sum(qs.astype(F32) * n_old, axis=-1, keepdims=True)
        den = jnp.sum(s, axis=-1, keepdims=True) + wp * qn
        hh = num / jnp.maximum(jnp.abs(den), jnp.exp(-mt))

        m_new = mt[L - 1:L, :]
        wk = jnp.exp(bc[L - 1:L, :] - bc + ic - m_new)
        wprev = jnp.exp(inter[L - 1:L, :] - m_new)
        vw = (v * wk).astype(BF16)
        c_ref[0, h] = wprev * c_old + lax.dot_general(vw, kb, TN_DIMS, preferred_element_type=F32)
        n_ref[0, h:h + 1, :] = wprev * n_old + jnp.sum(wk * kb.astype(F32), axis=0, keepdims=True)
        m_ref[0, h:h + 1, :] = jnp.broadcast_to(m_new, (1, ML_DK))

        mu = jnp.mean(hh, axis=-1, keepdims=True)
        hc = hh - mu
        var = jnp.mean(hc * hc, axis=-1, keepdims=True)
        hn = hc * lax.rsqrt(var + LN_EPS) * ng_ref[:, h * ML_DV:(h + 1) * ML_DV]
        hg_ref[0, :, h * ML_DV:(h + 1) * ML_DV] = (jax.nn.sigmoid(o) * hn).astype(BF16)


def _mlstm(p, b_gates, norm_g, c0, n0, m0, chunk):
    b, t, n = p.shape
    nc = t // chunk
    gt = jnp.swapaxes(p[:, :, ML_GATE_OFF:ML_GATE_OFF + 2 * ML_HEADS], 1, 2)
    if nc > 1:
        gt_spec = pl.BlockSpec((1, 2 * ML_HEADS, chunk), lambda i, j: (i, 0, j))
    else:
        gt_spec = pl.BlockSpec((1, 2 * ML_HEADS, t), lambda i, j: (i, 0, 0))
    if nc > 1 and chunk % LANES != 0:
        gt = gt.reshape(b, 2 * ML_HEADS, nc, chunk).transpose(0, 2, 1, 3).reshape(b * nc, 2 * ML_HEADS, chunk)
        gt_spec = pl.BlockSpec((1, 2 * ML_HEADS, chunk), lambda i, j: (i * nc + j, 0, 0))
    m0b = jnp.broadcast_to(m0[..., None], (b, ML_HEADS, ML_DK))
    cspec = pl.BlockSpec((1, ML_HEADS, ML_DV, ML_DK), lambda i, j: (i, 0, 0, 0))
    nspec = pl.BlockSpec((1, ML_HEADS, ML_DK), lambda i, j: (i, 0, 0))
    hg, c, nn, m = pl.pallas_call(
        functools.partial(_mlstm_kernel, chunk=chunk),
        grid=(b, nc),
        in_specs=[
            pl.BlockSpec((1, chunk, n), lambda i, j: (i, j, 0)),
            gt_spec,
            pl.BlockSpec((1, 2 * ML_HEADS), lambda i, j: (0, 0)),
            pl.BlockSpec((2 * ML_HEADS, 1), lambda i, j: (0, 0)),
            pl.BlockSpec((1, ML_V), lambda i, j: (0, 0)),
            cspec, nspec, nspec,
        ],
        out_specs=[pl.BlockSpec((1, chunk, ML_V), lambda i, j: (i, j, 0)), cspec, nspec, nspec],
        out_shape=[
            jax.ShapeDtypeStruct((b, t, ML_V), BF16),
            jax.ShapeDtypeStruct((b, ML_HEADS, ML_DV, ML_DK), F32),
            jax.ShapeDtypeStruct((b, ML_HEADS, ML_DK), F32),
            jax.ShapeDtypeStruct((b, ML_HEADS, ML_DK), F32),
        ],
        compiler_params=_params("arbitrary", "arbitrary"),
        name="mlstm_scan",
    )(p, gt, b_gates.reshape(1, -1), b_gates.reshape(-1, 1), norm_g.reshape(1, -1), c0, n0, m0b)
    return hg, c, nn, m[..., 0]


def _outproj_kernel(a_ref, w_ref, x_ref, g_ref, sc_ref, sh_ref, lng_ref, lnb_ref, wrh_ref, wrl_ref, br_ref,
                    x1_ref, u2_ref, te_ref, tg_ref):
    y = jnp.dot(a_ref[0], w_ref[...], preferred_element_type=F32)
    z = DEEPNORM_ALPHA * x_ref[0] + (1.0 + g_ref[0]) * y
    x1 = _layer_norm(z, lng_ref[...], lnb_ref[...])
    x1_ref[0] = x1
    u2 = x1 * (1.0 + sc_ref[0]) + sh_ref[0]
    u_hi = u2.astype(BF16)
    u2_ref[0] = u_hi
    u_lo = (u2 - u_hi.astype(F32)).astype(BF16)
    logits = (jnp.dot(u_hi, wrh_ref[...], preferred_element_type=F32)
              + jnp.dot(u_lo, wrh_ref[...], preferred_element_type=F32)
              + jnp.dot(u_hi, wrl_ref[...], preferred_element_type=F32)) + br_ref[...]
    lane = lax.broadcasted_iota(I32, logits.shape, 1).astype(F32)
    vals, idxs = [], []
    cur = logits
    for _ in range(TOP_K):
        mx = jnp.max(cur, axis=-1, keepdims=True)
        idx = jnp.min(jnp.where(cur == mx, lane, float(LANES)), axis=-1, keepdims=True)
        vals.append(mx)
        idxs.append(idx)
        cur = jnp.where(lane == idx, -jnp.inf, cur)
    es = [jnp.exp(v - vals[0]) for v in vals]
    tot = es[0] + es[1] + es[2] + es[3]
    for k in range(TOP_K):
        te_ref[0, :, k:k + 1] = idxs[k].astype(I32)
        tg_ref[0, :, k:k + 1] = es[k] / tot


def _outproj(a, w_bf, x, g, sc, sh, lng, lnb, wr_pad, br_pad):
    b, t, d = x.shape
    tm = _row_tile(t, 256)
    wr_hi = wr_pad.astype(BF16)
    wr_lo = (wr_pad - wr_hi.astype(F32)).astype(BF16)
    row = pl.BlockSpec((1, tm, d), lambda i, j: (i, j, 0))
    vec = pl.BlockSpec((1, 1, d), lambda i, j: (i, 0, 0))
    par = pl.BlockSpec((1, d), lambda i, j: (0, 0))
    top = pl.BlockSpec((1, tm, TOP_K), lambda i, j: (i, j, 0))
    return pl.pallas_call(
        _outproj_kernel,
        grid=(b, t // tm),
        in_specs=[row, pl.BlockSpec((d, d), lambda i, j: (0, 0)), row, vec, vec, vec, par, par,
                  pl.BlockSpec((d, LANES), lambda i, j: (0, 0)), pl.BlockSpec((d, LANES), lambda i, j: (0, 0)),
                  pl.BlockSpec((1, LANES), lambda i, j: (0, 0))],
        out_specs=[row, row, top, top],
        out_shape=[jax.ShapeDtypeStruct((b, t, d), F32), jax.ShapeDtypeStruct((b, t, d), BF16),
                   jax.ShapeDtypeStruct((b, t, TOP_K), I32), jax.ShapeDtypeStruct((b, t, TOP_K), F32)],
        compiler_params=_params("arbitrary", "arbitrary"),
        name="outproj_ln_router",
    )(a, w_bf, x, g, sc, sh, lng, lnb, wr_hi, wr_lo, br_pad)


def _moe_kernel(x_ref, te_ref, tg_ref, w1_ref, b1_ref, w2_ref, b2_ref, o_ref, pos_ref, tri_ref, *, tm, rb):
    e = pl.program_id(1)

    @pl.when(jnp.logical_and(pl.program_id(0) == 0, e == 0))
    def _():
        si = lax.broadcasted_iota(I32, (tm, tm), 0)
        ti = lax.broadcasted_iota(I32, (tm, tm), 1)
        tri_ref[...] = jnp.where(si <= ti, 1.0, 0.0).astype(BF16)

    @pl.when(e == 0)
    def _():
        o_ref[...] = jnp.zeros_like(o_ref)
        eio = lax.broadcasted_iota(I32, (N_EXPERTS, tm), 0)
        sel = jnp.zeros((N_EXPERTS, tm), F32)
        for k in range(TOP_K):
            sel = sel + jnp.where(te_ref[k:k + 1, :] == eio, 1.0, 0.0)
        rank = jnp.dot(sel.astype(BF16), tri_ref[...], preferred_element_type=F32)
        pos_ref[...] = rank * sel

    pm = pos_ref[pl.ds(e, 1), :]
    gate = jnp.zeros((1, tm), F32)
    for k in range(TOP_K):
        gate = gate + jnp.where(te_ref[k:k + 1, :] == e, tg_ref[k:k + 1, :], 0.0)
    cnt = jnp.max(pm).astype(I32)
    nblk = (cnt + (rb - 1)) // rb

    def body(blk, carry):
        r = (blk * rb + 1 + lax.broadcasted_iota(I32, (rb, 1), 0)).astype(F32)
        hit = pm == r
        onehot = jnp.where(hit, 1.0, 0.0).astype(BF16)
        xg = jnp.dot(onehot, x_ref[...], preferred_element_type=F32).astype(BF16)
        h = jnp.dot(xg, w1_ref[0], preferred_element_type=F32) + b1_ref[0]
        hg = jnp.minimum(h[:, :D_FF], SWIGLU_LIMIT)
        hl = jnp.clip(h[:, D_FF:], -SWIGLU_LIMIT, SWIGLU_LIMIT)
        a = hg * jax.nn.sigmoid(SWIGLU_ALPHA * hg) * (hl + 1.0)
        y = jnp.dot(a.astype(BF16), w2_ref[0], preferred_element_type=F32) + b2_ref[0]
        grow = jnp.sum(jnp.where(hit, gate, 0.0), axis=-1, keepdims=True)
        yg = (y * grow).astype(BF16)
        o_ref[...] += lax.dot_general(onehot, yg, TN_DIMS, preferred_element_type=F32)
        return carry

    lax.fori_loop(0, nblk, body, 0)


def _moe(u2, te, tg, w1_bf, b1, w2_bf, b2, layer):
    n, d = u2.shape
    tm = _row_tile(n, 1024)
    rb = min(tm, tm * TOP_K // N_EXPERTS + 32)
    te_t = te.T
    tg_t = tg.T
    return pl.pallas_call(
        functools.partial(_moe_kernel, tm=tm, rb=rb),
        grid=(n // tm, N_EXPERTS),
        in_specs=[
            pl.BlockSpec((tm, d), lambda i, e: (i, 0)),
            pl.BlockSpec((TOP_K, tm), lambda i, e: (0, i)),
            pl.BlockSpec((TOP_K, tm), lambda i, e: (0, i)),
            pl.BlockSpec((None, 1, d, 2 * D_FF), lambda i, e: (layer, e, 0, 0)),
            pl.BlockSpec((None, 1, 1, 2 * D_FF), lambda i, e: (layer, e, 0, 0)),
            pl.BlockSpec((None, 1, D_FF, d), lambda i, e: (layer, e, 0, 0)),
            pl.BlockSpec((None, 1, 1, d), lambda i, e: (layer, e, 0, 0)),
        ],
        out_specs=pl.BlockSpec((tm, d), lambda i, e: (i, 0)),
        out_shape=jax.ShapeDtypeStruct((n, d), F32),
        scratch_shapes=[pltpu.VMEM((N_EXPERTS, tm), F32), pltpu.VMEM((tm, tm), BF16)],
        compiler_params=_params("arbitrary", "arbitrary"),
        name="moe_experts",
    )(u2, te_t, tg_t, w1_bf, b1.reshape(DEPTH, N_EXPERTS, 1, -1), w2_bf, b2.reshape(DEPTH, N_EXPERTS, 1, -1))


def _rope_tables(pos, head_dim):
    rd = head_dim // ROPE_FRACTION
    half = rd // 2
    inv = jnp.power(ROPE_THETA, -jnp.arange(half, dtype=F32) / half)
    ang = jnp.asarray(pos).astype(F32)[:, None] * inv[None, :]
    cos, sin = jnp.cos(ang), jnp.sin(ang)
    t = pos.shape[0]
    ones = jnp.ones((t, head_dim - rd), F32)
    zeros = jnp.zeros((t, head_dim - rd), F32)
    zh = jnp.zeros((t, half), F32)
    c = jnp.concatenate([cos, cos, ones], axis=-1)
    sa = jnp.concatenate([zh, sin, zeros], axis=-1)
    sb = jnp.concatenate([-sin, zh, zeros], axis=-1)
    rep = LANES // head_dim
    return tuple(jnp.tile(a, (1, rep)) for a in (c, sa, sb))


def _rope(z, c, sa, sb, half):
    w = z.shape[-1]
    rep = w // LANES
    if rep > 1:
        c, sa, sb = (jnp.concatenate([a] * rep, axis=-1) for a in (c, sa, sb))
    return z * c + pltpu.roll(z, half, 1) * sa + pltpu.roll(z, w - half, 1) * sb


def _dsa_prep_kernel(p_ref, c1_ref, sa1_ref, sb1_ref, c2_ref, sa2_ref, sb2_ref,
                     q_ref, kf_ref, vf_ref, kif_ref, kb_ref, vb_ref, kib_ref, qi_ref, w_ref):
    c1, sa1, sb1 = c1_ref[...], sa1_ref[...], sb1_ref[...]
    c2, sa2, sb2 = c2_ref[...], sa2_ref[...], sb2_ref[...]
    h1 = DSA_HD // ROPE_FRACTION // 2
    h2 = IDX_DIM // ROPE_FRACTION // 2
    o_k = DSA_Q
    o_v = DSA_Q + DSA_KV
    o_qi = DSA_Q + 2 * DSA_KV
    o_ki = o_qi + IDX_HEADS * IDX_DIM
    q = _rope(p_ref[0, :, :DSA_Q], c1, sa1, sb1, h1)
    q_ref[0] = q.astype(BF16)
    k = _rope(p_ref[0, :, o_k:o_v], c1, sa1, sb1, h1)
    kf_ref[0] = k
    kb_ref[0] = k.astype(BF16)
    v = p_ref[0, :, o_v:o_qi]
    vf_ref[0] = v
    vb_ref[0] = v.astype(BF16)
    qi = _rope(p_ref[0, :, o_qi:o_ki], c2, sa2, sb2, h2)
    qi_ref[0] = qi.astype(BF16)
    tail = p_ref[0, :, o_ki:o_ki + LANES]
    ki = _rope(tail, c2, sa2, sb2, h2)[:, :IDX_DIM]
    kif_ref[0] = ki
    kib_ref[0] = ki.astype(BF16)
    wi = tail[:, IDX_DIM:IDX_DIM + IDX_HEADS]
    w_ref[0] = (wi * (IDX_HEADS ** -0.5)) * (IDX_DIM ** -0.5)


def _dsa_prep(p, pos):
    b, t, n = p.shape
    tm = _row_tile(t, 512)
    tabs = _rope_tables(pos, DSA_HD) + _rope_tables(pos, IDX_DIM)
    tab = pl.BlockSpec((tm, LANES), lambda i, j: (j, 0))

    def row(w):
        return pl.BlockSpec((1, tm, w), lambda i, j: (i, j, 0))

    widths = [(DSA_Q, BF16), (DSA_KV, F32), (DSA_KV, F32), (IDX_DIM, F32), (DSA_KV, BF16), (DSA_KV, BF16),
              (IDX_DIM, BF16), (IDX_HEADS * IDX_DIM, BF16), (IDX_HEADS, F32)]
    return pl.pallas_call(
        _dsa_prep_kernel,
        grid=(b, t // tm),
        in_specs=[row(n)] + [tab] * 6,
        out_specs=[row(w) for w, _ in widths],
        out_shape=[jax.ShapeDtypeStruct((b, t, w), dt) for w, dt in widths],
        compiler_params=_params("arbitrary", "arbitrary"),
        name="dsa_rope_split",
    )(p, *tabs)


def _f2key(f):
    bits = lax.bitcast_convert_type(f, I32)
    return bits ^ ((bits >> 31) & 0x7FFFFFFF)


def _key2f(k):
    return lax.bitcast_convert_type(k ^ ((k >> 31) & 0x7FFFFFFF), F32)


def _dsa_kernel(nkt_ref, q_ref, qi_ref, w_ref, k_ref, v_ref, ki_ref, kpos_ref, qlim_ref, nadm_ref, o_ref,
                s_ref, wb_ref, qg_ref, m_ref, l_ref, acc_ref, flag_ref, *, tq, tk, nsel, value_iters):
    nkt = nkt_ref[pl.program_id(1)]
    qlim = qlim_ref[...]
    w = w_ref[0]
    qi = qi_ref[0]
    qis = [qi[:, h * IDX_DIM:(h + 1) * IDX_DIM] for h in range(IDX_HEADS)]
    nlg = tk // LANES
    for h in range(IDX_HEADS):
        wb_ref[h] = jnp.broadcast_to(w[:, h:h + 1], (tq, LANES))

    def p1(kt, carry):
        rmax, rmin, cgt0, cge0 = carry
        off = pl.multiple_of(kt * tk, tk)
        kit = ki_ref[0, pl.ds(off, tk), :]
        acc = jnp.zeros((tq, tk), F32)
        for h in range(IDX_HEADS):
            d = lax.dot_general(qis[h], kit, NT_DIMS, preferred_element_type=F32)
            acc = acc + jnp.maximum(d, 0.0) * jnp.concatenate([wb_ref[h]] * nlg, axis=1)
        s = jnp.where(kpos_ref[kt] < qlim, acc, -jnp.inf)
        s_ref[kt] = s
        gt0 = jnp.where(s > 0.0, 1.0, 0.0)
        ge0 = jnp.where(s >= 0.0, 1.0, 0.0)
        for g in range(nlg):
            lanes = slice(g * LANES, (g + 1) * LANES)
            rmax = jnp.maximum(rmax, s[:, lanes])
            rmin = jnp.minimum(rmin, acc[:, lanes])
            cgt0 = cgt0 + gt0[:, lanes]
            cge0 = cge0 + ge0[:, lanes]
        return rmax, rmin, cgt0, cge0

    zcol = jnp.zeros((tq, 1), F32)
    zlan = jnp.zeros((tq, LANES), F32)
    init = (jnp.full((tq, LANES), -jnp.inf, F32), jnp.full((tq, LANES), jnp.inf, F32), zlan, zlan)
    rmax, rmin, cgt0, cge0 = lax.fori_loop(0, nkt, p1, init)
    rmax = jnp.max(rmax, axis=-1, keepdims=True)
    rmin = jnp.min(rmin, axis=-1, keepdims=True)
    cgt0 = jnp.sum(cgt0, axis=-1, keepdims=True)
    cge0 = jnp.sum(cge0, axis=-1, keepdims=True)
    nadm = nadm_ref[...]

    rg = min(tq, 32)
    fsel = float(nsel)
    unbounded = float(2 ** 30)

    def count_ge(tb):
        def body(kt, c):
            out = []
            for r in range(tq // rg):
                tr = jnp.concatenate([tb[r * rg:(r + 1) * rg, :]] * nlg, axis=1)
                hit = jnp.where(s_ref[kt, r * rg:(r + 1) * rg, :] >= tr, 1.0, 0.0)
                cr = c[r * rg:(r + 1) * rg]
                for g in range(nlg):
                    cr = cr + hit[:, g * LANES:(g + 1) * LANES]
                out.append(cr)
            return jnp.concatenate(out, axis=0)
        c = lax.fori_loop(0, nkt, body, zlan)
        return jnp.broadcast_to(jnp.sum(c, axis=-1, keepdims=True), (tq, LANES))

    rmax, rmin, cgt0, cge0, nadm = (jnp.broadcast_to(a, (tq, LANES)) for a in (rmax, rmin, cgt0, cge0, nadm))
    few = nadm <= fsel
    many = jnp.logical_not(few)
    pos = jnp.logical_and(many, cgt0 > fsel)
    neg = jnp.logical_and(many, cge0 < fsel)
    zero = jnp.logical_and(many, jnp.logical_and(cgt0 <= fsel, cge0 >= fsel))
    lo0 = jnp.where(pos, 0, _f2key(jnp.where(neg, rmin, 0.0)))
    hi0 = jnp.where(pos, _f2key(jnp.where(pos, rmax, 0.0)) + 1, 0)
    clo0 = jnp.where(pos, cge0, nadm)
    chi0 = jnp.where(pos, 0.0, cge0)
    act0 = jnp.where(jnp.logical_or(pos, neg), 1.0, 0.0)
    t0 = jnp.where(few, -jnp.inf, 0.0)
    need0 = jnp.where(zero, fsel - cgt0, unbounded)

    def cond(st):
        return st[0] > 0.0

    def step(st):
        _, it, lo, hi, clo, chi, act, thr, need = st
        conv = jnp.logical_and(act > 0.0, lo + 1 >= hi)
        thr = jnp.where(conv, _key2f(lo), thr)
        need = jnp.where(conv, fsel - chi, need)
        act = jnp.where(conv, 0.0, act)
        flo, fhi = _key2f(lo), _key2f(hi)
        mid_i = (lo >> 1) + (hi >> 1) + (lo & hi & 1)
        lclo = jnp.log(clo)
        interp = (lclo - float(np.log(nsel - 0.5))) / (lclo - jnp.log(jnp.maximum(chi, 0.5)))
        frac = jnp.where(it % 3 == 2, 0.5, interp)
        mid_v = _f2key(flo + frac * (fhi - flo))
        use_v = jnp.logical_and(it < value_iters, jnp.logical_and(mid_v > lo, mid_v < hi))
        mid = jnp.where(use_v, mid_v, mid_i)
        tm = _key2f(mid)
        c = count_ge(jnp.where(act > 0.0, tm, jnp.inf))
        hit = jnp.logical_and(act > 0.0, c == fsel)
        thr = jnp.where(hit, tm, thr)
        up = jnp.logical_and(act > 0.0, c > fsel)
        dn = jnp.logical_and(act > 0.0, c < fsel)
        lo = jnp.where(up, mid, lo)
        clo = jnp.where(up, c, clo)
        hi = jnp.where(dn, mid, hi)
        chi = jnp.where(dn, c, chi)
        act = jnp.where(hit, 0.0, act)
        return jnp.max(act), it + 1, lo, hi, clo, chi, act, thr, need

    st = (jnp.max(act0), jnp.int32(0), lo0, hi0, clo0, chi0, act0, t0, need0)
    _, _, _, _, _, _, _, thr, need = lax.while_loop(cond, step, st)
    thr, need = thr[:, 0:1], need[:, 0:1]
    tied = need < unbounded * 0.5
    has_ties = jnp.max(jnp.where(tied, 1.0, 0.0)) > 0.0
    thr = jnp.maximum(thr, -F32_MAX)

    for g in range(DSA_KV_HEADS):
        for hh in range(DSA_GROUP):
            hd = g * DSA_GROUP + hh
            qg_ref[g, hh * tq:(hh + 1) * tq, :] = q_ref[0, :, hd * DSA_HD:(hd + 1) * DSA_HD]
    m_ref[...] = jnp.full(m_ref.shape, NEG_BIG, F32)
    l_ref[...] = jnp.zeros(l_ref.shape, F32)
    acc_ref[...] = jnp.zeros(acc_ref.shape, F32)

    log2_scale = (DSA_HD ** -0.5) * float(np.log2(np.e))
    ones_cols = jnp.ones((tk, DSA_HD), BF16)

    def attend(kt, bias):
        off = pl.multiple_of(kt * tk, tk)
        qk = [lax.dot_general(qg_ref[g], k_ref[0, pl.ds(off, tk), g * DSA_HD:(g + 1) * DSA_HD], NT_DIMS,
                              preferred_element_type=F32) for g in range(DSA_KV_HEADS)]
        for g in range(DSA_KV_HEADS):
            vt_g = jnp.concatenate([v_ref[0, pl.ds(off, tk), g * DSA_HD:(g + 1) * DSA_HD], ones_cols], axis=1)
            for hh in range(DSA_GROUP):
                rows = slice(hh * tq, (hh + 1) * tq)
                lg = qk[g][rows] * log2_scale + bias
                m_old = m_ref[g, rows, :]
                m_new = jnp.maximum(m_old, jnp.max(lg, axis=-1, keepdims=True))
                alpha = jnp.exp2(m_old - m_new)
                p = jnp.exp2(lg - jnp.concatenate([m_new] * nlg, axis=1))
                pv = jnp.dot(p.astype(BF16), vt_g, preferred_element_type=F32)
                acc_ref[g, rows, :] = alpha * acc_ref[g, rows, :] + pv[:, :DSA_HD]
                l_ref[g, rows, :] = alpha * l_ref[g, rows, :] + pv[:, DSA_HD:]
                m_ref[g, rows, :] = m_new

    @pl.when(jnp.logical_not(has_ties))
    def _():
        def body(kt, carry):
            attend(kt, jnp.where(s_ref[kt] >= thr, 0.0, NEG_BIG))
            return carry
        lax.fori_loop(0, nkt, body, 0)

    @pl.when(has_ties)
    def _():
        def scan(kt, carry):
            seen, cut, nin = carry
            eqc = jnp.sum(jnp.where(s_ref[kt] == thr, 1.0, 0.0), axis=-1, keepdims=True)
            here = jnp.logical_and(tied, jnp.logical_and(seen < need, seen + eqc >= need))
            cut = jnp.where(here, kt, cut)
            nin = jnp.where(here, need - seen, nin)
            flag_ref[kt] = jnp.max(jnp.where(here, 1.0, 0.0)).astype(I32)
            return seen + eqc, cut, nin

        cut0 = jnp.where(tied, -1, nkt).astype(I32)
        _, cut, nin = lax.fori_loop(0, nkt, scan, (zcol, cut0, zcol))

        def body(kt, carry):
            s = s_ref[kt]
            before = jnp.where(kt < cut, 0.0, NEG_BIG)

            def plain():
                return jnp.where(s == thr, before, NEG_BIG)

            def ranked():
                si = lax.broadcasted_iota(I32, (tk, tk), 0)
                ti = lax.broadcasted_iota(I32, (tk, tk), 1)
                triu = jnp.where(si <= ti, 1.0, 0.0).astype(BF16)
                eq = jnp.where(s == thr, 1.0, 0.0).astype(BF16)
                rank = jnp.dot(eq, triu, preferred_element_type=F32)
                inside = jnp.where(rank <= nin, 0.0, NEG_BIG)
                return jnp.where(s == thr, jnp.where(kt == cut, inside, before), NEG_BIG)

            tie_bias = lax.cond(flag_ref[kt] > 0, ranked, plain)
            attend(kt, jnp.where(s > thr, 0.0, tie_bias))
            return carry
        lax.fori_loop(0, nkt, body, 0)

    for g in range(DSA_KV_HEADS):
        og = acc_ref[g] / l_ref[g]
        for hh in range(DSA_GROUP):
            hd = g * DSA_GROUP + hh
            o_ref[0, :, hd * DSA_HD:(hd + 1) * DSA_HD] = og[hh * tq:(hh + 1) * tq, :].astype(BF16)


def _dsa_attend(q, qi, w, k, v, ki, kpos, qlim, tq, tk):
    b, t, _ = q.shape
    lp = k.shape[1]
    nq = t // tq
    nsel = min(DSA_TOPK, int(np.sum(kpos < np.iinfo(np.int32).max)) // 4)
    qmax = qlim.reshape(nq, tq).max(axis=1)
    nadm = np.searchsorted(kpos, qlim, side="left")
    nkt = np.minimum(-(-np.searchsorted(kpos, qmax, side="left") // tk), lp // tk).astype(np.int32)
    nkt = np.maximum(nkt, 1)
    single = pl.Buffered(1)

    def full(wd):
        return pl.BlockSpec((1, lp, wd), lambda i, j, n: (i, 0, 0), pipeline_mode=single)

    def row(wd):
        return pl.BlockSpec((1, tq, wd), lambda i, j, n: (i, j, 0))

    grid_spec = pltpu.PrefetchScalarGridSpec(
        num_scalar_prefetch=1,
        grid=(b, nq),
        in_specs=[row(DSA_Q), row(IDX_HEADS * IDX_DIM), row(IDX_HEADS), full(DSA_KV), full(DSA_KV), full(IDX_DIM),
                  pl.BlockSpec((lp // tk, 1, tk), lambda i, j, n: (0, 0, 0)),
                  pl.BlockSpec((tq, 1), lambda i, j, n: (j, 0)),
                  pl.BlockSpec((tq, 1), lambda i, j, n: (j, 0))],
        out_specs=row(DSA_Q),
        scratch_shapes=[
            pltpu.VMEM((lp // tk, tq, tk), F32),
            pltpu.VMEM((IDX_HEADS, tq, LANES), F32),
            pltpu.VMEM((DSA_KV_HEADS, DSA_GROUP * tq, DSA_HD), BF16),
            pltpu.VMEM((DSA_KV_HEADS, DSA_GROUP * tq, LANES), F32),
            pltpu.VMEM((DSA_KV_HEADS, DSA_GROUP * tq, LANES), F32),
            pltpu.VMEM((DSA_KV_HEADS, DSA_GROUP * tq, DSA_HD), F32),
            pltpu.SMEM((lp // tk,), I32),
        ],
    )
    return pl.pallas_call(
        functools.partial(_dsa_kernel, tq=tq, tk=tk, nsel=nsel, value_iters=16),
        grid_spec=grid_spec,
        out_shape=jax.ShapeDtypeStruct((b, t, DSA_Q), BF16),
        compiler_params=_params("arbitrary", "arbitrary"),
        name="dsa_select_attend",
    )(jnp.asarray(nkt), q, qi, w, k, v, ki, jnp.asarray(kpos, I32).reshape(lp // tk, 1, tk),
      jnp.asarray(qlim, I32).reshape(t, 1), jnp.asarray(nadm, F32).reshape(t, 1))


def _trunk(x, mods, pos, ml_state, ds_cache, prm):
    b, t, d = x.shape
    chunk = next((c for c in (ML_EVAL_CHUNK, CHUNK) if t % c == 0), t)

    def mod(i, k):
        return mods[i][:, None, k * d:(k + 1) * d]

    _, p = _inproj(x, mod(0, 1), mod(0, 0), prm["ml_w_in"])
    if ml_state is None:
        c0 = jnp.zeros((b, ML_HEADS, ML_DV, ML_DK), F32)
        n0 = jnp.zeros((b, ML_HEADS, ML_DK), F32)
        m0 = jnp.zeros((b, ML_HEADS), F32)
    else:
        c0, n0, m0 = (s[0] for s in ml_state)
    hg, c_new, n_new, m_new = _mlstm(p, prm["ml_b_gates"], prm["ml_norm_g"], c0, n0, m0, chunk)
    x1, u2, te, tg = _outproj(hg, prm["ml_w_out"], x, mod(0, 2), mod(0, 4), mod(0, 3),
                              prm["ln_g"][0, 0:1], prm["ln_b"][0, 0:1], prm["w_router"][0], prm["b_router"][0])
    f = _moe(u2.reshape(b * t, d), te.reshape(b * t, TOP_K), tg.reshape(b * t, TOP_K),
             prm["moe_w1"], prm["moe_b1"], prm["moe_w2"], prm["moe_b2"], 0).reshape(b, t, d)

    x2, p = _inproj(x1, mod(1, 1), mod(1, 0), prm["ds_w_in"],
                    res=(f, mod(0, 5), prm["ln_g"][0, 1:2], prm["ln_b"][0, 1:2]))
    q, kf, vf, kif, kb, vb, kib, qi, w = _dsa_prep(p, pos)
    pos_np = np.asarray(pos)
    qlim = (pos_np // CHUNK + 1) * CHUNK
    if ds_cache is None:
        kpos = pos_np
        tq, tk = 128, 1024
        if t % tk:
            tq, tk = t, t
        k_all, v_all, ki_all = kb, vb, kib
    else:
        ck, cv, cki = ds_cache
        past = ck.shape[1]
        total = past + t
        lp = -(-total // (3 * LANES)) * (3 * LANES)
        tq, tk = t, 3 * LANES
        kpos = np.concatenate([np.arange(past), pos_np, np.full((lp - total,), np.iinfo(np.int32).max)])

        def cat(cache, new):
            full = jnp.concatenate([cache.reshape(b, past, -1).astype(BF16), new], axis=1)
            return jnp.pad(full, ((0, 0), (0, lp - total), (0, 0)))

        k_all, v_all, ki_all = cat(ck, kb), cat(cv, vb), cat(cki, kib)
    o = _dsa_attend(q, qi, w, k_all, v_all, ki_all, kpos.astype(np.int64), qlim, tq, tk)
    x3, u2, te, tg = _outproj(o, prm["ds_w_out"], x2, mod(1, 2), mod(1, 4), mod(1, 3),
                              prm["ln_g"][1, 0:1], prm["ln_b"][1, 0:1], prm["w_router"][1], prm["b_router"][1])
    f = _moe(u2.reshape(b * t, d), te.reshape(b * t, TOP_K), tg.reshape(b * t, TOP_K),
             prm["moe_w1"], prm["moe_b1"], prm["moe_w2"], prm["moe_b2"], 1).reshape(b, t, d)
    y = _final_ln(x3, f, mod(1, 5), prm["ln_g"][1, 1:2], prm["ln_b"][1, 1:2])

    ml_out = (c_new[None], n_new[None], m_new[None])
    ds_out = (kf.reshape(1, b, t, DSA_KV_HEADS, DSA_HD), vf.reshape(1, b, t, DSA_KV_HEADS, DSA_HD), kif[None])
    return y, ml_out, ds_out


def _prepare(w_ada, b_ada, ln_g, ln_b, ml_w_in, ml_b_gates, ml_norm_g, ml_w_out, ds_w_in, ds_w_out,
             moe_w_router, moe_b_router, moe_w1, moe_b1, moe_w2, moe_b2):
    return {
        "ln_g": ln_g, "ln_b": ln_b,
        "ml_w_in": jnp.pad(ml_w_in[0], ((0, 0), (0, ML_IN_PAD - ML_IN))).astype(BF16),
        "ml_b_gates": ml_b_gates[0], "ml_norm_g": ml_norm_g[0],
        "ml_w_out": ml_w_out[0].astype(BF16),
        "ds_w_in": jnp.pad(ds_w_in[0], ((0, 0), (0, DSA_IN_PAD - DSA_IN))).astype(BF16),
        "ds_w_out": ds_w_out[0].astype(BF16),
        "w_router": jnp.pad(moe_w_router, ((0, 0), (0, 0), (0, LANES - N_EXPERTS))),
        "b_router": jnp.pad(moe_b_router, ((0, 0), (0, LANES - N_EXPERTS)), constant_values=NEG_BIG)[:, None, :],
        "moe_w1": moe_w1.astype(BF16), "moe_b1": moe_b1,
        "moe_w2": moe_w2.astype(BF16), "moe_b2": moe_b2,
    }


def kernel(x_prompt, x_sample, state_mlstm_C, state_mlstm_n, state_mlstm_m, cache_dsa_k, cache_dsa_v,
           cache_dsa_kidx, c_prompt, c_sample, w_ada, b_ada, ln_g, ln_b, ml_w_in, ml_b_gates, ml_norm_g,
           ml_w_out, ds_w_in, ds_w_out, moe_w_router, moe_b_router, moe_w1, moe_b1, moe_w2, moe_b2):
    prm = _prepare(w_ada, b_ada, ln_g, ln_b, ml_w_in, ml_b_gates, ml_norm_g, ml_w_out, ds_w_in, ds_w_out,
                   moe_w_router, moe_b_router, moe_w1, moe_b1, moe_w2, moe_b2)
    bp, bs = c_prompt.shape[0], c_sample.shape[0]
    rows = -(-(bp + bs) // 8) * 8
    c_rows = jnp.pad(jnp.concatenate([c_prompt, c_sample], axis=0), ((0, rows - bp - bs), (0, 0)))
    mods = _ada(c_rows, w_ada, b_ada)
    mods_p = [mods[i, :bp] for i in range(DEPTH)]
    mods_s = [mods[i, bp:bp + bs] for i in range(DEPTH)]

    pos_p = np.arange(x_prompt.shape[1], dtype=np.int32)
    y_p, ml_p, ds_p = _trunk(x_prompt, mods_p, pos_p, None, None, prm)
    past = cache_dsa_k.shape[2]
    pos_s = past + np.arange(x_sample.shape[1], dtype=np.int32)
    y_s, ml_s, ds_s = _trunk(x_sample, mods_s, pos_s, (state_mlstm_C, state_mlstm_n, state_mlstm_m),
                             (cache_dsa_k[0], cache_dsa_v[0], cache_dsa_kidx[0]), prm)
    return (y_p, y_s, ml_p[0], ml_p[1], ml_p[2], ds_p[0], ds_p[1], ds_p[2],
            ml_s[0], ml_s[1], ml_s[2], ds_s[0], ds_s[1], ds_s[2])
```

```python
import functools

import numpy as np
import jax
import jax.numpy as jnp
from jax import lax
from jax.experimental import pallas as pl
from jax.experimental.pallas import tpu as pltpu

F32 = jnp.float32
BF16 = jnp.bfloat16
I32 = jnp.int32

D_MODEL = 1024
DEPTH = 2
CHUNK = 64
ML_HEADS = 4
ML_DV = D_MODEL // ML_HEADS
ML_DK = ML_DV // 2
ML_QK = ML_HEADS * ML_DK
ML_V = ML_HEADS * ML_DV
ML_IN = 2 * ML_QK + 2 * ML_V + 2 * ML_HEADS
ML_GATE_OFF = 2 * ML_QK + 2 * ML_V
ML_EVAL_CHUNK = 256
DSA_HEADS = 8
DSA_KV_HEADS = 2
DSA_HD = D_MODEL // DSA_HEADS
DSA_GROUP = DSA_HEADS // DSA_KV_HEADS
IDX_HEADS = 8
IDX_DIM = 64
DSA_TOPK = 256
DSA_Q = DSA_HEADS * DSA_HD
DSA_KV = DSA_KV_HEADS * DSA_HD
DSA_IN = DSA_Q + 2 * DSA_KV + IDX_HEADS * IDX_DIM + IDX_DIM + IDX_HEADS
DSA_LOG2_SCALE = (DSA_HD ** -0.5) * float(np.log2(np.e))
ROPE_THETA = 500000.0
ROPE_FRACTION = 4
N_EXPERTS = 32
TOP_K = 4
D_FF = D_MODEL
SWIGLU_LIMIT = 7.0
SWIGLU_ALPHA = 1.702
DEEPNORM_ALPHA = (2 * DEPTH) ** 0.25
LN_EPS = 1e-5

LANES = 128
VMEM_LIMIT = 56 * 1024 * 1024

ML_IN_PAD = -(-ML_IN // LANES) * LANES
DSA_IN_PAD = -(-DSA_IN // LANES) * LANES
NEG_BIG = -1e30
F32_MAX = float(np.finfo(np.float32).max)
HIGHEST = lax.Precision.HIGHEST
NT_DIMS = (((1,), (1,)), ((), ()))
TN_DIMS = (((0,), (0,)), ((), ()))


def _params(*sem):
    return pltpu.CompilerParams(dimension_semantics=sem, vmem_limit_bytes=VMEM_LIMIT)


def _row_tile(t, pref):
    return pref if t % pref == 0 else t


def _ada_kernel(c_ref, w_ref, b_ref, o_ref):
    c = c_ref[...]
    cond = (c * jax.nn.sigmoid(c)).astype(BF16)
    o_ref[0] = jnp.dot(cond, w_ref[0].astype(BF16), preferred_element_type=F32) + b_ref[0]


def _ada(c_rows, w_ada, b_ada):
    rows = c_rows.shape[0]
    n = w_ada.shape[-1]
    tn = 1536
    return pl.pallas_call(
        _ada_kernel,
        grid=(DEPTH, n // tn),
        in_specs=[
            pl.BlockSpec((rows, D_MODEL), lambda i, j: (0, 0)),
            pl.BlockSpec((1, D_MODEL, tn), lambda i, j: (i, 0, j)),
            pl.BlockSpec((1, 1, tn), lambda i, j: (i, 0, j)),
        ],
        out_specs=pl.BlockSpec((1, rows, tn), lambda i, j: (i, 0, j)),
        out_shape=jax.ShapeDtypeStruct((DEPTH, rows, n), F32),
        compiler_params=_params("arbitrary", "arbitrary"),
        name="ada_mod",
    )(c_rows, w_ada, b_ada.reshape(DEPTH, 1, n))


def _layer_norm(z, g, b):
    mu = jnp.mean(z, axis=-1, keepdims=True)
    zc = z - mu
    var = jnp.mean(zc * zc, axis=-1, keepdims=True)
    return zc * lax.rsqrt(var + LN_EPS) * g + b


def _ml_inproj_kernel(x_ref, sc_ref, sh_ref, w_ref, pm_ref, pg_ref):
    u = x_ref[0] * (1.0 + sc_ref[0]) + sh_ref[0]
    p = jnp.dot(u.astype(BF16), w_ref[...], preferred_element_type=F32)
    pm_ref[0, :, :ML_QK] = (p[:, :ML_QK] * (ML_DK ** -0.5)).astype(BF16)
    pm_ref[0, :, ML_QK:] = p[:, ML_QK:ML_GATE_OFF].astype(BF16)
    pg_ref[0] = p[:, ML_GATE_OFF:]


def _ml_inproj(x, sc, sh, w_bf):
    b, t, d = x.shape
    n = w_bf.shape[1]
    tm = _row_tile(t, 512)
    row = pl.BlockSpec((1, tm, d), lambda i, j: (i, j, 0))
    vec = pl.BlockSpec((1, 1, d), lambda i, j: (i, 0, 0))
    return pl.pallas_call(
        _ml_inproj_kernel,
        grid=(b, t // tm),
        in_specs=[row, vec, vec, pl.BlockSpec((d, n), lambda i, j: (0, 0))],
        out_specs=[pl.BlockSpec((1, tm, ML_GATE_OFF), lambda i, j: (i, j, 0)),
                   pl.BlockSpec((1, tm, n - ML_GATE_OFF), lambda i, j: (i, j, 0))],
        out_shape=[jax.ShapeDtypeStruct((b, t, ML_GATE_OFF), BF16),
                   jax.ShapeDtypeStruct((b, t, n - ML_GATE_OFF), F32)],
        compiler_params=_params("arbitrary", "arbitrary"),
        name="ml_inproj",
    )(x, sc, sh, w_bf)


def _final_ln_kernel(x_ref, f_ref, g_ref, lng_ref, lnb_ref, o_ref):
    z = DEEPNORM_ALPHA * x_ref[0] + (1.0 + g_ref[0]) * f_ref[0]
    o_ref[0] = _layer_norm(z, lng_ref[...], lnb_ref[...])


def _final_ln(x, f, g, lng, lnb):
    b, t, d = x.shape
    tm = _row_tile(t, 1024)
    row = pl.BlockSpec((1, tm, d), lambda i, j: (i, j, 0))
    vec = pl.BlockSpec((1, 1, d), lambda i, j: (i, 0, 0))
    par = pl.BlockSpec((1, d), lambda i, j: (0, 0))
    return pl.pallas_call(
        _final_ln_kernel,
        grid=(b, t // tm),
        in_specs=[row, row, vec, par, par],
        out_specs=row,
        out_shape=jax.ShapeDtypeStruct((b, t, d), F32),
        compiler_params=_params("arbitrary", "arbitrary"),
        name="final_ln",
    )(x, f, g, lng, lnb)


def _log_sigmoid(x):
    return jnp.minimum(x, 0.0) - jnp.log1p(jnp.exp(-jnp.abs(x)))


def _mlstm_kernel(p_ref, pg_ref, gt_ref, bcol_ref, brow_ref, ng_ref, c0_ref, n0_ref, m0_ref,
                  hg_ref, c_ref, n_ref, m_ref, *, chunk):
    L = chunk

    @pl.when(pl.program_id(1) == 0)
    def _():
        c_ref[...] = c0_ref[...]
        n_ref[...] = n0_ref[...]
        m_ref[...] = m0_ref[...]

    gcol = pg_ref[0, :, :2 * ML_HEADS] + bcol_ref[...]
    grow = gt_ref[0] + brow_ref[...]
    lf_col = _log_sigmoid(gcol)
    lf_row = _log_sigmoid(grow)
    ti = lax.broadcasted_iota(I32, (L, L), 0)
    si = lax.broadcasted_iota(I32, (L, L), 1)
    causal = si <= ti
    tril = jnp.where(causal, 1.0, 0.0).astype(F32)
    triu = jnp.where(ti <= si, 1.0, 0.0).astype(F32)
    b_col = jnp.dot(tril, lf_col, precision=HIGHEST, preferred_element_type=F32)
    b_row = jnp.dot(lf_row, triu, precision=HIGHEST, preferred_element_type=F32)

    for h in range(ML_HEADS):
        qs = p_ref[0, :, h * ML_DK:(h + 1) * ML_DK]
        kb = p_ref[0, :, ML_QK + h * ML_DK:ML_QK + (h + 1) * ML_DK]
        vb = p_ref[0, :, 2 * ML_QK + h * ML_DV:2 * ML_QK + (h + 1) * ML_DV]
        v = vb.astype(F32)
        o = p_ref[0, :, 2 * ML_QK + ML_V + h * ML_DV:2 * ML_QK + ML_V + (h + 1) * ML_DV].astype(F32)
        bc = b_col[:, ML_HEADS + h:ML_HEADS + h + 1]
        ic = gcol[:, h:h + 1]
        br = b_row[ML_HEADS + h:ML_HEADS + h + 1, :]
        ir = grow[h:h + 1, :]
        c_old = c_ref[0, h]
        n_old = n_ref[0, h:h + 1, :]
        m_old = m_ref[0, h:h + 1, 0:1]

        dm = jnp.where(causal, bc - br + ir, -jnp.inf)
        inter = bc + m_old
        mt = jnp.maximum(inter, jnp.max(dm, axis=-1, keepdims=True))
        qk = lax.dot_general(qs, kb, NT_DIMS, preferred_element_type=F32)
        s = jnp.exp(dm - mt) * qk
        wp = jnp.exp(inter - mt)
        qc = lax.dot_general(qs, c_old.astype(BF16), NT_DIMS, preferred_element_type=F32)
        num = jnp.dot(s.astype(BF16), vb, preferred_element_type=F32) + wp * qc
        qn = jnp.sum(qs.astype(F32) * n_old, axis=-1, keepdims=True)
        den = jnp.sum(s, axis=-1, keepdims=True) + wp * qn
        hh = num / jnp.maximum(jnp.abs(den), jnp.exp(-mt))

        m_new = mt[L - 1:L, :]
        wk = jnp.exp(bc[L - 1:L, :] - bc + ic - m_new)
        wprev = jnp.exp(inter[L - 1:L, :] - m_new)
        vw = (v * wk).astype(BF16)
        c_ref[0, h] = wprev * c_old + lax.dot_general(vw, kb, TN_DIMS, preferred_element_type=F32)
        n_ref[0, h:h + 1, :] = wprev * n_old + jnp.sum(wk * kb.astype(F32), axis=0, keepdims=True)
        m_ref[0, h:h + 1, :] = jnp.broadcast_to(m_new, (1, ML_DK))

        mu = jnp.mean(hh, axis=-1, keepdims=True)
        hc = hh - mu
        var = jnp.mean(hc * hc, axis=-1, keepdims=True)
        hn = hc * lax.rsqrt(var + LN_EPS) * ng_ref[:, h * ML_DV:(h + 1) * ML_DV]
        hg_ref[0, :, h * ML_DV:(h + 1) * ML_DV] = (jax.nn.sigmoid(o) * hn).astype(BF16)


def _mlstm(p, pg, b_gates, norm_g, c0, n0, m0, chunk):
    b, t, n = p.shape
    nc = t // chunk
    gt = jnp.swapaxes(pg[:, :, :2 * ML_HEADS], 1, 2)
    if nc > 1:
        gt_spec = pl.BlockSpec((1, 2 * ML_HEADS, chunk), lambda i, j: (i, 0, j))
    else:
        gt_spec = pl.BlockSpec((1, 2 * ML_HEADS, t), lambda i, j: (i, 0, 0))
    if nc > 1 and chunk % LANES != 0:
        gt = gt.reshape(b, 2 * ML_HEADS, nc, chunk).transpose(0, 2, 1, 3).reshape(b * nc, 2 * ML_HEADS, chunk)
        gt_spec = pl.BlockSpec((1, 2 * ML_HEADS, chunk), lambda i, j: (i * nc + j, 0, 0))
    m0b = jnp.broadcast_to(m0[..., None], (b, ML_HEADS, ML_DK))
    cspec = pl.BlockSpec((1, ML_HEADS, ML_DV, ML_DK), lambda i, j: (i, 0, 0, 0))
    nspec = pl.BlockSpec((1, ML_HEADS, ML_DK), lambda i, j: (i, 0, 0))
    hg, c, nn, m = pl.pallas_call(
        functools.partial(_mlstm_kernel, chunk=chunk),
        grid=(b, nc),
        in_specs=[
            pl.BlockSpec((1, chunk, n), lambda i, j: (i, j, 0)),
            pl.BlockSpec((1, chunk, pg.shape[-1]), lambda i, j: (i, j, 0)),
            gt_spec,
            pl.BlockSpec((1, 2 * ML_HEADS), lambda i, j: (0, 0)),
            pl.BlockSpec((2 * ML_HEADS, 1), lambda i, j: (0, 0)),
            pl.BlockSpec((1, ML_V), lambda i, j: (0, 0)),
            cspec, nspec, nspec,
        ],
        out_specs=[pl.BlockSpec((1, chunk, ML_V), lambda i, j: (i, j, 0)), cspec, nspec, nspec],
        out_shape=[
            jax.ShapeDtypeStruct((b, t, ML_V), BF16),
            jax.ShapeDtypeStruct((b, ML_HEADS, ML_DV, ML_DK), F32),
            jax.ShapeDtypeStruct((b, ML_HEADS, ML_DK), F32),
            jax.ShapeDtypeStruct((b, ML_HEADS, ML_DK), F32),
        ],
        compiler_params=_params("arbitrary", "arbitrary"),
        name="mlstm_scan",
    )(p, pg, gt, b_gates.reshape(1, -1), b_gates.reshape(-1, 1), norm_g.reshape(1, -1), c0, n0, m0b)
    return hg, c, nn, m[..., 0]


def _outproj_kernel(a_ref, w_ref, x_ref, g_ref, sc_ref, sh_ref, lng_ref, lnb_ref, wrh_ref, wrl_ref, br_ref,
                    x1_ref, u2_ref, te_ref, tg_ref):
    y = jnp.dot(a_ref[0], w_ref[...], preferred_element_type=F32)
    z = DEEPNORM_ALPHA * x_ref[0] + (1.0 + g_ref[0]) * y
    x1 = _layer_norm(z, lng_ref[...], lnb_ref[...])
    x1_ref[0] = x1
    u2 = x1 * (1.0 + sc_ref[0]) + sh_ref[0]
    u_hi = u2.astype(BF16)
    u2_ref[0] = u_hi
    u_lo = (u2 - u_hi.astype(F32)).astype(BF16)
    logits = (jnp.dot(u_hi, wrh_ref[...], preferred_element_type=F32)
              + jnp.dot(u_lo, wrh_ref[...], preferred_element_type=F32)
              + jnp.dot(u_hi, wrl_ref[...], preferred_element_type=F32)) + br_ref[...]
    lane = lax.broadcasted_iota(I32, logits.shape, 1).astype(F32)
    vals, idxs = [], []
    cur = logits
    for _ in range(TOP_K):
        mx = jnp.max(cur, axis=-1, keepdims=True)
        idx = jnp.min(jnp.where(cur == mx, lane, float(LANES)), axis=-1, keepdims=True)
        vals.append(mx)
        idxs.append(idx)
        cur = jnp.where(lane == idx, -jnp.inf, cur)
    es = [jnp.exp(v - vals[0]) for v in vals]
    tot = es[0] + es[1] + es[2] + es[3]
    for k in range(TOP_K):
        te_ref[0, :, k:k + 1] = idxs[k].astype(I32)
        tg_ref[0, :, k:k + 1] = es[k] / tot


def _outproj(a, w_bf, x, g, sc, sh, lng, lnb, wr_pad, br_pad):
    b, t, d = x.shape
    tm = _row_tile(t, 256)
    wr_hi = wr_pad.astype(BF16)
    wr_lo = (wr_pad - wr_hi.astype(F32)).astype(BF16)
    row = pl.BlockSpec((1, tm, d), lambda i, j: (i, j, 0))
    vec = pl.BlockSpec((1, 1, d), lambda i, j: (i, 0, 0))
    par = pl.BlockSpec((1, d), lambda i, j: (0, 0))
    top = pl.BlockSpec((1, tm, TOP_K), lambda i, j: (i, j, 0))
    return pl.pallas_call(
        _outproj_kernel,
        grid=(b, t // tm),
        in_specs=[row, pl.BlockSpec((d, d), lambda i, j: (0, 0)), row, vec, vec, vec, par, par,
                  pl.BlockSpec((d, LANES), lambda i, j: (0, 0)), pl.BlockSpec((d, LANES), lambda i, j: (0, 0)),
                  pl.BlockSpec((1, LANES), lambda i, j: (0, 0))],
        out_specs=[row, row, top, top],
        out_shape=[jax.ShapeDtypeStruct((b, t, d), F32), jax.ShapeDtypeStruct((b, t, d), BF16),
                   jax.ShapeDtypeStruct((b, t, TOP_K), I32), jax.ShapeDtypeStruct((b, t, TOP_K), F32)],
        compiler_params=_params("arbitrary", "arbitrary"),
        name="outproj_ln_router",
    )(a, w_bf, x, g, sc, sh, lng, lnb, wr_hi, wr_lo, br_pad)


def _moe_kernel(x_ref, te_ref, tg_ref, w1_ref, b1_ref, w2_ref, b2_ref, o_ref, pos_ref, tri_ref, *, tm, rb):
    e = pl.program_id(1)

    @pl.when(jnp.logical_and(pl.program_id(0) == 0, e == 0))
    def _():
        si = lax.broadcasted_iota(I32, (tm, tm), 0)
        ti = lax.broadcasted_iota(I32, (tm, tm), 1)
        tri_ref[...] = jnp.where(si <= ti, 1.0, 0.0).astype(BF16)

    @pl.when(e == 0)
    def _():
        o_ref[...] = jnp.zeros_like(o_ref)
        eio = lax.broadcasted_iota(I32, (N_EXPERTS, tm), 0)
        sel = jnp.zeros((N_EXPERTS, tm), F32)
        for k in range(TOP_K):
            sel = sel + jnp.where(te_ref[k:k + 1, :] == eio, 1.0, 0.0)
        rank = jnp.dot(sel.astype(BF16), tri_ref[...], preferred_element_type=F32)
        pos_ref[...] = rank * sel

    pm = pos_ref[pl.ds(e, 1), :]
    gate = jnp.zeros((1, tm), F32)
    for k in range(TOP_K):
        gate = gate + jnp.where(te_ref[k:k + 1, :] == e, tg_ref[k:k + 1, :], 0.0)
    cnt = jnp.max(pm).astype(I32)
    nblk = (cnt + (rb - 1)) // rb

    def body(blk, carry):
        r = (blk * rb + 1 + lax.broadcasted_iota(I32, (rb, 1), 0)).astype(F32)
        hit = pm == r
        onehot = jnp.where(hit, 1.0, 0.0).astype(BF16)
        xg = jnp.dot(onehot, x_ref[...], preferred_element_type=F32).astype(BF16)
        h = jnp.dot(xg, w1_ref[0], preferred_element_type=F32) + b1_ref[0]
        hg = jnp.minimum(h[:, :D_FF], SWIGLU_LIMIT)
        hl = jnp.clip(h[:, D_FF:], -SWIGLU_LIMIT, SWIGLU_LIMIT)
        a = hg * jax.nn.sigmoid(SWIGLU_ALPHA * hg) * (hl + 1.0)
        y = jnp.dot(a.astype(BF16), w2_ref[0], preferred_element_type=F32) + b2_ref[0]
        grow = jnp.sum(jnp.where(hit, gate, 0.0), axis=-1, keepdims=True)
        yg = (y * grow).astype(BF16)
        o_ref[...] += lax.dot_general(onehot, yg, TN_DIMS, preferred_element_type=F32)
        return carry

    lax.fori_loop(0, nblk, body, 0)


def _moe(u2, te, tg, w1_bf, b1, w2_bf, b2, layer):
    n, d = u2.shape
    tm = _row_tile(n, 1024)
    rb = min(tm, tm * TOP_K // N_EXPERTS + 32)
    te_t = te.T
    tg_t = tg.T
    return pl.pallas_call(
        functools.partial(_moe_kernel, tm=tm, rb=rb),
        grid=(n // tm, N_EXPERTS),
        in_specs=[
            pl.BlockSpec((tm, d), lambda i, e: (i, 0)),
            pl.BlockSpec((TOP_K, tm), lambda i, e: (0, i)),
            pl.BlockSpec((TOP_K, tm), lambda i, e: (0, i)),
            pl.BlockSpec((None, 1, d, 2 * D_FF), lambda i, e: (layer, e, 0, 0)),
            pl.BlockSpec((None, 1, 1, 2 * D_FF), lambda i, e: (layer, e, 0, 0)),
            pl.BlockSpec((None, 1, D_FF, d), lambda i, e: (layer, e, 0, 0)),
            pl.BlockSpec((None, 1, 1, d), lambda i, e: (layer, e, 0, 0)),
        ],
        out_specs=pl.BlockSpec((tm, d), lambda i, e: (i, 0)),
        out_shape=jax.ShapeDtypeStruct((n, d), F32),
        scratch_shapes=[pltpu.VMEM((N_EXPERTS, tm), F32), pltpu.VMEM((tm, tm), BF16)],
        compiler_params=_params("arbitrary", "arbitrary"),
        name="moe_experts",
    )(u2, te_t, tg_t, w1_bf, b1.reshape(DEPTH, N_EXPERTS, 1, -1), w2_bf, b2.reshape(DEPTH, N_EXPERTS, 1, -1))


def _rope_tables(pos, head_dim):
    rd = head_dim // ROPE_FRACTION
    half = rd // 2
    inv = jnp.power(ROPE_THETA, -jnp.arange(half, dtype=F32) / half)
    ang = jnp.asarray(pos).astype(F32)[:, None] * inv[None, :]
    cos, sin = jnp.cos(ang), jnp.sin(ang)
    t = pos.shape[0]
    ones = jnp.ones((t, head_dim - rd), F32)
    zeros = jnp.zeros((t, head_dim - rd), F32)
    zh = jnp.zeros((t, half), F32)
    c = jnp.concatenate([cos, cos, ones], axis=-1)
    sa = jnp.concatenate([zh, sin, zeros], axis=-1)
    sb = jnp.concatenate([-sin, zh, zeros], axis=-1)
    rep = LANES // head_dim
    return tuple(jnp.tile(a, (1, rep)) for a in (c, sa, sb))


def _rope(z, c, sa, sb, half):
    w = z.shape[-1]
    rep = w // LANES
    if rep > 1:
        c, sa, sb = (jnp.concatenate([a] * rep, axis=-1) for a in (c, sa, sb))
    return z * c + pltpu.roll(z, half, 1) * sa + pltpu.roll(z, w - half, 1) * sb


def _dsa_inproj_kernel(x_ref, f_ref, g_ref, lng_ref, lnb_ref, sc_ref, sh_ref, w_in_ref,
                       c1_ref, sa1_ref, sb1_ref, c2_ref, sa2_ref, sb2_ref,
                       xo_ref, q_ref, kf_ref, vf_ref, kif_ref, kb_ref, vb_ref, kib_ref, qi_ref, w_ref):
    z = DEEPNORM_ALPHA * x_ref[0] + (1.0 + g_ref[0]) * f_ref[0]
    x = _layer_norm(z, lng_ref[...], lnb_ref[...])
    xo_ref[0] = x
    u = x * (1.0 + sc_ref[0]) + sh_ref[0]
    p = jnp.dot(u.astype(BF16), w_in_ref[...], preferred_element_type=F32)

    c1, sa1, sb1 = c1_ref[...], sa1_ref[...], sb1_ref[...]
    c2, sa2, sb2 = c2_ref[...], sa2_ref[...], sb2_ref[...]
    h1 = DSA_HD // ROPE_FRACTION // 2
    h2 = IDX_DIM // ROPE_FRACTION // 2
    o_k = DSA_Q
    o_v = DSA_Q + DSA_KV
    o_qi = DSA_Q + 2 * DSA_KV
    o_ki = o_qi + IDX_HEADS * IDX_DIM
    q = _rope(p[:, :DSA_Q], c1, sa1, sb1, h1)
    q_ref[0] = (q * DSA_LOG2_SCALE).astype(BF16)
    k = _rope(p[:, o_k:o_v], c1, sa1, sb1, h1)
    kf_ref[0] = k
    kb_ref[0] = k.astype(BF16)
    v = p[:, o_v:o_qi]
    vf_ref[0] = v
    vb_ref[0] = v.astype(BF16)
    qi = _rope(p[:, o_qi:o_ki], c2, sa2, sb2, h2)
    qi_ref[0] = qi.astype(BF16)
    tail = p[:, o_ki:o_ki + LANES]
    ki = _rope(tail, c2, sa2, sb2, h2)[:, :IDX_DIM]
    kif_ref[0] = ki
    kib_ref[0] = ki.astype(BF16)
    wi = tail[:, IDX_DIM:IDX_DIM + IDX_HEADS]
    w_ref[0] = (wi * (IDX_HEADS ** -0.5)) * (IDX_DIM ** -0.5)


def _dsa_inproj(x, f, g, lng, lnb, sc, sh, w_bf, pos):
    b, t, d = x.shape
    n = w_bf.shape[1]
    tm = _row_tile(t, 512)
    tabs = _rope_tables(pos, DSA_HD) + _rope_tables(pos, IDX_DIM)
    tab = pl.BlockSpec((tm, LANES), lambda i, j: (j, 0))
    vec = pl.BlockSpec((1, 1, d), lambda i, j: (i, 0, 0))
    par = pl.BlockSpec((1, d), lambda i, j: (0, 0))

    def row(w):
        return pl.BlockSpec((1, tm, w), lambda i, j: (i, j, 0))

    widths = [(d, F32), (DSA_Q, BF16), (DSA_KV, F32), (DSA_KV, F32), (IDX_DIM, F32), (DSA_KV, BF16), (DSA_KV, BF16),
              (IDX_DIM, BF16), (IDX_HEADS * IDX_DIM, BF16), (IDX_HEADS, F32)]
    return pl.pallas_call(
        _dsa_inproj_kernel,
        grid=(b, t // tm),
        in_specs=[row(d), row(d), vec, par, par, vec, vec, pl.BlockSpec((d, n), lambda i, j: (0, 0))] + [tab] * 6,
        out_specs=[row(w) for w, _ in widths],
        out_shape=[jax.ShapeDtypeStruct((b, t, w), dt) for w, dt in widths],
        compiler_params=_params("arbitrary", "arbitrary"),
        name="ln_inproj_rope_split",
    )(x, f, g, lng, lnb, sc, sh, w_bf, *tabs)


def _f2key(f):
    bits = lax.bitcast_convert_type(f, I32)
    return bits ^ ((bits >> 31) & 0x7FFFFFFF)


def _key2f(k):
    return lax.bitcast_convert_type(k ^ ((k >> 31) & 0x7FFFFFFF), F32)


def _dsa_kernel(nkt_ref, q_ref, qi_ref, w_ref, k_ref, v_ref, ki_ref, kpos_ref, qlim_ref, nadm_ref, o_ref,
                s_ref, wb_ref, qg_ref, m_ref, l_ref, acc_ref, flag_ref, *, tq, tk, nsel, value_iters):
    nkt = nkt_ref[pl.program_id(1)]
    qlim = qlim_ref[...]
    w = w_ref[0]
    qi = qi_ref[0]
    qis = [qi[:, h * IDX_DIM:(h + 1) * IDX_DIM] for h in range(IDX_HEADS)]
    nlg = tk // LANES
    for h in range(IDX_HEADS):
        wb_ref[h] = jnp.broadcast_to(w[:, h:h + 1], (tq, LANES))

    def p1(kt, carry):
        rmax, rmin, cgt0, cge0 = carry
        off = pl.multiple_of(kt * tk, tk)
        kit = ki_ref[0, pl.ds(off, tk), :]
        acc = jnp.zeros((tq, tk), F32)
        for h in range(IDX_HEADS):
            d = lax.dot_general(qis[h], kit, NT_DIMS, preferred_element_type=F32)
            acc = acc + jnp.maximum(d, 0.0) * jnp.concatenate([wb_ref[h]] * nlg, axis=1)
        s = jnp.where(kpos_ref[kt] < qlim, acc, -jnp.inf)
        s_ref[kt] = s
        gt0 = jnp.where(s > 0.0, 1.0, 0.0)
        ge0 = jnp.where(s >= 0.0, 1.0, 0.0)
        for g in range(nlg):
            lanes = slice(g * LANES, (g + 1) * LANES)
            rmax = jnp.maximum(rmax, s[:, lanes])
            rmin = jnp.minimum(rmin, acc[:, lanes])
            cgt0 = cgt0 + gt0[:, lanes]
            cge0 = cge0 + ge0[:, lanes]
        return rmax, rmin, cgt0, cge0

    zcol = jnp.zeros((tq, 1), F32)
    zlan = jnp.zeros((tq, LANES), F32)
    init = (jnp.full((tq, LANES), -jnp.inf, F32), jnp.full((tq, LANES), jnp.inf, F32), zlan, zlan)
    rmax, rmin, cgt0, cge0 = lax.fori_loop(0, nkt, p1, init)
    rmax = jnp.max(rmax, axis=-1, keepdims=True)
    rmin = jnp.min(rmin, axis=-1, keepdims=True)
    cgt0 = jnp.sum(cgt0, axis=-1, keepdims=True)
    cge0 = jnp.sum(cge0, axis=-1, keepdims=True)
    nadm = nadm_ref[...]

    rg = min(tq, 32)
    fsel = float(nsel)
    unbounded = float(2 ** 30)

    def count_ge(tb):
        def body(kt, c):
            out = []
            for r in range(tq // rg):
                tr = jnp.concatenate([tb[r * rg:(r + 1) * rg, :]] * nlg, axis=1)
                hit = jnp.where(s_ref[kt, r * rg:(r + 1) * rg, :] >= tr, 1.0, 0.0)
                cr = c[r * rg:(r + 1) * rg]
                for g in range(nlg):
                    cr = cr + hit[:, g * LANES:(g + 1) * LANES]
                out.append(cr)
            return jnp.concatenate(out, axis=0)
        c = lax.fori_loop(0, nkt, body, zlan)
        return jnp.broadcast_to(jnp.sum(c, axis=-1, keepdims=True), (tq, LANES))

    rmax, rmin, cgt0, cge0, nadm = (jnp.broadcast_to(a, (tq, LANES)) for a in (rmax, rmin, cgt0, cge0, nadm))
    few = nadm <= fsel
    many = jnp.logical_not(few)
    pos = jnp.logical_and(many, cgt0 > fsel)
    neg = jnp.logical_and(many, cge0 < fsel)
    zero = jnp.logical_and(many, jnp.logical_and(cgt0 <= fsel, cge0 >= fsel))
    lo0 = jnp.where(pos, 0, _f2key(jnp.where(neg, rmin, 0.0)))
    hi0 = jnp.where(pos, _f2key(jnp.where(pos, rmax, 0.0)) + 1, 0)
    clo0 = jnp.where(pos, cge0, nadm)
    chi0 = jnp.where(pos, 0.0, cge0)
    act0 = jnp.where(jnp.logical_or(pos, neg), 1.0, 0.0)
    t0 = jnp.where(few, -jnp.inf, 0.0)
    need0 = jnp.where(zero, fsel - cgt0, unbounded)

    def cond(st):
        return st[0] > 0.0

    def step(st):
        _, it, lo, hi, clo, chi, act, thr, need = st
        conv = jnp.logical_and(act > 0.0, lo + 1 >= hi)
        thr = jnp.where(conv, _key2f(lo), thr)
        need = jnp.where(conv, fsel - chi, need)
        act = jnp.where(conv, 0.0, act)
        flo, fhi = _key2f(lo), _key2f(hi)
        mid_i = (lo >> 1) + (hi >> 1) + (lo & hi & 1)
        lclo = jnp.log(clo)
        interp = (lclo - float(np.log(nsel - 0.5))) / (lclo - jnp.log(jnp.maximum(chi, 0.5)))
        frac = jnp.where(it % 3 == 2, 0.5, interp)
        mid_v = _f2key(flo + frac * (fhi - flo))
        use_v = jnp.logical_and(it < value_iters, jnp.logical_and(mid_v > lo, mid_v < hi))
        mid = jnp.where(use_v, mid_v, mid_i)
        tm = _key2f(mid)
        c = count_ge(jnp.where(act > 0.0, tm, jnp.inf))
        hit = jnp.logical_and(act > 0.0, c == fsel)
        thr = jnp.where(hit, tm, thr)
        up = jnp.logical_and(act > 0.0, c > fsel)
        dn = jnp.logical_and(act > 0.0, c < fsel)
        lo = jnp.where(up, mid, lo)
        clo = jnp.where(up, c, clo)
        hi = jnp.where(dn, mid, hi)
        chi = jnp.where(dn, c, chi)
        act = jnp.where(hit, 0.0, act)
        return jnp.max(act), it + 1, lo, hi, clo, chi, act, thr, need

    st = (jnp.max(act0), jnp.int32(0), lo0, hi0, clo0, chi0, act0, t0, need0)
    _, _, _, _, _, _, _, thr, need = lax.while_loop(cond, step, st)
    thr, need = thr[:, 0:1], need[:, 0:1]
    tied = need < unbounded * 0.5
    has_ties = jnp.max(jnp.where(tied, 1.0, 0.0)) > 0.0
    thr = jnp.maximum(thr, -F32_MAX)

    for g in range(DSA_KV_HEADS):
        for hh in range(DSA_GROUP):
            hd = g * DSA_GROUP + hh
            qg_ref[g, hh * tq:(hh + 1) * tq, :] = q_ref[0, :, hd * DSA_HD:(hd + 1) * DSA_HD]
    m_ref[...] = jnp.full(m_ref.shape, NEG_BIG, F32)
    l_ref[...] = jnp.zeros(l_ref.shape, F32)
    acc_ref[...] = jnp.zeros(acc_ref.shape, F32)

    ones_cols = jnp.ones((tk, DSA_HD), BF16)

    def attend(kt, bias):
        off = pl.multiple_of(kt * tk, tk)
        qk = [lax.dot_general(qg_ref[g], k_ref[0, pl.ds(off, tk), g * DSA_HD:(g + 1) * DSA_HD], NT_DIMS,
                              preferred_element_type=F32) for g in range(DSA_KV_HEADS)]
        for g in range(DSA_KV_HEADS):
            vt_g = jnp.concatenate([v_ref[0, pl.ds(off, tk), g * DSA_HD:(g + 1) * DSA_HD], ones_cols], axis=1)
            for hh in range(DSA_GROUP):
                rows = slice(hh * tq, (hh + 1) * tq)
                lg = qk[g][rows] + bias
                m_old = m_ref[g, rows, :]
                m_new = jnp.maximum(m_old, jnp.max(lg, axis=-1, keepdims=True))
                alpha = jnp.exp2(m_old - m_new)
                p = jnp.exp2(lg - jnp.concatenate([m_new] * nlg, axis=1))
                pv = jnp.dot(p.astype(BF16), vt_g, preferred_element_type=F32)
                acc_ref[g, rows, :] = alpha * acc_ref[g, rows, :] + pv[:, :DSA_HD]
                l_ref[g, rows, :] = alpha * l_ref[g, rows, :] + pv[:, DSA_HD:]
                m_ref[g, rows, :] = m_new

    @pl.when(jnp.logical_not(has_ties))
    def _():
        def body(kt, carry):
            attend(kt, jnp.where(s_ref[kt] >= thr, 0.0, NEG_BIG))
            return carry
        lax.fori_loop(0, nkt, body, 0)

    @pl.when(has_ties)
    def _():
        def scan(kt, carry):
            seen, cut, nin = carry
            eqc = jnp.sum(jnp.where(s_ref[kt] == thr, 1.0, 0.0), axis=-1, keepdims=True)
            here = jnp.logical_and(tied, jnp.logical_and(seen < need, seen + eqc >= need))
            cut = jnp.where(here, kt, cut)
            nin = jnp.where(here, need - seen, nin)
            flag_ref[kt] = jnp.max(jnp.where(here, 1.0, 0.0)).astype(I32)
            return seen + eqc, cut, nin

        cut0 = jnp.where(tied, -1, nkt).astype(I32)
        _, cut, nin = lax.fori_loop(0, nkt, scan, (zcol, cut0, zcol))

        def body(kt, carry):
            s = s_ref[kt]
            before = jnp.where(kt < cut, 0.0, NEG_BIG)

            def plain():
                return jnp.where(s == thr, before, NEG_BIG)

            def ranked():
                si = lax.broadcasted_iota(I32, (tk, tk), 0)
                ti = lax.broadcasted_iota(I32, (tk, tk), 1)
                triu = jnp.where(si <= ti, 1.0, 0.0).astype(BF16)
                eq = jnp.where(s == thr, 1.0, 0.0).astype(BF16)
                rank = jnp.dot(eq, triu, preferred_element_type=F32)
                inside = jnp.where(rank <= nin, 0.0, NEG_BIG)
                return jnp.where(s == thr, jnp.where(kt == cut, inside, before), NEG_BIG)

            tie_bias = lax.cond(flag_ref[kt] > 0, ranked, plain)
            attend(kt, jnp.where(s > thr, 0.0, tie_bias))
            return carry
        lax.fori_loop(0, nkt, body, 0)

    for g in range(DSA_KV_HEADS):
        og = acc_ref[g] / l_ref[g]
        for hh in range(DSA_GROUP):
            hd = g * DSA_GROUP + hh
            o_ref[0, :, hd * DSA_HD:(hd + 1) * DSA_HD] = og[hh * tq:(hh + 1) * tq, :].astype(BF16)


def _dsa_attend(q, qi, w, k, v, ki, kpos, qlim, tq, tk):
    b, t, _ = q.shape
    lp = k.shape[1]
    nq = t // tq
    nsel = min(DSA_TOPK, int(np.sum(kpos < np.iinfo(np.int32).max)) // 4)
    qmax = qlim.reshape(nq, tq).max(axis=1)
    nadm = np.searchsorted(kpos, qlim, side="left")
    nkt = np.minimum(-(-np.searchsorted(kpos, qmax, side="left") // tk), lp // tk).astype(np.int32)
    nkt = np.maximum(nkt, 1)
    single = pl.Buffered(1)

    def full(wd):
        return pl.BlockSpec((1, lp, wd), lambda i, j, n: (i, 0, 0), pipeline_mode=single)

    def row(wd):
        return pl.BlockSpec((1, tq, wd), lambda i, j, n: (i, j, 0))

    grid_spec = pltpu.PrefetchScalarGridSpec(
        num_scalar_prefetch=1,
        grid=(b, nq),
        in_specs=[row(DSA_Q), row(IDX_HEADS * IDX_DIM), row(IDX_HEADS), full(DSA_KV), full(DSA_KV), full(IDX_DIM),
                  pl.BlockSpec((lp // tk, 1, tk), lambda i, j, n: (0, 0, 0)),
                  pl.BlockSpec((tq, 1), lambda i, j, n: (j, 0)),
                  pl.BlockSpec((tq, 1), lambda i, j, n: (j, 0))],
        out_specs=row(DSA_Q),
        scratch_shapes=[
            pltpu.VMEM((lp // tk, tq, tk), F32),
            pltpu.VMEM((IDX_HEADS, tq, LANES), F32),
            pltpu.VMEM((DSA_KV_HEADS, DSA_GROUP * tq, DSA_HD), BF16),
            pltpu.VMEM((DSA_KV_HEADS, DSA_GROUP * tq, LANES), F32),
            pltpu.VMEM((DSA_KV_HEADS, DSA_GROUP * tq, LANES), F32),
            pltpu.VMEM((DSA_KV_HEADS, DSA_GROUP * tq, DSA_HD), F32),
            pltpu.SMEM((lp // tk,), I32),
        ],
    )
    return pl.pallas_call(
        functools.partial(_dsa_kernel, tq=tq, tk=tk, nsel=nsel, value_iters=16),
        grid_spec=grid_spec,
        out_shape=jax.ShapeDtypeStruct((b, t, DSA_Q), BF16),
        compiler_params=_params("arbitrary", "arbitrary"),
        name="dsa_select_attend",
    )(jnp.asarray(nkt), q, qi, w, k, v, ki, jnp.asarray(kpos, I32).reshape(lp // tk, 1, tk),
      jnp.asarray(qlim, I32).reshape(t, 1), jnp.asarray(nadm, F32).reshape(t, 1))


def _trunk(x, mods, pos, ml_state, ds_cache, prm):
    b, t, d = x.shape
    chunk = next((c for c in (ML_EVAL_CHUNK, CHUNK) if t % c == 0), t)

    def mod(i, k):
        return mods[i][:, None, k * d:(k + 1) * d]

    p, pg = _ml_inproj(x, mod(0, 1), mod(0, 0), prm["ml_w_in"])
    if ml_state is None:
        c0 = jnp.zeros((b, ML_HEADS, ML_DV, ML_DK), F32)
        n0 = jnp.zeros((b, ML_HEADS, ML_DK), F32)
        m0 = jnp.zeros((b, ML_HEADS), F32)
    else:
        c0, n0, m0 = (s[0] for s in ml_state)
    hg, c_new, n_new, m_new = _mlstm(p, pg, prm["ml_b_gates"], prm["ml_norm_g"], c0, n0, m0, chunk)
    x1, u2, te, tg = _outproj(hg, prm["ml_w_out"], x, mod(0, 2), mod(0, 4), mod(0, 3),
                              prm["ln_g"][0, 0:1], prm["ln_b"][0, 0:1], prm["w_router"][0], prm["b_router"][0])
    f = _moe(u2.reshape(b * t, d), te.reshape(b * t, TOP_K), tg.reshape(b * t, TOP_K),
             prm["moe_w1"], prm["moe_b1"], prm["moe_w2"], prm["moe_b2"], 0).reshape(b, t, d)

    x2, q, kf, vf, kif, kb, vb, kib, qi, w = _dsa_inproj(
        x1, f, mod(0, 5), prm["ln_g"][0, 1:2], prm["ln_b"][0, 1:2], mod(1, 1), mod(1, 0), prm["ds_w_in"], pos)
    pos_np = np.asarray(pos)
    qlim = (pos_np // CHUNK + 1) * CHUNK
    if ds_cache is None:
        kpos = pos_np
        tq, tk = 128, 1024
        if t % tk:
            tq, tk = t, t
        k_all, v_all, ki_all = kb, vb, kib
    else:
        ck, cv, cki = ds_cache
        past = ck.shape[1]
        total = past + t
        lp = -(-total // (3 * LANES)) * (3 * LANES)
        tq, tk = t, 3 * LANES
        kpos = np.concatenate([np.arange(past), pos_np, np.full((lp - total,), np.iinfo(np.int32).max)])

        def cat(cache, new):
            full = jnp.concatenate([cache.reshape(b, past, -1).astype(BF16), new], axis=1)
            return jnp.pad(full, ((0, 0), (0, lp - total), (0, 0)))

        k_all, v_all, ki_all = cat(ck, kb), cat(cv, vb), cat(cki, kib)
    o = _dsa_attend(q, qi, w, k_all, v_all, ki_all, kpos.astype(np.int64), qlim, tq, tk)
    x3, u2, te, tg = _outproj(o, prm["ds_w_out"], x2, mod(1, 2), mod(1, 4), mod(1, 3),
                              prm["ln_g"][1, 0:1], prm["ln_b"][1, 0:1], prm["w_router"][1], prm["b_router"][1])
    f = _moe(u2.reshape(b * t, d), te.reshape(b * t, TOP_K), tg.reshape(b * t, TOP_K),
             prm["moe_w1"], prm["moe_b1"], prm["moe_w2"], prm["moe_b2"], 1).reshape(b, t, d)
    y = _final_ln(x3, f, mod(1, 5), prm["ln_g"][1, 1:2], prm["ln_b"][1, 1:2])

    ml_out = (c_new[None], n_new[None], m_new[None])
    ds_out = (kf.reshape(1, b, t, DSA_KV_HEADS, DSA_HD), vf.reshape(1, b, t, DSA_KV_HEADS, DSA_HD), kif[None])
    return y, ml_out, ds_out


def _prepare(w_ada, b_ada, ln_g, ln_b, ml_w_in, ml_b_gates, ml_norm_g, ml_w_out, ds_w_in, ds_w_out,
             moe_w_router, moe_b_router, moe_w1, moe_b1, moe_w2, moe_b2):
    return {
        "ln_g": ln_g, "ln_b": ln_b,
        "ml_w_in": jnp.pad(ml_w_in[0], ((0, 0), (0, ML_IN_PAD - ML_IN))).astype(BF16),
        "ml_b_gates": ml_b_gates[0], "ml_norm_g": ml_norm_g[0],
        "ml_w_out": ml_w_out[0].astype(BF16),
        "ds_w_in": jnp.pad(ds_w_in[0], ((0, 0), (0, DSA_IN_PAD - DSA_IN))).astype(BF16),
        "ds_w_out": ds_w_out[0].astype(BF16),
        "w_router": jnp.pad(moe_w_router, ((0, 0), (0, 0), (0, LANES - N_EXPERTS))),
        "b_router": jnp.pad(moe_b_router, ((0, 0), (0, LANES - N_EXPERTS)), constant_values=NEG_BIG)[:, None, :],
        "moe_w1": moe_w1.astype(BF16), "moe_b1": moe_b1,
        "moe_w2": moe_w2.astype(BF16), "moe_b2": moe_b2,
    }


def kernel(x_prompt, x_sample, state_mlstm_C, state_mlstm_n, state_mlstm_m, cache_dsa_k, cache_dsa_v,
           cache_dsa_kidx, c_prompt, c_sample, w_ada, b_ada, ln_g, ln_b, ml_w_in, ml_b_gates, ml_norm_g,
           ml_w_out, ds_w_in, ds_w_out, moe_w_router, moe_b_router, moe_w1, moe_b1, moe_w2, moe_b2):
    prm = _prepare(w_ada, b_ada, ln_g, ln_b, ml_w_in, ml_b_gates, ml_norm_g, ml_w_out, ds_w_in, ds_w_out,
                   moe_w_router, moe_b_router, moe_w1, moe_b1, moe_w2, moe_b2)
    bp, bs = c_prompt.shape[0], c_sample.shape[0]
    rows = -(-(bp + bs) // 8) * 8
    c_rows = jnp.pad(jnp.concatenate([c_prompt, c_sample], axis=0), ((0, rows - bp - bs), (0, 0)))
    mods = _ada(c_rows, w_ada, b_ada)
    mods_p = [mods[i, :bp] for i in range(DEPTH)]
    mods_s = [mods[i, bp:bp + bs] for i in range(DEPTH)]

    pos_p = np.arange(x_prompt.shape[1], dtype=np.int32)
    y_p, ml_p, ds_p = _trunk(x_prompt, mods_p, pos_p, None, None, prm)
    past = cache_dsa_k.shape[2]
    pos_s = past + np.arange(x_sample.shape[1], dtype=np.int32)
    y_s, ml_s, ds_s = _trunk(x_sample, mods_s, pos_s, (state_mlstm_C, state_mlstm_n, state_mlstm_m),
                             (cache_dsa_k[0], cache_dsa_v[0], cache_dsa_kidx[0]), prm)
    return (y_p, y_s, ml_p[0], ml_p[1], ml_p[2], ds_p[0], ds_p[1], ds_p[2],
            ml_s[0], ml_s[1], ml_s[2], ds_s[0], ds_s[1], ds_s[2])
```

```python
import functools

import numpy as np
import jax
import jax.numpy as jnp
from jax import lax
from jax.experimental import pallas as pl
from jax.experimental.pallas import tpu as pltpu

F32 = jnp.float32
BF16 = jnp.bfloat16
I32 = jnp.int32

D_MODEL = 1024
DEPTH = 2
CHUNK = 64
ML_HEADS = 4
ML_DV = D_MODEL // ML_HEADS
ML_DK = ML_DV // 2
ML_QK = ML_HEADS * ML_DK
ML_V = ML_HEADS * ML_DV
ML_IN = 2 * ML_QK + 2 * ML_V + 2 * ML_HEADS
ML_GATE_OFF = 2 * ML_QK + 2 * ML_V
ML_EVAL_CHUNK = 256
DSA_HEADS = 8
DSA_KV_HEADS = 2
DSA_HD = D_MODEL // DSA_HEADS
DSA_GROUP = DSA_HEADS // DSA_KV_HEADS
IDX_HEADS = 8
IDX_DIM = 64
DSA_TOPK = 256
DSA_Q = DSA_HEADS * DSA_HD
DSA_KV = DSA_KV_HEADS * DSA_HD
DSA_IN = DSA_Q + 2 * DSA_KV + IDX_HEADS * IDX_DIM + IDX_DIM + IDX_HEADS
DSA_LOG2_SCALE = (DSA_HD ** -0.5) * float(np.log2(np.e))
ROPE_THETA = 500000.0
ROPE_FRACTION = 4
N_EXPERTS = 32
TOP_K = 4
D_FF = D_MODEL
SWIGLU_LIMIT = 7.0
SWIGLU_ALPHA = 1.702
DEEPNORM_ALPHA = (2 * DEPTH) ** 0.25
LN_EPS = 1e-5

LANES = 128
V7X_VMEM_BYTES = 64 * 1024 * 1024
VMEM_LIMIT = V7X_VMEM_BYTES - 8 * 1024 * 1024

PROJ_ROW_TILE = 512
OUTPROJ_ROW_TILE = 256
FINAL_LN_ROW_TILE = 1024
ADA_COL_TILE = 1536
MOE_TOKEN_TILE = 1024
DSA_Q_TILE = 128
DSA_K_TILE = 1024
DSA_VALUE_ITERS = 16

ML_IN_PAD = -(-ML_IN // LANES) * LANES
DSA_IN_PAD = -(-DSA_IN // LANES) * LANES
NEG_BIG = -1e30
F32_MAX = float(np.finfo(np.float32).max)
HIGHEST = lax.Precision.HIGHEST
NT_DIMS = (((1,), (1,)), ((), ()))
TN_DIMS = (((0,), (0,)), ((), ()))


def _params(*sem):
    return pltpu.CompilerParams(dimension_semantics=sem, vmem_limit_bytes=VMEM_LIMIT)


def _row_tile(t, pref):
    return pref if t % pref == 0 else t


def _ada_kernel(c_ref, w_ref, b_ref, o_ref):
    c = c_ref[...]
    cond = (c * jax.nn.sigmoid(c)).astype(BF16)
    o_ref[0] = jnp.dot(cond, w_ref[0].astype(BF16), preferred_element_type=F32) + b_ref[0]


def _ada(c_rows, w_ada, b_ada):
    rows = c_rows.shape[0]
    n = w_ada.shape[-1]
    tn = ADA_COL_TILE
    return pl.pallas_call(
        _ada_kernel,
        grid=(DEPTH, n // tn),
        in_specs=[
            pl.BlockSpec((rows, D_MODEL), lambda i, j: (0, 0)),
            pl.BlockSpec((1, D_MODEL, tn), lambda i, j: (i, 0, j)),
            pl.BlockSpec((1, 1, tn), lambda i, j: (i, 0, j)),
        ],
        out_specs=pl.BlockSpec((1, rows, tn), lambda i, j: (i, 0, j)),
        out_shape=jax.ShapeDtypeStruct((DEPTH, rows, n), F32),
        compiler_params=_params("arbitrary", "arbitrary"),
        name="ada_mod",
    )(c_rows, w_ada, b_ada.reshape(DEPTH, 1, n))


def _layer_norm(z, g, b):
    mu = jnp.mean(z, axis=-1, keepdims=True)
    zc = z - mu
    var = jnp.mean(zc * zc, axis=-1, keepdims=True)
    return zc * lax.rsqrt(var + LN_EPS) * g + b


def _ml_inproj_kernel(x_ref, sc_ref, sh_ref, w_ref, pm_ref, pg_ref):
    u = x_ref[0] * (1.0 + sc_ref[0]) + sh_ref[0]
    p = jnp.dot(u.astype(BF16), w_ref[...], preferred_element_type=F32)
    pm_ref[0, :, :ML_QK] = (p[:, :ML_QK] * (ML_DK ** -0.5)).astype(BF16)
    pm_ref[0, :, ML_QK:] = p[:, ML_QK:ML_GATE_OFF].astype(BF16)
    pg_ref[0] = p[:, ML_GATE_OFF:]


def _ml_inproj(x, sc, sh, w_bf):
    b, t, d = x.shape
    n = w_bf.shape[1]
    tm = _row_tile(t, PROJ_ROW_TILE)
    row = pl.BlockSpec((1, tm, d), lambda i, j: (i, j, 0))
    vec = pl.BlockSpec((1, 1, d), lambda i, j: (i, 0, 0))
    return pl.pallas_call(
        _ml_inproj_kernel,
        grid=(b, t // tm),
        in_specs=[row, vec, vec, pl.BlockSpec((d, n), lambda i, j: (0, 0))],
        out_specs=[pl.BlockSpec((1, tm, ML_GATE_OFF), lambda i, j: (i, j, 0)),
                   pl.BlockSpec((1, tm, n - ML_GATE_OFF), lambda i, j: (i, j, 0))],
        out_shape=[jax.ShapeDtypeStruct((b, t, ML_GATE_OFF), BF16),
                   jax.ShapeDtypeStruct((b, t, n - ML_GATE_OFF), F32)],
        compiler_params=_params("arbitrary", "arbitrary"),
        name="ml_inproj",
    )(x, sc, sh, w_bf)


def _final_ln_kernel(x_ref, f_ref, g_ref, lng_ref, lnb_ref, o_ref):
    z = DEEPNORM_ALPHA * x_ref[0] + (1.0 + g_ref[0]) * f_ref[0]
    o_ref[0] = _layer_norm(z, lng_ref[...], lnb_ref[...])


def _final_ln(x, f, g, lng, lnb):
    b, t, d = x.shape
    tm = _row_tile(t, FINAL_LN_ROW_TILE)
    row = pl.BlockSpec((1, tm, d), lambda i, j: (i, j, 0))
    vec = pl.BlockSpec((1, 1, d), lambda i, j: (i, 0, 0))
    par = pl.BlockSpec((1, d), lambda i, j: (0, 0))
    return pl.pallas_call(
        _final_ln_kernel,
        grid=(b, t // tm),
        in_specs=[row, row, vec, par, par],
        out_specs=row,
        out_shape=jax.ShapeDtypeStruct((b, t, d), F32),
        compiler_params=_params("arbitrary", "arbitrary"),
        name="final_ln",
    )(x, f, g, lng, lnb)


def _log_sigmoid(x):
    return jnp.minimum(x, 0.0) - jnp.log1p(jnp.exp(-jnp.abs(x)))


def _mlstm_kernel(p_ref, pg_ref, gt_ref, bcol_ref, brow_ref, ng_ref, c0_ref, n0_ref, m0_ref,
                  hg_ref, c_ref, n_ref, m_ref, *, chunk):
    L = chunk

    @pl.when(pl.program_id(1) == 0)
    def _():
        c_ref[...] = c0_ref[...]
        n_ref[...] = n0_ref[...]
        m_ref[...] = m0_ref[...]

    gcol = pg_ref[0, :, :2 * ML_HEADS] + bcol_ref[...]
    grow = gt_ref[0] + brow_ref[...]
    lf_col = _log_sigmoid(gcol)
    lf_row = _log_sigmoid(grow)
    ti = lax.broadcasted_iota(I32, (L, L), 0)
    si = lax.broadcasted_iota(I32, (L, L), 1)
    causal = si <= ti
    tril = jnp.where(causal, 1.0, 0.0).astype(F32)
    triu = jnp.where(ti <= si, 1.0, 0.0).astype(F32)
    b_col = jnp.dot(tril, lf_col, precision=HIGHEST, preferred_element_type=F32)
    b_row = jnp.dot(lf_row, triu, precision=HIGHEST, preferred_element_type=F32)

    for h in range(ML_HEADS):
        qs = p_ref[0, :, h * ML_DK:(h + 1) * ML_DK]
        kb = p_ref[0, :, ML_QK + h * ML_DK:ML_QK + (h + 1) * ML_DK]
        vb = p_ref[0, :, 2 * ML_QK + h * ML_DV:2 * ML_QK + (h + 1) * ML_DV]
        v = vb.astype(F32)
        o = p_ref[0, :, 2 * ML_QK + ML_V + h * ML_DV:2 * ML_QK + ML_V + (h + 1) * ML_DV].astype(F32)
        bc = b_col[:, ML_HEADS + h:ML_HEADS + h + 1]
        ic = gcol[:, h:h + 1]
        br = b_row[ML_HEADS + h:ML_HEADS + h + 1, :]
        ir = grow[h:h + 1, :]
        c_old = c_ref[0, h]
        n_old = n_ref[0, h:h + 1, :]
        m_old = m_ref[0, h:h + 1, 0:1]

        dm = jnp.where(causal, bc - br + ir, -jnp.inf)
        inter = bc + m_old
        mt = jnp.maximum(inter, jnp.max(dm, axis=-1, keepdims=True))
        qk = lax.dot_general(qs, kb, NT_DIMS, preferred_element_type=F32)
        s = jnp.exp(dm - mt) * qk
        wp = jnp.exp(inter - mt)
        qc = lax.dot_general(qs, c_old.astype(BF16), NT_DIMS, preferred_element_type=F32)
        num = jnp.dot(s.astype(BF16), vb, preferred_element_type=F32) + wp * qc
        qn = jnp.sum(qs.astype(F32) * n_old, axis=-1, keepdims=True)
        den = jnp.sum(s, axis=-1, keepdims=True) + wp * qn
        hh = num / jnp.maximum(jnp.abs(den), jnp.exp(-mt))

        m_new = mt[L - 1:L, :]
        wk = jnp.exp(bc[L - 1:L, :] - bc + ic - m_new)
        wprev = jnp.exp(inter[L - 1:L, :] - m_new)
        vw = (v * wk).astype(BF16)
        c_ref[0, h] = wprev * c_old + lax.dot_general(vw, kb, TN_DIMS, preferred_element_type=F32)
        n_ref[0, h:h + 1, :] = wprev * n_old + jnp.sum(wk * kb.astype(F32), axis=0, keepdims=True)
        m_ref[0, h:h + 1, :] = jnp.broadcast_to(m_new, (1, ML_DK))

        mu = jnp.mean(hh, axis=-1, keepdims=True)
        hc = hh - mu
        var = jnp.mean(hc * hc, axis=-1, keepdims=True)
        hn = hc * lax.rsqrt(var + LN_EPS) * ng_ref[:, h * ML_DV:(h + 1) * ML_DV]
        hg_ref[0, :, h * ML_DV:(h + 1) * ML_DV] = (jax.nn.sigmoid(o) * hn).astype(BF16)


def _mlstm(p, pg, b_gates, norm_g, c0, n0, m0, chunk):
    b, t, n = p.shape
    nc = t // chunk
    gt = jnp.swapaxes(pg[:, :, :2 * ML_HEADS], 1, 2)
    if nc > 1:
        gt_spec = pl.BlockSpec((1, 2 * ML_HEADS, chunk), lambda i, j: (i, 0, j))
    else:
        gt_spec = pl.BlockSpec((1, 2 * ML_HEADS, t), lambda i, j: (i, 0, 0))
    if nc > 1 and chunk % LANES != 0:
        gt = gt.reshape(b, 2 * ML_HEADS, nc, chunk).transpose(0, 2, 1, 3).reshape(b * nc, 2 * ML_HEADS, chunk)
        gt_spec = pl.BlockSpec((1, 2 * ML_HEADS, chunk), lambda i, j: (i * nc + j, 0, 0))
    m0b = jnp.broadcast_to(m0[..., None], (b, ML_HEADS, ML_DK))
    cspec = pl.BlockSpec((1, ML_HEADS, ML_DV, ML_DK), lambda i, j: (i, 0, 0, 0))
    nspec = pl.BlockSpec((1, ML_HEADS, ML_DK), lambda i, j: (i, 0, 0))
    hg, c, nn, m = pl.pallas_call(
        functools.partial(_mlstm_kernel, chunk=chunk),
        grid=(b, nc),
        in_specs=[
            pl.BlockSpec((1, chunk, n), lambda i, j: (i, j, 0)),
            pl.BlockSpec((1, chunk, pg.shape[-1]), lambda i, j: (i, j, 0)),
            gt_spec,
            pl.BlockSpec((1, 2 * ML_HEADS), lambda i, j: (0, 0)),
            pl.BlockSpec((2 * ML_HEADS, 1), lambda i, j: (0, 0)),
            pl.BlockSpec((1, ML_V), lambda i, j: (0, 0)),
            cspec, nspec, nspec,
        ],
        out_specs=[pl.BlockSpec((1, chunk, ML_V), lambda i, j: (i, j, 0)), cspec, nspec, nspec],
        out_shape=[
            jax.ShapeDtypeStruct((b, t, ML_V), BF16),
            jax.ShapeDtypeStruct((b, ML_HEADS, ML_DV, ML_DK), F32),
            jax.ShapeDtypeStruct((b, ML_HEADS, ML_DK), F32),
            jax.ShapeDtypeStruct((b, ML_HEADS, ML_DK), F32),
        ],
        compiler_params=_params("arbitrary", "arbitrary"),
        name="mlstm_scan",
    )(p, pg, gt, b_gates.reshape(1, -1), b_gates.reshape(-1, 1), norm_g.reshape(1, -1), c0, n0, m0b)
    return hg, c, nn, m[..., 0]


def _outproj_kernel(a_ref, w_ref, x_ref, g_ref, sc_ref, sh_ref, lng_ref, lnb_ref, wrh_ref, wrl_ref, br_ref,
                    x1_ref, u2_ref, te_ref, tg_ref):
    y = jnp.dot(a_ref[0], w_ref[...], preferred_element_type=F32)
    z = DEEPNORM_ALPHA * x_ref[0] + (1.0 + g_ref[0]) * y
    x1 = _layer_norm(z, lng_ref[...], lnb_ref[...])
    x1_ref[0] = x1
    u2 = x1 * (1.0 + sc_ref[0]) + sh_ref[0]
    u_hi = u2.astype(BF16)
    u2_ref[0] = u_hi
    u_lo = (u2 - u_hi.astype(F32)).astype(BF16)
    logits = (jnp.dot(u_hi, wrh_ref[...], preferred_element_type=F32)
              + jnp.dot(u_lo, wrh_ref[...], preferred_element_type=F32)
              + jnp.dot(u_hi, wrl_ref[...], preferred_element_type=F32)) + br_ref[...]
    lane = lax.broadcasted_iota(I32, logits.shape, 1).astype(F32)
    vals, idxs = [], []
    cur = logits
    for _ in range(TOP_K):
        mx = jnp.max(cur, axis=-1, keepdims=True)
        idx = jnp.min(jnp.where(cur == mx, lane, float(LANES)), axis=-1, keepdims=True)
        vals.append(mx)
        idxs.append(idx)
        cur = jnp.where(lane == idx, -jnp.inf, cur)
    es = [jnp.exp(v - vals[0]) for v in vals]
    tot = es[0] + es[1] + es[2] + es[3]
    for k in range(TOP_K):
        te_ref[0, :, k:k + 1] = idxs[k].astype(I32)
        tg_ref[0, :, k:k + 1] = es[k] / tot


def _outproj(a, w_bf, x, g, sc, sh, lng, lnb, wr_pad, br_pad):
    b, t, d = x.shape
    tm = _row_tile(t, OUTPROJ_ROW_TILE)
    wr_hi = wr_pad.astype(BF16)
    wr_lo = (wr_pad - wr_hi.astype(F32)).astype(BF16)
    row = pl.BlockSpec((1, tm, d), lambda i, j: (i, j, 0))
    vec = pl.BlockSpec((1, 1, d), lambda i, j: (i, 0, 0))
    par = pl.BlockSpec((1, d), lambda i, j: (0, 0))
    top = pl.BlockSpec((1, tm, TOP_K), lambda i, j: (i, j, 0))
    return pl.pallas_call(
        _outproj_kernel,
        grid=(b, t // tm),
        in_specs=[row, pl.BlockSpec((d, d), lambda i, j: (0, 0)), row, vec, vec, vec, par, par,
                  pl.BlockSpec((d, LANES), lambda i, j: (0, 0)), pl.BlockSpec((d, LANES), lambda i, j: (0, 0)),
                  pl.BlockSpec((1, LANES), lambda i, j: (0, 0))],
        out_specs=[row, row, top, top],
        out_shape=[jax.ShapeDtypeStruct((b, t, d), F32), jax.ShapeDtypeStruct((b, t, d), BF16),
                   jax.ShapeDtypeStruct((b, t, TOP_K), I32), jax.ShapeDtypeStruct((b, t, TOP_K), F32)],
        compiler_params=_params("arbitrary", "arbitrary"),
        name="outproj_ln_router",
    )(a, w_bf, x, g, sc, sh, lng, lnb, wr_hi, wr_lo, br_pad)


def _moe_kernel(cnt_ref, x_ref, te_ref, tg_ref, w1_ref, b1_ref, w2_ref, b2_ref, o_ref, pos_ref, tri_ref, *, tm, rb):
    e = pl.program_id(1)

    @pl.when(jnp.logical_and(pl.program_id(0) == 0, e == 0))
    def _():
        si = lax.broadcasted_iota(I32, (tm, tm), 0)
        ti = lax.broadcasted_iota(I32, (tm, tm), 1)
        tri_ref[...] = jnp.where(si <= ti, 1.0, 0.0).astype(BF16)

    @pl.when(e == 0)
    def _():
        o_ref[...] = jnp.zeros_like(o_ref)
        eio = lax.broadcasted_iota(I32, (N_EXPERTS, tm), 0)
        sel = jnp.zeros((N_EXPERTS, tm), F32)
        for k in range(TOP_K):
            sel = sel + jnp.where(te_ref[k:k + 1, :] == eio, 1.0, 0.0)
        rank = jnp.dot(sel.astype(BF16), tri_ref[...], preferred_element_type=F32)
        pos_ref[...] = rank * sel

    pm = pos_ref[pl.ds(e, 1), :]
    gate = jnp.zeros((1, tm), F32)
    for k in range(TOP_K):
        gate = gate + jnp.where(te_ref[k:k + 1, :] == e, tg_ref[k:k + 1, :], 0.0)
    cnt = cnt_ref[pl.program_id(0) * N_EXPERTS + e]
    nblk = (cnt + (rb - 1)) // rb

    def body(blk, carry):
        r = (blk * rb + 1 + lax.broadcasted_iota(I32, (rb, 1), 0)).astype(F32)
        hit = pm == r
        onehot = jnp.where(hit, 1.0, 0.0).astype(BF16)
        xg = jnp.dot(onehot, x_ref[...], preferred_element_type=F32).astype(BF16)
        h = jnp.dot(xg, w1_ref[0], preferred_element_type=F32) + b1_ref[0]
        hg = jnp.minimum(h[:, :D_FF], SWIGLU_LIMIT)
        hl = jnp.clip(h[:, D_FF:], -SWIGLU_LIMIT, SWIGLU_LIMIT)
        a = hg * jax.nn.sigmoid(SWIGLU_ALPHA * hg) * (hl + 1.0)
        y = jnp.dot(a.astype(BF16), w2_ref[0], preferred_element_type=F32) + b2_ref[0]
        grow = jnp.sum(jnp.where(hit, gate, 0.0), axis=-1, keepdims=True)
        yg = (y * grow).astype(BF16)
        o_ref[...] += lax.dot_general(onehot, yg, TN_DIMS, preferred_element_type=F32)
        return carry

    lax.fori_loop(0, nblk, body, 0)


def _moe(u2, te, tg, w1_bf, b1, w2_bf, b2, layer):
    n, d = u2.shape
    tm = _row_tile(n, MOE_TOKEN_TILE)
    rb = min(tm, tm * TOP_K // N_EXPERTS + 32)
    te_t = te.T
    tg_t = tg.T
    hits = te.reshape(n // tm, tm * TOP_K)[:, :, None] == jnp.arange(N_EXPERTS, dtype=I32)
    counts = jnp.sum(hits.astype(I32), axis=1).reshape(-1)
    grid_spec = pltpu.PrefetchScalarGridSpec(
        num_scalar_prefetch=1,
        grid=(n // tm, N_EXPERTS),
        in_specs=[
            pl.BlockSpec((tm, d), lambda i, e, c: (i, 0)),
            pl.BlockSpec((TOP_K, tm), lambda i, e, c: (0, i)),
            pl.BlockSpec((TOP_K, tm), lambda i, e, c: (0, i)),
            pl.BlockSpec((None, 1, d, 2 * D_FF), lambda i, e, c: (layer, e, 0, 0)),
            pl.BlockSpec((None, 1, 1, 2 * D_FF), lambda i, e, c: (layer, e, 0, 0)),
            pl.BlockSpec((None, 1, D_FF, d), lambda i, e, c: (layer, e, 0, 0)),
            pl.BlockSpec((None, 1, 1, d), lambda i, e, c: (layer, e, 0, 0)),
        ],
        out_specs=pl.BlockSpec((tm, d), lambda i, e, c: (i, 0)),
        scratch_shapes=[pltpu.VMEM((N_EXPERTS, tm), F32), pltpu.VMEM((tm, tm), BF16)],
    )
    return pl.pallas_call(
        functools.partial(_moe_kernel, tm=tm, rb=rb),
        grid_spec=grid_spec,
        out_shape=jax.ShapeDtypeStruct((n, d), F32),
        compiler_params=_params("arbitrary", "arbitrary"),
        name="moe_experts",
    )(counts, u2, te_t, tg_t, w1_bf, b1.reshape(DEPTH, N_EXPERTS, 1, -1), w2_bf, b2.reshape(DEPTH, N_EXPERTS, 1, -1))


def _rope_tables(pos, head_dim):
    rd = head_dim // ROPE_FRACTION
    half = rd // 2
    inv = jnp.power(ROPE_THETA, -jnp.arange(half, dtype=F32) / half)
    ang = jnp.asarray(pos).astype(F32)[:, None] * inv[None, :]
    cos, sin = jnp.cos(ang), jnp.sin(ang)
    t = pos.shape[0]
    ones = jnp.ones((t, head_dim - rd), F32)
    zeros = jnp.zeros((t, head_dim - rd), F32)
    zh = jnp.zeros((t, half), F32)
    c = jnp.concatenate([cos, cos, ones], axis=-1)
    sa = jnp.concatenate([zh, sin, zeros], axis=-1)
    sb = jnp.concatenate([-sin, zh, zeros], axis=-1)
    rep = LANES // head_dim
    return tuple(jnp.tile(a, (1, rep)) for a in (c, sa, sb))


def _rope(z, c, sa, sb, half):
    w = z.shape[-1]
    rep = w // LANES
    if rep > 1:
        c, sa, sb = (jnp.concatenate([a] * rep, axis=-1) for a in (c, sa, sb))
    return z * c + pltpu.roll(z, half, 1) * sa + pltpu.roll(z, w - half, 1) * sb


def _dsa_inproj_kernel(x_ref, f_ref, g_ref, lng_ref, lnb_ref, sc_ref, sh_ref, w_in_ref,
                       c1_ref, sa1_ref, sb1_ref, c2_ref, sa2_ref, sb2_ref,
                       xo_ref, q_ref, kf_ref, vf_ref, kif_ref, kb_ref, vb_ref, kib_ref, qi_ref, w_ref):
    z = DEEPNORM_ALPHA * x_ref[0] + (1.0 + g_ref[0]) * f_ref[0]
    x = _layer_norm(z, lng_ref[...], lnb_ref[...])
    xo_ref[0] = x
    u = x * (1.0 + sc_ref[0]) + sh_ref[0]
    p = jnp.dot(u.astype(BF16), w_in_ref[...], preferred_element_type=F32)

    c1, sa1, sb1 = c1_ref[...], sa1_ref[...], sb1_ref[...]
    c2, sa2, sb2 = c2_ref[...], sa2_ref[...], sb2_ref[...]
    h1 = DSA_HD // ROPE_FRACTION // 2
    h2 = IDX_DIM // ROPE_FRACTION // 2
    o_k = DSA_Q
    o_v = DSA_Q + DSA_KV
    o_qi = DSA_Q + 2 * DSA_KV
    o_ki = o_qi + IDX_HEADS * IDX_DIM
    q = _rope(p[:, :DSA_Q], c1, sa1, sb1, h1)
    q_ref[0] = (q * DSA_LOG2_SCALE).astype(BF16)
    k = _rope(p[:, o_k:o_v], c1, sa1, sb1, h1)
    kf_ref[0] = k
    kb_ref[0] = k.astype(BF16)
    v = p[:, o_v:o_qi]
    vf_ref[0] = v
    vb_ref[0] = v.astype(BF16)
    qi = _rope(p[:, o_qi:o_ki], c2, sa2, sb2, h2)
    qi_ref[0] = qi.astype(BF16)
    tail = p[:, o_ki:o_ki + LANES]
    ki = _rope(tail, c2, sa2, sb2, h2)[:, :IDX_DIM]
    kif_ref[0] = ki
    kib_ref[0] = ki.astype(BF16)
    wi = tail[:, IDX_DIM:IDX_DIM + IDX_HEADS]
    w_ref[0] = (wi * (IDX_HEADS ** -0.5)) * (IDX_DIM ** -0.5)


def _dsa_inproj(x, f, g, lng, lnb, sc, sh, w_bf, pos):
    b, t, d = x.shape
    n = w_bf.shape[1]
    tm = _row_tile(t, PROJ_ROW_TILE)
    tabs = _rope_tables(pos, DSA_HD) + _rope_tables(pos, IDX_DIM)
    tab = pl.BlockSpec((tm, LANES), lambda i, j: (j, 0))
    vec = pl.BlockSpec((1, 1, d), lambda i, j: (i, 0, 0))
    par = pl.BlockSpec((1, d), lambda i, j: (0, 0))

    def row(w):
        return pl.BlockSpec((1, tm, w), lambda i, j: (i, j, 0))

    widths = [(d, F32), (DSA_Q, BF16), (DSA_KV, F32), (DSA_KV, F32), (IDX_DIM, F32), (DSA_KV, BF16), (DSA_KV, BF16),
              (IDX_DIM, BF16), (IDX_HEADS * IDX_DIM, BF16), (IDX_HEADS, F32)]
    return pl.pallas_call(
        _dsa_inproj_kernel,
        grid=(b, t // tm),
        in_specs=[row(d), row(d), vec, par, par, vec, vec, pl.BlockSpec((d, n), lambda i, j: (0, 0))] + [tab] * 6,
        out_specs=[row(w) for w, _ in widths],
        out_shape=[jax.ShapeDtypeStruct((b, t, w), dt) for w, dt in widths],
        compiler_params=_params("arbitrary", "arbitrary"),
        name="ln_inproj_rope_split",
    )(x, f, g, lng, lnb, sc, sh, w_bf, *tabs)


def _f2key(f):
    bits = lax.bitcast_convert_type(f, I32)
    return bits ^ ((bits >> 31) & 0x7FFFFFFF)


def _key2f(k):
    return lax.bitcast_convert_type(k ^ ((k >> 31) & 0x7FFFFFFF), F32)


def _dsa_kernel(nkt_ref, q_ref, qi_ref, w_ref, k_ref, v_ref, ki_ref, kpos_ref, qlim_ref, nadm_ref, o_ref,
                s_ref, wb_ref, qg_ref, m_ref, l_ref, acc_ref, flag_ref, *, tq, tk, nsel, value_iters):
    nkt = nkt_ref[pl.program_id(1)]
    qlim = qlim_ref[...]
    w = w_ref[0]
    qi = qi_ref[0]
    qis_all = jnp.concatenate([qi[:, h * IDX_DIM:(h + 1) * IDX_DIM] for h in range(IDX_HEADS)], axis=0)
    nlg = tk // LANES
    for h in range(IDX_HEADS):
        wb_ref[h] = jnp.broadcast_to(w[:, h:h + 1], (tq, LANES))

    def p1(kt, carry):
        rmax, rmin, cgt0, cge0 = carry
        off = pl.multiple_of(kt * tk, tk)
        kit = ki_ref[0, pl.ds(off, tk), :]
        acc = jnp.zeros((tq, tk), F32)
        d_all = lax.dot_general(qis_all, kit, NT_DIMS, preferred_element_type=F32)
        for h in range(IDX_HEADS):
            d = d_all[h * tq:(h + 1) * tq]
            acc = acc + jnp.maximum(d, 0.0) * jnp.concatenate([wb_ref[h]] * nlg, axis=1)
        s = jnp.where(kpos_ref[kt] < qlim, acc, -jnp.inf)
        s_ref[kt] = s
        gt0 = jnp.where(s > 0.0, 1.0, 0.0)
        ge0 = jnp.where(s >= 0.0, 1.0, 0.0)
        for g in range(nlg):
            lanes = slice(g * LANES, (g + 1) * LANES)
            rmax = jnp.maximum(rmax, s[:, lanes])
            rmin = jnp.minimum(rmin, acc[:, lanes])
            cgt0 = cgt0 + gt0[:, lanes]
            cge0 = cge0 + ge0[:, lanes]
        return rmax, rmin, cgt0, cge0

    zcol = jnp.zeros((tq, 1), F32)
    zlan = jnp.zeros((tq, LANES), F32)
    init = (jnp.full((tq, LANES), -jnp.inf, F32), jnp.full((tq, LANES), jnp.inf, F32), zlan, zlan)
    rmax, rmin, cgt0, cge0 = lax.fori_loop(0, nkt, p1, init)
    rmax = jnp.max(rmax, axis=-1, keepdims=True)
    rmin = jnp.min(rmin, axis=-1, keepdims=True)
    cgt0 = jnp.sum(cgt0, axis=-1, keepdims=True)
    cge0 = jnp.sum(cge0, axis=-1, keepdims=True)
    nadm = nadm_ref[...]

    rg = min(tq, 32)
    fsel = float(nsel)
    unbounded = float(2 ** 30)

    def count_ge(tb):
        def body(kt, c):
            out = []
            for r in range(tq // rg):
                tr = jnp.concatenate([tb[r * rg:(r + 1) * rg, :]] * nlg, axis=1)
                hit = jnp.where(s_ref[kt, r * rg:(r + 1) * rg, :] >= tr, 1.0, 0.0)
                cr = c[r * rg:(r + 1) * rg]
                for g in range(nlg):
                    cr = cr + hit[:, g * LANES:(g + 1) * LANES]
                out.append(cr)
            return jnp.concatenate(out, axis=0)
        c = lax.fori_loop(0, nkt, body, zlan)
        return jnp.broadcast_to(jnp.sum(c, axis=-1, keepdims=True), (tq, LANES))

    rmax, rmin, cgt0, cge0, nadm = (jnp.broadcast_to(a, (tq, LANES)) for a in (rmax, rmin, cgt0, cge0, nadm))
    few = nadm <= fsel
    many = jnp.logical_not(few)
    pos = jnp.logical_and(many, cgt0 > fsel)
    neg = jnp.logical_and(many, cge0 < fsel)
    zero = jnp.logical_and(many, jnp.logical_and(cgt0 <= fsel, cge0 >= fsel))
    lo0 = jnp.where(pos, 0, _f2key(jnp.where(neg, rmin, 0.0)))
    hi0 = jnp.where(pos, _f2key(jnp.where(pos, rmax, 0.0)) + 1, 0)
    clo0 = jnp.where(pos, cge0, nadm)
    chi0 = jnp.where(pos, 0.0, cge0)
    act0 = jnp.where(jnp.logical_or(pos, neg), 1.0, 0.0)
    t0 = jnp.where(few, -jnp.inf, 0.0)
    need0 = jnp.where(zero, fsel - cgt0, unbounded)

    def cond(st):
        return st[0] > 0.0

    def step(st):
        _, it, lo, hi, clo, chi, act, thr, need = st
        conv = jnp.logical_and(act > 0.0, lo + 1 >= hi)
        thr = jnp.where(conv, _key2f(lo), thr)
        need = jnp.where(conv, fsel - chi, need)
        act = jnp.where(conv, 0.0, act)
        flo, fhi = _key2f(lo), _key2f(hi)
        mid_i = (lo >> 1) + (hi >> 1) + (lo & hi & 1)
        lclo = jnp.log(clo)
        interp = (lclo - float(np.log(nsel - 0.5))) / (lclo - jnp.log(jnp.maximum(chi, 0.5)))
        frac = jnp.where(it % 3 == 2, 0.5, interp)
        mid_v = _f2key(flo + frac * (fhi - flo))
        use_v = jnp.logical_and(it < value_iters, jnp.logical_and(mid_v > lo, mid_v < hi))
        mid = jnp.where(use_v, mid_v, mid_i)
        tm = _key2f(mid)
        c = count_ge(jnp.where(act > 0.0, tm, jnp.inf))
        hit = jnp.logical_and(act > 0.0, c == fsel)
        thr = jnp.where(hit, tm, thr)
        up = jnp.logical_and(act > 0.0, c > fsel)
        dn = jnp.logical_and(act > 0.0, c < fsel)
        lo = jnp.where(up, mid, lo)
        clo = jnp.where(up, c, clo)
        hi = jnp.where(dn, mid, hi)
        chi = jnp.where(dn, c, chi)
        act = jnp.where(hit, 0.0, act)
        return jnp.max(act), it + 1, lo, hi, clo, chi, act, thr, need

    st = (jnp.max(act0), jnp.int32(0), lo0, hi0, clo0, chi0, act0, t0, need0)
    _, _, _, _, _, _, _, thr, need = lax.while_loop(cond, step, st)
    thr, need = thr[:, 0:1], need[:, 0:1]
    tied = need < unbounded * 0.5
    has_ties = jnp.max(jnp.where(tied, 1.0, 0.0)) > 0.0
    thr = jnp.maximum(thr, -F32_MAX)

    for g in range(DSA_KV_HEADS):
        for hh in range(DSA_GROUP):
            hd = g * DSA_GROUP + hh
            qg_ref[g, hh * tq:(hh + 1) * tq, :] = q_ref[0, :, hd * DSA_HD:(hd + 1) * DSA_HD]
    m_ref[...] = jnp.full(m_ref.shape, NEG_BIG, F32)
    l_ref[...] = jnp.zeros(l_ref.shape, F32)
    acc_ref[...] = jnp.zeros(acc_ref.shape, F32)

    ones_cols = jnp.ones((tk, DSA_HD), BF16)

    def attend(kt, bias):
        off = pl.multiple_of(kt * tk, tk)
        qk = [lax.dot_general(qg_ref[g], k_ref[0, pl.ds(off, tk), g * DSA_HD:(g + 1) * DSA_HD], NT_DIMS,
                              preferred_element_type=F32) for g in range(DSA_KV_HEADS)]
        for g in range(DSA_KV_HEADS):
            vt_g = jnp.concatenate([v_ref[0, pl.ds(off, tk), g * DSA_HD:(g + 1) * DSA_HD], ones_cols], axis=1)
            for hh in range(DSA_GROUP):
                rows = slice(hh * tq, (hh + 1) * tq)
                lg = qk[g][rows] + bias
                m_old = m_ref[g, rows, :]
                m_new = jnp.maximum(m_old, jnp.max(lg, axis=-1, keepdims=True))
                alpha = jnp.exp2(m_old - m_new)
                p = jnp.exp2(lg - jnp.concatenate([m_new] * nlg, axis=1))
                pv = jnp.dot(p.astype(BF16), vt_g, preferred_element_type=F32)
                acc_ref[g, rows, :] = alpha * acc_ref[g, rows, :] + pv[:, :DSA_HD]
                l_ref[g, rows, :] = alpha * l_ref[g, rows, :] + pv[:, DSA_HD:]
                m_ref[g, rows, :] = m_new

    @pl.when(jnp.logical_not(has_ties))
    def _():
        def body(kt, carry):
            attend(kt, jnp.where(s_ref[kt] >= thr, 0.0, NEG_BIG))
            return carry
        lax.fori_loop(0, nkt, body, 0)

    @pl.when(has_ties)
    def _():
        def scan(kt, carry):
            seen, cut, nin = carry
            eqc = jnp.sum(jnp.where(s_ref[kt] == thr, 1.0, 0.0), axis=-1, keepdims=True)
            here = jnp.logical_and(tied, jnp.logical_and(seen < need, seen + eqc >= need))
            cut = jnp.where(here, kt, cut)
            nin = jnp.where(here, need - seen, nin)
            flag_ref[kt] = jnp.max(jnp.where(here, 1.0, 0.0)).astype(I32)
            return seen + eqc, cut, nin

        cut0 = jnp.where(tied, -1, nkt).astype(I32)
        _, cut, nin = lax.fori_loop(0, nkt, scan, (zcol, cut0, zcol))

        def body(kt, carry):
            s = s_ref[kt]
            before = jnp.where(kt < cut, 0.0, NEG_BIG)

            def plain():
                return jnp.where(s == thr, before, NEG_BIG)

            def ranked():
                si = lax.broadcasted_iota(I32, (tk, tk), 0)
                ti = lax.broadcasted_iota(I32, (tk, tk), 1)
                triu = jnp.where(si <= ti, 1.0, 0.0).astype(BF16)
                eq = jnp.where(s == thr, 1.0, 0.0).astype(BF16)
                rank = jnp.dot(eq, triu, preferred_element_type=F32)
                inside = jnp.where(rank <= nin, 0.0, NEG_BIG)
                return jnp.where(s == thr, jnp.where(kt == cut, inside, before), NEG_BIG)

            tie_bias = lax.cond(flag_ref[kt] > 0, ranked, plain)
            attend(kt, jnp.where(s > thr, 0.0, tie_bias))
            return carry
        lax.fori_loop(0, nkt, body, 0)

    for g in range(DSA_KV_HEADS):
        og = acc_ref[g] / l_ref[g]
        for hh in range(DSA_GROUP):
            hd = g * DSA_GROUP + hh
            o_ref[0, :, hd * DSA_HD:(hd + 1) * DSA_HD] = og[hh * tq:(hh + 1) * tq, :].astype(BF16)


def _dsa_attend(q, qi, w, k, v, ki, kpos, qlim, tq, tk):
    b, t, _ = q.shape
    lp = k.shape[1]
    nq = t // tq
    nsel = min(DSA_TOPK, int(np.sum(kpos < np.iinfo(np.int32).max)) // 4)
    qmax = qlim.reshape(nq, tq).max(axis=1)
    nadm = np.searchsorted(kpos, qlim, side="left")
    nkt = np.minimum(-(-np.searchsorted(kpos, qmax, side="left") // tk), lp // tk).astype(np.int32)
    nkt = np.maximum(nkt, 1)
    single = pl.Buffered(1)

    def full(wd):
        return pl.BlockSpec((1, lp, wd), lambda i, j, n: (i, 0, 0), pipeline_mode=single)

    def row(wd):
        return pl.BlockSpec((1, tq, wd), lambda i, j, n: (i, j, 0))

    grid_spec = pltpu.PrefetchScalarGridSpec(
        num_scalar_prefetch=1,
        grid=(b, nq),
        in_specs=[row(DSA_Q), row(IDX_HEADS * IDX_DIM), row(IDX_HEADS), full(DSA_KV), full(DSA_KV), full(IDX_DIM),
                  pl.BlockSpec((lp // tk, 1, tk), lambda i, j, n: (0, 0, 0)),
                  pl.BlockSpec((tq, 1), lambda i, j, n: (j, 0)),
                  pl.BlockSpec((tq, 1), lambda i, j, n: (j, 0))],
        out_specs=row(DSA_Q),
        scratch_shapes=[
            pltpu.VMEM((lp // tk, tq, tk), F32),
            pltpu.VMEM((IDX_HEADS, tq, LANES), F32),
            pltpu.VMEM((DSA_KV_HEADS, DSA_GROUP * tq, DSA_HD), BF16),
            pltpu.VMEM((DSA_KV_HEADS, DSA_GROUP * tq, LANES), F32),
            pltpu.VMEM((DSA_KV_HEADS, DSA_GROUP * tq, LANES), F32),
            pltpu.VMEM((DSA_KV_HEADS, DSA_GROUP * tq, DSA_HD), F32),
            pltpu.SMEM((lp // tk,), I32),
        ],
    )
    return pl.pallas_call(
        functools.partial(_dsa_kernel, tq=tq, tk=tk, nsel=nsel, value_iters=DSA_VALUE_ITERS),
        grid_spec=grid_spec,
        out_shape=jax.ShapeDtypeStruct((b, t, DSA_Q), BF16),
        compiler_params=_params("arbitrary", "arbitrary"),
        name="dsa_select_attend",
    )(jnp.asarray(nkt), q, qi, w, k, v, ki, jnp.asarray(kpos, I32).reshape(lp // tk, 1, tk),
      jnp.asarray(qlim, I32).reshape(t, 1), jnp.asarray(nadm, F32).reshape(t, 1))


def _trunk(x, mods, pos, ml_state, ds_cache, prm):
    b, t, d = x.shape
    chunk = next((c for c in (ML_EVAL_CHUNK, CHUNK) if t % c == 0), t)

    def mod(i, k):
        return mods[i][:, None, k * d:(k + 1) * d]

    p, pg = _ml_inproj(x, mod(0, 1), mod(0, 0), prm["ml_w_in"])
    if ml_state is None:
        c0 = jnp.zeros((b, ML_HEADS, ML_DV, ML_DK), F32)
        n0 = jnp.zeros((b, ML_HEADS, ML_DK), F32)
        m0 = jnp.zeros((b, ML_HEADS), F32)
    else:
        c0, n0, m0 = (s[0] for s in ml_state)
    hg, c_new, n_new, m_new = _mlstm(p, pg, prm["ml_b_gates"], prm["ml_norm_g"], c0, n0, m0, chunk)
    x1, u2, te, tg = _outproj(hg, prm["ml_w_out"], x, mod(0, 2), mod(0, 4), mod(0, 3),
                              prm["ln_g"][0, 0:1], prm["ln_b"][0, 0:1], prm["w_router"][0], prm["b_router"][0])
    f = _moe(u2.reshape(b * t, d), te.reshape(b * t, TOP_K), tg.reshape(b * t, TOP_K),
             prm["moe_w1"], prm["moe_b1"], prm["moe_w2"], prm["moe_b2"], 0).reshape(b, t, d)

    x2, q, kf, vf, kif, kb, vb, kib, qi, w = _dsa_inproj(
        x1, f, mod(0, 5), prm["ln_g"][0, 1:2], prm["ln_b"][0, 1:2], mod(1, 1), mod(1, 0), prm["ds_w_in"], pos)
    pos_np = np.asarray(pos)
    qlim = (pos_np // CHUNK + 1) * CHUNK
    if ds_cache is None:
        kpos = pos_np
        tq, tk = DSA_Q_TILE, DSA_K_TILE
        if t % tk:
            tq, tk = t, t
        k_all, v_all, ki_all = kb, vb, kib
    else:
        ck, cv, cki = ds_cache
        past = ck.shape[1]
        total = past + t
        lp = -(-total // (3 * LANES)) * (3 * LANES)
        tq, tk = t, 3 * LANES
        kpos = np.concatenate([np.arange(past), pos_np, np.full((lp - total,), np.iinfo(np.int32).max)])

        def cat(cache, new):
            full = jnp.concatenate([cache.reshape(b, past, -1).astype(BF16), new], axis=1)
            return jnp.pad(full, ((0, 0), (0, lp - total), (0, 0)))

        k_all, v_all, ki_all = cat(ck, kb), cat(cv, vb), cat(cki, kib)
    o = _dsa_attend(q, qi, w, k_all, v_all, ki_all, kpos.astype(np.int64), qlim, tq, tk)
    x3, u2, te, tg = _outproj(o, prm["ds_w_out"], x2, mod(1, 2), mod(1, 4), mod(1, 3),
                              prm["ln_g"][1, 0:1], prm["ln_b"][1, 0:1], prm["w_router"][1], prm["b_router"][1])
    f = _moe(u2.reshape(b * t, d), te.reshape(b * t, TOP_K), tg.reshape(b * t, TOP_K),
             prm["moe_w1"], prm["moe_b1"], prm["moe_w2"], prm["moe_b2"], 1).reshape(b, t, d)
    y = _final_ln(x3, f, mod(1, 5), prm["ln_g"][1, 1:2], prm["ln_b"][1, 1:2])

    ml_out = (c_new[None], n_new[None], m_new[None])
    ds_out = (kf.reshape(1, b, t, DSA_KV_HEADS, DSA_HD), vf.reshape(1, b, t, DSA_KV_HEADS, DSA_HD), kif[None])
    return y, ml_out, ds_out


def _prepare(w_ada, b_ada, ln_g, ln_b, ml_w_in, ml_b_gates, ml_norm_g, ml_w_out, ds_w_in, ds_w_out,
             moe_w_router, moe_b_router, moe_w1, moe_b1, moe_w2, moe_b2):
    return {
        "ln_g": ln_g, "ln_b": ln_b,
        "ml_w_in": jnp.pad(ml_w_in[0], ((0, 0), (0, ML_IN_PAD - ML_IN))).astype(BF16),
        "ml_b_gates": ml_b_gates[0], "ml_norm_g": ml_norm_g[0],
        "ml_w_out": ml_w_out[0].astype(BF16),
        "ds_w_in": jnp.pad(ds_w_in[0], ((0, 0), (0, DSA_IN_PAD - DSA_IN))).astype(BF16),
        "ds_w_out": ds_w_out[0].astype(BF16),
        "w_router": jnp.pad(moe_w_router, ((0, 0), (0, 0), (0, LANES - N_EXPERTS))),
        "b_router": jnp.pad(moe_b_router, ((0, 0), (0, LANES - N_EXPERTS)), constant_values=NEG_BIG)[:, None, :],
        "moe_w1": moe_w1.astype(BF16), "moe_b1": moe_b1,
        "moe_w2": moe_w2.astype(BF16), "moe_b2": moe_b2,
    }


def kernel(x_prompt, x_sample, state_mlstm_C, state_mlstm_n, state_mlstm_m, cache_dsa_k, cache_dsa_v,
           cache_dsa_kidx, c_prompt, c_sample, w_ada, b_ada, ln_g, ln_b, ml_w_in, ml_b_gates, ml_norm_g,
           ml_w_out, ds_w_in, ds_w_out, moe_w_router, moe_b_router, moe_w1, moe_b1, moe_w2, moe_b2):
    prm = _prepare(w_ada, b_ada, ln_g, ln_b, ml_w_in, ml_b_gates, ml_norm_g, ml_w_out, ds_w_in, ds_w_out,
                   moe_w_router, moe_b_router, moe_w1, moe_b1, moe_w2, moe_b2)
    bp, bs = c_prompt.shape[0], c_sample.shape[0]
    rows = -(-(bp + bs) // 8) * 8
    c_rows = jnp.pad(jnp.concatenate([c_prompt, c_sample], axis=0), ((0, rows - bp - bs), (0, 0)))
    mods = _ada(c_rows, w_ada, b_ada)
    mods_p = [mods[i, :bp] for i in range(DEPTH)]
    mods_s = [mods[i, bp:bp + bs] for i in range(DEPTH)]

    pos_p = np.arange(x_prompt.shape[1], dtype=np.int32)
    y_p, ml_p, ds_p = _trunk(x_prompt, mods_p, pos_p, None, None, prm)
    past = cache_dsa_k.shape[2]
    pos_s = past + np.arange(x_sample.shape[1], dtype=np.int32)
    y_s, ml_s, ds_s = _trunk(x_sample, mods_s, pos_s, (state_mlstm_C, state_mlstm_n, state_mlstm_m),
                             (cache_dsa_k[0], cache_dsa_v[0], cache_dsa_kidx[0]), prm)
    return (y_p, y_s, ml_p[0], ml_p[1], ml_p[2], ds_p[0], ds_p[1], ds_p[2],
            ml_s[0], ml_s[1], ml_s[2], ds_s[0], ds_s[1], ds_s[2])
```

```python
import functools

import numpy as np
import jax
import jax.numpy as jnp
from jax import lax
from jax.experimental import pallas as pl
from jax.experimental.pallas import tpu as pltpu

F32 = jnp.float32
BF16 = jnp.bfloat16
I32 = jnp.int32

D_MODEL = 1024
DEPTH = 2
CHUNK = 64
ML_HEADS = 4
ML_DV = D_MODEL // ML_HEADS
ML_DK = ML_DV // 2
ML_QK = ML_HEADS * ML_DK
ML_V = ML_HEADS * ML_DV
ML_IN = 2 * ML_QK + 2 * ML_V + 2 * ML_HEADS
ML_GATE_OFF = 2 * ML_QK + 2 * ML_V
ML_EVAL_CHUNK = 256
DSA_HEADS = 8
DSA_KV_HEADS = 2
DSA_HD = D_MODEL // DSA_HEADS
DSA_GROUP = DSA_HEADS // DSA_KV_HEADS
IDX_HEADS = 8
IDX_DIM = 64
DSA_TOPK = 256
DSA_Q = DSA_HEADS * DSA_HD
DSA_KV = DSA_KV_HEADS * DSA_HD
DSA_IN = DSA_Q + 2 * DSA_KV + IDX_HEADS * IDX_DIM + IDX_DIM + IDX_HEADS
DSA_LOG2_SCALE = (DSA_HD ** -0.5) * float(np.log2(np.e))
ROPE_THETA = 500000.0
ROPE_FRACTION = 4
N_EXPERTS = 32
TOP_K = 4
D_FF = D_MODEL
SWIGLU_LIMIT = 7.0
SWIGLU_ALPHA = 1.702
DEEPNORM_ALPHA = (2 * DEPTH) ** 0.25
LN_EPS = 1e-5

LANES = 128
V7X_VMEM_BYTES = 64 * 1024 * 1024
VMEM_LIMIT = V7X_VMEM_BYTES - 8 * 1024 * 1024

PROJ_ROW_TILE = 512
OUTPROJ_ROW_TILE = 256
FINAL_LN_ROW_TILE = 1024
ADA_COL_TILE = 1536
MOE_TOKEN_TILE = 1024
DSA_Q_TILE = 128
DSA_K_TILE = 1024
DSA_VALUE_ITERS = 16

ML_IN_PAD = -(-ML_IN // LANES) * LANES
DSA_IN_PAD = -(-DSA_IN // LANES) * LANES
NEG_BIG = -1e30
F32_MAX = float(np.finfo(np.float32).max)
HIGHEST = lax.Precision.HIGHEST
NT_DIMS = (((1,), (1,)), ((), ()))
TN_DIMS = (((0,), (0,)), ((), ()))


def _params(*sem):
    return pltpu.CompilerParams(dimension_semantics=sem, vmem_limit_bytes=VMEM_LIMIT)


def _row_tile(t, pref):
    return pref if t % pref == 0 else t


def _ada_kernel(c_ref, w_ref, b_ref, o_ref):
    c = c_ref[...]
    cond = (c * jax.nn.sigmoid(c)).astype(BF16)
    o_ref[0] = jnp.dot(cond, w_ref[0].astype(BF16), preferred_element_type=F32) + b_ref[0]


def _ada(c_rows, w_ada, b_ada):
    rows = c_rows.shape[0]
    n = w_ada.shape[-1]
    tn = ADA_COL_TILE
    return pl.pallas_call(
        _ada_kernel,
        grid=(DEPTH, n // tn),
        in_specs=[
            pl.BlockSpec((rows, D_MODEL), lambda i, j: (0, 0)),
            pl.BlockSpec((1, D_MODEL, tn), lambda i, j: (i, 0, j)),
            pl.BlockSpec((1, 1, tn), lambda i, j: (i, 0, j)),
        ],
        out_specs=pl.BlockSpec((1, rows, tn), lambda i, j: (i, 0, j)),
        out_shape=jax.ShapeDtypeStruct((DEPTH, rows, n), F32),
        compiler_params=_params("arbitrary", "arbitrary"),
        name="ada_mod",
    )(c_rows, w_ada, b_ada.reshape(DEPTH, 1, n))


def _layer_norm(z, g, b):
    mu = jnp.mean(z, axis=-1, keepdims=True)
    zc = z - mu
    var = jnp.mean(zc * zc, axis=-1, keepdims=True)
    return zc * lax.rsqrt(var + LN_EPS) * g + b


def _ml_inproj_kernel(x_ref, sc_ref, sh_ref, w_ref, pm_ref, pg_ref):
    u = x_ref[0] * (1.0 + sc_ref[0]) + sh_ref[0]
    p = jnp.dot(u.astype(BF16), w_ref[...], preferred_element_type=F32)
    pm_ref[0, :, :ML_QK] = (p[:, :ML_QK] * (ML_DK ** -0.5)).astype(BF16)
    pm_ref[0, :, ML_QK:] = p[:, ML_QK:ML_GATE_OFF].astype(BF16)
    pg_ref[0] = p[:, ML_GATE_OFF:]


def _ml_inproj(x, sc, sh, w_bf):
    b, t, d = x.shape
    n = w_bf.shape[1]
    tm = _row_tile(t, PROJ_ROW_TILE)
    row = pl.BlockSpec((1, tm, d), lambda i, j: (i, j, 0))
    vec = pl.BlockSpec((1, 1, d), lambda i, j: (i, 0, 0))
    return pl.pallas_call(
        _ml_inproj_kernel,
        grid=(b, t // tm),
        in_specs=[row, vec, vec, pl.BlockSpec((d, n), lambda i, j: (0, 0))],
        out_specs=[pl.BlockSpec((1, tm, ML_GATE_OFF), lambda i, j: (i, j, 0)),
                   pl.BlockSpec((1, tm, n - ML_GATE_OFF), lambda i, j: (i, j, 0))],
        out_shape=[jax.ShapeDtypeStruct((b, t, ML_GATE_OFF), BF16),
                   jax.ShapeDtypeStruct((b, t, n - ML_GATE_OFF), F32)],
        compiler_params=_params("arbitrary", "arbitrary"),
        name="ml_inproj",
    )(x, sc, sh, w_bf)


def _final_ln_kernel(x_ref, f_ref, g_ref, lng_ref, lnb_ref, o_ref):
    z = DEEPNORM_ALPHA * x_ref[0] + (1.0 + g_ref[0]) * f_ref[0]
    o_ref[0] = _layer_norm(z, lng_ref[...], lnb_ref[...])


def _final_ln(x, f, g, lng, lnb):
    b, t, d = x.shape
    tm = _row_tile(t, FINAL_LN_ROW_TILE)
    row = pl.BlockSpec((1, tm, d), lambda i, j: (i, j, 0))
    vec = pl.BlockSpec((1, 1, d), lambda i, j: (i, 0, 0))
    par = pl.BlockSpec((1, d), lambda i, j: (0, 0))
    return pl.pallas_call(
        _final_ln_kernel,
        grid=(b, t // tm),
        in_specs=[row, row, vec, par, par],
        out_specs=row,
        out_shape=jax.ShapeDtypeStruct((b, t, d), F32),
        compiler_params=_params("arbitrary", "arbitrary"),
        name="final_ln",
    )(x, f, g, lng, lnb)


def _log_sigmoid(x):
    return jnp.minimum(x, 0.0) - jnp.log1p(jnp.exp(-jnp.abs(x)))


def _mlstm_kernel(p_ref, pg_ref, gt_ref, bcol_ref, brow_ref, ng_ref, c0_ref, n0_ref, m0_ref,
                  hg_ref, c_ref, n_ref, m_ref, *, chunk):
    L = chunk

    @pl.when(pl.program_id(1) == 0)
    def _():
        c_ref[...] = c0_ref[...]
        n_ref[...] = n0_ref[...]
        m_ref[...] = m0_ref[...]

    gcol = pg_ref[0, :, :2 * ML_HEADS] + bcol_ref[...]
    grow = gt_ref[0] + brow_ref[...]
    lf_col = _log_sigmoid(gcol)
    lf_row = _log_sigmoid(grow)
    ti = lax.broadcasted_iota(I32, (L, L), 0)
    si = lax.broadcasted_iota(I32, (L, L), 1)
    causal = si <= ti
    tril = jnp.where(causal, 1.0, 0.0).astype(F32)
    triu = jnp.where(ti <= si, 1.0, 0.0).astype(F32)
    b_col = jnp.dot(tril, lf_col, precision=HIGHEST, preferred_element_type=F32)
    b_row = jnp.dot(lf_row, triu, precision=HIGHEST, preferred_element_type=F32)

    for h in range(ML_HEADS):
        qs = p_ref[0, :, h * ML_DK:(h + 1) * ML_DK]
        kb = p_ref[0, :, ML_QK + h * ML_DK:ML_QK + (h + 1) * ML_DK]
        vb = p_ref[0, :, 2 * ML_QK + h * ML_DV:2 * ML_QK + (h + 1) * ML_DV]
        v = vb.astype(F32)
        o = p_ref[0, :, 2 * ML_QK + ML_V + h * ML_DV:2 * ML_QK + ML_V + (h + 1) * ML_DV].astype(F32)
        bc = b_col[:, ML_HEADS + h:ML_HEADS + h + 1]
        ic = gcol[:, h:h + 1]
        br = b_row[ML_HEADS + h:ML_HEADS + h + 1, :]
        ir = grow[h:h + 1, :]
        c_old = c_ref[0, h]
        n_old = n_ref[0, h:h + 1, :]
        m_old = m_ref[0, h:h + 1, 0:1]

        dm = jnp.where(causal, bc - br + ir, -jnp.inf)
        inter = bc + m_old
        mt = jnp.maximum(inter, jnp.max(dm, axis=-1, keepdims=True))
        qk = lax.dot_general(qs, kb, NT_DIMS, preferred_element_type=F32)
        s = jnp.exp(dm - mt) * qk
        wp = jnp.exp(inter - mt)
        qc = lax.dot_general(qs, c_old.astype(BF16), NT_DIMS, preferred_element_type=F32)
        num = jnp.dot(s.astype(BF16), vb, preferred_element_type=F32) + wp * qc
        qn = jnp.sum(qs.astype(F32) * n_old, axis=-1, keepdims=True)
        den = jnp.sum(s, axis=-1, keepdims=True) + wp * qn
        hh = num / jnp.maximum(jnp.abs(den), jnp.exp(-mt))

        m_new = mt[L - 1:L, :]
        wk = jnp.exp(bc[L - 1:L, :] - bc + ic - m_new)
        wprev = jnp.exp(inter[L - 1:L, :] - m_new)
        vw = (v * wk).astype(BF16)
        c_ref[0, h] = wprev * c_old + lax.dot_general(vw, kb, TN_DIMS, preferred_element_type=F32)
        n_ref[0, h:h + 1, :] = wprev * n_old + jnp.sum(wk * kb.astype(F32), axis=0, keepdims=True)
        m_ref[0, h:h + 1, :] = jnp.broadcast_to(m_new, (1, ML_DK))

        mu = jnp.mean(hh, axis=-1, keepdims=True)
        hc = hh - mu
        var = jnp.mean(hc * hc, axis=-1, keepdims=True)
        hn = hc * lax.rsqrt(var + LN_EPS) * ng_ref[:, h * ML_DV:(h + 1) * ML_DV]
        hg_ref[0, :, h * ML_DV:(h + 1) * ML_DV] = (jax.nn.sigmoid(o) * hn).astype(BF16)


def _mlstm(p, pg, b_gates, norm_g, c0, n0, m0, chunk):
    b, t, n = p.shape
    nc = t // chunk
    gt = jnp.swapaxes(pg[:, :, :2 * ML_HEADS], 1, 2)
    if nc > 1:
        gt_spec = pl.BlockSpec((1, 2 * ML_HEADS, chunk), lambda i, j: (i, 0, j))
    else:
        gt_spec = pl.BlockSpec((1, 2 * ML_HEADS, t), lambda i, j: (i, 0, 0))
    if nc > 1 and chunk % LANES != 0:
        gt = gt.reshape(b, 2 * ML_HEADS, nc, chunk).transpose(0, 2, 1, 3).reshape(b * nc, 2 * ML_HEADS, chunk)
        gt_spec = pl.BlockSpec((1, 2 * ML_HEADS, chunk), lambda i, j: (i * nc + j, 0, 0))
    m0b = jnp.broadcast_to(m0[..., None], (b, ML_HEADS, ML_DK))
    cspec = pl.BlockSpec((1, ML_HEADS, ML_DV, ML_DK), lambda i, j: (i, 0, 0, 0))
    nspec = pl.BlockSpec((1, ML_HEADS, ML_DK), lambda i, j: (i, 0, 0))
    hg, c, nn, m = pl.pallas_call(
        functools.partial(_mlstm_kernel, chunk=chunk),
        grid=(b, nc),
        in_specs=[
            pl.BlockSpec((1, chunk, n), lambda i, j: (i, j, 0)),
            pl.BlockSpec((1, chunk, pg.shape[-1]), lambda i, j: (i, j, 0)),
            gt_spec,
            pl.BlockSpec((1, 2 * ML_HEADS), lambda i, j: (0, 0)),
            pl.BlockSpec((2 * ML_HEADS, 1), lambda i, j: (0, 0)),
            pl.BlockSpec((1, ML_V), lambda i, j: (0, 0)),
            cspec, nspec, nspec,
        ],
        out_specs=[pl.BlockSpec((1, chunk, ML_V), lambda i, j: (i, j, 0)), cspec, nspec, nspec],
        out_shape=[
            jax.ShapeDtypeStruct((b, t, ML_V), BF16),
            jax.ShapeDtypeStruct((b, ML_HEADS, ML_DV, ML_DK), F32),
            jax.ShapeDtypeStruct((b, ML_HEADS, ML_DK), F32),
            jax.ShapeDtypeStruct((b, ML_HEADS, ML_DK), F32),
        ],
        compiler_params=_params("arbitrary", "arbitrary"),
        name="mlstm_scan",
    )(p, pg, gt, b_gates.reshape(1, -1), b_gates.reshape(-1, 1), norm_g.reshape(1, -1), c0, n0, m0b)
    return hg, c, nn, m[..., 0]


def _outproj_kernel(a_ref, w_ref, x_ref, g_ref, sc_ref, sh_ref, lng_ref, lnb_ref, wrh_ref, wrl_ref, br_ref,
                    x1_ref, u2_ref, te_ref, tg_ref):
    y = jnp.dot(a_ref[0], w_ref[...], preferred_element_type=F32)
    z = DEEPNORM_ALPHA * x_ref[0] + (1.0 + g_ref[0]) * y
    x1 = _layer_norm(z, lng_ref[...], lnb_ref[...])
    x1_ref[0] = x1
    u2 = x1 * (1.0 + sc_ref[0]) + sh_ref[0]
    u_hi = u2.astype(BF16)
    u2_ref[0] = u_hi
    u_lo = (u2 - u_hi.astype(F32)).astype(BF16)
    logits = (jnp.dot(u_hi, wrh_ref[...], preferred_element_type=F32)
              + jnp.dot(u_lo, wrh_ref[...], preferred_element_type=F32)
              + jnp.dot(u_hi, wrl_ref[...], preferred_element_type=F32)) + br_ref[...]
    lane = lax.broadcasted_iota(I32, logits.shape, 1).astype(F32)
    vals, idxs = [], []
    cur = logits
    for _ in range(TOP_K):
        mx = jnp.max(cur, axis=-1, keepdims=True)
        idx = jnp.min(jnp.where(cur == mx, lane, float(LANES)), axis=-1, keepdims=True)
        vals.append(mx)
        idxs.append(idx)
        cur = jnp.where(lane == idx, -jnp.inf, cur)
    es = [jnp.exp(v - vals[0]) for v in vals]
    tot = es[0] + es[1] + es[2] + es[3]
    for k in range(TOP_K):
        te_ref[0, :, k:k + 1] = idxs[k].astype(I32)
        tg_ref[0, :, k:k + 1] = es[k] / tot


def _outproj(a, w_bf, x, g, sc, sh, lng, lnb, wr_pad, br_pad):
    b, t, d = x.shape
    tm = _row_tile(t, OUTPROJ_ROW_TILE)
    wr_hi = wr_pad.astype(BF16)
    wr_lo = (wr_pad - wr_hi.astype(F32)).astype(BF16)
    row = pl.BlockSpec((1, tm, d), lambda i, j: (i, j, 0))
    vec = pl.BlockSpec((1, 1, d), lambda i, j: (i, 0, 0))
    par = pl.BlockSpec((1, d), lambda i, j: (0, 0))
    top = pl.BlockSpec((1, tm, TOP_K), lambda i, j: (i, j, 0))
    return pl.pallas_call(
        _outproj_kernel,
        grid=(b, t // tm),
        in_specs=[row, pl.BlockSpec((d, d), lambda i, j: (0, 0)), row, vec, vec, vec, par, par,
                  pl.BlockSpec((d, LANES), lambda i, j: (0, 0)), pl.BlockSpec((d, LANES), lambda i, j: (0, 0)),
                  pl.BlockSpec((1, LANES), lambda i, j: (0, 0))],
        out_specs=[row, row, top, top],
        out_shape=[jax.ShapeDtypeStruct((b, t, d), F32), jax.ShapeDtypeStruct((b, t, d), BF16),
                   jax.ShapeDtypeStruct((b, t, TOP_K), I32), jax.ShapeDtypeStruct((b, t, TOP_K), F32)],
        compiler_params=_params("arbitrary", "arbitrary"),
        name="outproj_ln_router",
    )(a, w_bf, x, g, sc, sh, lng, lnb, wr_hi, wr_lo, br_pad)


MOE_SCATTER_GROUP = 3


def _moe_kernel(cnt_ref, x_ref, te_ref, tg_ref, w1_ref, b1_ref, w2_ref, b2_ref, *rest, tm, rb, emit_cast):
    e = pl.program_id(1)
    if emit_cast:
        o_ref, w1o_ref, w2o_ref, pos_ref, tri_ref, sp_ref, sy_ref = rest
        w1o_ref[0] = w1_ref[0].astype(BF16)
        w2o_ref[0] = w2_ref[0].astype(BF16)
        w1_ref, w2_ref = w1o_ref, w2o_ref
    else:
        o_ref, pos_ref, tri_ref, sp_ref, sy_ref = rest

    @pl.when(jnp.logical_and(pl.program_id(0) == 0, e == 0))
    def _():
        si = lax.broadcasted_iota(I32, (tm, tm), 0)
        ti = lax.broadcasted_iota(I32, (tm, tm), 1)
        tri_ref[...] = jnp.where(si <= ti, 1.0, 0.0).astype(BF16)

    @pl.when(e == 0)
    def _():
        o_ref[...] = jnp.zeros_like(o_ref)
        eio = lax.broadcasted_iota(I32, (N_EXPERTS, tm), 0)
        sel = jnp.zeros((N_EXPERTS, tm), F32)
        for k in range(TOP_K):
            sel = sel + jnp.where(te_ref[k:k + 1, :] == eio, 1.0, 0.0)
        rank = jnp.dot(sel.astype(BF16), tri_ref[...], preferred_element_type=F32)
        pos_ref[...] = rank * sel

    pm = pos_ref[pl.ds(e, 1), :]
    gate = jnp.zeros((1, tm), F32)
    for k in range(TOP_K):
        gate = gate + jnp.where(te_ref[k:k + 1, :] == e, tg_ref[k:k + 1, :], 0.0)
    cnt = cnt_ref[pl.program_id(0) * N_EXPERTS + e]
    nblk = (cnt + (rb - 1)) // rb

    def block(blk):
        r = (blk * rb + 1 + lax.broadcasted_iota(I32, (rb, 1), 0)).astype(F32)
        hit = pm == r
        onehot = jnp.where(hit, 1.0, 0.0).astype(BF16)
        xg = jnp.dot(onehot, x_ref[...], preferred_element_type=F32).astype(BF16)
        h = jnp.dot(xg, w1_ref[0], preferred_element_type=F32) + b1_ref[0]
        hg = jnp.minimum(h[:, :D_FF], SWIGLU_LIMIT)
        hl = jnp.clip(h[:, D_FF:], -SWIGLU_LIMIT, SWIGLU_LIMIT)
        a = hg * jax.nn.sigmoid(SWIGLU_ALPHA * hg) * (hl + 1.0)
        y = jnp.dot(a.astype(BF16), w2_ref[0], preferred_element_type=F32) + b2_ref[0]
        grow = jnp.sum(jnp.where(hit, gate, 0.0), axis=-1, keepdims=True)
        return onehot, (y * grow).astype(BF16)

    slot = pl.multiple_of((e % MOE_SCATTER_GROUP) * rb, rb)

    @pl.when(cnt > 0)
    def _():
        onehot, yg = block(0)
        sp_ref[pl.ds(slot, rb), :] = onehot
        sy_ref[pl.ds(slot, rb), :] = yg

    @pl.when(cnt == 0)
    def _():
        sp_ref[pl.ds(slot, rb), :] = jnp.zeros((rb, tm), BF16)
        sy_ref[pl.ds(slot, rb), :] = jnp.zeros((rb, sy_ref.shape[1]), BF16)

    def extra(blk, carry):
        onehot, yg = block(blk)
        o_ref[...] += lax.dot_general(onehot, yg, TN_DIMS, preferred_element_type=F32)
        return carry

    lax.fori_loop(1, nblk, extra, 0)

    def flush(rows):
        o_ref[...] += lax.dot_general(sp_ref[:rows, :], sy_ref[:rows, :], TN_DIMS, preferred_element_type=F32)

    last = N_EXPERTS - 1
    if last % MOE_SCATTER_GROUP != MOE_SCATTER_GROUP - 1:
        @pl.when(e == last)
        def _():
            flush((last % MOE_SCATTER_GROUP + 1) * rb)

    @pl.when(e % MOE_SCATTER_GROUP == MOE_SCATTER_GROUP - 1)
    def _():
        flush(MOE_SCATTER_GROUP * rb)


def _moe(u2, te, tg, w1, b1, w2, b2, layer, emit_cast=False):
    n, d = u2.shape
    tm = _row_tile(n, MOE_TOKEN_TILE)
    assert not emit_cast or n == tm
    rb = min(tm, tm * TOP_K // N_EXPERTS + 32)
    te_t = te.T
    tg_t = tg.T
    hits = te.reshape(n // tm, tm * TOP_K)[:, :, None] == jnp.arange(N_EXPERTS, dtype=I32)
    counts = jnp.sum(hits.astype(I32), axis=1).reshape(-1)
    grid_spec = pltpu.PrefetchScalarGridSpec(
        num_scalar_prefetch=1,
        grid=(n // tm, N_EXPERTS),
        in_specs=[
            pl.BlockSpec((tm, d), lambda i, e, c: (i, 0)),
            pl.BlockSpec((TOP_K, tm), lambda i, e, c: (0, i)),
            pl.BlockSpec((TOP_K, tm), lambda i, e, c: (0, i)),
            pl.BlockSpec((None, 1, d, 2 * D_FF), lambda i, e, c: (layer, e, 0, 0)),
            pl.BlockSpec((1, 1, 2 * D_FF), lambda i, e, c: (e, 0, 0)),
            pl.BlockSpec((None, 1, D_FF, d), lambda i, e, c: (layer, e, 0, 0)),
            pl.BlockSpec((1, 1, d), lambda i, e, c: (e, 0, 0)),
        ],
        out_specs=[pl.BlockSpec((tm, d), lambda i, e, c: (i, 0))] + ([
            pl.BlockSpec((1, d, 2 * D_FF), lambda i, e, c: (e, 0, 0)),
            pl.BlockSpec((1, D_FF, d), lambda i, e, c: (e, 0, 0))] if emit_cast else []),
        scratch_shapes=[pltpu.VMEM((N_EXPERTS, tm), F32), pltpu.VMEM((tm, tm), BF16),
                        pltpu.VMEM((MOE_SCATTER_GROUP * rb, tm), BF16), pltpu.VMEM((MOE_SCATTER_GROUP * rb, d), BF16)],
    )
    out_shape = [jax.ShapeDtypeStruct((n, d), F32)] + ([
        jax.ShapeDtypeStruct((N_EXPERTS, d, 2 * D_FF), BF16),
        jax.ShapeDtypeStruct((N_EXPERTS, D_FF, d), BF16)] if emit_cast else [])
    outs = pl.pallas_call(
        functools.partial(_moe_kernel, tm=tm, rb=rb, emit_cast=emit_cast),
        grid_spec=grid_spec,
        out_shape=out_shape,
        compiler_params=_params("arbitrary", "arbitrary"),
        name="moe_experts_cast" if emit_cast else "moe_experts",
    )(counts, u2, te_t, tg_t, w1, b1.reshape(N_EXPERTS, 1, -1), w2, b2.reshape(N_EXPERTS, 1, -1))
    return outs if emit_cast else outs[0]


def _rope_tables(pos, head_dim):
    rd = head_dim // ROPE_FRACTION
    half = rd // 2
    inv = jnp.power(ROPE_THETA, -jnp.arange(half, dtype=F32) / half)
    ang = jnp.asarray(pos).astype(F32)[:, None] * inv[None, :]
    cos, sin = jnp.cos(ang), jnp.sin(ang)
    t = pos.shape[0]
    ones = jnp.ones((t, head_dim - rd), F32)
    zeros = jnp.zeros((t, head_dim - rd), F32)
    zh = jnp.zeros((t, half), F32)
    c = jnp.concatenate([cos, cos, ones], axis=-1)
    sa = jnp.concatenate([zh, sin, zeros], axis=-1)
    sb = jnp.concatenate([-sin, zh, zeros], axis=-1)
    rep = LANES // head_dim
    return tuple(jnp.tile(a, (1, rep)) for a in (c, sa, sb))


def _rope(z, c, sa, sb, half):
    w = z.shape[-1]
    rep = w // LANES
    if rep > 1:
        c, sa, sb = (jnp.concatenate([a] * rep, axis=-1) for a in (c, sa, sb))
    return z * c + pltpu.roll(z, half, 1) * sa + pltpu.roll(z, w - half, 1) * sb


def _dsa_inproj_kernel(x_ref, f_ref, g_ref, lng_ref, lnb_ref, sc_ref, sh_ref, w_in_ref,
                       c1_ref, sa1_ref, sb1_ref, c2_ref, sa2_ref, sb2_ref,
                       xo_ref, q_ref, kf_ref, vf_ref, kif_ref, kb_ref, vb_ref, kib_ref, qi_ref, w_ref):
    z = DEEPNORM_ALPHA * x_ref[0] + (1.0 + g_ref[0]) * f_ref[0]
    x = _layer_norm(z, lng_ref[...], lnb_ref[...])
    xo_ref[0] = x
    u = x * (1.0 + sc_ref[0]) + sh_ref[0]
    p = jnp.dot(u.astype(BF16), w_in_ref[...], preferred_element_type=F32)

    c1, sa1, sb1 = c1_ref[...], sa1_ref[...], sb1_ref[...]
    c2, sa2, sb2 = c2_ref[...], sa2_ref[...], sb2_ref[...]
    h1 = DSA_HD // ROPE_FRACTION // 2
    h2 = IDX_DIM // ROPE_FRACTION // 2
    o_k = DSA_Q
    o_v = DSA_Q + DSA_KV
    o_qi = DSA_Q + 2 * DSA_KV
    o_ki = o_qi + IDX_HEADS * IDX_DIM
    q = _rope(p[:, :DSA_Q], c1, sa1, sb1, h1)
    q_ref[0] = (q * DSA_LOG2_SCALE).astype(BF16)
    k = _rope(p[:, o_k:o_v], c1, sa1, sb1, h1)
    kf_ref[0] = k
    kb_ref[0] = k.astype(BF16)
    v = p[:, o_v:o_qi]
    vf_ref[0] = v
    vb_ref[0] = v.astype(BF16)
    qi = _rope(p[:, o_qi:o_ki], c2, sa2, sb2, h2)
    qi_ref[0] = qi.astype(BF16)
    tail = p[:, o_ki:o_ki + LANES]
    ki = _rope(tail, c2, sa2, sb2, h2)[:, :IDX_DIM]
    kif_ref[0] = ki
    kib_ref[0] = ki.astype(BF16)
    wi = tail[:, IDX_DIM:IDX_DIM + IDX_HEADS]
    w_ref[0] = (wi * (IDX_HEADS ** -0.5)) * (IDX_DIM ** -0.5)


def _dsa_inproj(x, f, g, lng, lnb, sc, sh, w_bf, pos):
    b, t, d = x.shape
    n = w_bf.shape[1]
    tm = _row_tile(t, PROJ_ROW_TILE)
    tabs = _rope_tables(pos, DSA_HD) + _rope_tables(pos, IDX_DIM)
    tab = pl.BlockSpec((tm, LANES), lambda i, j: (j, 0))
    vec = pl.BlockSpec((1, 1, d), lambda i, j: (i, 0, 0))
    par = pl.BlockSpec((1, d), lambda i, j: (0, 0))

    def row(w):
        return pl.BlockSpec((1, tm, w), lambda i, j: (i, j, 0))

    widths = [(d, F32), (DSA_Q, BF16), (DSA_KV, F32), (DSA_KV, F32), (IDX_DIM, F32), (DSA_KV, BF16), (DSA_KV, BF16),
              (IDX_DIM, BF16), (IDX_HEADS * IDX_DIM, BF16), (IDX_HEADS, F32)]
    return pl.pallas_call(
        _dsa_inproj_kernel,
        grid=(b, t // tm),
        in_specs=[row(d), row(d), vec, par, par, vec, vec, pl.BlockSpec((d, n), lambda i, j: (0, 0))] + [tab] * 6,
        out_specs=[row(w) for w, _ in widths],
        out_shape=[jax.ShapeDtypeStruct((b, t, w), dt) for w, dt in widths],
        compiler_params=_params("arbitrary", "arbitrary"),
        name="ln_inproj_rope_split",
    )(x, f, g, lng, lnb, sc, sh, w_bf, *tabs)


def _f2key(f):
    bits = lax.bitcast_convert_type(f, I32)
    return bits ^ ((bits >> 31) & 0x7FFFFFFF)


def _key2f(k):
    return lax.bitcast_convert_type(k ^ ((k >> 31) & 0x7FFFFFFF), F32)


def _dsa_kernel(nkt_ref, q_ref, qi_ref, w_ref, k_ref, v_ref, ki_ref, kpos_ref, qlim_ref, nadm_ref, o_ref,
                s_ref, wb_ref, qg_ref, m_ref, l_ref, acc_ref, flag_ref, *, tq, tk, nsel, value_iters):
    nkt = nkt_ref[pl.program_id(1)]
    qlim = qlim_ref[...]
    w = w_ref[0]
    qi = qi_ref[0]
    qis_all = jnp.concatenate([qi[:, h * IDX_DIM:(h + 1) * IDX_DIM] for h in range(IDX_HEADS)], axis=0)
    nlg = tk // LANES
    for h in range(IDX_HEADS):
        wb_ref[h] = jnp.broadcast_to(w[:, h:h + 1], (tq, LANES))

    def p1(kt, carry):
        rmax, rmin, cgt0, cge0 = carry
        off = pl.multiple_of(kt * tk, tk)
        kit = ki_ref[0, pl.ds(off, tk), :]
        acc = jnp.zeros((tq, tk), F32)
        d_all = lax.dot_general(qis_all, kit, NT_DIMS, preferred_element_type=F32)
        for h in range(IDX_HEADS):
            d = d_all[h * tq:(h + 1) * tq]
            acc = acc + jnp.maximum(d, 0.0) * jnp.concatenate([wb_ref[h]] * nlg, axis=1)
        s = jnp.where(kpos_ref[kt] < qlim, acc, -jnp.inf)
        s_ref[kt] = s
        gt0 = jnp.where(s > 0.0, 1.0, 0.0)
        ge0 = jnp.where(s >= 0.0, 1.0, 0.0)
        for g in range(nlg):
            lanes = slice(g * LANES, (g + 1) * LANES)
            rmax = jnp.maximum(rmax, s[:, lanes])
            rmin = jnp.minimum(rmin, acc[:, lanes])
            cgt0 = cgt0 + gt0[:, lanes]
            cge0 = cge0 + ge0[:, lanes]
        return rmax, rmin, cgt0, cge0

    zcol = jnp.zeros((tq, 1), F32)
    zlan = jnp.zeros((tq, LANES), F32)
    init = (jnp.full((tq, LANES), -jnp.inf, F32), jnp.full((tq, LANES), jnp.inf, F32), zlan, zlan)
    rmax, rmin, cgt0, cge0 = lax.fori_loop(0, nkt, p1, init)
    rmax = jnp.max(rmax, axis=-1, keepdims=True)
    rmin = jnp.min(rmin, axis=-1, keepdims=True)
    cgt0 = jnp.sum(cgt0, axis=-1, keepdims=True)
    cge0 = jnp.sum(cge0, axis=-1, keepdims=True)
    nadm = nadm_ref[...]

    rg = min(tq, 32)
    fsel = float(nsel)
    unbounded = float(2 ** 30)

    def count_ge(tb):
        def body(kt, c):
            out = []
            for r in range(tq // rg):
                tr = jnp.concatenate([tb[r * rg:(r + 1) * rg, :]] * nlg, axis=1)
                hit = jnp.where(s_ref[kt, r * rg:(r + 1) * rg, :] >= tr, 1.0, 0.0)
                cr = c[r * rg:(r + 1) * rg]
                for g in range(nlg):
                    cr = cr + hit[:, g * LANES:(g + 1) * LANES]
                out.append(cr)
            return jnp.concatenate(out, axis=0)
        c = lax.fori_loop(0, nkt, body, zlan)
        return jnp.broadcast_to(jnp.sum(c, axis=-1, keepdims=True), (tq, LANES))

    rmax, rmin, cgt0, cge0, nadm = (jnp.broadcast_to(a, (tq, LANES)) for a in (rmax, rmin, cgt0, cge0, nadm))
    few = nadm <= fsel
    many = jnp.logical_not(few)
    pos = jnp.logical_and(many, cgt0 > fsel)
    neg = jnp.logical_and(many, cge0 < fsel)
    zero = jnp.logical_and(many, jnp.logical_and(cgt0 <= fsel, cge0 >= fsel))
    lo0 = jnp.where(pos, 0, _f2key(jnp.where(neg, rmin, 0.0)))
    hi0 = jnp.where(pos, _f2key(jnp.where(pos, rmax, 0.0)) + 1, 0)
    clo0 = jnp.where(pos, cge0, nadm)
    chi0 = jnp.where(pos, 0.0, cge0)
    act0 = jnp.where(jnp.logical_or(pos, neg), 1.0, 0.0)
    t0 = jnp.where(few, -jnp.inf, 0.0)
    need0 = jnp.where(zero, fsel - cgt0, unbounded)

    def cond(st):
        return st[0] > 0.0

    def step(st):
        _, it, lo, hi, clo, chi, act, thr, need = st
        conv = jnp.logical_and(act > 0.0, lo + 1 >= hi)
        thr = jnp.where(conv, _key2f(lo), thr)
        need = jnp.where(conv, fsel - chi, need)
        act = jnp.where(conv, 0.0, act)
        flo, fhi = _key2f(lo), _key2f(hi)
        mid_i = (lo >> 1) + (hi >> 1) + (lo & hi & 1)
        lclo = jnp.log(clo)
        interp = (lclo - float(np.log(nsel - 0.5))) / (lclo - jnp.log(jnp.maximum(chi, 0.5)))
        frac = jnp.where(it % 3 == 2, 0.5, interp)
        mid_v = _f2key(flo + frac * (fhi - flo))
        use_v = jnp.logical_and(it < value_iters, jnp.logical_and(mid_v > lo, mid_v < hi))
        mid = jnp.where(use_v, mid_v, mid_i)
        tm = _key2f(mid)
        c = count_ge(jnp.where(act > 0.0, tm, jnp.inf))
        hit = jnp.logical_and(act > 0.0, c == fsel)
        thr = jnp.where(hit, tm, thr)
        up = jnp.logical_and(act > 0.0, c > fsel)
        dn = jnp.logical_and(act > 0.0, c < fsel)
        lo = jnp.where(up, mid, lo)
        clo = jnp.where(up, c, clo)
        hi = jnp.where(dn, mid, hi)
        chi = jnp.where(dn, c, chi)
        act = jnp.where(hit, 0.0, act)
        return jnp.max(act), it + 1, lo, hi, clo, chi, act, thr, need

    st = (jnp.max(act0), jnp.int32(0), lo0, hi0, clo0, chi0, act0, t0, need0)
    _, _, _, _, _, _, _, thr, need = lax.while_loop(cond, step, st)
    thr, need = thr[:, 0:1], need[:, 0:1]
    tied = need < unbounded * 0.5
    has_ties = jnp.max(jnp.where(tied, 1.0, 0.0)) > 0.0
    thr = jnp.maximum(thr, -F32_MAX)

    for g in range(DSA_KV_HEADS):
        for hh in range(DSA_GROUP):
            hd = g * DSA_GROUP + hh
            qg_ref[g, hh * tq:(hh + 1) * tq, :] = q_ref[0, :, hd * DSA_HD:(hd + 1) * DSA_HD]
    m_ref[...] = jnp.full(m_ref.shape, NEG_BIG, F32)
    l_ref[...] = jnp.zeros(l_ref.shape, F32)
    acc_ref[...] = jnp.zeros(acc_ref.shape, F32)

    ones_cols = jnp.ones((tk, DSA_HD), BF16)

    def attend(kt, bias):
        off = pl.multiple_of(kt * tk, tk)
        qk = [lax.dot_general(qg_ref[g], k_ref[0, pl.ds(off, tk), g * DSA_HD:(g + 1) * DSA_HD], NT_DIMS,
                              preferred_element_type=F32) for g in range(DSA_KV_HEADS)]
        for g in range(DSA_KV_HEADS):
            vt_g = jnp.concatenate([v_ref[0, pl.ds(off, tk), g * DSA_HD:(g + 1) * DSA_HD], ones_cols], axis=1)
            for hh in range(DSA_GROUP):
                rows = slice(hh * tq, (hh + 1) * tq)
                lg = qk[g][rows] + bias
                m_old = m_ref[g, rows, :]
                m_new = jnp.maximum(m_old, jnp.max(lg, axis=-1, keepdims=True))
                alpha = jnp.exp2(m_old - m_new)
                p = jnp.exp2(lg - jnp.concatenate([m_new] * nlg, axis=1))
                pv = jnp.dot(p.astype(BF16), vt_g, preferred_element_type=F32)
                acc_ref[g, rows, :] = alpha * acc_ref[g, rows, :] + pv[:, :DSA_HD]
                l_ref[g, rows, :] = alpha * l_ref[g, rows, :] + pv[:, DSA_HD:]
                m_ref[g, rows, :] = m_new

    @pl.when(jnp.logical_not(has_ties))
    def _():
        def body(kt, carry):
            attend(kt, jnp.where(s_ref[kt] >= thr, 0.0, NEG_BIG))
            return carry
        lax.fori_loop(0, nkt, body, 0)

    @pl.when(has_ties)
    def _():
        def scan(kt, carry):
            seen, cut, nin = carry
            eqc = jnp.sum(jnp.where(s_ref[kt] == thr, 1.0, 0.0), axis=-1, keepdims=True)
            here = jnp.logical_and(tied, jnp.logical_and(seen < need, seen + eqc >= need))
            cut = jnp.where(here, kt, cut)
            nin = jnp.where(here, need - seen, nin)
            flag_ref[kt] = jnp.max(jnp.where(here, 1.0, 0.0)).astype(I32)
            return seen + eqc, cut, nin

        cut0 = jnp.where(tied, -1, nkt).astype(I32)
        _, cut, nin = lax.fori_loop(0, nkt, scan, (zcol, cut0, zcol))

        def body(kt, carry):
            s = s_ref[kt]
            before = jnp.where(kt < cut, 0.0, NEG_BIG)

            def plain():
                return jnp.where(s == thr, before, NEG_BIG)

            def ranked():
                si = lax.broadcasted_iota(I32, (tk, tk), 0)
                ti = lax.broadcasted_iota(I32, (tk, tk), 1)
                triu = jnp.where(si <= ti, 1.0, 0.0).astype(BF16)
                eq = jnp.where(s == thr, 1.0, 0.0).astype(BF16)
                rank = jnp.dot(eq, triu, preferred_element_type=F32)
                inside = jnp.where(rank <= nin, 0.0, NEG_BIG)
                return jnp.where(s == thr, jnp.where(kt == cut, inside, before), NEG_BIG)

            tie_bias = lax.cond(flag_ref[kt] > 0, ranked, plain)
            attend(kt, jnp.where(s > thr, 0.0, tie_bias))
            return carry
        lax.fori_loop(0, nkt, body, 0)

    for g in range(DSA_KV_HEADS):
        og = acc_ref[g] / l_ref[g]
        for hh in range(DSA_GROUP):
            hd = g * DSA_GROUP + hh
            o_ref[0, :, hd * DSA_HD:(hd + 1) * DSA_HD] = og[hh * tq:(hh + 1) * tq, :].astype(BF16)


def _dsa_attend(q, qi, w, k, v, ki, kpos, qlim, tq, tk):
    b, t, _ = q.shape
    lp = k.shape[1]
    nq = t // tq
    nsel = min(DSA_TOPK, int(np.sum(kpos < np.iinfo(np.int32).max)) // 4)
    qmax = qlim.reshape(nq, tq).max(axis=1)
    nadm = np.searchsorted(kpos, qlim, side="left")
    nkt = np.minimum(-(-np.searchsorted(kpos, qmax, side="left") // tk), lp // tk).astype(np.int32)
    nkt = np.maximum(nkt, 1)
    single = pl.Buffered(1)

    def full(wd):
        return pl.BlockSpec((1, lp, wd), lambda i, j, n: (i, 0, 0), pipeline_mode=single)

    def row(wd):
        return pl.BlockSpec((1, tq, wd), lambda i, j, n: (i, j, 0))

    grid_spec = pltpu.PrefetchScalarGridSpec(
        num_scalar_prefetch=1,
        grid=(b, nq),
        in_specs=[row(DSA_Q), row(IDX_HEADS * IDX_DIM), row(IDX_HEADS), full(DSA_KV), full(DSA_KV), full(IDX_DIM),
                  pl.BlockSpec((lp // tk, 1, tk), lambda i, j, n: (0, 0, 0)),
                  pl.BlockSpec((tq, 1), lambda i, j, n: (j, 0)),
                  pl.BlockSpec((tq, 1), lambda i, j, n: (j, 0))],
        out_specs=row(DSA_Q),
        scratch_shapes=[
            pltpu.VMEM((lp // tk, tq, tk), F32),
            pltpu.VMEM((IDX_HEADS, tq, LANES), F32),
            pltpu.VMEM((DSA_KV_HEADS, DSA_GROUP * tq, DSA_HD), BF16),
            pltpu.VMEM((DSA_KV_HEADS, DSA_GROUP * tq, LANES), F32),
            pltpu.VMEM((DSA_KV_HEADS, DSA_GROUP * tq, LANES), F32),
            pltpu.VMEM((DSA_KV_HEADS, DSA_GROUP * tq, DSA_HD), F32),
            pltpu.SMEM((lp // tk,), I32),
        ],
    )
    return pl.pallas_call(
        functools.partial(_dsa_kernel, tq=tq, tk=tk, nsel=nsel, value_iters=DSA_VALUE_ITERS),
        grid_spec=grid_spec,
        out_shape=jax.ShapeDtypeStruct((b, t, DSA_Q), BF16),
        compiler_params=_params("arbitrary", "arbitrary"),
        name="dsa_select_attend",
    )(jnp.asarray(nkt), q, qi, w, k, v, ki, jnp.asarray(kpos, I32).reshape(lp // tk, 1, tk),
      jnp.asarray(qlim, I32).reshape(t, 1), jnp.asarray(nadm, F32).reshape(t, 1))


def _trunk(x, mods, pos, ml_state, ds_cache, prm, moe_bf=None):
    b, t, d = x.shape
    casts = []

    def moe(i, u2, te, tg):
        args = (u2.reshape(b * t, d), te.reshape(b * t, TOP_K), tg.reshape(b * t, TOP_K))
        if moe_bf is None:
            f, w1b, w2b = _moe(*args, prm["moe_w1"], prm["moe_b1"][i], prm["moe_w2"], prm["moe_b2"][i], i,
                               emit_cast=True)
            casts.append((w1b, w2b))
        else:
            w1b, w2b = moe_bf[i]
            f = _moe(*args, w1b[None], prm["moe_b1"][i], w2b[None], prm["moe_b2"][i], 0)
        return f.reshape(b, t, d)

    chunk = next((c for c in (ML_EVAL_CHUNK, CHUNK) if t % c == 0), t)

    def mod(i, k):
        return mods[i][:, None, k * d:(k + 1) * d]

    p, pg = _ml_inproj(x, mod(0, 1), mod(0, 0), prm["ml_w_in"])
    if ml_state is None:
        c0 = jnp.zeros((b, ML_HEADS, ML_DV, ML_DK), F32)
        n0 = jnp.zeros((b, ML_HEADS, ML_DK), F32)
        m0 = jnp.zeros((b, ML_HEADS), F32)
    else:
        c0, n0, m0 = (s[0] for s in ml_state)
    hg, c_new, n_new, m_new = _mlstm(p, pg, prm["ml_b_gates"], prm["ml_norm_g"], c0, n0, m0, chunk)
    x1, u2, te, tg = _outproj(hg, prm["ml_w_out"], x, mod(0, 2), mod(0, 4), mod(0, 3),
                              prm["ln_g"][0, 0:1], prm["ln_b"][0, 0:1], prm["w_router"][0], prm["b_router"][0])
    f = moe(0, u2, te, tg)

    x2, q, kf, vf, kif, kb, vb, kib, qi, w = _dsa_inproj(
        x1, f, mod(0, 5), prm["ln_g"][0, 1:2], prm["ln_b"][0, 1:2], mod(1, 1), mod(1, 0), prm["ds_w_in"], pos)
    pos_np = np.asarray(pos)
    qlim = (pos_np // CHUNK + 1) * CHUNK
    if ds_cache is None:
        kpos = pos_np
        tq, tk = DSA_Q_TILE, DSA_K_TILE
        if t % tk:
            tq, tk = t, t
        k_all, v_all, ki_all = kb, vb, kib
    else:
        ck, cv, cki = ds_cache
        past = ck.shape[1]
        total = past + t
        lp = -(-total // (3 * LANES)) * (3 * LANES)
        tq, tk = t, 3 * LANES
        kpos = np.concatenate([np.arange(past), pos_np, np.full((lp - total,), np.iinfo(np.int32).max)])

        def cat(cache, new):
            full = jnp.concatenate([cache.reshape(b, past, -1).astype(BF16), new], axis=1)
            return jnp.pad(full, ((0, 0), (0, lp - total), (0, 0)))

        k_all, v_all, ki_all = cat(ck, kb), cat(cv, vb), cat(cki, kib)
    o = _dsa_attend(q, qi, w, k_all, v_all, ki_all, kpos.astype(np.int64), qlim, tq, tk)
    x3, u2, te, tg = _outproj(o, prm["ds_w_out"], x2, mod(1, 2), mod(1, 4), mod(1, 3),
                              prm["ln_g"][1, 0:1], prm["ln_b"][1, 0:1], prm["w_router"][1], prm["b_router"][1])
    f = moe(1, u2, te, tg)
    y = _final_ln(x3, f, mod(1, 5), prm["ln_g"][1, 1:2], prm["ln_b"][1, 1:2])

    ml_out = (c_new[None], n_new[None], m_new[None])
    ds_out = (kf.reshape(1, b, t, DSA_KV_HEADS, DSA_HD), vf.reshape(1, b, t, DSA_KV_HEADS, DSA_HD), kif[None])
    return y, ml_out, ds_out, casts


def _prepare(w_ada, b_ada, ln_g, ln_b, ml_w_in, ml_b_gates, ml_norm_g, ml_w_out, ds_w_in, ds_w_out,
             moe_w_router, moe_b_router, moe_w1, moe_b1, moe_w2, moe_b2):
    return {
        "ln_g": ln_g, "ln_b": ln_b,
        "ml_w_in": jnp.pad(ml_w_in[0], ((0, 0), (0, ML_IN_PAD - ML_IN))).astype(BF16),
        "ml_b_gates": ml_b_gates[0], "ml_norm_g": ml_norm_g[0],
        "ml_w_out": ml_w_out[0].astype(BF16),
        "ds_w_in": jnp.pad(ds_w_in[0], ((0, 0), (0, DSA_IN_PAD - DSA_IN))).astype(BF16),
        "ds_w_out": ds_w_out[0].astype(BF16),
        "w_router": jnp.pad(moe_w_router, ((0, 0), (0, 0), (0, LANES - N_EXPERTS))),
        "b_router": jnp.pad(moe_b_router, ((0, 0), (0, LANES - N_EXPERTS)), constant_values=NEG_BIG)[:, None, :],
        "moe_w1": moe_w1, "moe_b1": moe_b1, "moe_w2": moe_w2, "moe_b2": moe_b2,
    }


def kernel(x_prompt, x_sample, state_mlstm_C, state_mlstm_n, state_mlstm_m, cache_dsa_k, cache_dsa_v,
           cache_dsa_kidx, c_prompt, c_sample, w_ada, b_ada, ln_g, ln_b, ml_w_in, ml_b_gates, ml_norm_g,
           ml_w_out, ds_w_in, ds_w_out, moe_w_router, moe_b_router, moe_w1, moe_b1, moe_w2, moe_b2):
    prm = _prepare(w_ada, b_ada, ln_g, ln_b, ml_w_in, ml_b_gates, ml_norm_g, ml_w_out, ds_w_in, ds_w_out,
                   moe_w_router, moe_b_router, moe_w1, moe_b1, moe_w2, moe_b2)
    bp, bs = c_prompt.shape[0], c_sample.shape[0]
    rows = -(-(bp + bs) // 8) * 8
    c_rows = jnp.pad(jnp.concatenate([c_prompt, c_sample], axis=0), ((0, rows - bp - bs), (0, 0)))
    mods = _ada(c_rows, w_ada, b_ada)
    mods_p = [mods[i, :bp] for i in range(DEPTH)]
    mods_s = [mods[i, bp:bp + bs] for i in range(DEPTH)]

    past = cache_dsa_k.shape[2]
    pos_s = past + np.arange(x_sample.shape[1], dtype=np.int32)
    y_s, ml_s, ds_s, moe_bf = _trunk(x_sample, mods_s, pos_s, (state_mlstm_C, state_mlstm_n, state_mlstm_m),
                                     (cache_dsa_k[0], cache_dsa_v[0], cache_dsa_kidx[0]), prm)
    pos_p = np.arange(x_prompt.shape[1], dtype=np.int32)
    y_p, ml_p, ds_p, _ = _trunk(x_prompt, mods_p, pos_p, None, None, prm, moe_bf=moe_bf)
    return (y_p, y_s, ml_p[0], ml_p[1], ml_p[2], ds_p[0], ds_p[1], ds_p[2],
            ml_s[0], ml_s[1], ml_s[2], ds_s[0], ds_s[1], ds_s[2])
```

```python
import functools

import numpy as np
import jax
import jax.numpy as jnp
from jax import lax
from jax.experimental import pallas as pl
from jax.experimental.pallas import tpu as pltpu

F32 = jnp.float32
BF16 = jnp.bfloat16
I32 = jnp.int32

D_MODEL = 1024
DEPTH = 2
CHUNK = 64
ML_HEADS = 4
ML_DV = D_MODEL // ML_HEADS
ML_DK = ML_DV // 2
ML_QK = ML_HEADS * ML_DK
ML_V = ML_HEADS * ML_DV
ML_IN = 2 * ML_QK + 2 * ML_V + 2 * ML_HEADS
ML_GATE_OFF = 2 * ML_QK + 2 * ML_V
ML_EVAL_CHUNK = 256
DSA_HEADS = 8
DSA_KV_HEADS = 2
DSA_HD = D_MODEL // DSA_HEADS
DSA_GROUP = DSA_HEADS // DSA_KV_HEADS
IDX_HEADS = 8
IDX_DIM = 64
DSA_TOPK = 256
DSA_Q = DSA_HEADS * DSA_HD
DSA_KV = DSA_KV_HEADS * DSA_HD
DSA_IN = DSA_Q + 2 * DSA_KV + IDX_HEADS * IDX_DIM + IDX_DIM + IDX_HEADS
DSA_LOG2_SCALE = (DSA_HD ** -0.5) * float(np.log2(np.e))
ROPE_THETA = 500000.0
ROPE_FRACTION = 4
N_EXPERTS = 32
TOP_K = 4
D_FF = D_MODEL
SWIGLU_LIMIT = 7.0
SWIGLU_ALPHA = 1.702
DEEPNORM_ALPHA = (2 * DEPTH) ** 0.25
LN_EPS = 1e-5

LANES = 128
V7X_VMEM_BYTES = 64 * 1024 * 1024
VMEM_LIMIT = V7X_VMEM_BYTES - 8 * 1024 * 1024

PROJ_ROW_TILE = 512
OUTPROJ_ROW_TILE = 256
FINAL_LN_ROW_TILE = 1024
ADA_COL_TILE = 1536
MOE_TOKEN_TILE = 1024
MOE_SUBTILES = 2
DSA_Q_TILE = 128
DSA_K_TILE = 1024
DSA_VALUE_ITERS = 16

ML_IN_PAD = -(-ML_IN // LANES) * LANES
DSA_IN_PAD = -(-DSA_IN // LANES) * LANES
NEG_BIG = -1e30
F32_MAX = float(np.finfo(np.float32).max)
HIGHEST = lax.Precision.HIGHEST
NT_DIMS = (((1,), (1,)), ((), ()))
TN_DIMS = (((0,), (0,)), ((), ()))


def _params(*sem):
    return pltpu.CompilerParams(dimension_semantics=sem, vmem_limit_bytes=VMEM_LIMIT)


def _row_tile(t, pref):
    return pref if t % pref == 0 else t


def _ada_kernel(c_ref, w_ref, b_ref, o_ref):
    c = c_ref[...]
    cond = (c * jax.nn.sigmoid(c)).astype(BF16)
    o_ref[0] = jnp.dot(cond, w_ref[0].astype(BF16), preferred_element_type=F32) + b_ref[0]


def _ada(c_rows, w_ada, b_ada):
    rows = c_rows.shape[0]
    n = w_ada.shape[-1]
    tn = ADA_COL_TILE
    return pl.pallas_call(
        _ada_kernel,
        grid=(DEPTH, n // tn),
        in_specs=[
            pl.BlockSpec((rows, D_MODEL), lambda i, j: (0, 0)),
            pl.BlockSpec((1, D_MODEL, tn), lambda i, j: (i, 0, j)),
            pl.BlockSpec((1, 1, tn), lambda i, j: (i, 0, j)),
        ],
        out_specs=pl.BlockSpec((1, rows, tn), lambda i, j: (i, 0, j)),
        out_shape=jax.ShapeDtypeStruct((DEPTH, rows, n), F32),
        compiler_params=_params("arbitrary", "arbitrary"),
        name="ada_mod",
    )(c_rows, w_ada, b_ada.reshape(DEPTH, 1, n))


def _layer_norm(z, g, b):
    mu = jnp.mean(z, axis=-1, keepdims=True)
    zc = z - mu
    var = jnp.mean(zc * zc, axis=-1, keepdims=True)
    return zc * lax.rsqrt(var + LN_EPS) * g + b


def _ml_inproj_kernel(x_ref, sc_ref, sh_ref, w_ref, pm_ref, pg_ref):
    u = x_ref[0] * (1.0 + sc_ref[0]) + sh_ref[0]
    p = jnp.dot(u.astype(BF16), w_ref[...], preferred_element_type=F32)
    pm_ref[0, :, :ML_QK] = (p[:, :ML_QK] * (ML_DK ** -0.5)).astype(BF16)
    pm_ref[0, :, ML_QK:] = p[:, ML_QK:ML_GATE_OFF].astype(BF16)
    pg_ref[0] = p[:, ML_GATE_OFF:]


def _ml_inproj(x, sc, sh, w_bf):
    b, t, d = x.shape
    n = w_bf.shape[1]
    tm = _row_tile(t, PROJ_ROW_TILE)
    row = pl.BlockSpec((1, tm, d), lambda i, j: (i, j, 0))
    vec = pl.BlockSpec((1, 1, d), lambda i, j: (i, 0, 0))
    return pl.pallas_call(
        _ml_inproj_kernel,
        grid=(b, t // tm),
        in_specs=[row, vec, vec, pl.BlockSpec((d, n), lambda i, j: (0, 0))],
        out_specs=[pl.BlockSpec((1, tm, ML_GATE_OFF), lambda i, j: (i, j, 0)),
                   pl.BlockSpec((1, tm, n - ML_GATE_OFF), lambda i, j: (i, j, 0))],
        out_shape=[jax.ShapeDtypeStruct((b, t, ML_GATE_OFF), BF16),
                   jax.ShapeDtypeStruct((b, t, n - ML_GATE_OFF), F32)],
        compiler_params=_params("arbitrary", "arbitrary"),
        name="ml_inproj",
    )(x, sc, sh, w_bf)


def _final_ln_kernel(x_ref, f_ref, g_ref, lng_ref, lnb_ref, o_ref):
    z = DEEPNORM_ALPHA * x_ref[0] + (1.0 + g_ref[0]) * f_ref[0]
    o_ref[0] = _layer_norm(z, lng_ref[...], lnb_ref[...])


def _final_ln(x, f, g, lng, lnb):
    b, t, d = x.shape
    tm = _row_tile(t, FINAL_LN_ROW_TILE)
    row = pl.BlockSpec((1, tm, d), lambda i, j: (i, j, 0))
    vec = pl.BlockSpec((1, 1, d), lambda i, j: (i, 0, 0))
    par = pl.BlockSpec((1, d), lambda i, j: (0, 0))
    return pl.pallas_call(
        _final_ln_kernel,
        grid=(b, t // tm),
        in_specs=[row, row, vec, par, par],
        out_specs=row,
        out_shape=jax.ShapeDtypeStruct((b, t, d), F32),
        compiler_params=_params("arbitrary", "arbitrary"),
        name="final_ln",
    )(x, f, g, lng, lnb)


def _log_sigmoid(x):
    return jnp.minimum(x, 0.0) - jnp.log1p(jnp.exp(-jnp.abs(x)))


def _mlstm_kernel(p_ref, pg_ref, gt_ref, bcol_ref, brow_ref, ng_ref, c0_ref, n0_ref, m0_ref,
                  hg_ref, c_ref, n_ref, m_ref, *, chunk):
    L = chunk

    @pl.when(pl.program_id(1) == 0)
    def _():
        c_ref[...] = c0_ref[...]
        n_ref[...] = n0_ref[...]
        m_ref[...] = m0_ref[...]

    gcol = pg_ref[0, :, :2 * ML_HEADS] + bcol_ref[...]
    grow = gt_ref[0] + brow_ref[...]
    lf_col = _log_sigmoid(gcol)
    lf_row = _log_sigmoid(grow)
    ti = lax.broadcasted_iota(I32, (L, L), 0)
    si = lax.broadcasted_iota(I32, (L, L), 1)
    causal = si <= ti
    tril = jnp.where(causal, 1.0, 0.0).astype(F32)
    triu = jnp.where(ti <= si, 1.0, 0.0).astype(F32)
    b_col = jnp.dot(tril, lf_col, precision=HIGHEST, preferred_element_type=F32)
    b_row = jnp.dot(lf_row, triu, precision=HIGHEST, preferred_element_type=F32)

    for h in range(ML_HEADS):
        qs = p_ref[0, :, h * ML_DK:(h + 1) * ML_DK]
        kb = p_ref[0, :, ML_QK + h * ML_DK:ML_QK + (h + 1) * ML_DK]
        vb = p_ref[0, :, 2 * ML_QK + h * ML_DV:2 * ML_QK + (h + 1) * ML_DV]
        v = vb.astype(F32)
        o = p_ref[0, :, 2 * ML_QK + ML_V + h * ML_DV:2 * ML_QK + ML_V + (h + 1) * ML_DV].astype(F32)
        bc = b_col[:, ML_HEADS + h:ML_HEADS + h + 1]
        ic = gcol[:, h:h + 1]
        br = b_row[ML_HEADS + h:ML_HEADS + h + 1, :]
        ir = grow[h:h + 1, :]
        c_old = c_ref[0, h]
        n_old = n_ref[0, h:h + 1, :]
        m_old = m_ref[0, h:h + 1, 0:1]

        dm = jnp.where(causal, bc - br + ir, -jnp.inf)
        inter = bc + m_old
        mt = jnp.maximum(inter, jnp.max(dm, axis=-1, keepdims=True))
        qk = lax.dot_general(qs, kb, NT_DIMS, preferred_element_type=F32)
        s = jnp.exp(dm - mt) * qk
        wp = jnp.exp(inter - mt)
        qc = lax.dot_general(qs, c_old.astype(BF16), NT_DIMS, preferred_element_type=F32)
        num = jnp.dot(s.astype(BF16), vb, preferred_element_type=F32) + wp * qc
        qn = jnp.sum(qs.astype(F32) * n_old, axis=-1, keepdims=True)
        den = jnp.sum(s, axis=-1, keepdims=True) + wp * qn
        hh = num / jnp.maximum(jnp.abs(den), jnp.exp(-mt))

        m_new = mt[L - 1:L, :]
        wk = jnp.exp(bc[L - 1:L, :] - bc + ic - m_new)
        wprev = jnp.exp(inter[L - 1:L, :] - m_new)
        vw = (v * wk).astype(BF16)
        c_ref[0, h] = wprev * c_old + lax.dot_general(vw, kb, TN_DIMS, preferred_element_type=F32)
        n_ref[0, h:h + 1, :] = wprev * n_old + jnp.sum(wk * kb.astype(F32), axis=0, keepdims=True)
        m_ref[0, h:h + 1, :] = jnp.broadcast_to(m_new, (1, ML_DK))

        mu = jnp.mean(hh, axis=-1, keepdims=True)
        hc = hh - mu
        var = jnp.mean(hc * hc, axis=-1, keepdims=True)
        hn = hc * lax.rsqrt(var + LN_EPS) * ng_ref[:, h * ML_DV:(h + 1) * ML_DV]
        hg_ref[0, :, h * ML_DV:(h + 1) * ML_DV] = (jax.nn.sigmoid(o) * hn).astype(BF16)


def _mlstm(p, pg, b_gates, norm_g, c0, n0, m0, chunk):
    b, t, n = p.shape
    nc = t // chunk
    gt = jnp.swapaxes(pg[:, :, :2 * ML_HEADS], 1, 2)
    if nc > 1:
        gt_spec = pl.BlockSpec((1, 2 * ML_HEADS, chunk), lambda i, j: (i, 0, j))
    else:
        gt_spec = pl.BlockSpec((1, 2 * ML_HEADS, t), lambda i, j: (i, 0, 0))
    if nc > 1 and chunk % LANES != 0:
        gt = gt.reshape(b, 2 * ML_HEADS, nc, chunk).transpose(0, 2, 1, 3).reshape(b * nc, 2 * ML_HEADS, chunk)
        gt_spec = pl.BlockSpec((1, 2 * ML_HEADS, chunk), lambda i, j: (i * nc + j, 0, 0))
    m0b = jnp.broadcast_to(m0[..., None], (b, ML_HEADS, ML_DK))
    cspec = pl.BlockSpec((1, ML_HEADS, ML_DV, ML_DK), lambda i, j: (i, 0, 0, 0))
    nspec = pl.BlockSpec((1, ML_HEADS, ML_DK), lambda i, j: (i, 0, 0))
    hg, c, nn, m = pl.pallas_call(
        functools.partial(_mlstm_kernel, chunk=chunk),
        grid=(b, nc),
        in_specs=[
            pl.BlockSpec((1, chunk, n), lambda i, j: (i, j, 0)),
            pl.BlockSpec((1, chunk, pg.shape[-1]), lambda i, j: (i, j, 0)),
            gt_spec,
            pl.BlockSpec((1, 2 * ML_HEADS), lambda i, j: (0, 0)),
            pl.BlockSpec((2 * ML_HEADS, 1), lambda i, j: (0, 0)),
            pl.BlockSpec((1, ML_V), lambda i, j: (0, 0)),
            cspec, nspec, nspec,
        ],
        out_specs=[pl.BlockSpec((1, chunk, ML_V), lambda i, j: (i, j, 0)), cspec, nspec, nspec],
        out_shape=[
            jax.ShapeDtypeStruct((b, t, ML_V), BF16),
            jax.ShapeDtypeStruct((b, ML_HEADS, ML_DV, ML_DK), F32),
            jax.ShapeDtypeStruct((b, ML_HEADS, ML_DK), F32),
            jax.ShapeDtypeStruct((b, ML_HEADS, ML_DK), F32),
        ],
        compiler_params=_params("arbitrary", "arbitrary"),
        name="mlstm_scan",
    )(p, pg, gt, b_gates.reshape(1, -1), b_gates.reshape(-1, 1), norm_g.reshape(1, -1), c0, n0, m0b)
    return hg, c, nn, m[..., 0]


def _outproj_kernel(a_ref, w_ref, x_ref, g_ref, sc_ref, sh_ref, lng_ref, lnb_ref, wrh_ref, wrl_ref, br_ref,
                    x1_ref, u2_ref, te_ref, tg_ref):
    y = jnp.dot(a_ref[0], w_ref[...], preferred_element_type=F32)
    z = DEEPNORM_ALPHA * x_ref[0] + (1.0 + g_ref[0]) * y
    x1 = _layer_norm(z, lng_ref[...], lnb_ref[...])
    x1_ref[0] = x1
    u2 = x1 * (1.0 + sc_ref[0]) + sh_ref[0]
    u_hi = u2.astype(BF16)
    u2_ref[0] = u_hi
    u_lo = (u2 - u_hi.astype(F32)).astype(BF16)
    logits = (jnp.dot(u_hi, wrh_ref[...], preferred_element_type=F32)
              + jnp.dot(u_lo, wrh_ref[...], preferred_element_type=F32)
              + jnp.dot(u_hi, wrl_ref[...], preferred_element_type=F32)) + br_ref[...]
    lane = lax.broadcasted_iota(I32, logits.shape, 1).astype(F32)
    vals, idxs = [], []
    cur = logits
    for _ in range(TOP_K):
        mx = jnp.max(cur, axis=-1, keepdims=True)
        idx = jnp.min(jnp.where(cur == mx, lane, float(LANES)), axis=-1, keepdims=True)
        vals.append(mx)
        idxs.append(idx)
        cur = jnp.where(lane == idx, -jnp.inf, cur)
    es = [jnp.exp(v - vals[0]) for v in vals]
    tot = es[0] + es[1] + es[2] + es[3]
    for k in range(TOP_K):
        te_ref[0, :, k:k + 1] = idxs[k].astype(I32)
        tg_ref[0, :, k:k + 1] = es[k] / tot


def _outproj(a, w_bf, x, g, sc, sh, lng, lnb, wr_pad, br_pad):
    b, t, d = x.shape
    tm = _row_tile(t, OUTPROJ_ROW_TILE)
    wr_hi = wr_pad.astype(BF16)
    wr_lo = (wr_pad - wr_hi.astype(F32)).astype(BF16)
    row = pl.BlockSpec((1, tm, d), lambda i, j: (i, j, 0))
    vec = pl.BlockSpec((1, 1, d), lambda i, j: (i, 0, 0))
    par = pl.BlockSpec((1, d), lambda i, j: (0, 0))
    top = pl.BlockSpec((1, tm, TOP_K), lambda i, j: (i, j, 0))
    return pl.pallas_call(
        _outproj_kernel,
        grid=(b, t // tm),
        in_specs=[row, pl.BlockSpec((d, d), lambda i, j: (0, 0)), row, vec, vec, vec, par, par,
                  pl.BlockSpec((d, LANES), lambda i, j: (0, 0)), pl.BlockSpec((d, LANES), lambda i, j: (0, 0)),
                  pl.BlockSpec((1, LANES), lambda i, j: (0, 0))],
        out_specs=[row, row, top, top],
        out_shape=[jax.ShapeDtypeStruct((b, t, d), F32), jax.ShapeDtypeStruct((b, t, d), BF16),
                   jax.ShapeDtypeStruct((b, t, TOP_K), I32), jax.ShapeDtypeStruct((b, t, TOP_K), F32)],
        compiler_params=_params("arbitrary", "arbitrary"),
        name="outproj_ln_router",
    )(a, w_bf, x, g, sc, sh, lng, lnb, wr_hi, wr_lo, br_pad)


MOE_SCATTER_GROUP = 3


def _moe_kernel(cnt_ref, x_ref, te_ref, tg_ref, w1_ref, b1_ref, w2_ref, b2_ref, *rest, ts, nsub, rb, emit_cast):
    e = pl.program_id(1)
    if emit_cast:
        o_ref, w1o_ref, w2o_ref, pos_ref, tri_ref, sp_ref, sy_ref = rest
        w1o_ref[0] = w1_ref[0].astype(BF16)
        w2o_ref[0] = w2_ref[0].astype(BF16)
        w1_ref, w2_ref = w1o_ref, w2o_ref
    else:
        o_ref, pos_ref, tri_ref, sp_ref, sy_ref = rest

    @pl.when(jnp.logical_and(pl.program_id(0) == 0, e == 0))
    def _():
        si = lax.broadcasted_iota(I32, (ts, ts), 0)
        ti = lax.broadcasted_iota(I32, (ts, ts), 1)
        tri_ref[...] = jnp.where(si <= ti, 1.0, 0.0).astype(BF16)

    @pl.when(e == 0)
    def _():
        o_ref[...] = jnp.zeros_like(o_ref)
        eio = lax.broadcasted_iota(I32, (N_EXPERTS, ts), 0)
        for s in range(nsub):
            sel = jnp.zeros((N_EXPERTS, ts), F32)
            for k in range(TOP_K):
                sel = sel + jnp.where(te_ref[k:k + 1, s * ts:(s + 1) * ts] == eio, 1.0, 0.0)
            rank = jnp.dot(sel.astype(BF16), tri_ref[...], preferred_element_type=F32)
            pos_ref[s] = rank * sel

    slot = pl.multiple_of((e % MOE_SCATTER_GROUP) * rb, rb)
    last = N_EXPERTS - 1

    for s in range(nsub):
        rows = slice(s * ts, (s + 1) * ts)
        pm = pos_ref[s, pl.ds(e, 1), :]
        gate = jnp.zeros((1, ts), F32)
        for k in range(TOP_K):
            gate = gate + jnp.where(te_ref[k:k + 1, rows] == e, tg_ref[k:k + 1, rows], 0.0)
        cnt = cnt_ref[(pl.program_id(0) * nsub + s) * N_EXPERTS + e]
        nblk = (cnt + (rb - 1)) // rb

        def block(blk, pm=pm, gate=gate, rows=rows):
            r = (blk * rb + 1 + lax.broadcasted_iota(I32, (rb, 1), 0)).astype(F32)
            hit = pm == r
            onehot = jnp.where(hit, 1.0, 0.0).astype(BF16)
            xg = jnp.dot(onehot, x_ref[rows, :], preferred_element_type=F32).astype(BF16)
            h = jnp.dot(xg, w1_ref[0], preferred_element_type=F32) + b1_ref[0]
            hg = jnp.minimum(h[:, :D_FF], SWIGLU_LIMIT)
            hl = jnp.clip(h[:, D_FF:], -SWIGLU_LIMIT, SWIGLU_LIMIT)
            a = hg * jax.nn.sigmoid(SWIGLU_ALPHA * hg) * (hl + 1.0)
            y = jnp.dot(a.astype(BF16), w2_ref[0], preferred_element_type=F32) + b2_ref[0]
            grow = jnp.sum(jnp.where(hit, gate, 0.0), axis=-1, keepdims=True)
            return onehot, (y * grow).astype(BF16)

        @pl.when(cnt > 0)
        def _(s=s, block=block):
            onehot, yg = block(0)
            sp_ref[s, pl.ds(slot, rb), :] = onehot
            sy_ref[s, pl.ds(slot, rb), :] = yg

        @pl.when(cnt == 0)
        def _(s=s):
            sp_ref[s, pl.ds(slot, rb), :] = jnp.zeros((rb, ts), BF16)
            sy_ref[s, pl.ds(slot, rb), :] = jnp.zeros((rb, sy_ref.shape[2]), BF16)

        def extra(blk, carry, block=block, rows=rows):
            onehot, yg = block(blk)
            o_ref[rows, :] += lax.dot_general(onehot, yg, TN_DIMS, preferred_element_type=F32)
            return carry

        lax.fori_loop(1, nblk, extra, 0)

        def flush(nrows, s=s, rows=rows):
            o_ref[rows, :] += lax.dot_general(sp_ref[s, :nrows, :], sy_ref[s, :nrows, :], TN_DIMS,
                                              preferred_element_type=F32)

        if last % MOE_SCATTER_GROUP != MOE_SCATTER_GROUP - 1:
            @pl.when(e == last)
            def _(flush=flush):
                flush((last % MOE_SCATTER_GROUP + 1) * rb)

        @pl.when(e % MOE_SCATTER_GROUP == MOE_SCATTER_GROUP - 1)
        def _(flush=flush):
            flush(MOE_SCATTER_GROUP * rb)


def _moe(u2, te, tg, w1, b1, w2, b2, layer, emit_cast=False):
    n, d = u2.shape
    ts = _row_tile(n, MOE_TOKEN_TILE)
    nsub = MOE_SUBTILES if n % (MOE_SUBTILES * ts) == 0 else 1
    tm = ts * nsub
    assert not emit_cast or n == tm
    rb = min(ts, ts * TOP_K // N_EXPERTS + 32)
    te_t = te.T
    tg_t = tg.T
    hits = te.reshape(n // ts, ts * TOP_K)[:, :, None] == jnp.arange(N_EXPERTS, dtype=I32)
    counts = jnp.sum(hits.astype(I32), axis=1).reshape(-1)
    grid_spec = pltpu.PrefetchScalarGridSpec(
        num_scalar_prefetch=1,
        grid=(n // tm, N_EXPERTS),
        in_specs=[
            pl.BlockSpec((tm, d), lambda i, e, c: (i, 0)),
            pl.BlockSpec((TOP_K, tm), lambda i, e, c: (0, i)),
            pl.BlockSpec((TOP_K, tm), lambda i, e, c: (0, i)),
            pl.BlockSpec((None, 1, d, 2 * D_FF), lambda i, e, c: (layer, e, 0, 0)),
            pl.BlockSpec((1, 1, 2 * D_FF), lambda i, e, c: (e, 0, 0)),
            pl.BlockSpec((None, 1, D_FF, d), lambda i, e, c: (layer, e, 0, 0)),
            pl.BlockSpec((1, 1, d), lambda i, e, c: (e, 0, 0)),
        ],
        out_specs=[pl.BlockSpec((tm, d), lambda i, e, c: (i, 0))] + ([
            pl.BlockSpec((1, d, 2 * D_FF), lambda i, e, c: (e, 0, 0)),
            pl.BlockSpec((1, D_FF, d), lambda i, e, c: (e, 0, 0))] if emit_cast else []),
        scratch_shapes=[pltpu.VMEM((nsub, N_EXPERTS, ts), F32), pltpu.VMEM((ts, ts), BF16),
                        pltpu.VMEM((nsub, MOE_SCATTER_GROUP * rb, ts), BF16),
                        pltpu.VMEM((nsub, MOE_SCATTER_GROUP * rb, d), BF16)],
    )
    out_shape = [jax.ShapeDtypeStruct((n, d), F32)] + ([
        jax.ShapeDtypeStruct((N_EXPERTS, d, 2 * D_FF), BF16),
        jax.ShapeDtypeStruct((N_EXPERTS, D_FF, d), BF16)] if emit_cast else [])
    outs = pl.pallas_call(
        functools.partial(_moe_kernel, ts=ts, nsub=nsub, rb=rb, emit_cast=emit_cast),
        grid_spec=grid_spec,
        out_shape=out_shape,
        compiler_params=_params("arbitrary", "arbitrary"),
        name="moe_experts_cast" if emit_cast else "moe_experts",
    )(counts, u2, te_t, tg_t, w1, b1.reshape(N_EXPERTS, 1, -1), w2, b2.reshape(N_EXPERTS, 1, -1))
    return outs if emit_cast else outs[0]


def _rope_tables(pos, head_dim):
    rd = head_dim // ROPE_FRACTION
    half = rd // 2
    inv = jnp.power(ROPE_THETA, -jnp.arange(half, dtype=F32) / half)
    ang = jnp.asarray(pos).astype(F32)[:, None] * inv[None, :]
    cos, sin = jnp.cos(ang), jnp.sin(ang)
    t = pos.shape[0]
    ones = jnp.ones((t, head_dim - rd), F32)
    zeros = jnp.zeros((t, head_dim - rd), F32)
    zh = jnp.zeros((t, half), F32)
    c = jnp.concatenate([cos, cos, ones], axis=-1)
    sa = jnp.concatenate([zh, sin, zeros], axis=-1)
    sb = jnp.concatenate([-sin, zh, zeros], axis=-1)
    rep = LANES // head_dim
    return tuple(jnp.tile(a, (1, rep)) for a in (c, sa, sb))


def _rope(z, c, sa, sb, half):
    w = z.shape[-1]
    rep = w // LANES
    if rep > 1:
        c, sa, sb = (jnp.concatenate([a] * rep, axis=-1) for a in (c, sa, sb))
    return z * c + pltpu.roll(z, half, 1) * sa + pltpu.roll(z, w - half, 1) * sb


def _dsa_inproj_kernel(x_ref, f_ref, g_ref, lng_ref, lnb_ref, sc_ref, sh_ref, w_in_ref,
                       c1_ref, sa1_ref, sb1_ref, c2_ref, sa2_ref, sb2_ref,
                       xo_ref, q_ref, kf_ref, vf_ref, kif_ref, kb_ref, vb_ref, kib_ref, qi_ref, w_ref):
    z = DEEPNORM_ALPHA * x_ref[0] + (1.0 + g_ref[0]) * f_ref[0]
    x = _layer_norm(z, lng_ref[...], lnb_ref[...])
    xo_ref[0] = x
    u = x * (1.0 + sc_ref[0]) + sh_ref[0]
    p = jnp.dot(u.astype(BF16), w_in_ref[...], preferred_element_type=F32)

    c1, sa1, sb1 = c1_ref[...], sa1_ref[...], sb1_ref[...]
    c2, sa2, sb2 = c2_ref[...], sa2_ref[...], sb2_ref[...]
    h1 = DSA_HD // ROPE_FRACTION // 2
    h2 = IDX_DIM // ROPE_FRACTION // 2
    o_k = DSA_Q
    o_v = DSA_Q + DSA_KV
    o_qi = DSA_Q + 2 * DSA_KV
    o_ki = o_qi + IDX_HEADS * IDX_DIM
    q = _rope(p[:, :DSA_Q], c1, sa1, sb1, h1)
    q_ref[0] = (q * DSA_LOG2_SCALE).astype(BF16)
    k = _rope(p[:, o_k:o_v], c1, sa1, sb1, h1)
    kf_ref[0] = k
    kb_ref[0] = k.astype(BF16)
    v = p[:, o_v:o_qi]
    vf_ref[0] = v
    vb_ref[0] = v.astype(BF16)
    qi = _rope(p[:, o_qi:o_ki], c2, sa2, sb2, h2)
    qi_ref[0] = qi.astype(BF16)
    tail = p[:, o_ki:o_ki + LANES]
    ki = _rope(tail, c2, sa2, sb2, h2)[:, :IDX_DIM]
    kif_ref[0] = ki
    kib_ref[0] = ki.astype(BF16)
    wi = tail[:, IDX_DIM:IDX_DIM + IDX_HEADS]
    w_ref[0] = (wi * (IDX_HEADS ** -0.5)) * (IDX_DIM ** -0.5)


def _dsa_inproj(x, f, g, lng, lnb, sc, sh, w_bf, pos):
    b, t, d = x.shape
    n = w_bf.shape[1]
    tm = _row_tile(t, PROJ_ROW_TILE)
    tabs = _rope_tables(pos, DSA_HD) + _rope_tables(pos, IDX_DIM)
    tab = pl.BlockSpec((tm, LANES), lambda i, j: (j, 0))
    vec = pl.BlockSpec((1, 1, d), lambda i, j: (i, 0, 0))
    par = pl.BlockSpec((1, d), lambda i, j: (0, 0))

    def row(w):
        return pl.BlockSpec((1, tm, w), lambda i, j: (i, j, 0))

    widths = [(d, F32), (DSA_Q, BF16), (DSA_KV, F32), (DSA_KV, F32), (IDX_DIM, F32), (DSA_KV, BF16), (DSA_KV, BF16),
              (IDX_DIM, BF16), (IDX_HEADS * IDX_DIM, BF16), (IDX_HEADS, F32)]
    return pl.pallas_call(
        _dsa_inproj_kernel,
        grid=(b, t // tm),
        in_specs=[row(d), row(d), vec, par, par, vec, vec, pl.BlockSpec((d, n), lambda i, j: (0, 0))] + [tab] * 6,
        out_specs=[row(w) for w, _ in widths],
        out_shape=[jax.ShapeDtypeStruct((b, t, w), dt) for w, dt in widths],
        compiler_params=_params("arbitrary", "arbitrary"),
        name="ln_inproj_rope_split",
    )(x, f, g, lng, lnb, sc, sh, w_bf, *tabs)


def _f2key(f):
    bits = lax.bitcast_convert_type(f, I32)
    return bits ^ ((bits >> 31) & 0x7FFFFFFF)


def _key2f(k):
    return lax.bitcast_convert_type(k ^ ((k >> 31) & 0x7FFFFFFF), F32)


def _dsa_kernel(nkt_ref, q_ref, qi_ref, w_ref, k_ref, v_ref, ki_ref, kpos_ref, qlim_ref, nadm_ref, o_ref,
                s_ref, wb_ref, qg_ref, m_ref, l_ref, acc_ref, flag_ref, *, tq, tk, nsel, value_iters):
    nkt = nkt_ref[pl.program_id(1)]
    qlim = qlim_ref[...]
    w = w_ref[0]
    qi = qi_ref[0]
    qis_all = jnp.concatenate([qi[:, h * IDX_DIM:(h + 1) * IDX_DIM] for h in range(IDX_HEADS)], axis=0)
    nlg = tk // LANES
    for h in range(IDX_HEADS):
        wb_ref[h] = jnp.broadcast_to(w[:, h:h + 1], (tq, LANES))

    def p1(kt, carry):
        rmax, rmin, cgt0, cge0 = carry
        off = pl.multiple_of(kt * tk, tk)
        kit = ki_ref[0, pl.ds(off, tk), :]
        acc = jnp.zeros((tq, tk), F32)
        d_all = lax.dot_general(qis_all, kit, NT_DIMS, preferred_element_type=F32)
        for h in range(IDX_HEADS):
            d = d_all[h * tq:(h + 1) * tq]
            acc = acc + jnp.maximum(d, 0.0) * jnp.concatenate([wb_ref[h]] * nlg, axis=1)
        s = jnp.where(kpos_ref[kt] < qlim, acc, -jnp.inf)
        s_ref[kt] = s
        gt0 = jnp.where(s > 0.0, 1.0, 0.0)
        ge0 = jnp.where(s >= 0.0, 1.0, 0.0)
        for g in range(nlg):
            lanes = slice(g * LANES, (g + 1) * LANES)
            rmax = jnp.maximum(rmax, s[:, lanes])
            rmin = jnp.minimum(rmin, acc[:, lanes])
            cgt0 = cgt0 + gt0[:, lanes]
            cge0 = cge0 + ge0[:, lanes]
        return rmax, rmin, cgt0, cge0

    zcol = jnp.zeros((tq, 1), F32)
    zlan = jnp.zeros((tq, LANES), F32)
    init = (jnp.full((tq, LANES), -jnp.inf, F32), jnp.full((tq, LANES), jnp.inf, F32), zlan, zlan)
    rmax, rmin, cgt0, cge0 = lax.fori_loop(0, nkt, p1, init)
    rmax = jnp.max(rmax, axis=-1, keepdims=True)
    rmin = jnp.min(rmin, axis=-1, keepdims=True)
    cgt0 = jnp.sum(cgt0, axis=-1, keepdims=True)
    cge0 = jnp.sum(cge0, axis=-1, keepdims=True)
    nadm = nadm_ref[...]

    rg = min(tq, 32)
    fsel = float(nsel)
    unbounded = float(2 ** 30)

    def count_ge(tb):
        def body(kt, c):
            out = []
            for r in range(tq // rg):
                tr = jnp.concatenate([tb[r * rg:(r + 1) * rg, :]] * nlg, axis=1)
                hit = jnp.where(s_ref[kt, r * rg:(r + 1) * rg, :] >= tr, 1.0, 0.0)
                cr = c[r * rg:(r + 1) * rg]
                for g in range(nlg):
                    cr = cr + hit[:, g * LANES:(g + 1) * LANES]
                out.append(cr)
            return jnp.concatenate(out, axis=0)
        c = lax.fori_loop(0, nkt, body, zlan)
        return jnp.broadcast_to(jnp.sum(c, axis=-1, keepdims=True), (tq, LANES))

    rmax, rmin, cgt0, cge0, nadm = (jnp.broadcast_to(a, (tq, LANES)) for a in (rmax, rmin, cgt0, cge0, nadm))
    few = nadm <= fsel
    many = jnp.logical_not(few)
    pos = jnp.logical_and(many, cgt0 > fsel)
    neg = jnp.logical_and(many, cge0 < fsel)
    zero = jnp.logical_and(many, jnp.logical_and(cgt0 <= fsel, cge0 >= fsel))
    lo0 = jnp.where(pos, 0, _f2key(jnp.where(neg, rmin, 0.0)))
    hi0 = jnp.where(pos, _f2key(jnp.where(pos, rmax, 0.0)) + 1, 0)
    clo0 = jnp.where(pos, cge0, nadm)
    chi0 = jnp.where(pos, 0.0, cge0)
    act0 = jnp.where(jnp.logical_or(pos, neg), 1.0, 0.0)
    t0 = jnp.where(few, -jnp.inf, 0.0)
    need0 = jnp.where(zero, fsel - cgt0, unbounded)

    def cond(st):
        return st[0] > 0.0

    def step(st):
        _, it, lo, hi, clo, chi, act, thr, need = st
        conv = jnp.logical_and(act > 0.0, lo + 1 >= hi)
        thr = jnp.where(conv, _key2f(lo), thr)
        need = jnp.where(conv, fsel - chi, need)
        act = jnp.where(conv, 0.0, act)
        flo, fhi = _key2f(lo), _key2f(hi)
        mid_i = (lo >> 1) + (hi >> 1) + (lo & hi & 1)
        lclo = jnp.log(clo)
        interp = (lclo - float(np.log(nsel - 0.5))) / (lclo - jnp.log(jnp.maximum(chi, 0.5)))
        frac = jnp.where(it % 3 == 2, 0.5, interp)
        mid_v = _f2key(flo + frac * (fhi - flo))
        use_v = jnp.logical_and(it < value_iters, jnp.logical_and(mid_v > lo, mid_v < hi))
        mid = jnp.where(use_v, mid_v, mid_i)
        tm = _key2f(mid)
        c = count_ge(jnp.where(act > 0.0, tm, jnp.inf))
        hit = jnp.logical_and(act > 0.0, c == fsel)
        thr = jnp.where(hit, tm, thr)
        up = jnp.logical_and(act > 0.0, c > fsel)
        dn = jnp.logical_and(act > 0.0, c < fsel)
        lo = jnp.where(up, mid, lo)
        clo = jnp.where(up, c, clo)
        hi = jnp.where(dn, mid, hi)
        chi = jnp.where(dn, c, chi)
        act = jnp.where(hit, 0.0, act)
        return jnp.max(act), it + 1, lo, hi, clo, chi, act, thr, need

    st = (jnp.max(act0), jnp.int32(0), lo0, hi0, clo0, chi0, act0, t0, need0)
    _, _, _, _, _, _, _, thr, need = lax.while_loop(cond, step, st)
    thr, need = thr[:, 0:1], need[:, 0:1]
    tied = need < unbounded * 0.5
    has_ties = jnp.max(jnp.where(tied, 1.0, 0.0)) > 0.0
    thr = jnp.maximum(thr, -F32_MAX)

    for g in range(DSA_KV_HEADS):
        for hh in range(DSA_GROUP):
            hd = g * DSA_GROUP + hh
            qg_ref[g, hh * tq:(hh + 1) * tq, :] = q_ref[0, :, hd * DSA_HD:(hd + 1) * DSA_HD]
    m_ref[...] = jnp.full(m_ref.shape, NEG_BIG, F32)
    l_ref[...] = jnp.zeros(l_ref.shape, F32)
    acc_ref[...] = jnp.zeros(acc_ref.shape, F32)

    ones_cols = jnp.ones((tk, DSA_HD), BF16)

    def attend(kt, bias):
        off = pl.multiple_of(kt * tk, tk)
        qk = [lax.dot_general(qg_ref[g], k_ref[0, pl.ds(off, tk), g * DSA_HD:(g + 1) * DSA_HD], NT_DIMS,
                              preferred_element_type=F32) for g in range(DSA_KV_HEADS)]
        for g in range(DSA_KV_HEADS):
            vt_g = jnp.concatenate([v_ref[0, pl.ds(off, tk), g * DSA_HD:(g + 1) * DSA_HD], ones_cols], axis=1)
            for hh in range(DSA_GROUP):
                rows = slice(hh * tq, (hh + 1) * tq)
                lg = qk[g][rows] + bias
                m_old = m_ref[g, rows, :]
                m_new = jnp.maximum(m_old, jnp.max(lg, axis=-1, keepdims=True))
                alpha = jnp.exp2(m_old - m_new)
                p = jnp.exp2(lg - jnp.concatenate([m_new] * nlg, axis=1))
                pv = jnp.dot(p.astype(BF16), vt_g, preferred_element_type=F32)
                acc_ref[g, rows, :] = alpha * acc_ref[g, rows, :] + pv[:, :DSA_HD]
                l_ref[g, rows, :] = alpha * l_ref[g, rows, :] + pv[:, DSA_HD:]
                m_ref[g, rows, :] = m_new

    @pl.when(jnp.logical_not(has_ties))
    def _():
        def body(kt, carry):
            attend(kt, jnp.where(s_ref[kt] >= thr, 0.0, NEG_BIG))
            return carry
        lax.fori_loop(0, nkt, body, 0)

    @pl.when(has_ties)
    def _():
        def scan(kt, carry):
            seen, cut, nin = carry
            eqc = jnp.sum(jnp.where(s_ref[kt] == thr, 1.0, 0.0), axis=-1, keepdims=True)
            here = jnp.logical_and(tied, jnp.logical_and(seen < need, seen + eqc >= need))
            cut = jnp.where(here, kt, cut)
            nin = jnp.where(here, need - seen, nin)
            flag_ref[kt] = jnp.max(jnp.where(here, 1.0, 0.0)).astype(I32)
            return seen + eqc, cut, nin

        cut0 = jnp.where(tied, -1, nkt).astype(I32)
        _, cut, nin = lax.fori_loop(0, nkt, scan, (zcol, cut0, zcol))

        def body(kt, carry):
            s = s_ref[kt]
            before = jnp.where(kt < cut, 0.0, NEG_BIG)

            def plain():
                return jnp.where(s == thr, before, NEG_BIG)

            def ranked():
                si = lax.broadcasted_iota(I32, (tk, tk), 0)
                ti = lax.broadcasted_iota(I32, (tk, tk), 1)
                triu = jnp.where(si <= ti, 1.0, 0.0).astype(BF16)
                eq = jnp.where(s == thr, 1.0, 0.0).astype(BF16)
                rank = jnp.dot(eq, triu, preferred_element_type=F32)
                inside = jnp.where(rank <= nin, 0.0, NEG_BIG)
                return jnp.where(s == thr, jnp.where(kt == cut, inside, before), NEG_BIG)

            tie_bias = lax.cond(flag_ref[kt] > 0, ranked, plain)
            attend(kt, jnp.where(s > thr, 0.0, tie_bias))
            return carry
        lax.fori_loop(0, nkt, body, 0)

    for g in range(DSA_KV_HEADS):
        og = acc_ref[g] / l_ref[g]
        for hh in range(DSA_GROUP):
            hd = g * DSA_GROUP + hh
            o_ref[0, :, hd * DSA_HD:(hd + 1) * DSA_HD] = og[hh * tq:(hh + 1) * tq, :].astype(BF16)


def _dsa_attend(q, qi, w, k, v, ki, kpos, qlim, tq, tk):
    b, t, _ = q.shape
    lp = k.shape[1]
    nq = t // tq
    nsel = min(DSA_TOPK, int(np.sum(kpos < np.iinfo(np.int32).max)) // 4)
    qmax = qlim.reshape(nq, tq).max(axis=1)
    nadm = np.searchsorted(kpos, qlim, side="left")
    nkt = np.minimum(-(-np.searchsorted(kpos, qmax, side="left") // tk), lp // tk).astype(np.int32)
    nkt = np.maximum(nkt, 1)
    single = pl.Buffered(1)

    def full(wd):
        return pl.BlockSpec((1, lp, wd), lambda i, j, n: (i, 0, 0), pipeline_mode=single)

    def row(wd):
        return pl.BlockSpec((1, tq, wd), lambda i, j, n: (i, j, 0))

    grid_spec = pltpu.PrefetchScalarGridSpec(
        num_scalar_prefetch=1,
        grid=(b, nq),
        in_specs=[row(DSA_Q), row(IDX_HEADS * IDX_DIM), row(IDX_HEADS), full(DSA_KV), full(DSA_KV), full(IDX_DIM),
                  pl.BlockSpec((lp // tk, 1, tk), lambda i, j, n: (0, 0, 0)),
                  pl.BlockSpec((tq, 1), lambda i, j, n: (j, 0)),
                  pl.BlockSpec((tq, 1), lambda i, j, n: (j, 0))],
        out_specs=row(DSA_Q),
        scratch_shapes=[
            pltpu.VMEM((lp // tk, tq, tk), F32),
            pltpu.VMEM((IDX_HEADS, tq, LANES), F32),
            pltpu.VMEM((DSA_KV_HEADS, DSA_GROUP * tq, DSA_HD), BF16),
            pltpu.VMEM((DSA_KV_HEADS, DSA_GROUP * tq, LANES), F32),
            pltpu.VMEM((DSA_KV_HEADS, DSA_GROUP * tq, LANES), F32),
            pltpu.VMEM((DSA_KV_HEADS, DSA_GROUP * tq, DSA_HD), F32),
            pltpu.SMEM((lp // tk,), I32),
        ],
    )
    return pl.pallas_call(
        functools.partial(_dsa_kernel, tq=tq, tk=tk, nsel=nsel, value_iters=DSA_VALUE_ITERS),
        grid_spec=grid_spec,
        out_shape=jax.ShapeDtypeStruct((b, t, DSA_Q), BF16),
        compiler_params=_params("arbitrary", "arbitrary"),
        name="dsa_select_attend",
    )(jnp.asarray(nkt), q, qi, w, k, v, ki, jnp.asarray(kpos, I32).reshape(lp // tk, 1, tk),
      jnp.asarray(qlim, I32).reshape(t, 1), jnp.asarray(nadm, F32).reshape(t, 1))


def _trunk(x, mods, pos, ml_state, ds_cache, prm, moe_bf=None):
    b, t, d = x.shape
    casts = []

    def moe(i, u2, te, tg):
        args = (u2.reshape(b * t, d), te.reshape(b * t, TOP_K), tg.reshape(b * t, TOP_K))
        if moe_bf is None:
            f, w1b, w2b = _moe(*args, prm["moe_w1"], prm["moe_b1"][i], prm["moe_w2"], prm["moe_b2"][i], i,
                               emit_cast=True)
            casts.append((w1b, w2b))
        else:
            w1b, w2b = moe_bf[i]
            f = _moe(*args, w1b[None], prm["moe_b1"][i], w2b[None], prm["moe_b2"][i], 0)
        return f.reshape(b, t, d)

    chunk = next((c for c in (ML_EVAL_CHUNK, CHUNK) if t % c == 0), t)

    def mod(i, k):
        return mods[i][:, None, k * d:(k + 1) * d]

    p, pg = _ml_inproj(x, mod(0, 1), mod(0, 0), prm["ml_w_in"])
    if ml_state is None:
        c0 = jnp.zeros((b, ML_HEADS, ML_DV, ML_DK), F32)
        n0 = jnp.zeros((b, ML_HEADS, ML_DK), F32)
        m0 = jnp.zeros((b, ML_HEADS), F32)
    else:
        c0, n0, m0 = (s[0] for s in ml_state)
    hg, c_new, n_new, m_new = _mlstm(p, pg, prm["ml_b_gates"], prm["ml_norm_g"], c0, n0, m0, chunk)
    x1, u2, te, tg = _outproj(hg, prm["ml_w_out"], x, mod(0, 2), mod(0, 4), mod(0, 3),
                              prm["ln_g"][0, 0:1], prm["ln_b"][0, 0:1], prm["w_router"][0], prm["b_router"][0])
    f = moe(0, u2, te, tg)

    x2, q, kf, vf, kif, kb, vb, kib, qi, w = _dsa_inproj(
        x1, f, mod(0, 5), prm["ln_g"][0, 1:2], prm["ln_b"][0, 1:2], mod(1, 1), mod(1, 0), prm["ds_w_in"], pos)
    pos_np = np.asarray(pos)
    qlim = (pos_np // CHUNK + 1) * CHUNK
    if ds_cache is None:
        kpos = pos_np
        tq, tk = DSA_Q_TILE, DSA_K_TILE
        if t % tk:
            tq, tk = t, t
        k_all, v_all, ki_all = kb, vb, kib
    else:
        ck, cv, cki = ds_cache
        past = ck.shape[1]
        total = past + t
        lp = -(-total // (3 * LANES)) * (3 * LANES)
        tq, tk = t, 3 * LANES
        kpos = np.concatenate([np.arange(past), pos_np, np.full((lp - total,), np.iinfo(np.int32).max)])

        def cat(cache, new):
            full = jnp.concatenate([cache.reshape(b, past, -1).astype(BF16), new], axis=1)
            return jnp.pad(full, ((0, 0), (0, lp - total), (0, 0)))

        k_all, v_all, ki_all = cat(ck, kb), cat(cv, vb), cat(cki, kib)
    o = _dsa_attend(q, qi, w, k_all, v_all, ki_all, kpos.astype(np.int64), qlim, tq, tk)
    x3, u2, te, tg = _outproj(o, prm["ds_w_out"], x2, mod(1, 2), mod(1, 4), mod(1, 3),
                              prm["ln_g"][1, 0:1], prm["ln_b"][1, 0:1], prm["w_router"][1], prm["b_router"][1])
    f = moe(1, u2, te, tg)
    y = _final_ln(x3, f, mod(1, 5), prm["ln_g"][1, 1:2], prm["ln_b"][1, 1:2])

    ml_out = (c_new[None], n_new[None], m_new[None])
    ds_out = (kf.reshape(1, b, t, DSA_KV_HEADS, DSA_HD), vf.reshape(1, b, t, DSA_KV_HEADS, DSA_HD), kif[None])
    return y, ml_out, ds_out, casts


def _prepare(w_ada, b_ada, ln_g, ln_b, ml_w_in, ml_b_gates, ml_norm_g, ml_w_out, ds_w_in, ds_w_out,
             moe_w_router, moe_b_router, moe_w1, moe_b1, moe_w2, moe_b2):
    return {
        "ln_g": ln_g, "ln_b": ln_b,
        "ml_w_in": jnp.pad(ml_w_in[0], ((0, 0), (0, ML_IN_PAD - ML_IN))).astype(BF16),
        "ml_b_gates": ml_b_gates[0], "ml_norm_g": ml_norm_g[0],
        "ml_w_out": ml_w_out[0].astype(BF16),
        "ds_w_in": jnp.pad(ds_w_in[0], ((0, 0), (0, DSA_IN_PAD - DSA_IN))).astype(BF16),
        "ds_w_out": ds_w_out[0].astype(BF16),
        "w_router": jnp.pad(moe_w_router, ((0, 0), (0, 0), (0, LANES - N_EXPERTS))),
        "b_router": jnp.pad(moe_b_router, ((0, 0), (0, LANES - N_EXPERTS)), constant_values=NEG_BIG)[:, None, :],
        "moe_w1": moe_w1, "moe_b1": moe_b1, "moe_w2": moe_w2, "moe_b2": moe_b2,
    }


def kernel(x_prompt, x_sample, state_mlstm_C, state_mlstm_n, state_mlstm_m, cache_dsa_k, cache_dsa_v,
           cache_dsa_kidx, c_prompt, c_sample, w_ada, b_ada, ln_g, ln_b, ml_w_in, ml_b_gates, ml_norm_g,
           ml_w_out, ds_w_in, ds_w_out, moe_w_router, moe_b_router, moe_w1, moe_b1, moe_w2, moe_b2):
    prm = _prepare(w_ada, b_ada, ln_g, ln_b, ml_w_in, ml_b_gates, ml_norm_g, ml_w_out, ds_w_in, ds_w_out,
                   moe_w_router, moe_b_router, moe_w1, moe_b1, moe_w2, moe_b2)
    bp, bs = c_prompt.shape[0], c_sample.shape[0]
    rows = -(-(bp + bs) // 8) * 8
    c_rows = jnp.pad(jnp.concatenate([c_prompt, c_sample], axis=0), ((0, rows - bp - bs), (0, 0)))
    mods = _ada(c_rows, w_ada, b_ada)
    mods_p = [mods[i, :bp] for i in range(DEPTH)]
    mods_s = [mods[i, bp:bp + bs] for i in range(DEPTH)]

    past = cache_dsa_k.shape[2]
    pos_s = past + np.arange(x_sample.shape[1], dtype=np.int32)
    y_s, ml_s, ds_s, moe_bf = _trunk(x_sample, mods_s, pos_s, (state_mlstm_C, state_mlstm_n, state_mlstm_m),
                                     (cache_dsa_k[0], cache_dsa_v[0], cache_dsa_kidx[0]), prm)
    pos_p = np.arange(x_prompt.shape[1], dtype=np.int32)
    y_p, ml_p, ds_p, _ = _trunk(x_prompt, mods_p, pos_p, None, None, prm, moe_bf=moe_bf)
    return (y_p, y_s, ml_p[0], ml_p[1], ml_p[2], ds_p[0], ds_p[1], ds_p[2],
            ml_s[0], ml_s[1], ml_s[2], ds_s[0], ds_s[1], ds_s[2])
```

```python
import functools

import numpy as np
import jax
import jax.numpy as jnp
from jax import lax
from jax.experimental import pallas as pl
from jax.experimental.pallas import tpu as pltpu

F32 = jnp.float32
BF16 = jnp.bfloat16
I32 = jnp.int32

D_MODEL = 1024
DEPTH = 2
CHUNK = 64
ML_HEADS = 4
ML_DV = D_MODEL // ML_HEADS
ML_DK = ML_DV // 2
ML_QK = ML_HEADS * ML_DK
ML_V = ML_HEADS * ML_DV
ML_IN = 2 * ML_QK + 2 * ML_V + 2 * ML_HEADS
ML_GATE_OFF = 2 * ML_QK + 2 * ML_V
ML_EVAL_CHUNK = 256
DSA_HEADS = 8
DSA_KV_HEADS = 2
DSA_HD = D_MODEL // DSA_HEADS
DSA_GROUP = DSA_HEADS // DSA_KV_HEADS
IDX_HEADS = 8
IDX_DIM = 64
DSA_TOPK = 256
DSA_Q = DSA_HEADS * DSA_HD
DSA_KV = DSA_KV_HEADS * DSA_HD
DSA_IN = DSA_Q + 2 * DSA_KV + IDX_HEADS * IDX_DIM + IDX_DIM + IDX_HEADS
DSA_LOG2_SCALE = (DSA_HD ** -0.5) * float(np.log2(np.e))
ROPE_THETA = 500000.0
ROPE_FRACTION = 4
N_EXPERTS = 32
TOP_K = 4
D_FF = D_MODEL
SWIGLU_LIMIT = 7.0
SWIGLU_ALPHA = 1.702
DEEPNORM_ALPHA = (2 * DEPTH) ** 0.25
LN_EPS = 1e-5

LANES = 128
V7X_VMEM_BYTES = 64 * 1024 * 1024
VMEM_LIMIT = V7X_VMEM_BYTES - 8 * 1024 * 1024

PROJ_ROW_TILE = 512
OUTPROJ_ROW_TILE = 256
FINAL_LN_ROW_TILE = 1024
ADA_COL_TILE = 1536
MOE_TOKEN_TILE = 1024
MOE_SUBTILES = 2
DSA_Q_TILE = 128
DSA_K_TILE = 1024
DSA_VALUE_ITERS = 16

ML_IN_PAD = -(-ML_IN // LANES) * LANES
DSA_IN_PAD = -(-DSA_IN // LANES) * LANES
NEG_BIG = -1e30
F32_MAX = float(np.finfo(np.float32).max)
HIGHEST = lax.Precision.HIGHEST
NT_DIMS = (((1,), (1,)), ((), ()))
TN_DIMS = (((0,), (0,)), ((), ()))


def _params(*sem):
    return pltpu.CompilerParams(dimension_semantics=sem, vmem_limit_bytes=VMEM_LIMIT)


def _row_tile(t, pref):
    return pref if t % pref == 0 else t


def _ada_kernel(c_ref, w_ref, b_ref, o_ref):
    c = c_ref[...]
    cond = (c * jax.nn.sigmoid(c)).astype(BF16)
    o_ref[0] = jnp.dot(cond, w_ref[0].astype(BF16), preferred_element_type=F32) + b_ref[0]


def _ada(c_rows, w_ada, b_ada):
    rows = c_rows.shape[0]
    n = w_ada.shape[-1]
    tn = ADA_COL_TILE
    return pl.pallas_call(
        _ada_kernel,
        grid=(DEPTH, n // tn),
        in_specs=[
            pl.BlockSpec((rows, D_MODEL), lambda i, j: (0, 0)),
            pl.BlockSpec((1, D_MODEL, tn), lambda i, j: (i, 0, j)),
            pl.BlockSpec((1, 1, tn), lambda i, j: (i, 0, j)),
        ],
        out_specs=pl.BlockSpec((1, rows, tn), lambda i, j: (i, 0, j)),
        out_shape=jax.ShapeDtypeStruct((DEPTH, rows, n), F32),
        compiler_params=_params("arbitrary", "arbitrary"),
        name="ada_mod",
    )(c_rows, w_ada, b_ada.reshape(DEPTH, 1, n))


def _layer_norm(z, g, b):
    mu = jnp.mean(z, axis=-1, keepdims=True)
    zc = z - mu
    var = jnp.mean(zc * zc, axis=-1, keepdims=True)
    return zc * lax.rsqrt(var + LN_EPS) * g + b


def _ml_inproj_kernel(x_ref, sc_ref, sh_ref, w_ref, pm_ref, pg_ref):
    u = x_ref[0] * (1.0 + sc_ref[0]) + sh_ref[0]
    p = jnp.dot(u.astype(BF16), w_ref[...], preferred_element_type=F32)
    pm_ref[0, :, :ML_QK] = (p[:, :ML_QK] * (ML_DK ** -0.5)).astype(BF16)
    pm_ref[0, :, ML_QK:] = p[:, ML_QK:ML_GATE_OFF].astype(BF16)
    pg_ref[0] = p[:, ML_GATE_OFF:]


def _ml_inproj(x, sc, sh, w_bf):
    b, t, d = x.shape
    n = w_bf.shape[1]
    tm = _row_tile(t, PROJ_ROW_TILE)
    row = pl.BlockSpec((1, tm, d), lambda i, j: (i, j, 0))
    vec = pl.BlockSpec((1, 1, d), lambda i, j: (i, 0, 0))
    return pl.pallas_call(
        _ml_inproj_kernel,
        grid=(b, t // tm),
        in_specs=[row, vec, vec, pl.BlockSpec((d, n), lambda i, j: (0, 0))],
        out_specs=[pl.BlockSpec((1, tm, ML_GATE_OFF), lambda i, j: (i, j, 0)),
                   pl.BlockSpec((1, tm, n - ML_GATE_OFF), lambda i, j: (i, j, 0))],
        out_shape=[jax.ShapeDtypeStruct((b, t, ML_GATE_OFF), BF16),
                   jax.ShapeDtypeStruct((b, t, n - ML_GATE_OFF), F32)],
        compiler_params=_params("arbitrary", "arbitrary"),
        name="ml_inproj",
    )(x, sc, sh, w_bf)


def _final_ln_kernel(x_ref, f_ref, g_ref, lng_ref, lnb_ref, o_ref):
    z = DEEPNORM_ALPHA * x_ref[0] + (1.0 + g_ref[0]) * f_ref[0]
    o_ref[0] = _layer_norm(z, lng_ref[...], lnb_ref[...])


def _final_ln(x, f, g, lng, lnb):
    b, t, d = x.shape
    tm = _row_tile(t, FINAL_LN_ROW_TILE)
    row = pl.BlockSpec((1, tm, d), lambda i, j: (i, j, 0))
    vec = pl.BlockSpec((1, 1, d), lambda i, j: (i, 0, 0))
    par = pl.BlockSpec((1, d), lambda i, j: (0, 0))
    return pl.pallas_call(
        _final_ln_kernel,
        grid=(b, t // tm),
        in_specs=[row, row, vec, par, par],
        out_specs=row,
        out_shape=jax.ShapeDtypeStruct((b, t, d), F32),
        compiler_params=_params("arbitrary", "arbitrary"),
        name="final_ln",
    )(x, f, g, lng, lnb)


def _log_sigmoid(x):
    return jnp.minimum(x, 0.0) - jnp.log1p(jnp.exp(-jnp.abs(x)))


def _mlstm_kernel(p_ref, pg_ref, gt_ref, bcol_ref, brow_ref, ng_ref, c0_ref, n0_ref, m0_ref,
                  hg_ref, c_ref, n_ref, m_ref, *, chunk):
    L = chunk

    @pl.when(pl.program_id(1) == 0)
    def _():
        c_ref[...] = c0_ref[...]
        n_ref[...] = n0_ref[...]
        m_ref[...] = m0_ref[...]

    gcol = pg_ref[0, :, :2 * ML_HEADS] + bcol_ref[...]
    grow = gt_ref[0] + brow_ref[...]
    lf_col = _log_sigmoid(gcol)
    lf_row = _log_sigmoid(grow)
    ti = lax.broadcasted_iota(I32, (L, L), 0)
    si = lax.broadcasted_iota(I32, (L, L), 1)
    causal = si <= ti
    tril = jnp.where(causal, 1.0, 0.0).astype(F32)
    triu = jnp.where(ti <= si, 1.0, 0.0).astype(F32)
    b_col = jnp.dot(tril, lf_col, precision=HIGHEST, preferred_element_type=F32)
    b_row = jnp.dot(lf_row, triu, precision=HIGHEST, preferred_element_type=F32)

    for h in range(ML_HEADS):
        qs = p_ref[0, :, h * ML_DK:(h + 1) * ML_DK]
        kb = p_ref[0, :, ML_QK + h * ML_DK:ML_QK + (h + 1) * ML_DK]
        vb = p_ref[0, :, 2 * ML_QK + h * ML_DV:2 * ML_QK + (h + 1) * ML_DV]
        v = vb.astype(F32)
        o = p_ref[0, :, 2 * ML_QK + ML_V + h * ML_DV:2 * ML_QK + ML_V + (h + 1) * ML_DV].astype(F32)
        bc = b_col[:, ML_HEADS + h:ML_HEADS + h + 1]
        ic = gcol[:, h:h + 1]
        br = b_row[ML_HEADS + h:ML_HEADS + h + 1, :]
        ir = grow[h:h + 1, :]
        c_old = c_ref[0, h]
        n_old = n_ref[0, h:h + 1, :]
        m_old = m_ref[0, h:h + 1, 0:1]

        dm = jnp.where(causal, bc - br + ir, -jnp.inf)
        inter = bc + m_old
        mt = jnp.maximum(inter, jnp.max(dm, axis=-1, keepdims=True))
        qk = lax.dot_general(qs, kb, NT_DIMS, preferred_element_type=F32)
        s = jnp.exp(dm - mt) * qk
        wp = jnp.exp(inter - mt)
        qc = lax.dot_general(qs, c_old.astype(BF16), NT_DIMS, preferred_element_type=F32)
        num = jnp.dot(s.astype(BF16), vb, preferred_element_type=F32) + wp * qc
        qn = jnp.sum(qs.astype(F32) * n_old, axis=-1, keepdims=True)
        den = jnp.sum(s, axis=-1, keepdims=True) + wp * qn
        hh = num / jnp.maximum(jnp.abs(den), jnp.exp(-mt))

        m_new = mt[L - 1:L, :]
        wk = jnp.exp(bc[L - 1:L, :] - bc + ic - m_new)
        wprev = jnp.exp(inter[L - 1:L, :] - m_new)
        vw = (v * wk).astype(BF16)
        c_ref[0, h] = wprev * c_old + lax.dot_general(vw, kb, TN_DIMS, preferred_element_type=F32)
        n_ref[0, h:h + 1, :] = wprev * n_old + jnp.sum(wk * kb.astype(F32), axis=0, keepdims=True)
        m_ref[0, h:h + 1, :] = jnp.broadcast_to(m_new, (1, ML_DK))

        mu = jnp.mean(hh, axis=-1, keepdims=True)
        hc = hh - mu
        var = jnp.mean(hc * hc, axis=-1, keepdims=True)
        hn = hc * lax.rsqrt(var + LN_EPS) * ng_ref[:, h * ML_DV:(h + 1) * ML_DV]
        hg_ref[0, :, h * ML_DV:(h + 1) * ML_DV] = (jax.nn.sigmoid(o) * hn).astype(BF16)


def _mlstm(p, pg, b_gates, norm_g, c0, n0, m0, chunk):
    b, t, n = p.shape
    nc = t // chunk
    gt = jnp.swapaxes(pg[:, :, :2 * ML_HEADS], 1, 2)
    if nc > 1:
        gt_spec = pl.BlockSpec((1, 2 * ML_HEADS, chunk), lambda i, j: (i, 0, j))
    else:
        gt_spec = pl.BlockSpec((1, 2 * ML_HEADS, t), lambda i, j: (i, 0, 0))
    if nc > 1 and chunk % LANES != 0:
        gt = gt.reshape(b, 2 * ML_HEADS, nc, chunk).transpose(0, 2, 1, 3).reshape(b * nc, 2 * ML_HEADS, chunk)
        gt_spec = pl.BlockSpec((1, 2 * ML_HEADS, chunk), lambda i, j: (i * nc + j, 0, 0))
    m0b = jnp.broadcast_to(m0[..., None], (b, ML_HEADS, ML_DK))
    cspec = pl.BlockSpec((1, ML_HEADS, ML_DV, ML_DK), lambda i, j: (i, 0, 0, 0))
    nspec = pl.BlockSpec((1, ML_HEADS, ML_DK), lambda i, j: (i, 0, 0))
    hg, c, nn, m = pl.pallas_call(
        functools.partial(_mlstm_kernel, chunk=chunk),
        grid=(b, nc),
        in_specs=[
            pl.BlockSpec((1, chunk, n), lambda i, j: (i, j, 0)),
            pl.BlockSpec((1, chunk, pg.shape[-1]), lambda i, j: (i, j, 0)),
            gt_spec,
            pl.BlockSpec((1, 2 * ML_HEADS), lambda i, j: (0, 0)),
            pl.BlockSpec((2 * ML_HEADS, 1), lambda i, j: (0, 0)),
            pl.BlockSpec((1, ML_V), lambda i, j: (0, 0)),
            cspec, nspec, nspec,
        ],
        out_specs=[pl.BlockSpec((1, chunk, ML_V), lambda i, j: (i, j, 0)), cspec, nspec, nspec],
        out_shape=[
            jax.ShapeDtypeStruct((b, t, ML_V), BF16),
            jax.ShapeDtypeStruct((b, ML_HEADS, ML_DV, ML_DK), F32),
            jax.ShapeDtypeStruct((b, ML_HEADS, ML_DK), F32),
            jax.ShapeDtypeStruct((b, ML_HEADS, ML_DK), F32),
        ],
        compiler_params=_params("arbitrary", "arbitrary"),
        name="mlstm_scan",
    )(p, pg, gt, b_gates.reshape(1, -1), b_gates.reshape(-1, 1), norm_g.reshape(1, -1), c0, n0, m0b)
    return hg, c, nn, m[..., 0]


def _outproj_kernel(a_ref, w_ref, x_ref, g_ref, sc_ref, sh_ref, lng_ref, lnb_ref, wrh_ref, wrl_ref, br_ref,
                    x1_ref, u2_ref, te_ref, tg_ref):
    y = jnp.dot(a_ref[0], w_ref[...], preferred_element_type=F32)
    z = DEEPNORM_ALPHA * x_ref[0] + (1.0 + g_ref[0]) * y
    x1 = _layer_norm(z, lng_ref[...], lnb_ref[...])
    x1_ref[0] = x1
    u2 = x1 * (1.0 + sc_ref[0]) + sh_ref[0]
    u_hi = u2.astype(BF16)
    u2_ref[0] = u_hi
    u_lo = (u2 - u_hi.astype(F32)).astype(BF16)
    logits = (jnp.dot(u_hi, wrh_ref[...], preferred_element_type=F32)
              + jnp.dot(u_lo, wrh_ref[...], preferred_element_type=F32)
              + jnp.dot(u_hi, wrl_ref[...], preferred_element_type=F32)) + br_ref[...]
    lane = lax.broadcasted_iota(I32, logits.shape, 1).astype(F32)
    vals, idxs = [], []
    cur = logits
    for _ in range(TOP_K):
        mx = jnp.max(cur, axis=-1, keepdims=True)
        idx = jnp.min(jnp.where(cur == mx, lane, float(LANES)), axis=-1, keepdims=True)
        vals.append(mx)
        idxs.append(idx)
        cur = jnp.where(lane == idx, -jnp.inf, cur)
    es = [jnp.exp(v - vals[0]) for v in vals]
    tot = es[0] + es[1] + es[2] + es[3]
    for k in range(TOP_K):
        te_ref[0, :, k:k + 1] = idxs[k].astype(I32)
        tg_ref[0, :, k:k + 1] = es[k] / tot


def _outproj(a, w_bf, x, g, sc, sh, lng, lnb, wr_pad, br_pad):
    b, t, d = x.shape
    tm = _row_tile(t, OUTPROJ_ROW_TILE)
    wr_hi = wr_pad.astype(BF16)
    wr_lo = (wr_pad - wr_hi.astype(F32)).astype(BF16)
    row = pl.BlockSpec((1, tm, d), lambda i, j: (i, j, 0))
    vec = pl.BlockSpec((1, 1, d), lambda i, j: (i, 0, 0))
    par = pl.BlockSpec((1, d), lambda i, j: (0, 0))
    top = pl.BlockSpec((1, tm, TOP_K), lambda i, j: (i, j, 0))
    return pl.pallas_call(
        _outproj_kernel,
        grid=(b, t // tm),
        in_specs=[row, pl.BlockSpec((d, d), lambda i, j: (0, 0)), row, vec, vec, vec, par, par,
                  pl.BlockSpec((d, LANES), lambda i, j: (0, 0)), pl.BlockSpec((d, LANES), lambda i, j: (0, 0)),
                  pl.BlockSpec((1, LANES), lambda i, j: (0, 0))],
        out_specs=[row, row, top, top],
        out_shape=[jax.ShapeDtypeStruct((b, t, d), F32), jax.ShapeDtypeStruct((b, t, d), BF16),
                   jax.ShapeDtypeStruct((b, t, TOP_K), I32), jax.ShapeDtypeStruct((b, t, TOP_K), F32)],
        compiler_params=_params("arbitrary", "arbitrary"),
        name="outproj_ln_router",
    )(a, w_bf, x, g, sc, sh, lng, lnb, wr_hi, wr_lo, br_pad)


MOE_SCATTER_GROUP = 3


def _moe_kernel(cnt_ref, x_ref, te_ref, tg_ref, w1_ref, b1_ref, w2_ref, b2_ref, *rest, ts, nsub, rb, emit_cast):
    e = pl.program_id(1)
    if emit_cast:
        o_ref, w1o_ref, w2o_ref, pos_ref, tri_ref, sp_ref, sy_ref = rest
        w1o_ref[0] = w1_ref[0].astype(BF16)
        w2o_ref[0] = w2_ref[0].astype(BF16)
        w1_ref, w2_ref = w1o_ref, w2o_ref
    else:
        o_ref, pos_ref, tri_ref, sp_ref, sy_ref = rest

    @pl.when(jnp.logical_and(pl.program_id(0) == 0, e == 0))
    def _():
        si = lax.broadcasted_iota(I32, (ts, ts), 0)
        ti = lax.broadcasted_iota(I32, (ts, ts), 1)
        tri_ref[...] = jnp.where(si <= ti, 1.0, 0.0).astype(BF16)

    @pl.when(e == 0)
    def _():
        o_ref[...] = jnp.zeros_like(o_ref)
        eio = lax.broadcasted_iota(I32, (N_EXPERTS, ts), 0)
        for s in range(nsub):
            sel = jnp.zeros((N_EXPERTS, ts), F32)
            for k in range(TOP_K):
                sel = sel + jnp.where(te_ref[k:k + 1, s * ts:(s + 1) * ts] == eio, 1.0, 0.0)
            rank = jnp.dot(sel.astype(BF16), tri_ref[...], preferred_element_type=F32)
            pos_ref[s] = rank * sel

    slot = pl.multiple_of((e % MOE_SCATTER_GROUP) * rb, rb)
    last = N_EXPERTS - 1

    for s in range(nsub):
        rows = slice(s * ts, (s + 1) * ts)
        pm = pos_ref[s, pl.ds(e, 1), :]
        gate = jnp.zeros((1, ts), F32)
        for k in range(TOP_K):
            gate = gate + jnp.where(te_ref[k:k + 1, rows] == e, tg_ref[k:k + 1, rows], 0.0)
        cnt = cnt_ref[(pl.program_id(0) * nsub + s) * N_EXPERTS + e]
        nblk = (cnt + (rb - 1)) // rb

        def block(blk, pm=pm, gate=gate, rows=rows):
            r = (blk * rb + 1 + lax.broadcasted_iota(I32, (rb, 1), 0)).astype(F32)
            hit = pm == r
            onehot = jnp.where(hit, 1.0, 0.0).astype(BF16)
            xg = jnp.dot(onehot, x_ref[rows, :], preferred_element_type=F32).astype(BF16)
            h = jnp.dot(xg, w1_ref[0], preferred_element_type=F32) + b1_ref[0]
            hg = jnp.minimum(h[:, :D_FF], SWIGLU_LIMIT)
            hl = jnp.clip(h[:, D_FF:], -SWIGLU_LIMIT, SWIGLU_LIMIT)
            a = hg * jax.nn.sigmoid(SWIGLU_ALPHA * hg) * (hl + 1.0)
            y = jnp.dot(a.astype(BF16), w2_ref[0], preferred_element_type=F32) + b2_ref[0]
            grow = jnp.sum(jnp.where(hit, gate, 0.0), axis=-1, keepdims=True)
            return onehot, (y * grow).astype(BF16)

        @pl.when(cnt > 0)
        def _(s=s, block=block):
            onehot, yg = block(0)
            sp_ref[s, pl.ds(slot, rb), :] = onehot
            sy_ref[s, pl.ds(slot, rb), :] = yg

        @pl.when(cnt == 0)
        def _(s=s):
            sp_ref[s, pl.ds(slot, rb), :] = jnp.zeros((rb, ts), BF16)
            sy_ref[s, pl.ds(slot, rb), :] = jnp.zeros((rb, sy_ref.shape[2]), BF16)

        def extra(blk, carry, block=block, rows=rows):
            onehot, yg = block(blk)
            o_ref[rows, :] += lax.dot_general(onehot, yg, TN_DIMS, preferred_element_type=F32)
            return carry

        lax.fori_loop(1, nblk, extra, 0)

        def flush(nrows, s=s, rows=rows):
            o_ref[rows, :] += lax.dot_general(sp_ref[s, :nrows, :], sy_ref[s, :nrows, :], TN_DIMS,
                                              preferred_element_type=F32)

        if last % MOE_SCATTER_GROUP != MOE_SCATTER_GROUP - 1:
            @pl.when(e == last)
            def _(flush=flush):
                flush((last % MOE_SCATTER_GROUP + 1) * rb)

        @pl.when(e % MOE_SCATTER_GROUP == MOE_SCATTER_GROUP - 1)
        def _(flush=flush):
            flush(MOE_SCATTER_GROUP * rb)


def _moe(u2, te, tg, w1, b1, w2, b2, layer, emit_cast=False):
    n, d = u2.shape
    ts = _row_tile(n, MOE_TOKEN_TILE)
    nsub = MOE_SUBTILES if n % (MOE_SUBTILES * ts) == 0 else 1
    tm = ts * nsub
    assert not emit_cast or n == tm
    rb = min(ts, ts * TOP_K // N_EXPERTS + 32)
    te_t = te.T
    tg_t = tg.T
    hits = te.reshape(n // ts, ts * TOP_K)[:, :, None] == jnp.arange(N_EXPERTS, dtype=I32)
    counts = jnp.sum(hits.astype(I32), axis=1).reshape(-1)
    grid_spec = pltpu.PrefetchScalarGridSpec(
        num_scalar_prefetch=1,
        grid=(n // tm, N_EXPERTS),
        in_specs=[
            pl.BlockSpec((tm, d), lambda i, e, c: (i, 0)),
            pl.BlockSpec((TOP_K, tm), lambda i, e, c: (0, i)),
            pl.BlockSpec((TOP_K, tm), lambda i, e, c: (0, i)),
            pl.BlockSpec((None, 1, d, 2 * D_FF), lambda i, e, c: (layer, e, 0, 0)),
            pl.BlockSpec((1, 1, 2 * D_FF), lambda i, e, c: (e, 0, 0)),
            pl.BlockSpec((None, 1, D_FF, d), lambda i, e, c: (layer, e, 0, 0)),
            pl.BlockSpec((1, 1, d), lambda i, e, c: (e, 0, 0)),
        ],
        out_specs=[pl.BlockSpec((tm, d), lambda i, e, c: (i, 0))] + ([
            pl.BlockSpec((1, d, 2 * D_FF), lambda i, e, c: (e, 0, 0)),
            pl.BlockSpec((1, D_FF, d), lambda i, e, c: (e, 0, 0))] if emit_cast else []),
        scratch_shapes=[pltpu.VMEM((nsub, N_EXPERTS, ts), F32), pltpu.VMEM((ts, ts), BF16),
                        pltpu.VMEM((nsub, MOE_SCATTER_GROUP * rb, ts), BF16),
                        pltpu.VMEM((nsub, MOE_SCATTER_GROUP * rb, d), BF16)],
    )
    out_shape = [jax.ShapeDtypeStruct((n, d), F32)] + ([
        jax.ShapeDtypeStruct((N_EXPERTS, d, 2 * D_FF), BF16),
        jax.ShapeDtypeStruct((N_EXPERTS, D_FF, d), BF16)] if emit_cast else [])
    outs = pl.pallas_call(
        functools.partial(_moe_kernel, ts=ts, nsub=nsub, rb=rb, emit_cast=emit_cast),
        grid_spec=grid_spec,
        out_shape=out_shape,
        compiler_params=_params("arbitrary", "arbitrary"),
        name="moe_experts_cast" if emit_cast else "moe_experts",
    )(counts, u2, te_t, tg_t, w1, b1.reshape(N_EXPERTS, 1, -1), w2, b2.reshape(N_EXPERTS, 1, -1))
    return outs if emit_cast else outs[0]


def _rope_tables(pos, head_dim):
    rd = head_dim // ROPE_FRACTION
    half = rd // 2
    inv = jnp.power(ROPE_THETA, -jnp.arange(half, dtype=F32) / half)
    ang = jnp.asarray(pos).astype(F32)[:, None] * inv[None, :]
    cos, sin = jnp.cos(ang), jnp.sin(ang)
    t = pos.shape[0]
    ones = jnp.ones((t, head_dim - rd), F32)
    zeros = jnp.zeros((t, head_dim - rd), F32)
    zh = jnp.zeros((t, half), F32)
    c = jnp.concatenate([cos, cos, ones], axis=-1)
    sa = jnp.concatenate([zh, sin, zeros], axis=-1)
    sb = jnp.concatenate([-sin, zh, zeros], axis=-1)
    rep = LANES // head_dim
    return tuple(jnp.tile(a, (1, rep)) for a in (c, sa, sb))


def _rope(z, c, sa, sb, half):
    w = z.shape[-1]
    rep = w // LANES
    if rep > 1:
        c, sa, sb = (jnp.concatenate([a] * rep, axis=-1) for a in (c, sa, sb))
    return z * c + pltpu.roll(z, half, 1) * sa + pltpu.roll(z, w - half, 1) * sb


def _dsa_inproj_kernel(x_ref, f_ref, g_ref, lng_ref, lnb_ref, sc_ref, sh_ref, w_in_ref,
                       c1_ref, sa1_ref, sb1_ref, c2_ref, sa2_ref, sb2_ref,
                       xo_ref, q_ref, kf_ref, vf_ref, kif_ref, kb_ref, vb_ref, kib_ref, qi_ref, w_ref):
    z = DEEPNORM_ALPHA * x_ref[0] + (1.0 + g_ref[0]) * f_ref[0]
    x = _layer_norm(z, lng_ref[...], lnb_ref[...])
    xo_ref[0] = x
    u = x * (1.0 + sc_ref[0]) + sh_ref[0]
    p = jnp.dot(u.astype(BF16), w_in_ref[...], preferred_element_type=F32)

    c1, sa1, sb1 = c1_ref[...], sa1_ref[...], sb1_ref[...]
    c2, sa2, sb2 = c2_ref[...], sa2_ref[...], sb2_ref[...]
    h1 = DSA_HD // ROPE_FRACTION // 2
    h2 = IDX_DIM // ROPE_FRACTION // 2
    o_k = DSA_Q
    o_v = DSA_Q + DSA_KV
    o_qi = DSA_Q + 2 * DSA_KV
    o_ki = o_qi + IDX_HEADS * IDX_DIM
    q = _rope(p[:, :DSA_Q], c1, sa1, sb1, h1)
    q_ref[0] = (q * DSA_LOG2_SCALE).astype(BF16)
    k = _rope(p[:, o_k:o_v], c1, sa1, sb1, h1)
    kf_ref[0] = k
    kb_ref[0] = k.astype(BF16)
    v = p[:, o_v:o_qi]
    vf_ref[0] = v
    vb_ref[0] = v.astype(BF16)
    qi = _rope(p[:, o_qi:o_ki], c2, sa2, sb2, h2)
    qi_ref[0] = qi.astype(BF16)
    tail = p[:, o_ki:o_ki + LANES]
    ki = _rope(tail, c2, sa2, sb2, h2)[:, :IDX_DIM]
    kif_ref[0] = ki
    kib_ref[0] = ki.astype(BF16)
    wi = tail[:, IDX_DIM:IDX_DIM + IDX_HEADS]
    w_ref[0] = (wi * (IDX_HEADS ** -0.5)) * (IDX_DIM ** -0.5)


def _dsa_inproj(x, f, g, lng, lnb, sc, sh, w_bf, pos):
    b, t, d = x.shape
    n = w_bf.shape[1]
    tm = _row_tile(t, PROJ_ROW_TILE)
    tabs = _rope_tables(pos, DSA_HD) + _rope_tables(pos, IDX_DIM)
    tab = pl.BlockSpec((tm, LANES), lambda i, j: (j, 0))
    vec = pl.BlockSpec((1, 1, d), lambda i, j: (i, 0, 0))
    par = pl.BlockSpec((1, d), lambda i, j: (0, 0))

    def row(w):
        return pl.BlockSpec((1, tm, w), lambda i, j: (i, j, 0))

    widths = [(d, F32), (DSA_Q, BF16), (DSA_KV, F32), (DSA_KV, F32), (IDX_DIM, F32), (DSA_KV, BF16), (DSA_KV, BF16),
              (IDX_DIM, BF16), (IDX_HEADS * IDX_DIM, BF16), (IDX_HEADS, F32)]
    return pl.pallas_call(
        _dsa_inproj_kernel,
        grid=(b, t // tm),
        in_specs=[row(d), row(d), vec, par, par, vec, vec, pl.BlockSpec((d, n), lambda i, j: (0, 0))] + [tab] * 6,
        out_specs=[row(w) for w, _ in widths],
        out_shape=[jax.ShapeDtypeStruct((b, t, w), dt) for w, dt in widths],
        compiler_params=_params("arbitrary", "arbitrary"),
        name="ln_inproj_rope_split",
    )(x, f, g, lng, lnb, sc, sh, w_bf, *tabs)


def _f2key(f):
    bits = lax.bitcast_convert_type(f, I32)
    return bits ^ ((bits >> 31) & 0x7FFFFFFF)


def _key2f(k):
    return lax.bitcast_convert_type(k ^ ((k >> 31) & 0x7FFFFFFF), F32)


def _dsa_kernel(nkt_ref, q_ref, qi_ref, w_ref, k_ref, v_ref, ki_ref, kpos_ref, qlim_ref, nadm_ref, o_ref,
                s_ref, wb_ref, qg_ref, m_ref, l_ref, acc_ref, flag_ref, *, tq, tk, nsel, value_iters):
    nkt = nkt_ref[pl.program_id(1)]
    qlim = qlim_ref[...]
    w = w_ref[0]
    qi = qi_ref[0]
    qis_all = jnp.concatenate([qi[:, h * IDX_DIM:(h + 1) * IDX_DIM] for h in range(IDX_HEADS)], axis=0)
    nlg = tk // LANES
    for h in range(IDX_HEADS):
        wb_ref[h] = jnp.broadcast_to(w[:, h:h + 1], (tq, LANES))

    def p1(kt, carry):
        rmax, rmin, cgt0, cge0 = carry
        off = pl.multiple_of(kt * tk, tk)
        kit = ki_ref[0, pl.ds(off, tk), :]
        acc = jnp.zeros((tq, tk), F32)
        d_all = lax.dot_general(qis_all, kit, NT_DIMS, preferred_element_type=F32)
        for h in range(IDX_HEADS):
            d = d_all[h * tq:(h + 1) * tq]
            acc = acc + jnp.maximum(d, 0.0) * jnp.concatenate([wb_ref[h]] * nlg, axis=1)
        s = jnp.where(kpos_ref[kt] < qlim, acc, -jnp.inf)
        s_ref[kt] = s
        gt0 = jnp.where(s > 0.0, 1.0, 0.0)
        ge0 = jnp.where(s >= 0.0, 1.0, 0.0)
        for g in range(nlg):
            lanes = slice(g * LANES, (g + 1) * LANES)
            rmax = jnp.maximum(rmax, s[:, lanes])
            rmin = jnp.minimum(rmin, acc[:, lanes])
            cgt0 = cgt0 + gt0[:, lanes]
            cge0 = cge0 + ge0[:, lanes]
        return rmax, rmin, cgt0, cge0

    zcol = jnp.zeros((tq, 1), F32)
    zlan = jnp.zeros((tq, LANES), F32)
    init = (jnp.full((tq, LANES), -jnp.inf, F32), jnp.full((tq, LANES), jnp.inf, F32), zlan, zlan)
    rmax, rmin, cgt0, cge0 = lax.fori_loop(0, nkt, p1, init)
    rmax = jnp.max(rmax, axis=-1, keepdims=True)
    rmin = jnp.min(rmin, axis=-1, keepdims=True)
    cgt0 = jnp.sum(cgt0, axis=-1, keepdims=True)
    cge0 = jnp.sum(cge0, axis=-1, keepdims=True)
    nadm = nadm_ref[...]

    rg = min(tq, 32)
    fsel = float(nsel)
    unbounded = float(2 ** 30)

    def count_ge(tb):
        def body(kt, c):
            out = []
            for r in range(tq // rg):
                tr = jnp.concatenate([tb[r * rg:(r + 1) * rg, :]] * nlg, axis=1)
                hit = jnp.where(s_ref[kt, r * rg:(r + 1) * rg, :] >= tr, 1.0, 0.0)
                cr = c[r * rg:(r + 1) * rg]
                for g in range(nlg):
                    cr = cr + hit[:, g * LANES:(g + 1) * LANES]
                out.append(cr)
            return jnp.concatenate(out, axis=0)
        c = lax.fori_loop(0, nkt, body, zlan)
        return jnp.broadcast_to(jnp.sum(c, axis=-1, keepdims=True), (tq, LANES))

    rmax, rmin, cgt0, cge0, nadm = (jnp.broadcast_to(a, (tq, LANES)) for a in (rmax, rmin, cgt0, cge0, nadm))
    few = nadm <= fsel
    many = jnp.logical_not(few)
    pos = jnp.logical_and(many, cgt0 > fsel)
    neg = jnp.logical_and(many, cge0 < fsel)
    zero = jnp.logical_and(many, jnp.logical_and(cgt0 <= fsel, cge0 >= fsel))
    lo0 = jnp.where(pos, 0, _f2key(jnp.where(neg, rmin, 0.0)))
    hi0 = jnp.where(pos, _f2key(jnp.where(pos, rmax, 0.0)) + 1, 0)
    clo0 = jnp.where(pos, cge0, nadm)
    chi0 = jnp.where(pos, 0.0, cge0)
    act0 = jnp.where(jnp.logical_or(pos, neg), 1.0, 0.0)
    t0 = jnp.where(few, -jnp.inf, 0.0)
    need0 = jnp.where(zero, fsel - cgt0, unbounded)

    def cond(st):
        return st[0] > 0.0

    def step(st):
        _, it, lo, hi, clo, chi, act, thr, need = st
        conv = jnp.logical_and(act > 0.0, lo + 1 >= hi)
        thr = jnp.where(conv, _key2f(lo), thr)
        need = jnp.where(conv, fsel - chi, need)
        act = jnp.where(conv, 0.0, act)
        flo, fhi = _key2f(lo), _key2f(hi)
        mid_i = (lo >> 1) + (hi >> 1) + (lo & hi & 1)
        lclo = jnp.log(clo)
        interp = (lclo - float(np.log(nsel - 0.5))) / (lclo - jnp.log(jnp.maximum(chi, 0.5)))
        frac = jnp.where(it % 3 == 2, 0.5, interp)
        mid_v = _f2key(flo + frac * (fhi - flo))
        use_v = jnp.logical_and(it < value_iters, jnp.logical_and(mid_v > lo, mid_v < hi))
        mid = jnp.where(use_v, mid_v, mid_i)
        tm = _key2f(mid)
        c = count_ge(jnp.where(act > 0.0, tm, jnp.inf))
        hit = jnp.logical_and(act > 0.0, c == fsel)
        thr = jnp.where(hit, tm, thr)
        up = jnp.logical_and(act > 0.0, c > fsel)
        dn = jnp.logical_and(act > 0.0, c < fsel)
        lo = jnp.where(up, mid, lo)
        clo = jnp.where(up, c, clo)
        hi = jnp.where(dn, mid, hi)
        chi = jnp.where(dn, c, chi)
        act = jnp.where(hit, 0.0, act)
        return jnp.max(act), it + 1, lo, hi, clo, chi, act, thr, need

    st = (jnp.max(act0), jnp.int32(0), lo0, hi0, clo0, chi0, act0, t0, need0)
    _, _, _, _, _, _, _, thr, need = lax.while_loop(cond, step, st)
    thr, need = thr[:, 0:1], need[:, 0:1]
    tied = need < unbounded * 0.5
    has_ties = jnp.max(jnp.where(tied, 1.0, 0.0)) > 0.0
    thr = jnp.maximum(thr, -F32_MAX)

    for g in range(DSA_KV_HEADS):
        for hh in range(DSA_GROUP):
            hd = g * DSA_GROUP + hh
            qg_ref[g, hh * tq:(hh + 1) * tq, :] = q_ref[0, :, hd * DSA_HD:(hd + 1) * DSA_HD]
    m_ref[...] = jnp.full(m_ref.shape, NEG_BIG, F32)
    l_ref[...] = jnp.zeros(l_ref.shape, F32)
    acc_ref[...] = jnp.zeros(acc_ref.shape, F32)

    ones_cols = jnp.ones((tk, DSA_HD), BF16)

    def attend(kt, bias):
        off = pl.multiple_of(kt * tk, tk)
        qk = [lax.dot_general(qg_ref[g], k_ref[0, pl.ds(off, tk), g * DSA_HD:(g + 1) * DSA_HD], NT_DIMS,
                              preferred_element_type=F32) for g in range(DSA_KV_HEADS)]
        for g in range(DSA_KV_HEADS):
            vt_g = jnp.concatenate([v_ref[0, pl.ds(off, tk), g * DSA_HD:(g + 1) * DSA_HD], ones_cols], axis=1)
            for hh in range(DSA_GROUP):
                rows = slice(hh * tq, (hh + 1) * tq)
                lg = qk[g][rows] + bias
                m_old = m_ref[g, rows, :]
                m_new = jnp.maximum(m_old, jnp.max(lg, axis=-1, keepdims=True))
                alpha = jnp.exp2(m_old - m_new)
                p = jnp.exp2(lg - jnp.concatenate([m_new] * nlg, axis=1))
                pv = jnp.dot(p.astype(BF16), vt_g, preferred_element_type=F32)
                acc_ref[g, rows, :] = alpha * acc_ref[g, rows, :] + pv[:, :DSA_HD]
                l_ref[g, rows, :] = alpha * l_ref[g, rows, :] + pv[:, DSA_HD:]
                m_ref[g, rows, :] = m_new

    @pl.when(jnp.logical_not(has_ties))
    def _():
        def body(kt, carry):
            attend(kt, jnp.where(s_ref[kt] >= thr, 0.0, NEG_BIG))
            return carry
        lax.fori_loop(0, nkt, body, 0)

    @pl.when(has_ties)
    def _():
        def scan(kt, carry):
            seen, cut, nin = carry
            eqc = jnp.sum(jnp.where(s_ref[kt] == thr, 1.0, 0.0), axis=-1, keepdims=True)
            here = jnp.logical_and(tied, jnp.logical_and(seen < need, seen + eqc >= need))
            cut = jnp.where(here, kt, cut)
            nin = jnp.where(here, need - seen, nin)
            flag_ref[kt] = jnp.max(jnp.where(here, 1.0, 0.0)).astype(I32)
            return seen + eqc, cut, nin

        cut0 = jnp.where(tied, -1, nkt).astype(I32)
        _, cut, nin = lax.fori_loop(0, nkt, scan, (zcol, cut0, zcol))

        def body(kt, carry):
            s = s_ref[kt]
            before = jnp.where(kt < cut, 0.0, NEG_BIG)

            def plain():
                return jnp.where(s == thr, before, NEG_BIG)

            def ranked():
                si = lax.broadcasted_iota(I32, (tk, tk), 0)
                ti = lax.broadcasted_iota(I32, (tk, tk), 1)
                triu = jnp.where(si <= ti, 1.0, 0.0).astype(BF16)
                eq = jnp.where(s == thr, 1.0, 0.0).astype(BF16)
                rank = jnp.dot(eq, triu, preferred_element_type=F32)
                inside = jnp.where(rank <= nin, 0.0, NEG_BIG)
                return jnp.where(s == thr, jnp.where(kt == cut, inside, before), NEG_BIG)

            tie_bias = lax.cond(flag_ref[kt] > 0, ranked, plain)
            attend(kt, jnp.where(s > thr, 0.0, tie_bias))
            return carry
        lax.fori_loop(0, nkt, body, 0)

    for g in range(DSA_KV_HEADS):
        og = acc_ref[g] / l_ref[g]
        for hh in range(DSA_GROUP):
            hd = g * DSA_GROUP + hh
            o_ref[0, :, hd * DSA_HD:(hd + 1) * DSA_HD] = og[hh * tq:(hh + 1) * tq, :].astype(BF16)


def _dsa_attend(q, qi, w, k, v, ki, kpos, qlim, tq, tk):
    b, t, _ = q.shape
    lp = k.shape[1]
    nq = t // tq
    nsel = min(DSA_TOPK, int(np.sum(kpos < np.iinfo(np.int32).max)) // 4)
    qmax = qlim.reshape(nq, tq).max(axis=1)
    nadm = np.searchsorted(kpos, qlim, side="left")
    nkt = np.minimum(-(-np.searchsorted(kpos, qmax, side="left") // tk), lp // tk).astype(np.int32)
    nkt = np.maximum(nkt, 1)
    single = pl.Buffered(1)

    def full(wd):
        return pl.BlockSpec((1, lp, wd), lambda i, j, n: (i, 0, 0), pipeline_mode=single)

    def row(wd):
        return pl.BlockSpec((1, tq, wd), lambda i, j, n: (i, j, 0))

    grid_spec = pltpu.PrefetchScalarGridSpec(
        num_scalar_prefetch=1,
        grid=(b, nq),
        in_specs=[row(DSA_Q), row(IDX_HEADS * IDX_DIM), row(IDX_HEADS), full(DSA_KV), full(DSA_KV), full(IDX_DIM),
                  pl.BlockSpec((lp // tk, 1, tk), lambda i, j, n: (0, 0, 0)),
                  pl.BlockSpec((tq, 1), lambda i, j, n: (j, 0)),
                  pl.BlockSpec((tq, 1), lambda i, j, n: (j, 0))],
        out_specs=row(DSA_Q),
        scratch_shapes=[
            pltpu.VMEM((lp // tk, tq, tk), F32),
            pltpu.VMEM((IDX_HEADS, tq, LANES), F32),
            pltpu.VMEM((DSA_KV_HEADS, DSA_GROUP * tq, DSA_HD), BF16),
            pltpu.VMEM((DSA_KV_HEADS, DSA_GROUP * tq, LANES), F32),
            pltpu.VMEM((DSA_KV_HEADS, DSA_GROUP * tq, LANES), F32),
            pltpu.VMEM((DSA_KV_HEADS, DSA_GROUP * tq, DSA_HD), F32),
            pltpu.SMEM((lp // tk,), I32),
        ],
    )
    return pl.pallas_call(
        functools.partial(_dsa_kernel, tq=tq, tk=tk, nsel=nsel, value_iters=DSA_VALUE_ITERS),
        grid_spec=grid_spec,
        out_shape=jax.ShapeDtypeStruct((b, t, DSA_Q), BF16),
        compiler_params=_params("arbitrary", "arbitrary"),
        name="dsa_select_attend",
    )(jnp.asarray(nkt), q, qi, w, k, v, ki, jnp.asarray(kpos, I32).reshape(lp // tk, 1, tk),
      jnp.asarray(qlim, I32).reshape(t, 1), jnp.asarray(nadm, F32).reshape(t, 1))


def _trunk(x, mods, pos, ml_state, ds_cache, prm, moe_bf=None):
    b, t, d = x.shape
    casts = []

    def moe(i, u2, te, tg):
        args = (u2.reshape(b * t, d), te.reshape(b * t, TOP_K), tg.reshape(b * t, TOP_K))
        if moe_bf is None:
            f, w1b, w2b = _moe(*args, prm["moe_w1"], prm["moe_b1"][i], prm["moe_w2"], prm["moe_b2"][i], i,
                               emit_cast=True)
            casts.append((w1b, w2b))
        else:
            w1b, w2b = moe_bf[i]
            f = _moe(*args, w1b[None], prm["moe_b1"][i], w2b[None], prm["moe_b2"][i], 0)
        return f.reshape(b, t, d)

    chunk = next((c for c in (ML_EVAL_CHUNK, CHUNK) if t % c == 0), t)

    def mod(i, k):
        return mods[i][:, None, k * d:(k + 1) * d]

    p, pg = _ml_inproj(x, mod(0, 1), mod(0, 0), prm["ml_w_in"])
    if ml_state is None:
        c0 = jnp.zeros((b, ML_HEADS, ML_DV, ML_DK), F32)
        n0 = jnp.zeros((b, ML_HEADS, ML_DK), F32)
        m0 = jnp.zeros((b, ML_HEADS), F32)
    else:
        c0, n0, m0 = (s[0] for s in ml_state)
    hg, c_new, n_new, m_new = _mlstm(p, pg, prm["ml_b_gates"], prm["ml_norm_g"], c0, n0, m0, chunk)
    x1, u2, te, tg = _outproj(hg, prm["ml_w_out"], x, mod(0, 2), mod(0, 4), mod(0, 3),
                              prm["ln_g"][0, 0:1], prm["ln_b"][0, 0:1], prm["w_router"][0], prm["b_router"][0])
    f = moe(0, u2, te, tg)

    x2, q, kf, vf, kif, kb, vb, kib, qi, w = _dsa_inproj(
        x1, f, mod(0, 5), prm["ln_g"][0, 1:2], prm["ln_b"][0, 1:2], mod(1, 1), mod(1, 0), prm["ds_w_in"], pos)
    pos_np = np.asarray(pos)
    qlim = (pos_np // CHUNK + 1) * CHUNK
    if ds_cache is None:
        kpos = pos_np
        tq, tk = DSA_Q_TILE, DSA_K_TILE
        if t % tk:
            tq, tk = t, t
        k_all, v_all, ki_all = kb, vb, kib
    else:
        ck, cv, cki = ds_cache
        past = ck.shape[1]
        total = past + t
        lp = -(-total // (3 * LANES)) * (3 * LANES)
        tq, tk = t, 3 * LANES
        kpos = np.concatenate([np.arange(past), pos_np, np.full((lp - total,), np.iinfo(np.int32).max)])

        def cat(cache, new):
            full = jnp.concatenate([cache.reshape(b, past, -1).astype(BF16), new], axis=1)
            return jnp.pad(full, ((0, 0), (0, lp - total), (0, 0)))

        k_all, v_all, ki_all = cat(ck, kb), cat(cv, vb), cat(cki, kib)
    o = _dsa_attend(q, qi, w, k_all, v_all, ki_all, kpos.astype(np.int64), qlim, tq, tk)
    x3, u2, te, tg = _outproj(o, prm["ds_w_out"], x2, mod(1, 2), mod(1, 4), mod(1, 3),
                              prm["ln_g"][1, 0:1], prm["ln_b"][1, 0:1], prm["w_router"][1], prm["b_router"][1])
    f = moe(1, u2, te, tg)
    y = _final_ln(x3, f, mod(1, 5), prm["ln_g"][1, 1:2], prm["ln_b"][1, 1:2])

    ml_out = (c_new[None], n_new[None], m_new[None])
    ds_out = (kf.reshape(1, b, t, DSA_KV_HEADS, DSA_HD), vf.reshape(1, b, t, DSA_KV_HEADS, DSA_HD), kif[None])
    return y, ml_out, ds_out, casts


def _prepare(w_ada, b_ada, ln_g, ln_b, ml_w_in, ml_b_gates, ml_norm_g, ml_w_out, ds_w_in, ds_w_out,
             moe_w_router, moe_b_router, moe_w1, moe_b1, moe_w2, moe_b2):
    return {
        "ln_g": ln_g, "ln_b": ln_b,
        "ml_w_in": jnp.pad(ml_w_in[0], ((0, 0), (0, ML_IN_PAD - ML_IN))).astype(BF16),
        "ml_b_gates": ml_b_gates[0], "ml_norm_g": ml_norm_g[0],
        "ml_w_out": ml_w_out[0].astype(BF16),
        "ds_w_in": jnp.pad(ds_w_in[0], ((0, 0), (0, DSA_IN_PAD - DSA_IN))).astype(BF16),
        "ds_w_out": ds_w_out[0].astype(BF16),
        "w_router": jnp.pad(moe_w_router, ((0, 0), (0, 0), (0, LANES - N_EXPERTS))),
        "b_router": jnp.pad(moe_b_router, ((0, 0), (0, LANES - N_EXPERTS)), constant_values=-jnp.inf)[:, None, :],
        "moe_w1": moe_w1, "moe_b1": moe_b1, "moe_w2": moe_w2, "moe_b2": moe_b2,
    }


def kernel(x_prompt, x_sample, state_mlstm_C, state_mlstm_n, state_mlstm_m, cache_dsa_k, cache_dsa_v,
           cache_dsa_kidx, c_prompt, c_sample, w_ada, b_ada, ln_g, ln_b, ml_w_in, ml_b_gates, ml_norm_g,
           ml_w_out, ds_w_in, ds_w_out, moe_w_router, moe_b_router, moe_w1, moe_b1, moe_w2, moe_b2):
    prm = _prepare(w_ada, b_ada, ln_g, ln_b, ml_w_in, ml_b_gates, ml_norm_g, ml_w_out, ds_w_in, ds_w_out,
                   moe_w_router, moe_b_router, moe_w1, moe_b1, moe_w2, moe_b2)
    bp, bs = c_prompt.shape[0], c_sample.shape[0]
    rows = -(-(bp + bs) // 8) * 8
    c_rows = jnp.pad(jnp.concatenate([c_prompt, c_sample], axis=0), ((0, rows - bp - bs), (0, 0)))
    mods = _ada(c_rows, w_ada, b_ada)
    mods_p = [mods[i, :bp] for i in range(DEPTH)]
    mods_s = [mods[i, bp:bp + bs] for i in range(DEPTH)]

    past = cache_dsa_k.shape[2]
    pos_s = past + np.arange(x_sample.shape[1], dtype=np.int32)
    y_s, ml_s, ds_s, moe_bf = _trunk(x_sample, mods_s, pos_s, (state_mlstm_C, state_mlstm_n, state_mlstm_m),
                                     (cache_dsa_k[0], cache_dsa_v[0], cache_dsa_kidx[0]), prm)
    pos_p = np.arange(x_prompt.shape[1], dtype=np.int32)
    y_p, ml_p, ds_p, _ = _trunk(x_prompt, mods_p, pos_p, None, None, prm, moe_bf=moe_bf)
    return (y_p, y_s, ml_p[0], ml_p[1], ml_p[2], ds_p[0], ds_p[1], ds_p[2],
            ml_s[0], ml_s[1], ml_s[2], ds_s[0], ds_s[1], ds_s[2])
```

```python
import functools

import numpy as np
import jax
import jax.numpy as jnp
from jax import lax
from jax.experimental import pallas as pl
from jax.experimental.pallas import tpu as pltpu

F32 = jnp.float32
BF16 = jnp.bfloat16
I32 = jnp.int32

D_MODEL = 1024
DEPTH = 2
CHUNK = 64
ML_HEADS = 4
ML_DV = D_MODEL // ML_HEADS
ML_DK = ML_DV // 2
ML_QK = ML_HEADS * ML_DK
ML_V = ML_HEADS * ML_DV
ML_IN = 2 * ML_QK + 2 * ML_V + 2 * ML_HEADS
ML_GATE_OFF = 2 * ML_QK + 2 * ML_V
ML_EVAL_CHUNK = 256
DSA_HEADS = 8
DSA_KV_HEADS = 2
DSA_HD = D_MODEL // DSA_HEADS
DSA_GROUP = DSA_HEADS // DSA_KV_HEADS
IDX_HEADS = 8
IDX_DIM = 64
DSA_TOPK = 256
DSA_Q = DSA_HEADS * DSA_HD
DSA_KV = DSA_KV_HEADS * DSA_HD
DSA_IN = DSA_Q + 2 * DSA_KV + IDX_HEADS * IDX_DIM + IDX_DIM + IDX_HEADS
DSA_LOG2_SCALE = (DSA_HD ** -0.5) * float(np.log2(np.e))
ROPE_THETA = 500000.0
ROPE_FRACTION = 4
N_EXPERTS = 32
TOP_K = 4
D_FF = D_MODEL
SWIGLU_LIMIT = 7.0
SWIGLU_ALPHA = 1.702
DEEPNORM_ALPHA = (2 * DEPTH) ** 0.25
LN_EPS = 1e-5

LANES = 128
V7X_VMEM_BYTES = 64 * 1024 * 1024
VMEM_LIMIT = V7X_VMEM_BYTES - 8 * 1024 * 1024

PROJ_ROW_TILE = 512
OUTPROJ_ROW_TILE = 256
FINAL_LN_ROW_TILE = 1024
ADA_COL_TILE = 1536
MOE_TOKEN_TILE = 1024
MOE_SUBTILES = 2
DSA_Q_TILE = 128
DSA_K_TILE = 1024
DSA_VALUE_ITERS = 16

ML_IN_PAD = -(-ML_IN // LANES) * LANES
DSA_IN_PAD = -(-DSA_IN // LANES) * LANES
NEG_BIG = -1e30
F32_MAX = float(np.finfo(np.float32).max)
HIGHEST = lax.Precision.HIGHEST
NT_DIMS = (((1,), (1,)), ((), ()))
TN_DIMS = (((0,), (0,)), ((), ()))


def _params(*sem):
    return pltpu.CompilerParams(dimension_semantics=sem, vmem_limit_bytes=VMEM_LIMIT)


def _row_tile(t, pref):
    return pref if t % pref == 0 else t


def _ada_kernel(c_ref, w_ref, b_ref, o_ref):
    c = c_ref[...]
    cond = (c * jax.nn.sigmoid(c)).astype(BF16)
    o_ref[0] = jnp.dot(cond, w_ref[0].astype(BF16), preferred_element_type=F32) + b_ref[0]


def _ada(c_rows, w_ada, b_ada):
    rows = c_rows.shape[0]
    n = w_ada.shape[-1]
    tn = ADA_COL_TILE
    return pl.pallas_call(
        _ada_kernel,
        grid=(DEPTH, n // tn),
        in_specs=[
            pl.BlockSpec((rows, D_MODEL), lambda i, j: (0, 0)),
            pl.BlockSpec((1, D_MODEL, tn), lambda i, j: (i, 0, j)),
            pl.BlockSpec((1, 1, tn), lambda i, j: (i, 0, j)),
        ],
        out_specs=pl.BlockSpec((1, rows, tn), lambda i, j: (i, 0, j)),
        out_shape=jax.ShapeDtypeStruct((DEPTH, rows, n), F32),
        compiler_params=_params("arbitrary", "arbitrary"),
        name="ada_mod",
    )(c_rows, w_ada, b_ada.reshape(DEPTH, 1, n))


def _layer_norm(z, g, b):
    mu = jnp.mean(z, axis=-1, keepdims=True)
    zc = z - mu
    var = jnp.mean(zc * zc, axis=-1, keepdims=True)
    return zc * lax.rsqrt(var + LN_EPS) * g + b


def _ml_inproj_kernel(x_ref, sc_ref, sh_ref, w_ref, pm_ref, pg_ref):
    u = x_ref[0] * (1.0 + sc_ref[0]) + sh_ref[0]
    p = jnp.dot(u.astype(BF16), w_ref[...], preferred_element_type=F32)
    pm_ref[0, :, :ML_QK] = (p[:, :ML_QK] * (ML_DK ** -0.5)).astype(BF16)
    pm_ref[0, :, ML_QK:] = p[:, ML_QK:ML_GATE_OFF].astype(BF16)
    pg_ref[0] = p[:, ML_GATE_OFF:]


def _ml_inproj(x, sc, sh, w_bf):
    b, t, d = x.shape
    n = w_bf.shape[1]
    tm = _row_tile(t, PROJ_ROW_TILE)
    row = pl.BlockSpec((1, tm, d), lambda i, j: (i, j, 0))
    vec = pl.BlockSpec((1, 1, d), lambda i, j: (i, 0, 0))
    return pl.pallas_call(
        _ml_inproj_kernel,
        grid=(b, t // tm),
        in_specs=[row, vec, vec, pl.BlockSpec((d, n), lambda i, j: (0, 0))],
        out_specs=[pl.BlockSpec((1, tm, ML_GATE_OFF), lambda i, j: (i, j, 0)),
                   pl.BlockSpec((1, tm, n - ML_GATE_OFF), lambda i, j: (i, j, 0))],
        out_shape=[jax.ShapeDtypeStruct((b, t, ML_GATE_OFF), BF16),
                   jax.ShapeDtypeStruct((b, t, n - ML_GATE_OFF), F32)],
        compiler_params=_params("arbitrary", "arbitrary"),
        name="ml_inproj",
    )(x, sc, sh, w_bf)


def _final_ln_kernel(x_ref, f_ref, g_ref, lng_ref, lnb_ref, o_ref):
    z = DEEPNORM_ALPHA * x_ref[0] + (1.0 + g_ref[0]) * f_ref[0]
    o_ref[0] = _layer_norm(z, lng_ref[...], lnb_ref[...])


def _final_ln(x, f, g, lng, lnb):
    b, t, d = x.shape
    tm = _row_tile(t, FINAL_LN_ROW_TILE)
    row = pl.BlockSpec((1, tm, d), lambda i, j: (i, j, 0))
    vec = pl.BlockSpec((1, 1, d), lambda i, j: (i, 0, 0))
    par = pl.BlockSpec((1, d), lambda i, j: (0, 0))
    return pl.pallas_call(
        _final_ln_kernel,
        grid=(b, t // tm),
        in_specs=[row, row, vec, par, par],
        out_specs=row,
        out_shape=jax.ShapeDtypeStruct((b, t, d), F32),
        compiler_params=_params("arbitrary", "arbitrary"),
        name="final_ln",
    )(x, f, g, lng, lnb)


def _log_sigmoid(x):
    return jnp.minimum(x, 0.0) - jnp.log1p(jnp.exp(-jnp.abs(x)))


def _mlstm_kernel(p_ref, pg_ref, gt_ref, bcol_ref, brow_ref, ng_ref, c0_ref, n0_ref, m0_ref,
                  hg_ref, c_ref, n_ref, m_ref, *, chunk):
    L = chunk

    @pl.when(pl.program_id(1) == 0)
    def _():
        c_ref[...] = c0_ref[...]
        n_ref[...] = n0_ref[...]
        m_ref[...] = m0_ref[...]

    gcol = pg_ref[0, :, :2 * ML_HEADS] + bcol_ref[...]
    grow = gt_ref[0] + brow_ref[...]
    lf_col = _log_sigmoid(gcol)
    lf_row = _log_sigmoid(grow)
    ti = lax.broadcasted_iota(I32, (L, L), 0)
    si = lax.broadcasted_iota(I32, (L, L), 1)
    causal = si <= ti
    tril = jnp.where(causal, 1.0, 0.0).astype(F32)
    triu = jnp.where(ti <= si, 1.0, 0.0).astype(F32)
    b_col = jnp.dot(tril, lf_col, precision=HIGHEST, preferred_element_type=F32)
    b_row = jnp.dot(lf_row, triu, precision=HIGHEST, preferred_element_type=F32)

    for h in range(ML_HEADS):
        qs = p_ref[0, :, h * ML_DK:(h + 1) * ML_DK]
        kb = p_ref[0, :, ML_QK + h * ML_DK:ML_QK + (h + 1) * ML_DK]
        vb = p_ref[0, :, 2 * ML_QK + h * ML_DV:2 * ML_QK + (h + 1) * ML_DV]
        v = vb.astype(F32)
        o = p_ref[0, :, 2 * ML_QK + ML_V + h * ML_DV:2 * ML_QK + ML_V + (h + 1) * ML_DV].astype(F32)
        bc = b_col[:, ML_HEADS + h:ML_HEADS + h + 1]
        ic = gcol[:, h:h + 1]
        br = b_row[ML_HEADS + h:ML_HEADS + h + 1, :]
        ir = grow[h:h + 1, :]
        c_old = c_ref[0, h]
        n_old = n_ref[0, h:h + 1, :]
        m_old = m_ref[0, h:h + 1, 0:1]

        dm = jnp.where(causal, bc - br + ir, -jnp.inf)
        inter = bc + m_old
        mt = jnp.maximum(inter, jnp.max(dm, axis=-1, keepdims=True))
        qk = lax.dot_general(qs, kb, NT_DIMS, preferred_element_type=F32)
        s = jnp.exp(dm - mt) * qk
        wp = jnp.exp(inter - mt)
        qc = lax.dot_general(qs, c_old.astype(BF16), NT_DIMS, preferred_element_type=F32)
        num = jnp.dot(s.astype(BF16), vb, preferred_element_type=F32) + wp * qc
        qn = jnp.sum(qs.astype(F32) * n_old, axis=-1, keepdims=True)
        den = jnp.sum(s, axis=-1, keepdims=True) + wp * qn
        hh = num / jnp.maximum(jnp.abs(den), jnp.exp(-mt))

        m_new = mt[L - 1:L, :]
        wk = jnp.exp(bc[L - 1:L, :] - bc + ic - m_new)
        wprev = jnp.exp(inter[L - 1:L, :] - m_new)
        vw = (v * wk).astype(BF16)
        c_ref[0, h] = wprev * c_old + lax.dot_general(vw, kb, TN_DIMS, preferred_element_type=F32)
        n_ref[0, h:h + 1, :] = wprev * n_old + jnp.sum(wk * kb.astype(F32), axis=0, keepdims=True)
        m_ref[0, h:h + 1, :] = jnp.broadcast_to(m_new, (1, ML_DK))

        mu = jnp.mean(hh, axis=-1, keepdims=True)
        hc = hh - mu
        var = jnp.mean(hc * hc, axis=-1, keepdims=True)
        hn = hc * lax.rsqrt(var + LN_EPS) * ng_ref[:, h * ML_DV:(h + 1) * ML_DV]
        hg_ref[0, :, h * ML_DV:(h + 1) * ML_DV] = (jax.nn.sigmoid(o) * hn).astype(BF16)


def _mlstm(p, pg, b_gates, norm_g, c0, n0, m0, chunk):
    b, t, n = p.shape
    nc = t // chunk
    gt = jnp.swapaxes(pg[:, :, :2 * ML_HEADS], 1, 2)
    if nc > 1:
        gt_spec = pl.BlockSpec((1, 2 * ML_HEADS, chunk), lambda i, j: (i, 0, j))
    else:
        gt_spec = pl.BlockSpec((1, 2 * ML_HEADS, t), lambda i, j: (i, 0, 0))
    if nc > 1 and chunk % LANES != 0:
        gt = gt.reshape(b, 2 * ML_HEADS, nc, chunk).transpose(0, 2, 1, 3).reshape(b * nc, 2 * ML_HEADS, chunk)
        gt_spec = pl.BlockSpec((1, 2 * ML_HEADS, chunk), lambda i, j: (i * nc + j, 0, 0))
    m0b = jnp.broadcast_to(m0[..., None], (b, ML_HEADS, ML_DK))
    cspec = pl.BlockSpec((1, ML_HEADS, ML_DV, ML_DK), lambda i, j: (i, 0, 0, 0))
    nspec = pl.BlockSpec((1, ML_HEADS, ML_DK), lambda i, j: (i, 0, 0))
    hg, c, nn, m = pl.pallas_call(
        functools.partial(_mlstm_kernel, chunk=chunk),
        grid=(b, nc),
        in_specs=[
            pl.BlockSpec((1, chunk, n), lambda i, j: (i, j, 0)),
            pl.BlockSpec((1, chunk, pg.shape[-1]), lambda i, j: (i, j, 0)),
            gt_spec,
            pl.BlockSpec((1, 2 * ML_HEADS), lambda i, j: (0, 0)),
            pl.BlockSpec((2 * ML_HEADS, 1), lambda i, j: (0, 0)),
            pl.BlockSpec((1, ML_V), lambda i, j: (0, 0)),
            cspec, nspec, nspec,
        ],
        out_specs=[pl.BlockSpec((1, chunk, ML_V), lambda i, j: (i, j, 0)), cspec, nspec, nspec],
        out_shape=[
            jax.ShapeDtypeStruct((b, t, ML_V), BF16),
            jax.ShapeDtypeStruct((b, ML_HEADS, ML_DV, ML_DK), F32),
            jax.ShapeDtypeStruct((b, ML_HEADS, ML_DK), F32),
            jax.ShapeDtypeStruct((b, ML_HEADS, ML_DK), F32),
        ],
        compiler_params=_params("arbitrary", "arbitrary"),
        name="mlstm_scan",
    )(p, pg, gt, b_gates.reshape(1, -1), b_gates.reshape(-1, 1), norm_g.reshape(1, -1), c0, n0, m0b)
    return hg, c, nn, m[..., 0]


def _outproj_kernel(a_ref, w_ref, x_ref, g_ref, sc_ref, sh_ref, lng_ref, lnb_ref, wrh_ref, wrl_ref, br_ref,
                    x1_ref, u2_ref, te_ref, tg_ref):
    y = jnp.dot(a_ref[0], w_ref[...], preferred_element_type=F32)
    z = DEEPNORM_ALPHA * x_ref[0] + (1.0 + g_ref[0]) * y
    x1 = _layer_norm(z, lng_ref[...], lnb_ref[...])
    x1_ref[0] = x1
    u2 = x1 * (1.0 + sc_ref[0]) + sh_ref[0]
    u_hi = u2.astype(BF16)
    u2_ref[0] = u_hi
    u_lo = (u2 - u_hi.astype(F32)).astype(BF16)
    logits = (jnp.dot(u_hi, wrh_ref[...], preferred_element_type=F32)
              + jnp.dot(u_lo, wrh_ref[...], preferred_element_type=F32)
              + jnp.dot(u_hi, wrl_ref[...], preferred_element_type=F32)) + br_ref[...]
    lane = lax.broadcasted_iota(I32, logits.shape, 1).astype(F32)
    vals, idxs = [], []
    cur = logits
    for _ in range(TOP_K):
        mx = jnp.max(cur, axis=-1, keepdims=True)
        idx = jnp.min(jnp.where(cur == mx, lane, float(LANES)), axis=-1, keepdims=True)
        vals.append(mx)
        idxs.append(idx)
        cur = jnp.where(lane == idx, -jnp.inf, cur)
    es = [jnp.exp(v - vals[0]) for v in vals]
    tot = es[0] + es[1] + es[2] + es[3]
    for k in range(TOP_K):
        te_ref[0, :, k:k + 1] = idxs[k].astype(I32)
        tg_ref[0, :, k:k + 1] = es[k] / tot


def _outproj(a, w_bf, x, g, sc, sh, lng, lnb, wr_pad, br_pad):
    b, t, d = x.shape
    tm = _row_tile(t, OUTPROJ_ROW_TILE)
    wr_hi = wr_pad.astype(BF16)
    wr_lo = (wr_pad - wr_hi.astype(F32)).astype(BF16)
    row = pl.BlockSpec((1, tm, d), lambda i, j: (i, j, 0))
    vec = pl.BlockSpec((1, 1, d), lambda i, j: (i, 0, 0))
    par = pl.BlockSpec((1, d), lambda i, j: (0, 0))
    top = pl.BlockSpec((1, tm, TOP_K), lambda i, j: (i, j, 0))
    return pl.pallas_call(
        _outproj_kernel,
        grid=(b, t // tm),
        in_specs=[row, pl.BlockSpec((d, d), lambda i, j: (0, 0)), row, vec, vec, vec, par, par,
                  pl.BlockSpec((d, LANES), lambda i, j: (0, 0)), pl.BlockSpec((d, LANES), lambda i, j: (0, 0)),
                  pl.BlockSpec((1, LANES), lambda i, j: (0, 0))],
        out_specs=[row, row, top, top],
        out_shape=[jax.ShapeDtypeStruct((b, t, d), F32), jax.ShapeDtypeStruct((b, t, d), BF16),
                   jax.ShapeDtypeStruct((b, t, TOP_K), I32), jax.ShapeDtypeStruct((b, t, TOP_K), F32)],
        compiler_params=_params("arbitrary", "arbitrary"),
        name="outproj_ln_router",
    )(a, w_bf, x, g, sc, sh, lng, lnb, wr_hi, wr_lo, br_pad)


MOE_SCATTER_GROUP = 3


def _moe_kernel(cnt_ref, x_ref, te_ref, tg_ref, w1_ref, b1_ref, w2_ref, b2_ref, *rest, ts, nsub, rb, first_rows,
                emit_cast):
    e = pl.program_id(1)
    if emit_cast:
        o_ref, w1o_ref, w2o_ref, pos_ref, tri_ref, sp_ref, sy_ref = rest
        w1o_ref[0] = w1_ref[0].astype(BF16)
        w2o_ref[0] = w2_ref[0].astype(BF16)
        w1_ref, w2_ref = w1o_ref, w2o_ref
    else:
        o_ref, pos_ref, tri_ref, sp_ref, sy_ref = rest

    @pl.when(jnp.logical_and(pl.program_id(0) == 0, e == 0))
    def _():
        si = lax.broadcasted_iota(I32, (ts, ts), 0)
        ti = lax.broadcasted_iota(I32, (ts, ts), 1)
        tri_ref[...] = jnp.where(si <= ti, 1.0, 0.0).astype(BF16)

    @pl.when(e == 0)
    def _():
        o_ref[...] = jnp.zeros_like(o_ref)
        eio = lax.broadcasted_iota(I32, (N_EXPERTS, ts), 0)
        for s in range(nsub):
            sel = jnp.zeros((N_EXPERTS, ts), F32)
            for k in range(TOP_K):
                sel = sel + jnp.where(te_ref[k:k + 1, s * ts:(s + 1) * ts] == eio, 1.0, 0.0)
            rank = jnp.dot(sel.astype(BF16), tri_ref[...], preferred_element_type=F32)
            pos_ref[s] = rank * sel

    slot = pl.multiple_of((e % MOE_SCATTER_GROUP) * rb, rb)
    last = N_EXPERTS - 1

    for s in range(nsub):
        rows = slice(s * ts, (s + 1) * ts)
        pm = pos_ref[s, pl.ds(e, 1), :]
        gate = jnp.zeros((1, ts), F32)
        for k in range(TOP_K):
            gate = gate + jnp.where(te_ref[k:k + 1, rows] == e, tg_ref[k:k + 1, rows], 0.0)
        cnt = cnt_ref[(pl.program_id(0) * nsub + s) * N_EXPERTS + e]
        nblk = (cnt + (rb - 1)) // rb

        def block(blk, nrow=rb, pm=pm, gate=gate, rows=rows):
            r = (blk * rb + 1 + lax.broadcasted_iota(I32, (nrow, 1), 0)).astype(F32)
            hit = pm == r
            onehot = jnp.where(hit, 1.0, 0.0).astype(BF16)
            xg = jnp.dot(onehot, x_ref[rows, :], preferred_element_type=F32).astype(BF16)
            h = jnp.dot(xg, w1_ref[0], preferred_element_type=F32) + b1_ref[0]
            hg = jnp.minimum(h[:, :D_FF], SWIGLU_LIMIT)
            hl = jnp.clip(h[:, D_FF:], -SWIGLU_LIMIT, SWIGLU_LIMIT)
            a = hg * jax.nn.sigmoid(SWIGLU_ALPHA * hg) * (hl + 1.0)
            y = jnp.dot(a.astype(BF16), w2_ref[0], preferred_element_type=F32) + b2_ref[0]
            grow = jnp.sum(jnp.where(hit, gate, 0.0), axis=-1, keepdims=True)
            yg = (y * grow).astype(BF16)
            if nrow < rb:
                onehot = jnp.concatenate([onehot, jnp.zeros((rb - nrow, ts), BF16)], axis=0)
                yg = jnp.concatenate([yg, jnp.zeros((rb - nrow, yg.shape[1]), BF16)], axis=0)
            return onehot, yg

        below = 0
        for nrow in first_rows:
            fits = cnt > below if nrow == rb else jnp.logical_and(cnt > below, cnt <= nrow)

            @pl.when(fits)
            def _(s=s, block=block, nrow=nrow):
                onehot, yg = block(0, nrow)
                sp_ref[s, pl.ds(slot, rb), :] = onehot
                sy_ref[s, pl.ds(slot, rb), :] = yg
            below = nrow

        @pl.when(cnt == 0)
        def _(s=s):
            sp_ref[s, pl.ds(slot, rb), :] = jnp.zeros((rb, ts), BF16)
            sy_ref[s, pl.ds(slot, rb), :] = jnp.zeros((rb, sy_ref.shape[2]), BF16)

        def extra(blk, carry, block=block, rows=rows):
            onehot, yg = block(blk)
            o_ref[rows, :] += lax.dot_general(onehot, yg, TN_DIMS, preferred_element_type=F32)
            return carry

        lax.fori_loop(1, nblk, extra, 0)

        def flush(nrows, s=s, rows=rows):
            o_ref[rows, :] += lax.dot_general(sp_ref[s, :nrows, :], sy_ref[s, :nrows, :], TN_DIMS,
                                              preferred_element_type=F32)

        if last % MOE_SCATTER_GROUP != MOE_SCATTER_GROUP - 1:
            @pl.when(e == last)
            def _(flush=flush):
                flush((last % MOE_SCATTER_GROUP + 1) * rb)

        @pl.when(e % MOE_SCATTER_GROUP == MOE_SCATTER_GROUP - 1)
        def _(flush=flush):
            flush(MOE_SCATTER_GROUP * rb)


def _moe(u2, te, tg, w1, b1, w2, b2, layer, emit_cast=False):
    n, d = u2.shape
    ts = _row_tile(n, MOE_TOKEN_TILE)
    nsub = MOE_SUBTILES if n % (MOE_SUBTILES * ts) == 0 else 1
    tm = ts * nsub
    assert not emit_cast or n == tm
    mean_rows = ts * TOP_K // N_EXPERTS
    rb = min(ts, mean_rows + 32)
    first_rows = tuple(sorted({min(rb, mean_rows), min(rb, mean_rows + 16), rb}))
    te_t = te.T
    tg_t = tg.T
    hits = te.reshape(n // ts, ts * TOP_K)[:, :, None] == jnp.arange(N_EXPERTS, dtype=I32)
    counts = jnp.sum(hits.astype(I32), axis=1).reshape(-1)
    grid_spec = pltpu.PrefetchScalarGridSpec(
        num_scalar_prefetch=1,
        grid=(n // tm, N_EXPERTS),
        in_specs=[
            pl.BlockSpec((tm, d), lambda i, e, c: (i, 0)),
            pl.BlockSpec((TOP_K, tm), lambda i, e, c: (0, i)),
            pl.BlockSpec((TOP_K, tm), lambda i, e, c: (0, i)),
            pl.BlockSpec((None, 1, d, 2 * D_FF), lambda i, e, c: (layer, e, 0, 0)),
            pl.BlockSpec((1, 1, 2 * D_FF), lambda i, e, c: (e, 0, 0)),
            pl.BlockSpec((None, 1, D_FF, d), lambda i, e, c: (layer, e, 0, 0)),
            pl.BlockSpec((1, 1, d), lambda i, e, c: (e, 0, 0)),
        ],
        out_specs=[pl.BlockSpec((tm, d), lambda i, e, c: (i, 0))] + ([
            pl.BlockSpec((1, d, 2 * D_FF), lambda i, e, c: (e, 0, 0)),
            pl.BlockSpec((1, D_FF, d), lambda i, e, c: (e, 0, 0))] if emit_cast else []),
        scratch_shapes=[pltpu.VMEM((nsub, N_EXPERTS, ts), F32), pltpu.VMEM((ts, ts), BF16),
                        pltpu.VMEM((nsub, MOE_SCATTER_GROUP * rb, ts), BF16),
                        pltpu.VMEM((nsub, MOE_SCATTER_GROUP * rb, d), BF16)],
    )
    out_shape = [jax.ShapeDtypeStruct((n, d), F32)] + ([
        jax.ShapeDtypeStruct((N_EXPERTS, d, 2 * D_FF), BF16),
        jax.ShapeDtypeStruct((N_EXPERTS, D_FF, d), BF16)] if emit_cast else [])
    outs = pl.pallas_call(
        functools.partial(_moe_kernel, ts=ts, nsub=nsub, rb=rb, first_rows=first_rows, emit_cast=emit_cast),
        grid_spec=grid_spec,
        out_shape=out_shape,
        compiler_params=_params("arbitrary", "arbitrary"),
        name="moe_experts_cast" if emit_cast else "moe_experts",
    )(counts, u2, te_t, tg_t, w1, b1.reshape(N_EXPERTS, 1, -1), w2, b2.reshape(N_EXPERTS, 1, -1))
    return outs if emit_cast else outs[0]


def _rope_tables(pos, head_dim):
    rd = head_dim // ROPE_FRACTION
    half = rd // 2
    inv = jnp.power(ROPE_THETA, -jnp.arange(half, dtype=F32) / half)
    ang = jnp.asarray(pos).astype(F32)[:, None] * inv[None, :]
    cos, sin = jnp.cos(ang), jnp.sin(ang)
    t = pos.shape[0]
    ones = jnp.ones((t, head_dim - rd), F32)
    zeros = jnp.zeros((t, head_dim - rd), F32)
    zh = jnp.zeros((t, half), F32)
    c = jnp.concatenate([cos, cos, ones], axis=-1)
    sa = jnp.concatenate([zh, sin, zeros], axis=-1)
    sb = jnp.concatenate([-sin, zh, zeros], axis=-1)
    rep = LANES // head_dim
    return tuple(jnp.tile(a, (1, rep)) for a in (c, sa, sb))


def _rope(z, c, sa, sb, half):
    w = z.shape[-1]
    rep = w // LANES
    if rep > 1:
        c, sa, sb = (jnp.concatenate([a] * rep, axis=-1) for a in (c, sa, sb))
    return z * c + pltpu.roll(z, half, 1) * sa + pltpu.roll(z, w - half, 1) * sb


def _dsa_inproj_kernel(x_ref, f_ref, g_ref, lng_ref, lnb_ref, sc_ref, sh_ref, w_in_ref,
                       c1_ref, sa1_ref, sb1_ref, c2_ref, sa2_ref, sb2_ref,
                       xo_ref, q_ref, kf_ref, vf_ref, kif_ref, kb_ref, vb_ref, kib_ref, qi_ref, w_ref):
    z = DEEPNORM_ALPHA * x_ref[0] + (1.0 + g_ref[0]) * f_ref[0]
    x = _layer_norm(z, lng_ref[...], lnb_ref[...])
    xo_ref[0] = x
    u = x * (1.0 + sc_ref[0]) + sh_ref[0]
    p = jnp.dot(u.astype(BF16), w_in_ref[...], preferred_element_type=F32)

    c1, sa1, sb1 = c1_ref[...], sa1_ref[...], sb1_ref[...]
    c2, sa2, sb2 = c2_ref[...], sa2_ref[...], sb2_ref[...]
    h1 = DSA_HD // ROPE_FRACTION // 2
    h2 = IDX_DIM // ROPE_FRACTION // 2
    o_k = DSA_Q
    o_v = DSA_Q + DSA_KV
    o_qi = DSA_Q + 2 * DSA_KV
    o_ki = o_qi + IDX_HEADS * IDX_DIM
    q = _rope(p[:, :DSA_Q], c1, sa1, sb1, h1)
    q_ref[0] = (q * DSA_LOG2_SCALE).astype(BF16)
    k = _rope(p[:, o_k:o_v], c1, sa1, sb1, h1)
    kf_ref[0] = k
    kb_ref[0] = k.astype(BF16)
    v = p[:, o_v:o_qi]
    vf_ref[0] = v
    vb_ref[0] = v.astype(BF16)
    qi = _rope(p[:, o_qi:o_ki], c2, sa2, sb2, h2)
    qi_ref[0] = qi.astype(BF16)
    tail = p[:, o_ki:o_ki + LANES]
    ki = _rope(tail, c2, sa2, sb2, h2)[:, :IDX_DIM]
    kif_ref[0] = ki
    kib_ref[0] = ki.astype(BF16)
    wi = tail[:, IDX_DIM:IDX_DIM + IDX_HEADS]
    w_ref[0] = (wi * (IDX_HEADS ** -0.5)) * (IDX_DIM ** -0.5)


def _dsa_inproj(x, f, g, lng, lnb, sc, sh, w_bf, pos):
    b, t, d = x.shape
    n = w_bf.shape[1]
    tm = _row_tile(t, PROJ_ROW_TILE)
    tabs = _rope_tables(pos, DSA_HD) + _rope_tables(pos, IDX_DIM)
    tab = pl.BlockSpec((tm, LANES), lambda i, j: (j, 0))
    vec = pl.BlockSpec((1, 1, d), lambda i, j: (i, 0, 0))
    par = pl.BlockSpec((1, d), lambda i, j: (0, 0))

    def row(w):
        return pl.BlockSpec((1, tm, w), lambda i, j: (i, j, 0))

    widths = [(d, F32), (DSA_Q, BF16), (DSA_KV, F32), (DSA_KV, F32), (IDX_DIM, F32), (DSA_KV, BF16), (DSA_KV, BF16),
              (IDX_DIM, BF16), (IDX_HEADS * IDX_DIM, BF16), (IDX_HEADS, F32)]
    return pl.pallas_call(
        _dsa_inproj_kernel,
        grid=(b, t // tm),
        in_specs=[row(d), row(d), vec, par, par, vec, vec, pl.BlockSpec((d, n), lambda i, j: (0, 0))] + [tab] * 6,
        out_specs=[row(w) for w, _ in widths],
        out_shape=[jax.ShapeDtypeStruct((b, t, w), dt) for w, dt in widths],
        compiler_params=_params("arbitrary", "arbitrary"),
        name="ln_inproj_rope_split",
    )(x, f, g, lng, lnb, sc, sh, w_bf, *tabs)


def _f2key(f):
    bits = lax.bitcast_convert_type(f, I32)
    return bits ^ ((bits >> 31) & 0x7FFFFFFF)


def _key2f(k):
    return lax.bitcast_convert_type(k ^ ((k >> 31) & 0x7FFFFFFF), F32)


def _dsa_kernel(nkt_ref, q_ref, qi_ref, w_ref, k_ref, v_ref, ki_ref, kpos_ref, qlim_ref, nadm_ref, o_ref,
                s_ref, wb_ref, qg_ref, m_ref, l_ref, acc_ref, flag_ref, *, tq, tk, nsel, value_iters):
    nkt = nkt_ref[pl.program_id(1)]
    qlim = qlim_ref[...]
    w = w_ref[0]
    qi = qi_ref[0]
    qis_all = jnp.concatenate([qi[:, h * IDX_DIM:(h + 1) * IDX_DIM] for h in range(IDX_HEADS)], axis=0)
    nlg = tk // LANES
    for h in range(IDX_HEADS):
        wb_ref[h] = jnp.broadcast_to(w[:, h:h + 1], (tq, LANES))

    def p1(kt, carry):
        rmax, rmin, cgt0, cge0 = carry
        off = pl.multiple_of(kt * tk, tk)
        kit = ki_ref[0, pl.ds(off, tk), :]
        acc = jnp.zeros((tq, tk), F32)
        d_all = lax.dot_general(qis_all, kit, NT_DIMS, preferred_element_type=F32)
        for h in range(IDX_HEADS):
            d = d_all[h * tq:(h + 1) * tq]
            acc = acc + jnp.maximum(d, 0.0) * jnp.concatenate([wb_ref[h]] * nlg, axis=1)
        s = jnp.where(kpos_ref[kt] < qlim, acc, -jnp.inf)
        s_ref[kt] = s
        gt0 = jnp.where(s > 0.0, 1.0, 0.0)
        ge0 = jnp.where(s >= 0.0, 1.0, 0.0)
        for g in range(nlg):
            lanes = slice(g * LANES, (g + 1) * LANES)
            rmax = jnp.maximum(rmax, s[:, lanes])
            rmin = jnp.minimum(rmin, acc[:, lanes])
            cgt0 = cgt0 + gt0[:, lanes]
            cge0 = cge0 + ge0[:, lanes]
        return rmax, rmin, cgt0, cge0

    zcol = jnp.zeros((tq, 1), F32)
    zlan = jnp.zeros((tq, LANES), F32)
    init = (jnp.full((tq, LANES), -jnp.inf, F32), jnp.full((tq, LANES), jnp.inf, F32), zlan, zlan)
    rmax, rmin, cgt0, cge0 = lax.fori_loop(0, nkt, p1, init)
    rmax = jnp.max(rmax, axis=-1, keepdims=True)
    rmin = jnp.min(rmin, axis=-1, keepdims=True)
    cgt0 = jnp.sum(cgt0, axis=-1, keepdims=True)
    cge0 = jnp.sum(cge0, axis=-1, keepdims=True)
    nadm = nadm_ref[...]

    rg = min(tq, 32)
    fsel = float(nsel)
    unbounded = float(2 ** 30)

    def count_ge(tb):
        def body(kt, c):
            out = []
            for r in range(tq // rg):
                tr = jnp.concatenate([tb[r * rg:(r + 1) * rg, :]] * nlg, axis=1)
                hit = jnp.where(s_ref[kt, r * rg:(r + 1) * rg, :] >= tr, 1.0, 0.0)
                cr = c[r * rg:(r + 1) * rg]
                for g in range(nlg):
                    cr = cr + hit[:, g * LANES:(g + 1) * LANES]
                out.append(cr)
            return jnp.concatenate(out, axis=0)
        c = lax.fori_loop(0, nkt, body, zlan)
        return jnp.broadcast_to(jnp.sum(c, axis=-1, keepdims=True), (tq, LANES))

    rmax, rmin, cgt0, cge0, nadm = (jnp.broadcast_to(a, (tq, LANES)) for a in (rmax, rmin, cgt0, cge0, nadm))
    few = nadm <= fsel
    many = jnp.logical_not(few)
    pos = jnp.logical_and(many, cgt0 > fsel)
    neg = jnp.logical_and(many, cge0 < fsel)
    zero = jnp.logical_and(many, jnp.logical_and(cgt0 <= fsel, cge0 >= fsel))
    lo0 = jnp.where(pos, 0, _f2key(jnp.where(neg, rmin, 0.0)))
    hi0 = jnp.where(pos, _f2key(jnp.where(pos, rmax, 0.0)) + 1, 0)
    clo0 = jnp.where(pos, cge0, nadm)
    chi0 = jnp.where(pos, 0.0, cge0)
    act0 = jnp.where(jnp.logical_or(pos, neg), 1.0, 0.0)
    t0 = jnp.where(few, -jnp.inf, 0.0)
    need0 = jnp.where(zero, fsel - cgt0, unbounded)

    def cond(st):
        return st[0] > 0.0

    def step(st):
        _, it, lo, hi, clo, chi, act, thr, need = st
        conv = jnp.logical_and(act > 0.0, lo + 1 >= hi)
        thr = jnp.where(conv, _key2f(lo), thr)
        need = jnp.where(conv, fsel - chi, need)
        act = jnp.where(conv, 0.0, act)
        flo, fhi = _key2f(lo), _key2f(hi)
        mid_i = (lo >> 1) + (hi >> 1) + (lo & hi & 1)
        lclo = jnp.log(clo)
        interp = (lclo - float(np.log(nsel - 0.5))) / (lclo - jnp.log(jnp.maximum(chi, 0.5)))
        frac = jnp.where(it % 3 == 2, 0.5, interp)
        mid_v = _f2key(flo + frac * (fhi - flo))
        use_v = jnp.logical_and(it < value_iters, jnp.logical_and(mid_v > lo, mid_v < hi))
        mid = jnp.where(use_v, mid_v, mid_i)
        tm = _key2f(mid)
        c = count_ge(jnp.where(act > 0.0, tm, jnp.inf))
        hit = jnp.logical_and(act > 0.0, c == fsel)
        thr = jnp.where(hit, tm, thr)
        up = jnp.logical_and(act > 0.0, c > fsel)
        dn = jnp.logical_and(act > 0.0, c < fsel)
        lo = jnp.where(up, mid, lo)
        clo = jnp.where(up, c, clo)
        hi = jnp.where(dn, mid, hi)
        chi = jnp.where(dn, c, chi)
        act = jnp.where(hit, 0.0, act)
        return jnp.max(act), it + 1, lo, hi, clo, chi, act, thr, need

    st = (jnp.max(act0), jnp.int32(0), lo0, hi0, clo0, chi0, act0, t0, need0)
    _, _, _, _, _, _, _, thr, need = lax.while_loop(cond, step, st)
    thr, need = thr[:, 0:1], need[:, 0:1]
    tied = need < unbounded * 0.5
    has_ties = jnp.max(jnp.where(tied, 1.0, 0.0)) > 0.0
    thr = jnp.maximum(thr, -F32_MAX)

    for g in range(DSA_KV_HEADS):
        for hh in range(DSA_GROUP):
            hd = g * DSA_GROUP + hh
            qg_ref[g, hh * tq:(hh + 1) * tq, :] = q_ref[0, :, hd * DSA_HD:(hd + 1) * DSA_HD]
    m_ref[...] = jnp.full(m_ref.shape, NEG_BIG, F32)
    l_ref[...] = jnp.zeros(l_ref.shape, F32)
    acc_ref[...] = jnp.zeros(acc_ref.shape, F32)

    ones_cols = jnp.ones((tk, DSA_HD), BF16)

    def attend(kt, bias):
        off = pl.multiple_of(kt * tk, tk)
        qk = [lax.dot_general(qg_ref[g], k_ref[0, pl.ds(off, tk), g * DSA_HD:(g + 1) * DSA_HD], NT_DIMS,
                              preferred_element_type=F32) for g in range(DSA_KV_HEADS)]
        for g in range(DSA_KV_HEADS):
            vt_g = jnp.concatenate([v_ref[0, pl.ds(off, tk), g * DSA_HD:(g + 1) * DSA_HD], ones_cols], axis=1)
            for hh in range(DSA_GROUP):
                rows = slice(hh * tq, (hh + 1) * tq)
                lg = qk[g][rows] + bias
                m_old = m_ref[g, rows, :]
                m_new = jnp.maximum(m_old, jnp.max(lg, axis=-1, keepdims=True))
                alpha = jnp.exp2(m_old - m_new)
                p = jnp.exp2(lg - jnp.concatenate([m_new] * nlg, axis=1))
                pv = jnp.dot(p.astype(BF16), vt_g, preferred_element_type=F32)
                acc_ref[g, rows, :] = alpha * acc_ref[g, rows, :] + pv[:, :DSA_HD]
                l_ref[g, rows, :] = alpha * l_ref[g, rows, :] + pv[:, DSA_HD:]
                m_ref[g, rows, :] = m_new

    @pl.when(jnp.logical_not(has_ties))
    def _():
        def body(kt, carry):
            attend(kt, jnp.where(s_ref[kt] >= thr, 0.0, NEG_BIG))
            return carry
        lax.fori_loop(0, nkt, body, 0)

    @pl.when(has_ties)
    def _():
        def scan(kt, carry):
            seen, cut, nin = carry
            eqc = jnp.sum(jnp.where(s_ref[kt] == thr, 1.0, 0.0), axis=-1, keepdims=True)
            here = jnp.logical_and(tied, jnp.logical_and(seen < need, seen + eqc >= need))
            cut = jnp.where(here, kt, cut)
            nin = jnp.where(here, need - seen, nin)
            flag_ref[kt] = jnp.max(jnp.where(here, 1.0, 0.0)).astype(I32)
            return seen + eqc, cut, nin

        cut0 = jnp.where(tied, -1, nkt).astype(I32)
        _, cut, nin = lax.fori_loop(0, nkt, scan, (zcol, cut0, zcol))

        def body(kt, carry):
            s = s_ref[kt]
            before = jnp.where(kt < cut, 0.0, NEG_BIG)

            def plain():
                return jnp.where(s == thr, before, NEG_BIG)

            def ranked():
                si = lax.broadcasted_iota(I32, (tk, tk), 0)
                ti = lax.broadcasted_iota(I32, (tk, tk), 1)
                triu = jnp.where(si <= ti, 1.0, 0.0).astype(BF16)
                eq = jnp.where(s == thr, 1.0, 0.0).astype(BF16)
                rank = jnp.dot(eq, triu, preferred_element_type=F32)
                inside = jnp.where(rank <= nin, 0.0, NEG_BIG)
                return jnp.where(s == thr, jnp.where(kt == cut, inside, before), NEG_BIG)

            tie_bias = lax.cond(flag_ref[kt] > 0, ranked, plain)
            attend(kt, jnp.where(s > thr, 0.0, tie_bias))
            return carry
        lax.fori_loop(0, nkt, body, 0)

    for g in range(DSA_KV_HEADS):
        og = acc_ref[g] / l_ref[g]
        for hh in range(DSA_GROUP):
            hd = g * DSA_GROUP + hh
            o_ref[0, :, hd * DSA_HD:(hd + 1) * DSA_HD] = og[hh * tq:(hh + 1) * tq, :].astype(BF16)


def _dsa_attend(q, qi, w, k, v, ki, kpos, qlim, tq, tk):
    b, t, _ = q.shape
    lp = k.shape[1]
    nq = t // tq
    nsel = min(DSA_TOPK, int(np.sum(kpos < np.iinfo(np.int32).max)) // 4)
    qmax = qlim.reshape(nq, tq).max(axis=1)
    nadm = np.searchsorted(kpos, qlim, side="left")
    nkt = np.minimum(-(-np.searchsorted(kpos, qmax, side="left") // tk), lp // tk).astype(np.int32)
    nkt = np.maximum(nkt, 1)
    single = pl.Buffered(1)

    def full(wd):
        return pl.BlockSpec((1, lp, wd), lambda i, j, n: (i, 0, 0), pipeline_mode=single)

    def row(wd):
        return pl.BlockSpec((1, tq, wd), lambda i, j, n: (i, j, 0))

    grid_spec = pltpu.PrefetchScalarGridSpec(
        num_scalar_prefetch=1,
        grid=(b, nq),
        in_specs=[row(DSA_Q), row(IDX_HEADS * IDX_DIM), row(IDX_HEADS), full(DSA_KV), full(DSA_KV), full(IDX_DIM),
                  pl.BlockSpec((lp // tk, 1, tk), lambda i, j, n: (0, 0, 0)),
                  pl.BlockSpec((tq, 1), lambda i, j, n: (j, 0)),
                  pl.BlockSpec((tq, 1), lambda i, j, n: (j, 0))],
        out_specs=row(DSA_Q),
        scratch_shapes=[
            pltpu.VMEM((lp // tk, tq, tk), F32),
            pltpu.VMEM((IDX_HEADS, tq, LANES), F32),
            pltpu.VMEM((DSA_KV_HEADS, DSA_GROUP * tq, DSA_HD), BF16),
            pltpu.VMEM((DSA_KV_HEADS, DSA_GROUP * tq, LANES), F32),
            pltpu.VMEM((DSA_KV_HEADS, DSA_GROUP * tq, LANES), F32),
            pltpu.VMEM((DSA_KV_HEADS, DSA_GROUP * tq, DSA_HD), F32),
            pltpu.SMEM((lp // tk,), I32),
        ],
    )
    return pl.pallas_call(
        functools.partial(_dsa_kernel, tq=tq, tk=tk, nsel=nsel, value_iters=DSA_VALUE_ITERS),
        grid_spec=grid_spec,
        out_shape=jax.ShapeDtypeStruct((b, t, DSA_Q), BF16),
        compiler_params=_params("arbitrary", "arbitrary"),
        name="dsa_select_attend",
    )(jnp.asarray(nkt), q, qi, w, k, v, ki, jnp.asarray(kpos, I32).reshape(lp // tk, 1, tk),
      jnp.asarray(qlim, I32).reshape(t, 1), jnp.asarray(nadm, F32).reshape(t, 1))


def _trunk(x, mods, pos, ml_state, ds_cache, prm, moe_bf=None):
    b, t, d = x.shape
    casts = []

    def moe(i, u2, te, tg):
        args = (u2.reshape(b * t, d), te.reshape(b * t, TOP_K), tg.reshape(b * t, TOP_K))
        if moe_bf is None:
            f, w1b, w2b = _moe(*args, prm["moe_w1"], prm["moe_b1"][i], prm["moe_w2"], prm["moe_b2"][i], i,
                               emit_cast=True)
            casts.append((w1b, w2b))
        else:
            w1b, w2b = moe_bf[i]
            f = _moe(*args, w1b[None], prm["moe_b1"][i], w2b[None], prm["moe_b2"][i], 0)
        return f.reshape(b, t, d)

    chunk = next((c for c in (ML_EVAL_CHUNK, CHUNK) if t % c == 0), t)

    def mod(i, k):
        return mods[i][:, None, k * d:(k + 1) * d]

    p, pg = _ml_inproj(x, mod(0, 1), mod(0, 0), prm["ml_w_in"])
    if ml_state is None:
        c0 = jnp.zeros((b, ML_HEADS, ML_DV, ML_DK), F32)
        n0 = jnp.zeros((b, ML_HEADS, ML_DK), F32)
        m0 = jnp.zeros((b, ML_HEADS), F32)
    else:
        c0, n0, m0 = (s[0] for s in ml_state)
    hg, c_new, n_new, m_new = _mlstm(p, pg, prm["ml_b_gates"], prm["ml_norm_g"], c0, n0, m0, chunk)
    x1, u2, te, tg = _outproj(hg, prm["ml_w_out"], x, mod(0, 2), mod(0, 4), mod(0, 3),
                              prm["ln_g"][0, 0:1], prm["ln_b"][0, 0:1], prm["w_router"][0], prm["b_router"][0])
    f = moe(0, u2, te, tg)

    x2, q, kf, vf, kif, kb, vb, kib, qi, w = _dsa_inproj(
        x1, f, mod(0, 5), prm["ln_g"][0, 1:2], prm["ln_b"][0, 1:2], mod(1, 1), mod(1, 0), prm["ds_w_in"], pos)
    pos_np = np.asarray(pos)
    qlim = (pos_np // CHUNK + 1) * CHUNK
    if ds_cache is None:
        kpos = pos_np
        tq, tk = DSA_Q_TILE, DSA_K_TILE
        if t % tk:
            tq, tk = t, t
        k_all, v_all, ki_all = kb, vb, kib
    else:
        ck, cv, cki = ds_cache
        past = ck.shape[1]
        total = past + t
        lp = -(-total // (3 * LANES)) * (3 * LANES)
        tq, tk = t, 3 * LANES
        kpos = np.concatenate([np.arange(past), pos_np, np.full((lp - total,), np.iinfo(np.int32).max)])

        def cat(cache, new):
            full = jnp.concatenate([cache.reshape(b, past, -1).astype(BF16), new], axis=1)
            return jnp.pad(full, ((0, 0), (0, lp - total), (0, 0)))

        k_all, v_all, ki_all = cat(ck, kb), cat(cv, vb), cat(cki, kib)
    o = _dsa_attend(q, qi, w, k_all, v_all, ki_all, kpos.astype(np.int64), qlim, tq, tk)
    x3, u2, te, tg = _outproj(o, prm["ds_w_out"], x2, mod(1, 2), mod(1, 4), mod(1, 3),
                              prm["ln_g"][1, 0:1], prm["ln_b"][1, 0:1], prm["w_router"][1], prm["b_router"][1])
    f = moe(1, u2, te, tg)
    y = _final_ln(x3, f, mod(1, 5), prm["ln_g"][1, 1:2], prm["ln_b"][1, 1:2])

    ml_out = (c_new[None], n_new[None], m_new[None])
    ds_out = (kf.reshape(1, b, t, DSA_KV_HEADS, DSA_HD), vf.reshape(1, b, t, DSA_KV_HEADS, DSA_HD), kif[None])
    return y, ml_out, ds_out, casts


def _prepare(w_ada, b_ada, ln_g, ln_b, ml_w_in, ml_b_gates, ml_norm_g, ml_w_out, ds_w_in, ds_w_out,
             moe_w_router, moe_b_router, moe_w1, moe_b1, moe_w2, moe_b2):
    return {
        "ln_g": ln_g, "ln_b": ln_b,
        "ml_w_in": jnp.pad(ml_w_in[0], ((0, 0), (0, ML_IN_PAD - ML_IN))).astype(BF16),
        "ml_b_gates": ml_b_gates[0], "ml_norm_g": ml_norm_g[0],
        "ml_w_out": ml_w_out[0].astype(BF16),
        "ds_w_in": jnp.pad(ds_w_in[0], ((0, 0), (0, DSA_IN_PAD - DSA_IN))).astype(BF16),
        "ds_w_out": ds_w_out[0].astype(BF16),
        "w_router": jnp.pad(moe_w_router, ((0, 0), (0, 0), (0, LANES - N_EXPERTS))),
        "b_router": jnp.pad(moe_b_router, ((0, 0), (0, LANES - N_EXPERTS)), constant_values=-jnp.inf)[:, None, :],
        "moe_w1": moe_w1, "moe_b1": moe_b1, "moe_w2": moe_w2, "moe_b2": moe_b2,
    }


def kernel(x_prompt, x_sample, state_mlstm_C, state_mlstm_n, state_mlstm_m, cache_dsa_k, cache_dsa_v,
           cache_dsa_kidx, c_prompt, c_sample, w_ada, b_ada, ln_g, ln_b, ml_w_in, ml_b_gates, ml_norm_g,
           ml_w_out, ds_w_in, ds_w_out, moe_w_router, moe_b_router, moe_w1, moe_b1, moe_w2, moe_b2):
    prm = _prepare(w_ada, b_ada, ln_g, ln_b, ml_w_in, ml_b_gates, ml_norm_g, ml_w_out, ds_w_in, ds_w_out,
                   moe_w_router, moe_b_router, moe_w1, moe_b1, moe_w2, moe_b2)
    bp, bs = c_prompt.shape[0], c_sample.shape[0]
    rows = -(-(bp + bs) // 8) * 8
    c_rows = jnp.pad(jnp.concatenate([c_prompt, c_sample], axis=0), ((0, rows - bp - bs), (0, 0)))
    mods = _ada(c_rows, w_ada, b_ada)
    mods_p = [mods[i, :bp] for i in range(DEPTH)]
    mods_s = [mods[i, bp:bp + bs] for i in range(DEPTH)]

    past = cache_dsa_k.shape[2]
    pos_s = past + np.arange(x_sample.shape[1], dtype=np.int32)
    y_s, ml_s, ds_s, moe_bf = _trunk(x_sample, mods_s, pos_s, (state_mlstm_C, state_mlstm_n, state_mlstm_m),
                                     (cache_dsa_k[0], cache_dsa_v[0], cache_dsa_kidx[0]), prm)
    pos_p = np.arange(x_prompt.shape[1], dtype=np.int32)
    y_p, ml_p, ds_p, _ = _trunk(x_prompt, mods_p, pos_p, None, None, prm, moe_bf=moe_bf)
    return (y_p, y_s, ml_p[0], ml_p[1], ml_p[2], ds_p[0], ds_p[1], ds_p[2],
            ml_s[0], ml_s[1], ml_s[2], ds_s[0], ds_s[1], ds_s[2])
```

```python
import functools

import numpy as np
import jax
import jax.numpy as jnp
from jax import lax
from jax.experimental import pallas as pl
from jax.experimental.pallas import tpu as pltpu

F32 = jnp.float32
BF16 = jnp.bfloat16
I32 = jnp.int32

D_MODEL = 1024
DEPTH = 2
CHUNK = 64
ML_HEADS = 4
ML_DV = D_MODEL // ML_HEADS
ML_DK = ML_DV // 2
ML_QK = ML_HEADS * ML_DK
ML_V = ML_HEADS * ML_DV
ML_IN = 2 * ML_QK + 2 * ML_V + 2 * ML_HEADS
ML_GATE_OFF = 2 * ML_QK + 2 * ML_V
ML_EVAL_CHUNK = 256
DSA_HEADS = 8
DSA_KV_HEADS = 2
DSA_HD = D_MODEL // DSA_HEADS
DSA_GROUP = DSA_HEADS // DSA_KV_HEADS
IDX_HEADS = 8
IDX_DIM = 64
DSA_TOPK = 256
DSA_Q = DSA_HEADS * DSA_HD
DSA_KV = DSA_KV_HEADS * DSA_HD
DSA_IN = DSA_Q + 2 * DSA_KV + IDX_HEADS * IDX_DIM + IDX_DIM + IDX_HEADS
DSA_LOG2_SCALE = (DSA_HD ** -0.5) * float(np.log2(np.e))
ROPE_THETA = 500000.0
ROPE_FRACTION = 4
N_EXPERTS = 32
TOP_K = 4
D_FF = D_MODEL
SWIGLU_LIMIT = 7.0
SWIGLU_ALPHA = 1.702
DEEPNORM_ALPHA = (2 * DEPTH) ** 0.25
LN_EPS = 1e-5

LANES = 128
V7X_VMEM_BYTES = 64 * 1024 * 1024
VMEM_LIMIT = V7X_VMEM_BYTES - 8 * 1024 * 1024

PROJ_ROW_TILE = 512
OUTPROJ_ROW_TILE = 256
FINAL_LN_ROW_TILE = 1024
ADA_COL_TILE = 1536
MOE_TOKEN_TILE = 1024
MOE_SUBTILES = 4
DSA_Q_TILE = 128
DSA_K_TILE = 1024
DSA_VALUE_ITERS = 16

ML_IN_PAD = -(-ML_IN // LANES) * LANES
DSA_IN_PAD = -(-DSA_IN // LANES) * LANES
NEG_BIG = -1e30
F32_MAX = float(np.finfo(np.float32).max)
HIGHEST = lax.Precision.HIGHEST
NT_DIMS = (((1,), (1,)), ((), ()))
TN_DIMS = (((0,), (0,)), ((), ()))


def _params(*sem):
    return pltpu.CompilerParams(dimension_semantics=sem, vmem_limit_bytes=VMEM_LIMIT)


def _row_tile(t, pref):
    return pref if t % pref == 0 else t


def _ada_kernel(c_ref, w_ref, b_ref, o_ref):
    c = c_ref[...]
    cond = (c * jax.nn.sigmoid(c)).astype(BF16)
    o_ref[0] = jnp.dot(cond, w_ref[0].astype(BF16), preferred_element_type=F32) + b_ref[0]


def _ada(c_rows, w_ada, b_ada):
    rows = c_rows.shape[0]
    n = w_ada.shape[-1]
    tn = ADA_COL_TILE
    return pl.pallas_call(
        _ada_kernel,
        grid=(DEPTH, n // tn),
        in_specs=[
            pl.BlockSpec((rows, D_MODEL), lambda i, j: (0, 0)),
            pl.BlockSpec((1, D_MODEL, tn), lambda i, j: (i, 0, j)),
            pl.BlockSpec((1, 1, tn), lambda i, j: (i, 0, j)),
        ],
        out_specs=pl.BlockSpec((1, rows, tn), lambda i, j: (i, 0, j)),
        out_shape=jax.ShapeDtypeStruct((DEPTH, rows, n), F32),
        compiler_params=_params("arbitrary", "arbitrary"),
        name="ada_mod",
    )(c_rows, w_ada, b_ada.reshape(DEPTH, 1, n))


def _layer_norm(z, g, b):
    mu = jnp.mean(z, axis=-1, keepdims=True)
    zc = z - mu
    var = jnp.mean(zc * zc, axis=-1, keepdims=True)
    return zc * lax.rsqrt(var + LN_EPS) * g + b


def _ml_inproj_kernel(x_ref, sc_ref, sh_ref, w_ref, pm_ref, pg_ref):
    u = x_ref[0] * (1.0 + sc_ref[0]) + sh_ref[0]
    p = jnp.dot(u.astype(BF16), w_ref[...], preferred_element_type=F32)
    pm_ref[0, :, :ML_QK] = (p[:, :ML_QK] * (ML_DK ** -0.5)).astype(BF16)
    pm_ref[0, :, ML_QK:] = p[:, ML_QK:ML_GATE_OFF].astype(BF16)
    pg_ref[0] = p[:, ML_GATE_OFF:]


def _ml_inproj(x, sc, sh, w_bf):
    b, t, d = x.shape
    n = w_bf.shape[1]
    tm = _row_tile(t, PROJ_ROW_TILE)
    row = pl.BlockSpec((1, tm, d), lambda i, j: (i, j, 0))
    vec = pl.BlockSpec((1, 1, d), lambda i, j: (i, 0, 0))
    return pl.pallas_call(
        _ml_inproj_kernel,
        grid=(b, t // tm),
        in_specs=[row, vec, vec, pl.BlockSpec((d, n), lambda i, j: (0, 0))],
        out_specs=[pl.BlockSpec((1, tm, ML_GATE_OFF), lambda i, j: (i, j, 0)),
                   pl.BlockSpec((1, tm, n - ML_GATE_OFF), lambda i, j: (i, j, 0))],
        out_shape=[jax.ShapeDtypeStruct((b, t, ML_GATE_OFF), BF16),
                   jax.ShapeDtypeStruct((b, t, n - ML_GATE_OFF), F32)],
        compiler_params=_params("arbitrary", "arbitrary"),
        name="ml_inproj",
    )(x, sc, sh, w_bf)


def _final_ln_kernel(x_ref, f_ref, g_ref, lng_ref, lnb_ref, o_ref):
    z = DEEPNORM_ALPHA * x_ref[0] + (1.0 + g_ref[0]) * f_ref[0]
    o_ref[0] = _layer_norm(z, lng_ref[...], lnb_ref[...])


def _final_ln(x, f, g, lng, lnb):
    b, t, d = x.shape
    tm = _row_tile(t, FINAL_LN_ROW_TILE)
    row = pl.BlockSpec((1, tm, d), lambda i, j: (i, j, 0))
    vec = pl.BlockSpec((1, 1, d), lambda i, j: (i, 0, 0))
    par = pl.BlockSpec((1, d), lambda i, j: (0, 0))
    return pl.pallas_call(
        _final_ln_kernel,
        grid=(b, t // tm),
        in_specs=[row, row, vec, par, par],
        out_specs=row,
        out_shape=jax.ShapeDtypeStruct((b, t, d), F32),
        compiler_params=_params("arbitrary", "arbitrary"),
        name="final_ln",
    )(x, f, g, lng, lnb)


def _log_sigmoid(x):
    return jnp.minimum(x, 0.0) - jnp.log1p(jnp.exp(-jnp.abs(x)))


def _mlstm_kernel(p_ref, pg_ref, gt_ref, bcol_ref, brow_ref, ng_ref, c0_ref, n0_ref, m0_ref,
                  hg_ref, c_ref, n_ref, m_ref, *, chunk):
    L = chunk

    @pl.when(pl.program_id(1) == 0)
    def _():
        c_ref[...] = c0_ref[...]
        n_ref[...] = n0_ref[...]
        m_ref[...] = m0_ref[...]

    gcol = pg_ref[0, :, :2 * ML_HEADS] + bcol_ref[...]
    grow = gt_ref[0] + brow_ref[...]
    lf_col = _log_sigmoid(gcol)
    lf_row = _log_sigmoid(grow)
    ti = lax.broadcasted_iota(I32, (L, L), 0)
    si = lax.broadcasted_iota(I32, (L, L), 1)
    causal = si <= ti
    tril = jnp.where(causal, 1.0, 0.0).astype(F32)
    triu = jnp.where(ti <= si, 1.0, 0.0).astype(F32)
    b_col = jnp.dot(tril, lf_col, precision=HIGHEST, preferred_element_type=F32)
    b_row = jnp.dot(lf_row, triu, precision=HIGHEST, preferred_element_type=F32)

    for h in range(ML_HEADS):
        qs = p_ref[0, :, h * ML_DK:(h + 1) * ML_DK]
        kb = p_ref[0, :, ML_QK + h * ML_DK:ML_QK + (h + 1) * ML_DK]
        vb = p_ref[0, :, 2 * ML_QK + h * ML_DV:2 * ML_QK + (h + 1) * ML_DV]
        v = vb.astype(F32)
        o = p_ref[0, :, 2 * ML_QK + ML_V + h * ML_DV:2 * ML_QK + ML_V + (h + 1) * ML_DV].astype(F32)
        bc = b_col[:, ML_HEADS + h:ML_HEADS + h + 1]
        ic = gcol[:, h:h + 1]
        br = b_row[ML_HEADS + h:ML_HEADS + h + 1, :]
        ir = grow[h:h + 1, :]
        c_old = c_ref[0, h]
        n_old = n_ref[0, h:h + 1, :]
        m_old = m_ref[0, h:h + 1, 0:1]

        dm = jnp.where(causal, bc - br + ir, -jnp.inf)
        inter = bc + m_old
        mt = jnp.maximum(inter, jnp.max(dm, axis=-1, keepdims=True))
        qk = lax.dot_general(qs, kb, NT_DIMS, preferred_element_type=F32)
        s = jnp.exp(dm - mt) * qk
        wp = jnp.exp(inter - mt)
        qc = lax.dot_general(qs, c_old.astype(BF16), NT_DIMS, preferred_element_type=F32)
        num = jnp.dot(s.astype(BF16), vb, preferred_element_type=F32) + wp * qc
        qn = jnp.sum(qs.astype(F32) * n_old, axis=-1, keepdims=True)
        den = jnp.sum(s, axis=-1, keepdims=True) + wp * qn
        hh = num / jnp.maximum(jnp.abs(den), jnp.exp(-mt))

        m_new = mt[L - 1:L, :]
        wk = jnp.exp(bc[L - 1:L, :] - bc + ic - m_new)
        wprev = jnp.exp(inter[L - 1:L, :] - m_new)
        vw = (v * wk).astype(BF16)
        c_ref[0, h] = wprev * c_old + lax.dot_general(vw, kb, TN_DIMS, preferred_element_type=F32)
        n_ref[0, h:h + 1, :] = wprev * n_old + jnp.sum(wk * kb.astype(F32), axis=0, keepdims=True)
        m_ref[0, h:h + 1, :] = jnp.broadcast_to(m_new, (1, ML_DK))

        mu = jnp.mean(hh, axis=-1, keepdims=True)
        hc = hh - mu
        var = jnp.mean(hc * hc, axis=-1, keepdims=True)
        hn = hc * lax.rsqrt(var + LN_EPS) * ng_ref[:, h * ML_DV:(h + 1) * ML_DV]
        hg_ref[0, :, h * ML_DV:(h + 1) * ML_DV] = (jax.nn.sigmoid(o) * hn).astype(BF16)


def _mlstm(p, pg, b_gates, norm_g, c0, n0, m0, chunk):
    b, t, n = p.shape
    nc = t // chunk
    gt = jnp.swapaxes(pg[:, :, :2 * ML_HEADS], 1, 2)
    if nc > 1:
        gt_spec = pl.BlockSpec((1, 2 * ML_HEADS, chunk), lambda i, j: (i, 0, j))
    else:
        gt_spec = pl.BlockSpec((1, 2 * ML_HEADS, t), lambda i, j: (i, 0, 0))
    if nc > 1 and chunk % LANES != 0:
        gt = gt.reshape(b, 2 * ML_HEADS, nc, chunk).transpose(0, 2, 1, 3).reshape(b * nc, 2 * ML_HEADS, chunk)
        gt_spec = pl.BlockSpec((1, 2 * ML_HEADS, chunk), lambda i, j: (i * nc + j, 0, 0))
    m0b = jnp.broadcast_to(m0[..., None], (b, ML_HEADS, ML_DK))
    cspec = pl.BlockSpec((1, ML_HEADS, ML_DV, ML_DK), lambda i, j: (i, 0, 0, 0))
    nspec = pl.BlockSpec((1, ML_HEADS, ML_DK), lambda i, j: (i, 0, 0))
    hg, c, nn, m = pl.pallas_call(
        functools.partial(_mlstm_kernel, chunk=chunk),
        grid=(b, nc),
        in_specs=[
            pl.BlockSpec((1, chunk, n), lambda i, j: (i, j, 0)),
            pl.BlockSpec((1, chunk, pg.shape[-1]), lambda i, j: (i, j, 0)),
            gt_spec,
            pl.BlockSpec((1, 2 * ML_HEADS), lambda i, j: (0, 0)),
            pl.BlockSpec((2 * ML_HEADS, 1), lambda i, j: (0, 0)),
            pl.BlockSpec((1, ML_V), lambda i, j: (0, 0)),
            cspec, nspec, nspec,
        ],
        out_specs=[pl.BlockSpec((1, chunk, ML_V), lambda i, j: (i, j, 0)), cspec, nspec, nspec],
        out_shape=[
            jax.ShapeDtypeStruct((b, t, ML_V), BF16),
            jax.ShapeDtypeStruct((b, ML_HEADS, ML_DV, ML_DK), F32),
            jax.ShapeDtypeStruct((b, ML_HEADS, ML_DK), F32),
            jax.ShapeDtypeStruct((b, ML_HEADS, ML_DK), F32),
        ],
        compiler_params=_params("arbitrary", "arbitrary"),
        name="mlstm_scan",
    )(p, pg, gt, b_gates.reshape(1, -1), b_gates.reshape(-1, 1), norm_g.reshape(1, -1), c0, n0, m0b)
    return hg, c, nn, m[..., 0]


def _outproj_kernel(a_ref, w_ref, x_ref, g_ref, sc_ref, sh_ref, lng_ref, lnb_ref, wrh_ref, wrl_ref, br_ref,
                    x1_ref, u2_ref, te_ref, tg_ref):
    y = jnp.dot(a_ref[0], w_ref[...], preferred_element_type=F32)
    z = DEEPNORM_ALPHA * x_ref[0] + (1.0 + g_ref[0]) * y
    x1 = _layer_norm(z, lng_ref[...], lnb_ref[...])
    x1_ref[0] = x1
    u2 = x1 * (1.0 + sc_ref[0]) + sh_ref[0]
    u_hi = u2.astype(BF16)
    u2_ref[0] = u_hi
    u_lo = (u2 - u_hi.astype(F32)).astype(BF16)
    logits = (jnp.dot(u_hi, wrh_ref[...], preferred_element_type=F32)
              + jnp.dot(u_lo, wrh_ref[...], preferred_element_type=F32)
              + jnp.dot(u_hi, wrl_ref[...], preferred_element_type=F32)) + br_ref[...]
    lane = lax.broadcasted_iota(I32, logits.shape, 1).astype(F32)
    vals, idxs = [], []
    cur = logits
    for _ in range(TOP_K):
        mx = jnp.max(cur, axis=-1, keepdims=True)
        idx = jnp.min(jnp.where(cur == mx, lane, float(LANES)), axis=-1, keepdims=True)
        vals.append(mx)
        idxs.append(idx)
        cur = jnp.where(lane == idx, -jnp.inf, cur)
    es = [jnp.exp(v - vals[0]) for v in vals]
    tot = es[0] + es[1] + es[2] + es[3]
    for k in range(TOP_K):
        te_ref[0, :, k:k + 1] = idxs[k].astype(I32)
        tg_ref[0, :, k:k + 1] = es[k] / tot


def _outproj(a, w_bf, x, g, sc, sh, lng, lnb, wr_pad, br_pad):
    b, t, d = x.shape
    tm = _row_tile(t, OUTPROJ_ROW_TILE)
    wr_hi = wr_pad.astype(BF16)
    wr_lo = (wr_pad - wr_hi.astype(F32)).astype(BF16)
    row = pl.BlockSpec((1, tm, d), lambda i, j: (i, j, 0))
    vec = pl.BlockSpec((1, 1, d), lambda i, j: (i, 0, 0))
    par = pl.BlockSpec((1, d), lambda i, j: (0, 0))
    top = pl.BlockSpec((1, tm, TOP_K), lambda i, j: (i, j, 0))
    return pl.pallas_call(
        _outproj_kernel,
        grid=(b, t // tm),
        in_specs=[row, pl.BlockSpec((d, d), lambda i, j: (0, 0)), row, vec, vec, vec, par, par,
                  pl.BlockSpec((d, LANES), lambda i, j: (0, 0)), pl.BlockSpec((d, LANES), lambda i, j: (0, 0)),
                  pl.BlockSpec((1, LANES), lambda i, j: (0, 0))],
        out_specs=[row, row, top, top],
        out_shape=[jax.ShapeDtypeStruct((b, t, d), F32), jax.ShapeDtypeStruct((b, t, d), BF16),
                   jax.ShapeDtypeStruct((b, t, TOP_K), I32), jax.ShapeDtypeStruct((b, t, TOP_K), F32)],
        compiler_params=_params("arbitrary", "arbitrary"),
        name="outproj_ln_router",
    )(a, w_bf, x, g, sc, sh, lng, lnb, wr_hi, wr_lo, br_pad)


MOE_SCATTER_GROUP = 3


def _moe_kernel(cnt_ref, x_ref, te_ref, tg_ref, w1_ref, b1_ref, w2_ref, b2_ref, *rest, ts, nsub, rb, first_rows,
                emit_cast):
    e = pl.program_id(1)
    if emit_cast:
        o_ref, w1o_ref, w2o_ref, pos_ref, tri_ref, sp_ref, sy_ref = rest
        w1o_ref[0] = w1_ref[0].astype(BF16)
        w2o_ref[0] = w2_ref[0].astype(BF16)
        w1_ref, w2_ref = w1o_ref, w2o_ref
    else:
        o_ref, pos_ref, tri_ref, sp_ref, sy_ref = rest

    @pl.when(jnp.logical_and(pl.program_id(0) == 0, e == 0))
    def _():
        si = lax.broadcasted_iota(I32, (ts, ts), 0)
        ti = lax.broadcasted_iota(I32, (ts, ts), 1)
        tri_ref[...] = jnp.where(si <= ti, 1.0, 0.0).astype(BF16)

    @pl.when(e == 0)
    def _():
        o_ref[...] = jnp.zeros_like(o_ref)
        eio = lax.broadcasted_iota(I32, (N_EXPERTS, ts), 0)
        for s in range(nsub):
            sel = jnp.zeros((N_EXPERTS, ts), F32)
            for k in range(TOP_K):
                sel = sel + jnp.where(te_ref[k:k + 1, s * ts:(s + 1) * ts] == eio, 1.0, 0.0)
            rank = jnp.dot(sel.astype(BF16), tri_ref[...], preferred_element_type=F32)
            pos_ref[s] = rank * sel

    slot = pl.multiple_of((e % MOE_SCATTER_GROUP) * rb, rb)
    last = N_EXPERTS - 1

    for s in range(nsub):
        rows = slice(s * ts, (s + 1) * ts)
        pm = pos_ref[s, pl.ds(e, 1), :]
        gate = jnp.zeros((1, ts), F32)
        for k in range(TOP_K):
            gate = gate + jnp.where(te_ref[k:k + 1, rows] == e, tg_ref[k:k + 1, rows], 0.0)
        cnt = cnt_ref[(pl.program_id(0) * nsub + s) * N_EXPERTS + e]
        nblk = (cnt + (rb - 1)) // rb

        def block(blk, nrow=rb, pm=pm, gate=gate, rows=rows):
            r = (blk * rb + 1 + lax.broadcasted_iota(I32, (nrow, 1), 0)).astype(F32)
            hit = pm == r
            onehot = jnp.where(hit, 1.0, 0.0).astype(BF16)
            xg = jnp.dot(onehot, x_ref[rows, :], preferred_element_type=F32).astype(BF16)
            h = jnp.dot(xg, w1_ref[0], preferred_element_type=F32) + b1_ref[0]
            hg = jnp.minimum(h[:, :D_FF], SWIGLU_LIMIT)
            hl = jnp.clip(h[:, D_FF:], -SWIGLU_LIMIT, SWIGLU_LIMIT)
            a = hg * jax.nn.sigmoid(SWIGLU_ALPHA * hg) * (hl + 1.0)
            y = jnp.dot(a.astype(BF16), w2_ref[0], preferred_element_type=F32) + b2_ref[0]
            grow = jnp.sum(jnp.where(hit, gate, 0.0), axis=-1, keepdims=True)
            yg = (y * grow).astype(BF16)
            if nrow < rb:
                onehot = jnp.concatenate([onehot, jnp.zeros((rb - nrow, ts), BF16)], axis=0)
                yg = jnp.concatenate([yg, jnp.zeros((rb - nrow, yg.shape[1]), BF16)], axis=0)
            return onehot, yg

        below = 0
        for nrow in first_rows:
            fits = cnt > below if nrow == rb else jnp.logical_and(cnt > below, cnt <= nrow)

            @pl.when(fits)
            def _(s=s, block=block, nrow=nrow):
                onehot, yg = block(0, nrow)
                sp_ref[s, pl.ds(slot, rb), :] = onehot
                sy_ref[s, pl.ds(slot, rb), :] = yg
            below = nrow

        @pl.when(cnt == 0)
        def _(s=s):
            sp_ref[s, pl.ds(slot, rb), :] = jnp.zeros((rb, ts), BF16)
            sy_ref[s, pl.ds(slot, rb), :] = jnp.zeros((rb, sy_ref.shape[2]), BF16)

        def extra(blk, carry, block=block, rows=rows):
            onehot, yg = block(blk)
            o_ref[rows, :] += lax.dot_general(onehot, yg, TN_DIMS, preferred_element_type=F32)
            return carry

        lax.fori_loop(1, nblk, extra, 0)

        def flush(nrows, s=s, rows=rows):
            o_ref[rows, :] += lax.dot_general(sp_ref[s, :nrows, :], sy_ref[s, :nrows, :], TN_DIMS,
                                              preferred_element_type=F32)

        if last % MOE_SCATTER_GROUP != MOE_SCATTER_GROUP - 1:
            @pl.when(e == last)
            def _(flush=flush):
                flush((last % MOE_SCATTER_GROUP + 1) * rb)

        @pl.when(e % MOE_SCATTER_GROUP == MOE_SCATTER_GROUP - 1)
        def _(flush=flush):
            flush(MOE_SCATTER_GROUP * rb)


def _moe(u2, te, tg, w1, b1, w2, b2, layer, emit_cast=False):
    n, d = u2.shape
    ts = _row_tile(n, MOE_TOKEN_TILE)
    nsub = MOE_SUBTILES if n % (MOE_SUBTILES * ts) == 0 else 1
    tm = ts * nsub
    assert not emit_cast or n == tm
    mean_rows = ts * TOP_K // N_EXPERTS
    rb = min(ts, mean_rows + 32)
    first_rows = tuple(sorted({min(rb, mean_rows), min(rb, mean_rows + 16), rb}))
    te_t = te.T
    tg_t = tg.T
    hits = te.reshape(n // ts, ts * TOP_K)[:, :, None] == jnp.arange(N_EXPERTS, dtype=I32)
    counts = jnp.sum(hits.astype(I32), axis=1).reshape(-1)
    once = pl.Buffered(1)
    grid_spec = pltpu.PrefetchScalarGridSpec(
        num_scalar_prefetch=1,
        grid=(n // tm, N_EXPERTS),
        in_specs=[
            pl.BlockSpec((tm, d), lambda i, e, c: (i, 0), pipeline_mode=once),
            pl.BlockSpec((TOP_K, tm), lambda i, e, c: (0, i)),
            pl.BlockSpec((TOP_K, tm), lambda i, e, c: (0, i)),
            pl.BlockSpec((None, 1, d, 2 * D_FF), lambda i, e, c: (layer, e, 0, 0)),
            pl.BlockSpec((1, 1, 2 * D_FF), lambda i, e, c: (e, 0, 0)),
            pl.BlockSpec((None, 1, D_FF, d), lambda i, e, c: (layer, e, 0, 0)),
            pl.BlockSpec((1, 1, d), lambda i, e, c: (e, 0, 0)),
        ],
        out_specs=[pl.BlockSpec((tm, d), lambda i, e, c: (i, 0), pipeline_mode=once)] + ([
            pl.BlockSpec((1, d, 2 * D_FF), lambda i, e, c: (e, 0, 0)),
            pl.BlockSpec((1, D_FF, d), lambda i, e, c: (e, 0, 0))] if emit_cast else []),
        scratch_shapes=[pltpu.VMEM((nsub, N_EXPERTS, ts), F32), pltpu.VMEM((ts, ts), BF16),
                        pltpu.VMEM((nsub, MOE_SCATTER_GROUP * rb, ts), BF16),
                        pltpu.VMEM((nsub, MOE_SCATTER_GROUP * rb, d), BF16)],
    )
    out_shape = [jax.ShapeDtypeStruct((n, d), F32)] + ([
        jax.ShapeDtypeStruct((N_EXPERTS, d, 2 * D_FF), BF16),
        jax.ShapeDtypeStruct((N_EXPERTS, D_FF, d), BF16)] if emit_cast else [])
    outs = pl.pallas_call(
        functools.partial(_moe_kernel, ts=ts, nsub=nsub, rb=rb, first_rows=first_rows, emit_cast=emit_cast),
        grid_spec=grid_spec,
        out_shape=out_shape,
        compiler_params=_params("arbitrary", "arbitrary"),
        name="moe_experts_cast" if emit_cast else "moe_experts",
    )(counts, u2, te_t, tg_t, w1, b1.reshape(N_EXPERTS, 1, -1), w2, b2.reshape(N_EXPERTS, 1, -1))
    return outs if emit_cast else outs[0]


def _rope_tables(pos, head_dim):
    rd = head_dim // ROPE_FRACTION
    half = rd // 2
    inv = jnp.power(ROPE_THETA, -jnp.arange(half, dtype=F32) / half)
    ang = jnp.asarray(pos).astype(F32)[:, None] * inv[None, :]
    cos, sin = jnp.cos(ang), jnp.sin(ang)
    t = pos.shape[0]
    ones = jnp.ones((t, head_dim - rd), F32)
    zeros = jnp.zeros((t, head_dim - rd), F32)
    zh = jnp.zeros((t, half), F32)
    c = jnp.concatenate([cos, cos, ones], axis=-1)
    sa = jnp.concatenate([zh, sin, zeros], axis=-1)
    sb = jnp.concatenate([-sin, zh, zeros], axis=-1)
    rep = LANES // head_dim
    return tuple(jnp.tile(a, (1, rep)) for a in (c, sa, sb))


def _rope(z, c, sa, sb, half):
    w = z.shape[-1]
    rep = w // LANES
    if rep > 1:
        c, sa, sb = (jnp.concatenate([a] * rep, axis=-1) for a in (c, sa, sb))
    return z * c + pltpu.roll(z, half, 1) * sa + pltpu.roll(z, w - half, 1) * sb


def _dsa_inproj_kernel(x_ref, f_ref, g_ref, lng_ref, lnb_ref, sc_ref, sh_ref, w_in_ref,
                       c1_ref, sa1_ref, sb1_ref, c2_ref, sa2_ref, sb2_ref,
                       xo_ref, q_ref, kf_ref, vf_ref, kif_ref, kb_ref, vb_ref, kib_ref, qi_ref, w_ref):
    z = DEEPNORM_ALPHA * x_ref[0] + (1.0 + g_ref[0]) * f_ref[0]
    x = _layer_norm(z, lng_ref[...], lnb_ref[...])
    xo_ref[0] = x
    u = x * (1.0 + sc_ref[0]) + sh_ref[0]
    p = jnp.dot(u.astype(BF16), w_in_ref[...], preferred_element_type=F32)

    c1, sa1, sb1 = c1_ref[...], sa1_ref[...], sb1_ref[...]
    c2, sa2, sb2 = c2_ref[...], sa2_ref[...], sb2_ref[...]
    h1 = DSA_HD // ROPE_FRACTION // 2
    h2 = IDX_DIM // ROPE_FRACTION // 2
    o_k = DSA_Q
    o_v = DSA_Q + DSA_KV
    o_qi = DSA_Q + 2 * DSA_KV
    o_ki = o_qi + IDX_HEADS * IDX_DIM
    q = _rope(p[:, :DSA_Q], c1, sa1, sb1, h1)
    q_ref[0] = (q * DSA_LOG2_SCALE).astype(BF16)
    k = _rope(p[:, o_k:o_v], c1, sa1, sb1, h1)
    kf_ref[0] = k
    kb_ref[0] = k.astype(BF16)
    v = p[:, o_v:o_qi]
    vf_ref[0] = v
    vb_ref[0] = v.astype(BF16)
    qi = _rope(p[:, o_qi:o_ki], c2, sa2, sb2, h2)
    qi_ref[0] = qi.astype(BF16)
    tail = p[:, o_ki:o_ki + LANES]
    ki = _rope(tail, c2, sa2, sb2, h2)[:, :IDX_DIM]
    kif_ref[0] = ki
    kib_ref[0] = ki.astype(BF16)
    wi = tail[:, IDX_DIM:IDX_DIM + IDX_HEADS]
    w_ref[0] = (wi * (IDX_HEADS ** -0.5)) * (IDX_DIM ** -0.5)


def _dsa_inproj(x, f, g, lng, lnb, sc, sh, w_bf, pos):
    b, t, d = x.shape
    n = w_bf.shape[1]
    tm = _row_tile(t, PROJ_ROW_TILE)
    tabs = _rope_tables(pos, DSA_HD) + _rope_tables(pos, IDX_DIM)
    tab = pl.BlockSpec((tm, LANES), lambda i, j: (j, 0))
    vec = pl.BlockSpec((1, 1, d), lambda i, j: (i, 0, 0))
    par = pl.BlockSpec((1, d), lambda i, j: (0, 0))

    def row(w):
        return pl.BlockSpec((1, tm, w), lambda i, j: (i, j, 0))

    widths = [(d, F32), (DSA_Q, BF16), (DSA_KV, F32), (DSA_KV, F32), (IDX_DIM, F32), (DSA_KV, BF16), (DSA_KV, BF16),
              (IDX_DIM, BF16), (IDX_HEADS * IDX_DIM, BF16), (IDX_HEADS, F32)]
    return pl.pallas_call(
        _dsa_inproj_kernel,
        grid=(b, t // tm),
        in_specs=[row(d), row(d), vec, par, par, vec, vec, pl.BlockSpec((d, n), lambda i, j: (0, 0))] + [tab] * 6,
        out_specs=[row(w) for w, _ in widths],
        out_shape=[jax.ShapeDtypeStruct((b, t, w), dt) for w, dt in widths],
        compiler_params=_params("arbitrary", "arbitrary"),
        name="ln_inproj_rope_split",
    )(x, f, g, lng, lnb, sc, sh, w_bf, *tabs)


def _f2key(f):
    bits = lax.bitcast_convert_type(f, I32)
    return bits ^ ((bits >> 31) & 0x7FFFFFFF)


def _key2f(k):
    return lax.bitcast_convert_type(k ^ ((k >> 31) & 0x7FFFFFFF), F32)


def _dsa_kernel(nkt_ref, q_ref, qi_ref, w_ref, k_ref, v_ref, ki_ref, kpos_ref, qlim_ref, nadm_ref, o_ref,
                s_ref, wb_ref, qg_ref, m_ref, l_ref, acc_ref, flag_ref, *, tq, tk, nsel, value_iters):
    nkt = nkt_ref[pl.program_id(1)]
    qlim = qlim_ref[...]
    w = w_ref[0]
    qi = qi_ref[0]
    qis_all = jnp.concatenate([qi[:, h * IDX_DIM:(h + 1) * IDX_DIM] for h in range(IDX_HEADS)], axis=0)
    nlg = tk // LANES
    for h in range(IDX_HEADS):
        wb_ref[h] = jnp.broadcast_to(w[:, h:h + 1], (tq, LANES))

    def p1(kt, carry):
        rmax, rmin, cgt0, cge0 = carry
        off = pl.multiple_of(kt * tk, tk)
        kit = ki_ref[0, pl.ds(off, tk), :]
        acc = jnp.zeros((tq, tk), F32)
        d_all = lax.dot_general(qis_all, kit, NT_DIMS, preferred_element_type=F32)
        for h in range(IDX_HEADS):
            d = d_all[h * tq:(h + 1) * tq]
            acc = acc + jnp.maximum(d, 0.0) * jnp.concatenate([wb_ref[h]] * nlg, axis=1)
        s = jnp.where(kpos_ref[kt] < qlim, acc, -jnp.inf)
        s_ref[kt] = s
        gt0 = jnp.where(s > 0.0, 1.0, 0.0)
        ge0 = jnp.where(s >= 0.0, 1.0, 0.0)
        for g in range(nlg):
            lanes = slice(g * LANES, (g + 1) * LANES)
            rmax = jnp.maximum(rmax, s[:, lanes])
            rmin = jnp.minimum(rmin, acc[:, lanes])
            cgt0 = cgt0 + gt0[:, lanes]
            cge0 = cge0 + ge0[:, lanes]
        return rmax, rmin, cgt0, cge0

    zcol = jnp.zeros((tq, 1), F32)
    zlan = jnp.zeros((tq, LANES), F32)
    init = (jnp.full((tq, LANES), -jnp.inf, F32), jnp.full((tq, LANES), jnp.inf, F32), zlan, zlan)
    rmax, rmin, cgt0, cge0 = lax.fori_loop(0, nkt, p1, init)
    rmax = jnp.max(rmax, axis=-1, keepdims=True)
    rmin = jnp.min(rmin, axis=-1, keepdims=True)
    cgt0 = jnp.sum(cgt0, axis=-1, keepdims=True)
    cge0 = jnp.sum(cge0, axis=-1, keepdims=True)
    nadm = nadm_ref[...]

    rg = min(tq, 32)
    fsel = float(nsel)
    unbounded = float(2 ** 30)

    def count_ge(tb):
        def body(kt, c):
            out = []
            for r in range(tq // rg):
                tr = jnp.concatenate([tb[r * rg:(r + 1) * rg, :]] * nlg, axis=1)
                hit = jnp.where(s_ref[kt, r * rg:(r + 1) * rg, :] >= tr, 1.0, 0.0)
                cr = c[r * rg:(r + 1) * rg]
                for g in range(nlg):
                    cr = cr + hit[:, g * LANES:(g + 1) * LANES]
                out.append(cr)
            return jnp.concatenate(out, axis=0)
        c = lax.fori_loop(0, nkt, body, zlan)
        return jnp.broadcast_to(jnp.sum(c, axis=-1, keepdims=True), (tq, LANES))

    rmax, rmin, cgt0, cge0, nadm = (jnp.broadcast_to(a, (tq, LANES)) for a in (rmax, rmin, cgt0, cge0, nadm))
    few = nadm <= fsel
    many = jnp.logical_not(few)
    pos = jnp.logical_and(many, cgt0 > fsel)
    neg = jnp.logical_and(many, cge0 < fsel)
    zero = jnp.logical_and(many, jnp.logical_and(cgt0 <= fsel, cge0 >= fsel))
    lo0 = jnp.where(pos, 0, _f2key(jnp.where(neg, rmin, 0.0)))
    hi0 = jnp.where(pos, _f2key(jnp.where(pos, rmax, 0.0)) + 1, 0)
    clo0 = jnp.where(pos, cge0, nadm)
    chi0 = jnp.where(pos, 0.0, cge0)
    act0 = jnp.where(jnp.logical_or(pos, neg), 1.0, 0.0)
    t0 = jnp.where(few, -jnp.inf, 0.0)
    need0 = jnp.where(zero, fsel - cgt0, unbounded)

    def cond(st):
        return st[0] > 0.0

    def step(st):
        _, it, lo, hi, clo, chi, act, thr, need = st
        conv = jnp.logical_and(act > 0.0, lo + 1 >= hi)
        thr = jnp.where(conv, _key2f(lo), thr)
        need = jnp.where(conv, fsel - chi, need)
        act = jnp.where(conv, 0.0, act)
        flo, fhi = _key2f(lo), _key2f(hi)
        mid_i = (lo >> 1) + (hi >> 1) + (lo & hi & 1)
        lclo = jnp.log(clo)
        interp = (lclo - float(np.log(nsel - 0.5))) / (lclo - jnp.log(jnp.maximum(chi, 0.5)))
        frac = jnp.where(it % 3 == 2, 0.5, interp)
        mid_v = _f2key(flo + frac * (fhi - flo))
        use_v = jnp.logical_and(it < value_iters, jnp.logical_and(mid_v > lo, mid_v < hi))
        mid = jnp.where(use_v, mid_v, mid_i)
        tm = _key2f(mid)
        c = count_ge(jnp.where(act > 0.0, tm, jnp.inf))
        hit = jnp.logical_and(act > 0.0, c == fsel)
        thr = jnp.where(hit, tm, thr)
        up = jnp.logical_and(act > 0.0, c > fsel)
        dn = jnp.logical_and(act > 0.0, c < fsel)
        lo = jnp.where(up, mid, lo)
        clo = jnp.where(up, c, clo)
        hi = jnp.where(dn, mid, hi)
        chi = jnp.where(dn, c, chi)
        act = jnp.where(hit, 0.0, act)
        return jnp.max(act), it + 1, lo, hi, clo, chi, act, thr, need

    st = (jnp.max(act0), jnp.int32(0), lo0, hi0, clo0, chi0, act0, t0, need0)
    _, _, _, _, _, _, _, thr, need = lax.while_loop(cond, step, st)
    thr, need = thr[:, 0:1], need[:, 0:1]
    tied = need < unbounded * 0.5
    has_ties = jnp.max(jnp.where(tied, 1.0, 0.0)) > 0.0
    thr = jnp.maximum(thr, -F32_MAX)

    for g in range(DSA_KV_HEADS):
        for hh in range(DSA_GROUP):
            hd = g * DSA_GROUP + hh
            qg_ref[g, hh * tq:(hh + 1) * tq, :] = q_ref[0, :, hd * DSA_HD:(hd + 1) * DSA_HD]
    m_ref[...] = jnp.full(m_ref.shape, NEG_BIG, F32)
    l_ref[...] = jnp.zeros(l_ref.shape, F32)
    acc_ref[...] = jnp.zeros(acc_ref.shape, F32)

    ones_cols = jnp.ones((tk, DSA_HD), BF16)

    def attend(kt, bias):
        off = pl.multiple_of(kt * tk, tk)
        qk = [lax.dot_general(qg_ref[g], k_ref[0, pl.ds(off, tk), g * DSA_HD:(g + 1) * DSA_HD], NT_DIMS,
                              preferred_element_type=F32) for g in range(DSA_KV_HEADS)]
        for g in range(DSA_KV_HEADS):
            vt_g = jnp.concatenate([v_ref[0, pl.ds(off, tk), g * DSA_HD:(g + 1) * DSA_HD], ones_cols], axis=1)
            for hh in range(DSA_GROUP):
                rows = slice(hh * tq, (hh + 1) * tq)
                lg = qk[g][rows] + bias
                m_old = m_ref[g, rows, :]
                m_new = jnp.maximum(m_old, jnp.max(lg, axis=-1, keepdims=True))
                alpha = jnp.exp2(m_old - m_new)
                p = jnp.exp2(lg - jnp.concatenate([m_new] * nlg, axis=1))
                pv = jnp.dot(p.astype(BF16), vt_g, preferred_element_type=F32)
                acc_ref[g, rows, :] = alpha * acc_ref[g, rows, :] + pv[:, :DSA_HD]
                l_ref[g, rows, :] = alpha * l_ref[g, rows, :] + pv[:, DSA_HD:]
                m_ref[g, rows, :] = m_new

    @pl.when(jnp.logical_not(has_ties))
    def _():
        def body(kt, carry):
            attend(kt, jnp.where(s_ref[kt] >= thr, 0.0, NEG_BIG))
            return carry
        lax.fori_loop(0, nkt, body, 0)

    @pl.when(has_ties)
    def _():
        def scan(kt, carry):
            seen, cut, nin = carry
            eqc = jnp.sum(jnp.where(s_ref[kt] == thr, 1.0, 0.0), axis=-1, keepdims=True)
            here = jnp.logical_and(tied, jnp.logical_and(seen < need, seen + eqc >= need))
            cut = jnp.where(here, kt, cut)
            nin = jnp.where(here, need - seen, nin)
            flag_ref[kt] = jnp.max(jnp.where(here, 1.0, 0.0)).astype(I32)
            return seen + eqc, cut, nin

        cut0 = jnp.where(tied, -1, nkt).astype(I32)
        _, cut, nin = lax.fori_loop(0, nkt, scan, (zcol, cut0, zcol))

        def body(kt, carry):
            s = s_ref[kt]
            before = jnp.where(kt < cut, 0.0, NEG_BIG)

            def plain():
                return jnp.where(s == thr, before, NEG_BIG)

            def ranked():
                si = lax.broadcasted_iota(I32, (tk, tk), 0)
                ti = lax.broadcasted_iota(I32, (tk, tk), 1)
                triu = jnp.where(si <= ti, 1.0, 0.0).astype(BF16)
                eq = jnp.where(s == thr, 1.0, 0.0).astype(BF16)
                rank = jnp.dot(eq, triu, preferred_element_type=F32)
                inside = jnp.where(rank <= nin, 0.0, NEG_BIG)
                return jnp.where(s == thr, jnp.where(kt == cut, inside, before), NEG_BIG)

            tie_bias = lax.cond(flag_ref[kt] > 0, ranked, plain)
            attend(kt, jnp.where(s > thr, 0.0, tie_bias))
            return carry
        lax.fori_loop(0, nkt, body, 0)

    for g in range(DSA_KV_HEADS):
        og = acc_ref[g] / l_ref[g]
        for hh in range(DSA_GROUP):
            hd = g * DSA_GROUP + hh
            o_ref[0, :, hd * DSA_HD:(hd + 1) * DSA_HD] = og[hh * tq:(hh + 1) * tq, :].astype(BF16)


def _dsa_attend(q, qi, w, k, v, ki, kpos, qlim, tq, tk):
    b, t, _ = q.shape
    lp = k.shape[1]
    nq = t // tq
    nsel = min(DSA_TOPK, int(np.sum(kpos < np.iinfo(np.int32).max)) // 4)
    qmax = qlim.reshape(nq, tq).max(axis=1)
    nadm = np.searchsorted(kpos, qlim, side="left")
    nkt = np.minimum(-(-np.searchsorted(kpos, qmax, side="left") // tk), lp // tk).astype(np.int32)
    nkt = np.maximum(nkt, 1)
    single = pl.Buffered(1)

    def full(wd):
        return pl.BlockSpec((1, lp, wd), lambda i, j, n: (i, 0, 0), pipeline_mode=single)

    def row(wd):
        return pl.BlockSpec((1, tq, wd), lambda i, j, n: (i, j, 0))

    grid_spec = pltpu.PrefetchScalarGridSpec(
        num_scalar_prefetch=1,
        grid=(b, nq),
        in_specs=[row(DSA_Q), row(IDX_HEADS * IDX_DIM), row(IDX_HEADS), full(DSA_KV), full(DSA_KV), full(IDX_DIM),
                  pl.BlockSpec((lp // tk, 1, tk), lambda i, j, n: (0, 0, 0)),
                  pl.BlockSpec((tq, 1), lambda i, j, n: (j, 0)),
                  pl.BlockSpec((tq, 1), lambda i, j, n: (j, 0))],
        out_specs=row(DSA_Q),
        scratch_shapes=[
            pltpu.VMEM((lp // tk, tq, tk), F32),
            pltpu.VMEM((IDX_HEADS, tq, LANES), F32),
            pltpu.VMEM((DSA_KV_HEADS, DSA_GROUP * tq, DSA_HD), BF16),
            pltpu.VMEM((DSA_KV_HEADS, DSA_GROUP * tq, LANES), F32),
            pltpu.VMEM((DSA_KV_HEADS, DSA_GROUP * tq, LANES), F32),
            pltpu.VMEM((DSA_KV_HEADS, DSA_GROUP * tq, DSA_HD), F32),
            pltpu.SMEM((lp // tk,), I32),
        ],
    )
    return pl.pallas_call(
        functools.partial(_dsa_kernel, tq=tq, tk=tk, nsel=nsel, value_iters=DSA_VALUE_ITERS),
        grid_spec=grid_spec,
        out_shape=jax.ShapeDtypeStruct((b, t, DSA_Q), BF16),
        compiler_params=_params("arbitrary", "arbitrary"),
        name="dsa_select_attend",
    )(jnp.asarray(nkt), q, qi, w, k, v, ki, jnp.asarray(kpos, I32).reshape(lp // tk, 1, tk),
      jnp.asarray(qlim, I32).reshape(t, 1), jnp.asarray(nadm, F32).reshape(t, 1))


def _trunk(x, mods, pos, ml_state, ds_cache, prm, moe_bf=None):
    b, t, d = x.shape
    casts = []

    def moe(i, u2, te, tg):
        args = (u2.reshape(b * t, d), te.reshape(b * t, TOP_K), tg.reshape(b * t, TOP_K))
        if moe_bf is None:
            f, w1b, w2b = _moe(*args, prm["moe_w1"], prm["moe_b1"][i], prm["moe_w2"], prm["moe_b2"][i], i,
                               emit_cast=True)
            casts.append((w1b, w2b))
        else:
            w1b, w2b = moe_bf[i]
            f = _moe(*args, w1b[None], prm["moe_b1"][i], w2b[None], prm["moe_b2"][i], 0)
        return f.reshape(b, t, d)

    chunk = next((c for c in (ML_EVAL_CHUNK, CHUNK) if t % c == 0), t)

    def mod(i, k):
        return mods[i][:, None, k * d:(k + 1) * d]

    p, pg = _ml_inproj(x, mod(0, 1), mod(0, 0), prm["ml_w_in"])
    if ml_state is None:
        c0 = jnp.zeros((b, ML_HEADS, ML_DV, ML_DK), F32)
        n0 = jnp.zeros((b, ML_HEADS, ML_DK), F32)
        m0 = jnp.zeros((b, ML_HEADS), F32)
    else:
        c0, n0, m0 = (s[0] for s in ml_state)
    hg, c_new, n_new, m_new = _mlstm(p, pg, prm["ml_b_gates"], prm["ml_norm_g"], c0, n0, m0, chunk)
    x1, u2, te, tg = _outproj(hg, prm["ml_w_out"], x, mod(0, 2), mod(0, 4), mod(0, 3),
                              prm["ln_g"][0, 0:1], prm["ln_b"][0, 0:1], prm["w_router"][0], prm["b_router"][0])
    f = moe(0, u2, te, tg)

    x2, q, kf, vf, kif, kb, vb, kib, qi, w = _dsa_inproj(
        x1, f, mod(0, 5), prm["ln_g"][0, 1:2], prm["ln_b"][0, 1:2], mod(1, 1), mod(1, 0), prm["ds_w_in"], pos)
    pos_np = np.asarray(pos)
    qlim = (pos_np // CHUNK + 1) * CHUNK
    if ds_cache is None:
        kpos = pos_np
        tq, tk = DSA_Q_TILE, DSA_K_TILE
        if t % tk:
            tq, tk = t, t
        k_all, v_all, ki_all = kb, vb, kib
    else:
        ck, cv, cki = ds_cache
        past = ck.shape[1]
        total = past + t
        lp = -(-total // (3 * LANES)) * (3 * LANES)
        tq, tk = t, 3 * LANES
        kpos = np.concatenate([np.arange(past), pos_np, np.full((lp - total,), np.iinfo(np.int32).max)])

        def cat(cache, new):
            full = jnp.concatenate([cache.reshape(b, past, -1).astype(BF16), new], axis=1)
            return jnp.pad(full, ((0, 0), (0, lp - total), (0, 0)))

        k_all, v_all, ki_all = cat(ck, kb), cat(cv, vb), cat(cki, kib)
    o = _dsa_attend(q, qi, w, k_all, v_all, ki_all, kpos.astype(np.int64), qlim, tq, tk)
    x3, u2, te, tg = _outproj(o, prm["ds_w_out"], x2, mod(1, 2), mod(1, 4), mod(1, 3),
                              prm["ln_g"][1, 0:1], prm["ln_b"][1, 0:1], prm["w_router"][1], prm["b_router"][1])
    f = moe(1, u2, te, tg)
    y = _final_ln(x3, f, mod(1, 5), prm["ln_g"][1, 1:2], prm["ln_b"][1, 1:2])

    ml_out = (c_new[None], n_new[None], m_new[None])
    ds_out = (kf.reshape(1, b, t, DSA_KV_HEADS, DSA_HD), vf.reshape(1, b, t, DSA_KV_HEADS, DSA_HD), kif[None])
    return y, ml_out, ds_out, casts


def _prepare(w_ada, b_ada, ln_g, ln_b, ml_w_in, ml_b_gates, ml_norm_g, ml_w_out, ds_w_in, ds_w_out,
             moe_w_router, moe_b_router, moe_w1, moe_b1, moe_w2, moe_b2):
    return {
        "ln_g": ln_g, "ln_b": ln_b,
        "ml_w_in": jnp.pad(ml_w_in[0], ((0, 0), (0, ML_IN_PAD - ML_IN))).astype(BF16),
        "ml_b_gates": ml_b_gates[0], "ml_norm_g": ml_norm_g[0],
        "ml_w_out": ml_w_out[0].astype(BF16),
        "ds_w_in": jnp.pad(ds_w_in[0], ((0, 0), (0, DSA_IN_PAD - DSA_IN))).astype(BF16),
        "ds_w_out": ds_w_out[0].astype(BF16),
        "w_router": jnp.pad(moe_w_router, ((0, 0), (0, 0), (0, LANES - N_EXPERTS))),
        "b_router": jnp.pad(moe_b_router, ((0, 0), (0, LANES - N_EXPERTS)), constant_values=-jnp.inf)[:, None, :],
        "moe_w1": moe_w1, "moe_b1": moe_b1, "moe_w2": moe_w2, "moe_b2": moe_b2,
    }


def kernel(x_prompt, x_sample, state_mlstm_C, state_mlstm_n, state_mlstm_m, cache_dsa_k, cache_dsa_v,
           cache_dsa_kidx, c_prompt, c_sample, w_ada, b_ada, ln_g, ln_b, ml_w_in, ml_b_gates, ml_norm_g,
           ml_w_out, ds_w_in, ds_w_out, moe_w_router, moe_b_router, moe_w1, moe_b1, moe_w2, moe_b2):
    prm = _prepare(w_ada, b_ada, ln_g, ln_b, ml_w_in, ml_b_gates, ml_norm_g, ml_w_out, ds_w_in, ds_w_out,
                   moe_w_router, moe_b_router, moe_w1, moe_b1, moe_w2, moe_b2)
    bp, bs = c_prompt.shape[0], c_sample.shape[0]
    rows = -(-(bp + bs) // 8) * 8
    c_rows = jnp.pad(jnp.concatenate([c_prompt, c_sample], axis=0), ((0, rows - bp - bs), (0, 0)))
    mods = _ada(c_rows, w_ada, b_ada)
    mods_p = [mods[i, :bp] for i in range(DEPTH)]
    mods_s = [mods[i, bp:bp + bs] for i in range(DEPTH)]

    past = cache_dsa_k.shape[2]
    pos_s = past + np.arange(x_sample.shape[1], dtype=np.int32)
    y_s, ml_s, ds_s, moe_bf = _trunk(x_sample, mods_s, pos_s, (state_mlstm_C, state_mlstm_n, state_mlstm_m),
                                     (cache_dsa_k[0], cache_dsa_v[0], cache_dsa_kidx[0]), prm)
    pos_p = np.arange(x_prompt.shape[1], dtype=np.int32)
    y_p, ml_p, ds_p, _ = _trunk(x_prompt, mods_p, pos_p, None, None, prm, moe_bf=moe_bf)
    return (y_p, y_s, ml_p[0], ml_p[1], ml_p[2], ds_p[0], ds_p[1], ds_p[2],
            ml_s[0], ml_s[1], ml_s[2], ds_s[0], ds_s[1], ds_s[2])
```
